```python
import math
import jax
import jax.numpy as jnp
from jax import lax
import numpy as np

D_MODEL = 2048
BATCH = 8
SEQ = 2048
DEPTH = 2

RET_HEADS = 4
RET_QK_DIM = 256
RET_V_DIM = D_MODEL // RET_HEADS
RET_QK_WIDTH = RET_HEADS * RET_QK_DIM
CHUNK = 128
ROPE_BASE = 10000.0
SSM_GROUP = 16
SSM_GROUPS = D_MODEL // SSM_GROUP
SSM_STATE = 64
DT_MIN = 0.001
DT_MAX = 0.1
D_FF = ((8 * D_MODEL // 3 + 255) // 256) * 256
IN_WIDTH = 2 * RET_QK_WIDTH + 5 * D_MODEL
EPS = 1e-6

kernel_name = "hybrid_retention_s5_gated_encoder"


def rms_norm(x, g):
    xf = x.astype(jnp.float32)
    y = xf * lax.rsqrt(jnp.mean(xf * xf, axis=-1, keepdims=True) + EPS)
    return (y * g.astype(jnp.float32)).astype(x.dtype)


def rotary(x):
    L = x.shape[1]
    half = x.shape[-1] // 2
    inv = 1.0 / (ROPE_BASE ** (jnp.arange(half, dtype=jnp.float32) / half))
    ang = jnp.arange(L, dtype=jnp.float32)[:, None] * inv[None, :]
    cos = jnp.cos(ang)[None, :, None, :]
    sin = jnp.sin(ang)[None, :, None, :]
    xf = x.astype(jnp.float32)
    x1, x2 = xf[..., :half], xf[..., half:]
    return jnp.concatenate([x1 * cos - x2 * sin, x1 * sin + x2 * cos], axis=-1)


def retention(q, k, v, log_gamma):
    f32 = jnp.float32
    b, l, h, dk = q.shape
    dv = v.shape[-1]
    nc = l // CHUNK
    q = q.reshape(b, nc, CHUNK, h, dk)
    k = (k * dk ** -0.5).reshape(b, nc, CHUNK, h, dk)
    v = v.astype(f32).reshape(b, nc, CHUNK, h, dv)
    lg = log_gamma.astype(f32)
    lg_f, lg_b = lg[0], lg[1]
    t = jnp.arange(CHUNK, dtype=f32)
    diff = t[:, None] - t[None, :]
    dmat = jnp.exp(jnp.where(diff >= 0, lg_f[:, None, None] * diff, -lg_b[:, None, None] * diff))
    scores = jnp.einsum('bnthd,bnshd->bnhts', q, k) * dmat
    y = jnp.einsum('bnhts,bnshe->bnthe', scores, v)
    kf = k * jnp.exp(lg_f[None, :] * (CHUNK - 1.0 - t)[:, None])[:, :, None]
    kb = k * jnp.exp(lg_b[None, :] * t[:, None])[:, :, None]
    kv_f = jnp.einsum('bnshd,bnshe->nbhde', kf, v)
    kv_b = jnp.einsum('bnshd,bnshe->nbhde', kb, v)
    decay_f = jnp.exp(lg_f * CHUNK)[None, :, None, None]
    decay_b = jnp.exp(lg_b * CHUNK)[None, :, None, None]

    def step_f(s, kv):
        return decay_f * s + kv, s

    def step_b(s, kv):
        return decay_b * s + kv, s

    zero = jnp.zeros((b, h, dk, dv), f32)
    _, s_f = lax.scan(step_f, zero, kv_f)
    _, s_b = lax.scan(step_b, zero, kv_b, reverse=True)
    qf = q * jnp.exp(lg_f[None, :] * (t[:, None] + 1.0))[:, :, None]
    qb = q * jnp.exp(lg_b[None, :] * (CHUNK - t)[:, None])[:, :, None]
    y = y + jnp.einsum('bnthd,nbhde->bnthe', qf, s_f) + jnp.einsum('bnthd,nbhde->bnthe', qb, s_b)
    return y.reshape(b, l, h, dv)


def _linear_recurrence(e1, e2):
    a1, b1 = e1
    a2, b2 = e2
    return a1 * a2, a2 * b1 + b2


def s5_direction(u, a_re, a_im, log_dt, b_re, b_im, c_re, c_im, reverse):
    f32 = jnp.float32
    lam = lax.complex(a_re.astype(f32), a_im.astype(f32))
    dt = jnp.exp(log_dt.astype(f32))[:, None]
    lam_bar = jnp.exp(lam * dt)
    b_c = lax.complex(b_re.astype(f32), b_im.astype(f32))
    b_bar = ((lam_bar - 1.0) / lam)[:, :, None] * b_c
    bu = lax.complex(jnp.einsum('blgh,gph->blgp', u, jnp.real(b_bar)),
                     jnp.einsum('blgh,gph->blgp', u, jnp.imag(b_bar)))
    a = jnp.broadcast_to(lam_bar, bu.shape)
    _, xs = lax.associative_scan(_linear_recurrence, (a, bu), axis=1, reverse=reverse)
    return (jnp.einsum('blgp,ghp->blgh', jnp.real(xs), c_re.astype(f32))
            - jnp.einsum('blgp,ghp->blgh', jnp.imag(xs), c_im.astype(f32)))


def hybrid_mixer(h, w_in, log_gamma, a_re, a_im, log_dt, b_re, b_im, c_re, c_im,
                 d_skip, w_glu, b_glu, w_out):
    bsz, l, _ = h.shape
    dt_in = h.dtype
    proj = h @ w_in
    cuts = [RET_QK_WIDTH, 2 * RET_QK_WIDTH, 2 * RET_QK_WIDTH + D_MODEL,
            2 * RET_QK_WIDTH + 2 * D_MODEL, 2 * RET_QK_WIDTH + 3 * D_MODEL,
            2 * RET_QK_WIDTH + 4 * D_MODEL]
    q, k, v, g, u, gate_r, gate_s = jnp.split(proj, cuts, axis=-1)

    q = rotary(q.reshape(bsz, l, RET_HEADS, RET_QK_DIM))
    k = rotary(k.reshape(bsz, l, RET_HEADS, RET_QK_DIM))
    v = v.reshape(bsz, l, RET_HEADS, RET_V_DIM)
    y = retention(q, k, v, log_gamma)
    y = y * lax.rsqrt(jnp.mean(y * y, axis=-1, keepdims=True) + EPS)
    ret_out = jax.nn.silu(g.astype(jnp.float32)) * y.reshape(bsz, l, D_MODEL)

    uf = u.astype(jnp.float32)
    ug = uf.reshape(bsz, l, SSM_GROUPS, SSM_GROUP)
    ys = (s5_direction(ug, a_re[0], a_im[0], log_dt[0], b_re[0], b_im[0], c_re[0], c_im[0], False)
          + s5_direction(ug, a_re[1], a_im[1], log_dt[1], b_re[1], b_im[1], c_re[1], c_im[1], True))
    ys = ys.reshape(bsz, l, D_MODEL) + d_skip.astype(jnp.float32) * uf
    ys = jax.nn.gelu(ys).astype(dt_in)
    ssm_out = ys * jax.nn.sigmoid(ys @ w_glu + b_glu)

    merged = (jax.nn.sigmoid(gate_r) * ret_out.astype(dt_in)
              + jax.nn.sigmoid(gate_s) * ssm_out)
    return merged @ w_out


def swiglu(h, w_gate, w_up, w_down):
    return (jax.nn.silu(h @ w_gate) * (h @ w_up)) @ w_down


def _fwd_setup_inputs(seed: int = 0) -> dict:
    key = jax.random.key(seed)
    ks = jax.random.split(key, 24)
    f32 = jnp.float32
    G, P, Hg = SSM_GROUPS, SSM_STATE, SSM_GROUP
    nrm = lambda k, shape, scale: jax.random.normal(k, shape, f32) * scale
    x = nrm(ks[0], (BATCH, SEQ, D_MODEL), 1.0)
    ln_mix_g = 1.0 + nrm(ks[1], (DEPTH, D_MODEL), 0.02)
    w_in = nrm(ks[2], (DEPTH, D_MODEL, IN_WIDTH), D_MODEL ** -0.5)
    base_lg = jnp.log(1.0 - 2.0 ** (-5.0 - jnp.arange(RET_HEADS, dtype=f32)))
    ret_log_gamma = base_lg[None, None, :] * (1.0 + nrm(ks[3], (DEPTH, 2, RET_HEADS), 0.05))
    n = jnp.arange(P, dtype=f32)
    ssm_a_re = -0.5 + nrm(ks[4], (DEPTH, 2, G, P), 0.01)
    ssm_a_im = math.pi * n[None, None, None, :] + nrm(ks[5], (DEPTH, 2, G, P), 0.01)
    ssm_log_dt = jax.random.uniform(ks[6], (DEPTH, 2, G), f32,
                                    math.log(DT_MIN), math.log(DT_MAX))
    ssm_b_re = nrm(ks[7], (DEPTH, 2, G, P, Hg), (2.0 * Hg) ** -0.5)
    ssm_b_im = nrm(ks[8], (DEPTH, 2, G, P, Hg), (2.0 * Hg) ** -0.5)
    ssm_c_re = nrm(ks[9], (DEPTH, 2, G, Hg, P), (2.0 * P) ** -0.5)
    ssm_c_im = nrm(ks[10], (DEPTH, 2, G, Hg, P), (2.0 * P) ** -0.5)
    ssm_d = nrm(ks[11], (DEPTH, D_MODEL), 1.0)
    w_glu = nrm(ks[12], (DEPTH, D_MODEL, D_MODEL), D_MODEL ** -0.5)
    b_glu = nrm(ks[13], (DEPTH, D_MODEL), 0.01)
    w_out = nrm(ks[14], (DEPTH, D_MODEL, D_MODEL), D_MODEL ** -0.5)
    ln_ffn_g = 1.0 + nrm(ks[15], (DEPTH, D_MODEL), 0.02)
    w_ffn_gate = nrm(ks[16], (DEPTH, D_MODEL, D_FF), D_MODEL ** -0.5)
    w_ffn_up = nrm(ks[17], (DEPTH, D_MODEL, D_FF), D_MODEL ** -0.5)
    w_ffn_down = nrm(ks[18], (DEPTH, D_FF, D_MODEL), D_FF ** -0.5)
    ln_final_g = 1.0 + nrm(ks[19], (D_MODEL,), 0.02)
    return {"x": x, "ln_mix_g": ln_mix_g, "w_in": w_in, "ret_log_gamma": ret_log_gamma,
            "ssm_a_re": ssm_a_re, "ssm_a_im": ssm_a_im, "ssm_log_dt": ssm_log_dt,
            "ssm_b_re": ssm_b_re, "ssm_b_im": ssm_b_im, "ssm_c_re": ssm_c_re,
            "ssm_c_im": ssm_c_im, "ssm_d": ssm_d, "w_glu": w_glu, "b_glu": b_glu,
            "w_out": w_out, "ln_ffn_g": ln_ffn_g, "w_ffn_gate": w_ffn_gate,
            "w_ffn_up": w_ffn_up, "w_ffn_down": w_ffn_down, "ln_final_g": ln_final_g}


def _fwd_reference(x, ln_mix_g, w_in, ret_log_gamma, ssm_a_re, ssm_a_im, ssm_log_dt,
              ssm_b_re, ssm_b_im, ssm_c_re, ssm_c_im, ssm_d, w_glu, b_glu, w_out,
              ln_ffn_g, w_ffn_gate, w_ffn_up, w_ffn_down, ln_final_g):
    for i in range(DEPTH):
        h = rms_norm(x, ln_mix_g[i])
        x = x + hybrid_mixer(h, w_in[i], ret_log_gamma[i], ssm_a_re[i], ssm_a_im[i],
                             ssm_log_dt[i], ssm_b_re[i], ssm_b_im[i], ssm_c_re[i],
                             ssm_c_im[i], ssm_d[i], w_glu[i], b_glu[i], w_out[i])
        h = rms_norm(x, ln_ffn_g[i])
        x = x + swiglu(h, w_ffn_gate[i], w_ffn_up[i], w_ffn_down[i])
    return rms_norm(x, ln_final_g)


import jax as _jax
import jax.numpy as _jnp

TWIN_FORMAT = 'train_step'
FWD_PARAMS = ['x', 'ln_mix_g', 'w_in', 'ret_log_gamma', 'ssm_a_re', 'ssm_a_im', 'ssm_log_dt', 'ssm_b_re', 'ssm_b_im', 'ssm_c_re', 'ssm_c_im', 'ssm_d', 'w_glu', 'b_glu', 'w_out', 'ln_ffn_g', 'w_ffn_gate', 'w_ffn_up', 'w_ffn_down', 'ln_final_g']
TWIN_WEIGHTS = ['ln_mix_g', 'w_in', 'ret_log_gamma', 'ssm_a_re', 'ssm_a_im', 'ssm_log_dt', 'ssm_b_re', 'ssm_b_im', 'ssm_c_re', 'ssm_c_im', 'ssm_d', 'w_glu', 'b_glu', 'w_out', 'ln_ffn_g', 'w_ffn_gate', 'w_ffn_up', 'w_ffn_down', 'ln_final_g']
TWIN_DIFF_INPUT = 'x'
TWIN_INPUTS = ['x', 'ln_mix_g', 'w_in', 'ret_log_gamma', 'ssm_a_re', 'ssm_a_im', 'ssm_log_dt', 'ssm_b_re', 'ssm_b_im', 'ssm_c_re', 'ssm_c_im', 'ssm_d', 'w_glu', 'b_glu', 'w_out', 'ln_ffn_g', 'w_ffn_gate', 'w_ffn_up', 'w_ffn_down', 'ln_final_g', 'loss_target', 'm_ln_mix_g', 'm_w_in', 'm_ret_log_gamma', 'm_ssm_a_re', 'm_ssm_a_im', 'm_ssm_log_dt', 'm_ssm_b_re', 'm_ssm_b_im', 'm_ssm_c_re', 'm_ssm_c_im', 'm_ssm_d', 'm_w_glu', 'm_b_glu', 'm_w_out', 'm_ln_ffn_g', 'm_w_ffn_gate', 'm_w_ffn_up', 'm_w_ffn_down', 'm_ln_final_g', 'v_ln_mix_g', 'v_w_in', 'v_ret_log_gamma', 'v_ssm_a_re', 'v_ssm_a_im', 'v_ssm_log_dt', 'v_ssm_b_re', 'v_ssm_b_im', 'v_ssm_c_re', 'v_ssm_c_im', 'v_ssm_d', 'v_w_glu', 'v_b_glu', 'v_w_out', 'v_ln_ffn_g', 'v_w_ffn_gate', 'v_w_ffn_up', 'v_w_ffn_down', 'v_ln_final_g']
TWIN_OUTPUTS = ['loss', 'grad_x', 'grad_ln_mix_g', 'grad_w_in', 'grad_ret_log_gamma', 'grad_ssm_a_re', 'grad_ssm_a_im', 'grad_ssm_log_dt', 'grad_ssm_b_re', 'grad_ssm_b_im', 'grad_ssm_c_re', 'grad_ssm_c_im', 'grad_ssm_d', 'grad_w_glu', 'grad_b_glu', 'grad_w_out', 'grad_ln_ffn_g', 'grad_w_ffn_gate', 'grad_w_ffn_up', 'grad_w_ffn_down', 'grad_ln_final_g', 'delta_ln_mix_g', 'delta_w_in', 'delta_ret_log_gamma', 'delta_ssm_a_re', 'delta_ssm_a_im', 'delta_ssm_log_dt', 'delta_ssm_b_re', 'delta_ssm_b_im', 'delta_ssm_c_re', 'delta_ssm_c_im', 'delta_ssm_d', 'delta_w_glu', 'delta_b_glu', 'delta_w_out', 'delta_ln_ffn_g', 'delta_w_ffn_gate', 'delta_w_ffn_up', 'delta_w_ffn_down', 'delta_ln_final_g', 'new_m_ln_mix_g', 'new_m_w_in', 'new_m_ret_log_gamma', 'new_m_ssm_a_re', 'new_m_ssm_a_im', 'new_m_ssm_log_dt', 'new_m_ssm_b_re', 'new_m_ssm_b_im', 'new_m_ssm_c_re', 'new_m_ssm_c_im', 'new_m_ssm_d', 'new_m_w_glu', 'new_m_b_glu', 'new_m_w_out', 'new_m_ln_ffn_g', 'new_m_w_ffn_gate', 'new_m_w_ffn_up', 'new_m_w_ffn_down', 'new_m_ln_final_g', 'new_v_ln_mix_g', 'new_v_w_in', 'new_v_ret_log_gamma', 'new_v_ssm_a_re', 'new_v_ssm_a_im', 'new_v_ssm_log_dt', 'new_v_ssm_b_re', 'new_v_ssm_b_im', 'new_v_ssm_c_re', 'new_v_ssm_c_im', 'new_v_ssm_d', 'new_v_w_glu', 'new_v_b_glu', 'new_v_w_out', 'new_v_ln_ffn_g', 'new_v_w_ffn_gate', 'new_v_w_ffn_up', 'new_v_w_ffn_down', 'new_v_ln_final_g']
TWIN_LEAF_KINDS = {'loss': 'loss', 'grad_x': 'grad_x', 'grad_ln_mix_g': 'grad_w', 'grad_w_in': 'grad_w', 'grad_ret_log_gamma': 'grad_w', 'grad_ssm_a_re': 'grad_w', 'grad_ssm_a_im': 'grad_w', 'grad_ssm_log_dt': 'grad_w', 'grad_ssm_b_re': 'grad_w', 'grad_ssm_b_im': 'grad_w', 'grad_ssm_c_re': 'grad_w', 'grad_ssm_c_im': 'grad_w', 'grad_ssm_d': 'grad_w', 'grad_w_glu': 'grad_w', 'grad_b_glu': 'grad_w', 'grad_w_out': 'grad_w', 'grad_ln_ffn_g': 'grad_w', 'grad_w_ffn_gate': 'grad_w', 'grad_w_ffn_up': 'grad_w', 'grad_w_ffn_down': 'grad_w', 'grad_ln_final_g': 'grad_w', 'delta_ln_mix_g': 'delta_w', 'delta_w_in': 'delta_w', 'delta_ret_log_gamma': 'delta_w', 'delta_ssm_a_re': 'delta_w', 'delta_ssm_a_im': 'delta_w', 'delta_ssm_log_dt': 'delta_w', 'delta_ssm_b_re': 'delta_w', 'delta_ssm_b_im': 'delta_w', 'delta_ssm_c_re': 'delta_w', 'delta_ssm_c_im': 'delta_w', 'delta_ssm_d': 'delta_w', 'delta_w_glu': 'delta_w', 'delta_b_glu': 'delta_w', 'delta_w_out': 'delta_w', 'delta_ln_ffn_g': 'delta_w', 'delta_w_ffn_gate': 'delta_w', 'delta_w_ffn_up': 'delta_w', 'delta_w_ffn_down': 'delta_w', 'delta_ln_final_g': 'delta_w', 'new_m_ln_mix_g': 'new_m', 'new_m_w_in': 'new_m', 'new_m_ret_log_gamma': 'new_m', 'new_m_ssm_a_re': 'new_m', 'new_m_ssm_a_im': 'new_m', 'new_m_ssm_log_dt': 'new_m', 'new_m_ssm_b_re': 'new_m', 'new_m_ssm_b_im': 'new_m', 'new_m_ssm_c_re': 'new_m', 'new_m_ssm_c_im': 'new_m', 'new_m_ssm_d': 'new_m', 'new_m_w_glu': 'new_m', 'new_m_b_glu': 'new_m', 'new_m_w_out': 'new_m', 'new_m_ln_ffn_g': 'new_m', 'new_m_w_ffn_gate': 'new_m', 'new_m_w_ffn_up': 'new_m', 'new_m_w_ffn_down': 'new_m', 'new_m_ln_final_g': 'new_m', 'new_v_ln_mix_g': 'new_v', 'new_v_w_in': 'new_v', 'new_v_ret_log_gamma': 'new_v', 'new_v_ssm_a_re': 'new_v', 'new_v_ssm_a_im': 'new_v', 'new_v_ssm_log_dt': 'new_v', 'new_v_ssm_b_re': 'new_v', 'new_v_ssm_b_im': 'new_v', 'new_v_ssm_c_re': 'new_v', 'new_v_ssm_c_im': 'new_v', 'new_v_ssm_d': 'new_v', 'new_v_w_glu': 'new_v', 'new_v_b_glu': 'new_v', 'new_v_w_out': 'new_v', 'new_v_ln_ffn_g': 'new_v', 'new_v_w_ffn_gate': 'new_v', 'new_v_w_ffn_up': 'new_v', 'new_v_w_ffn_down': 'new_v', 'new_v_ln_final_g': 'new_v'}


def _forward(args):
    return _fwd_reference(*[args[k] for k in FWD_PARAMS])


def _output_shape():
    out = _jax.eval_shape(lambda: _forward(_fwd_setup_inputs(0)))
    return out.shape, out.dtype

N_MICROBATCH = 1
ADAM_LR = 0.001
ADAM_B1 = 0.9
ADAM_B2 = 0.999
ADAM_EPS = 1e-08
ADAM_WD = 0.01
ADAM_STEP = 10
PER_EXAMPLE_BATCH_AXIS = {'x': 0, 'loss_target': 0}
SHARED_INPUTS = []
_WEIGHT_DTYPES = {'ln_mix_g': _jnp.float32, 'w_in': _jnp.float32, 'ret_log_gamma': _jnp.float32, 'ssm_a_re': _jnp.float32, 'ssm_a_im': _jnp.float32, 'ssm_log_dt': _jnp.float32, 'ssm_b_re': _jnp.float32, 'ssm_b_im': _jnp.float32, 'ssm_c_re': _jnp.float32, 'ssm_c_im': _jnp.float32, 'ssm_d': _jnp.float32, 'w_glu': _jnp.float32, 'b_glu': _jnp.float32, 'w_out': _jnp.float32, 'ln_ffn_g': _jnp.float32, 'w_ffn_gate': _jnp.float32, 'w_ffn_up': _jnp.float32, 'w_ffn_down': _jnp.float32, 'ln_final_g': _jnp.float32}
MOMENT_SCALE = {'ln_mix_g': 4.949191e-02, 'w_in': 1.921176e-02, 'ret_log_gamma': 8.033801e+01, 'ssm_a_re': 6.537575e-04, 'ssm_a_im': 6.715218e-04, 'ssm_log_dt': 4.258251e-01, 'ssm_b_re': 4.423048e-04, 'ssm_b_im': 4.441673e-04, 'ssm_c_re': 8.780724e-04, 'ssm_c_im': 8.839912e-04, 'ssm_d': 1.379628e-02, 'w_glu': 3.703967e-03, 'b_glu': 5.544359e-03, 'w_out': 2.517010e-02, 'ln_ffn_g': 4.582964e-02, 'w_ffn_gate': 1.988261e-02, 'w_ffn_up': 1.925278e-02, 'w_ffn_down': 3.194252e-02, 'ln_final_g': 8.000123e+00}


def _to_microbatches(a, axis):
    t = _jnp.moveaxis(a, axis, 0)
    t = t.reshape((N_MICROBATCH, t.shape[0] // N_MICROBATCH) + t.shape[1:])
    return _jnp.moveaxis(t, 1, axis + 1)


def setup_inputs(seed: int = 0) -> dict:
    inp = _fwd_setup_inputs(seed)
    key = _jax.random.fold_in(_jax.random.key(seed), 7919)
    shape, _ = _output_shape()
    out = dict(inp)
    out["loss_target"] = _jax.random.normal(_jax.random.fold_in(key, 0), shape, _jnp.float32)
    for i, name in enumerate(TWIN_WEIGHTS):
        w = inp[name].astype(_jnp.float32)
        if MOMENT_SCALE is None:
            s = _jnp.sqrt(_jnp.mean(_jnp.square(w)) + 1e-30)
        else:
            s = MOMENT_SCALE[name]
        km, kv = _jax.random.split(_jax.random.fold_in(key, i + 1))
        out[name] = w
        out["m_" + name] = s * _jax.random.normal(km, w.shape, _jnp.float32)
        out["v_" + name] = (s * s) * _jax.random.uniform(kv, w.shape, _jnp.float32, 0.5, 1.5)
    if N_MICROBATCH > 1:
        for name, axis in PER_EXAMPLE_BATCH_AXIS.items():
            out[name] = _to_microbatches(out[name], axis)
    return {'x': out['x'], 'ln_mix_g': out['ln_mix_g'], 'w_in': out['w_in'], 'ret_log_gamma': out['ret_log_gamma'], 'ssm_a_re': out['ssm_a_re'], 'ssm_a_im': out['ssm_a_im'], 'ssm_log_dt': out['ssm_log_dt'], 'ssm_b_re': out['ssm_b_re'], 'ssm_b_im': out['ssm_b_im'], 'ssm_c_re': out['ssm_c_re'], 'ssm_c_im': out['ssm_c_im'], 'ssm_d': out['ssm_d'], 'w_glu': out['w_glu'], 'b_glu': out['b_glu'], 'w_out': out['w_out'], 'ln_ffn_g': out['ln_ffn_g'], 'w_ffn_gate': out['w_ffn_gate'], 'w_ffn_up': out['w_ffn_up'], 'w_ffn_down': out['w_ffn_down'], 'ln_final_g': out['ln_final_g'], 'loss_target': out['loss_target'], 'm_ln_mix_g': out['m_ln_mix_g'], 'm_w_in': out['m_w_in'], 'm_ret_log_gamma': out['m_ret_log_gamma'], 'm_ssm_a_re': out['m_ssm_a_re'], 'm_ssm_a_im': out['m_ssm_a_im'], 'm_ssm_log_dt': out['m_ssm_log_dt'], 'm_ssm_b_re': out['m_ssm_b_re'], 'm_ssm_b_im': out['m_ssm_b_im'], 'm_ssm_c_re': out['m_ssm_c_re'], 'm_ssm_c_im': out['m_ssm_c_im'], 'm_ssm_d': out['m_ssm_d'], 'm_w_glu': out['m_w_glu'], 'm_b_glu': out['m_b_glu'], 'm_w_out': out['m_w_out'], 'm_ln_ffn_g': out['m_ln_ffn_g'], 'm_w_ffn_gate': out['m_w_ffn_gate'], 'm_w_ffn_up': out['m_w_ffn_up'], 'm_w_ffn_down': out['m_w_ffn_down'], 'm_ln_final_g': out['m_ln_final_g'], 'v_ln_mix_g': out['v_ln_mix_g'], 'v_w_in': out['v_w_in'], 'v_ret_log_gamma': out['v_ret_log_gamma'], 'v_ssm_a_re': out['v_ssm_a_re'], 'v_ssm_a_im': out['v_ssm_a_im'], 'v_ssm_log_dt': out['v_ssm_log_dt'], 'v_ssm_b_re': out['v_ssm_b_re'], 'v_ssm_b_im': out['v_ssm_b_im'], 'v_ssm_c_re': out['v_ssm_c_re'], 'v_ssm_c_im': out['v_ssm_c_im'], 'v_ssm_d': out['v_ssm_d'], 'v_w_glu': out['v_w_glu'], 'v_b_glu': out['v_b_glu'], 'v_w_out': out['v_w_out'], 'v_ln_ffn_g': out['v_ln_ffn_g'], 'v_w_ffn_gate': out['v_w_ffn_gate'], 'v_w_ffn_up': out['v_w_ffn_up'], 'v_w_ffn_down': out['v_w_ffn_down'], 'v_ln_final_g': out['v_ln_final_g']}


def _loss(weights, diff, rest, loss_target):
    with _jax.named_scope("forward"):
        args = {**rest, TWIN_DIFF_INPUT: diff, **{k: w.astype(_WEIGHT_DTYPES[k]) for k, w in weights.items()}}
        y = _forward(args)
    with _jax.named_scope("loss_head"):
        err = _jnp.square(y.astype(_jnp.float32) - loss_target)
        return 0.5 * _jnp.sum(_jnp.mean(err, axis=-1)) if err.ndim else 0.5 * err


def _adamw(w, g, m, v):
    m = ADAM_B1 * m + (1.0 - ADAM_B1) * g
    v = ADAM_B2 * v + (1.0 - ADAM_B2) * _jnp.square(g)
    m_hat = m / (1.0 - ADAM_B1 ** ADAM_STEP)
    v_hat = v / (1.0 - ADAM_B2 ** ADAM_STEP)
    delta = -ADAM_LR * (m_hat / (_jnp.sqrt(v_hat) + ADAM_EPS) + ADAM_WD * w)
    return delta, m, v


def reference(x, ln_mix_g, w_in, ret_log_gamma, ssm_a_re, ssm_a_im, ssm_log_dt, ssm_b_re, ssm_b_im, ssm_c_re, ssm_c_im, ssm_d, w_glu, b_glu, w_out, ln_ffn_g, w_ffn_gate, w_ffn_up, w_ffn_down, ln_final_g, loss_target, m_ln_mix_g, m_w_in, m_ret_log_gamma, m_ssm_a_re, m_ssm_a_im, m_ssm_log_dt, m_ssm_b_re, m_ssm_b_im, m_ssm_c_re, m_ssm_c_im, m_ssm_d, m_w_glu, m_b_glu, m_w_out, m_ln_ffn_g, m_w_ffn_gate, m_w_ffn_up, m_w_ffn_down, m_ln_final_g, v_ln_mix_g, v_w_in, v_ret_log_gamma, v_ssm_a_re, v_ssm_a_im, v_ssm_log_dt, v_ssm_b_re, v_ssm_b_im, v_ssm_c_re, v_ssm_c_im, v_ssm_d, v_w_glu, v_b_glu, v_w_out, v_ln_ffn_g, v_w_ffn_gate, v_w_ffn_up, v_w_ffn_down, v_ln_final_g):
    given = dict(x=x, ln_mix_g=ln_mix_g, w_in=w_in, ret_log_gamma=ret_log_gamma, ssm_a_re=ssm_a_re, ssm_a_im=ssm_a_im, ssm_log_dt=ssm_log_dt, ssm_b_re=ssm_b_re, ssm_b_im=ssm_b_im, ssm_c_re=ssm_c_re, ssm_c_im=ssm_c_im, ssm_d=ssm_d, w_glu=w_glu, b_glu=b_glu, w_out=w_out, ln_ffn_g=ln_ffn_g, w_ffn_gate=w_ffn_gate, w_ffn_up=w_ffn_up, w_ffn_down=w_ffn_down, ln_final_g=ln_final_g, loss_target=loss_target, m_ln_mix_g=m_ln_mix_g, m_w_in=m_w_in, m_ret_log_gamma=m_ret_log_gamma, m_ssm_a_re=m_ssm_a_re, m_ssm_a_im=m_ssm_a_im, m_ssm_log_dt=m_ssm_log_dt, m_ssm_b_re=m_ssm_b_re, m_ssm_b_im=m_ssm_b_im, m_ssm_c_re=m_ssm_c_re, m_ssm_c_im=m_ssm_c_im, m_ssm_d=m_ssm_d, m_w_glu=m_w_glu, m_b_glu=m_b_glu, m_w_out=m_w_out, m_ln_ffn_g=m_ln_ffn_g, m_w_ffn_gate=m_w_ffn_gate, m_w_ffn_up=m_w_ffn_up, m_w_ffn_down=m_w_ffn_down, m_ln_final_g=m_ln_final_g, v_ln_mix_g=v_ln_mix_g, v_w_in=v_w_in, v_ret_log_gamma=v_ret_log_gamma, v_ssm_a_re=v_ssm_a_re, v_ssm_a_im=v_ssm_a_im, v_ssm_log_dt=v_ssm_log_dt, v_ssm_b_re=v_ssm_b_re, v_ssm_b_im=v_ssm_b_im, v_ssm_c_re=v_ssm_c_re, v_ssm_c_im=v_ssm_c_im, v_ssm_d=v_ssm_d, v_w_glu=v_w_glu, v_b_glu=v_b_glu, v_w_out=v_w_out, v_ln_ffn_g=v_ln_ffn_g, v_w_ffn_gate=v_w_ffn_gate, v_w_ffn_up=v_w_ffn_up, v_w_ffn_down=v_w_ffn_down, v_ln_final_g=v_ln_final_g)
    weights = {n: given[n] for n in TWIN_WEIGHTS}
    shared = {n: given[n] for n in SHARED_INPUTS}
    per_example = {n: given[n] for n in ['x']}
    grad_fn = _jax.value_and_grad(_loss, argnums=(0, 1))

    def one_microbatch(ex, loss_target):
        ex = dict(ex)
        diff = ex.pop(TWIN_DIFF_INPUT)
        return grad_fn(weights, diff, {**shared, **ex}, loss_target)

    if N_MICROBATCH == 1:
        loss, (grad_w, grad_x) = one_microbatch(per_example, given["loss_target"])
    else:
        def body(carry, xs):
            loss_sum, grad_sum = carry
            l_k, (gw_k, gx_k) = one_microbatch(xs[0], xs[1])
            with _jax.named_scope("update"):
                return (loss_sum + l_k, _jax.tree.map(_jnp.add, grad_sum, gw_k)), gx_k

        init = (_jnp.zeros((), _jnp.float32), _jax.tree.map(_jnp.zeros_like, weights))
        (loss, grad_w), grad_x = _jax.lax.scan(body, init, (per_example, given["loss_target"]))
    with _jax.named_scope("update"):
        delta_w, new_m, new_v = {}, {}, {}
        for n in TWIN_WEIGHTS:
            delta_w[n], new_m[n], new_v[n] = _adamw(weights[n], grad_w[n], given["m_" + n], given["v_" + n])
    return (loss, grad_x, *[grad_w[n] for n in TWIN_WEIGHTS], *[delta_w[n] for n in TWIN_WEIGHTS],
            *[new_m[n] for n in TWIN_WEIGHTS], *[new_v[n] for n in TWIN_WEIGHTS])
```

```python
import math

import jax
import jax.numpy as jnp
from jax import lax
from jax.experimental import pallas as pl
from jax.experimental.pallas import tpu as pltpu

F32 = jnp.float32
BF16 = jnp.bfloat16
MESH = pl.DeviceIdType.MESH

N_DEV = 8
RET_HEADS = 4
CHUNK = 128
ROPE_BASE = 10000.0
SSM_GROUP = 16
SSM_STATE = 64
TILE_GROUPS = 8
TILE_U = TILE_GROUPS * SSM_GROUP
TILE_N = TILE_GROUPS * SSM_STATE
LANE = 128
SUBLANE = 8
N_SEG = SUBLANE
N_LT = TILE_N // LANE
PAD = SUBLANE
FLAT_ROWS = 1024
EPS = 1e-6
ADAM_LR = 0.001
ADAM_B1 = 0.9
ADAM_B2 = 0.999
ADAM_EPS = 1e-08
ADAM_WD = 0.01
ADAM_STEP = 10
VMEM_LIMIT = 56 * 1024 * 1024


def _params(*sem):
    return pltpu.CompilerParams(dimension_semantics=sem or None, vmem_limit_bytes=VMEM_LIMIT)


def _dg(a, b, ca, cb):
    return lax.dot_general(a.astype(BF16), b.astype(BF16), (((ca,), (cb,)), ((), ())),
                           preferred_element_type=F32)


@jax.custom_vjp
def _dnn(a, b):
    return _dg(a, b, 1, 0)


@jax.custom_vjp
def _dnt(a, b):
    return _dg(a, b, 1, 1)


@jax.custom_vjp
def _dtn(a, b):
    return _dg(a, b, 0, 0)


_dnn.defvjp(lambda a, b: (_dnn(a, b), (a, b)), lambda r, g: (_dnt(g, r[1]), _dtn(r[0], g)))
_dnt.defvjp(lambda a, b: (_dnt(a, b), (a, b)), lambda r, g: (_dnn(g, r[1]), _dtn(g, r[0])))
_dtn.defvjp(lambda a, b: (_dtn(a, b), (a, b)), lambda r, g: (_dnt(r[1], g), _dnn(r[0], g)))


def _matmul(a, b, *, mode, out_dtype, name, res=None, tm=1024, tn=1024, tk=512):
    if mode == "nn":
        (m, k), n = a.shape, b.shape[1]
    elif mode == "nt":
        (m, k), n = a.shape, b.shape[0]
    else:
        (k, m), n = a.shape, b.shape[1]
    tm, tn, tk = min(tm, m), min(tn, n), min(tk, k)
    assert m % tm == 0 and n % tn == 0 and k % tk == 0, (name, m, n, k)
    nk = k // tk
    if mode == "tn":
        a_spec = pl.BlockSpec((tk, tm), lambda i, j, kk: (kk, i))
    else:
        a_spec = pl.BlockSpec((tm, tk), lambda i, j, kk: (i, kk))
    if mode == "nt":
        b_spec = pl.BlockSpec((tn, tk), lambda i, j, kk: (j, kk))
    else:
        b_spec = pl.BlockSpec((tk, tn), lambda i, j, kk: (kk, j))
    ca, cb = {"nn": (1, 0), "nt": (1, 1), "tn": (0, 0)}[mode]
    o_spec = pl.BlockSpec((tm, tn), lambda i, j, kk: (i, j))
    has_res = res is not None

    def body(*refs):
        if has_res:
            a_ref, b_ref, r_ref, o_ref, acc = refs
        else:
            a_ref, b_ref, o_ref, acc = refs
        kk = pl.program_id(2)

        @pl.when(kk == 0)
        def _():
            acc[...] = r_ref[...] if has_res else jnp.zeros_like(acc)

        acc[...] += _dg(a_ref[...], b_ref[...], ca, cb)

        @pl.when(kk == nk - 1)
        def _():
            o_ref[...] = acc[...].astype(out_dtype)

    return pl.pallas_call(
        body, name=name, grid=(m // tm, n // tn, nk),
        in_specs=[a_spec, b_spec] + ([o_spec] if has_res else []),
        out_specs=o_spec, out_shape=jax.ShapeDtypeStruct((m, n), out_dtype),
        scratch_shapes=[pltpu.VMEM((tm, tn), F32)],
        compiler_params=_params("parallel", "parallel", "arbitrary"),
    )(*((a, b, res) if has_res else (a, b)))


def _row(arr, tl, width=None, cb=0):
    width = arr.shape[1] if width is None else width
    return arr, pl.BlockSpec((tl, width), lambda i, cb=cb: (i, cb))


def _par(arr):
    return arr, pl.BlockSpec(arr.shape, lambda i: (0,) * arr.ndim)


def _rowwise(body, *, rows, tl, ins, outs, name):
    arrays = [a for a, _ in ins]
    in_specs = [s for _, s in ins]
    out_shape, out_specs, acc_ids = [], [], []
    for n, o in enumerate(outs):
        if o[0] == "row":
            out_shape.append(jax.ShapeDtypeStruct((rows, o[1]), o[2]))
            out_specs.append(pl.BlockSpec((tl, o[1]), lambda i: (i, 0)))
        else:
            out_shape.append(jax.ShapeDtypeStruct((1, o[1]), F32))
            out_specs.append(pl.BlockSpec((1, o[1]), lambda i: (0, 0)))
            acc_ids.append(n)
    n_in = len(arrays)
    assert rows % tl == 0, (name, rows, tl)

    def wrapped(*refs):
        in_refs, out_refs = refs[:n_in], refs[n_in:]

        @pl.when(pl.program_id(0) == 0)
        def _():
            for n in acc_ids:
                out_refs[n][...] = jnp.zeros_like(out_refs[n])

        body(in_refs, out_refs)

    return pl.pallas_call(
        wrapped, name=name, grid=(rows // tl,), in_specs=in_specs, out_specs=out_specs,
        out_shape=out_shape, compiler_params=_params("arbitrary"),
    )(*arrays)


def _rms(x, g):
    return x * lax.rsqrt(jnp.mean(x * x, axis=-1, keepdims=True) + EPS) * g


def _norm_fwd(x, g, *, name, tl=256):
    def body(i, o):
        o[0][...] = _rms(i[0][...], i[1][...]).astype(BF16)

    return _rowwise(body, rows=x.shape[0], tl=tl, ins=[_row(x, tl), _par(g)],
                    outs=[("row", x.shape[1], BF16)], name=name)[0]


def _norm_bwd(x, g, dh, dres, *, name, tl=256):
    def body(i, o):
        _, vjp = jax.vjp(_rms, i[0][...], i[1][...])
        dx, dg = vjp(i[2][...])
        o[0][...] = i[3][...] + dx
        o[1][...] += dg

    d = x.shape[1]
    return _rowwise(body, rows=x.shape[0], tl=tl, ins=[_row(x, tl), _par(g), _row(dh, tl), _row(dres, tl)],
                    outs=[("row", d, F32), ("acc", d)], name=name)


def _final(x, g, target, *, name, tl=256):
    d = x.shape[1]

    def body(i, o):
        y, vjp = jax.vjp(_rms, i[0][...], i[1][...])
        err = y - i[2][...]
        dx, dg = vjp(err * (1.0 / d))
        o[0][...] = dx
        o[1][...] += dg
        o[2][...] += jnp.full((1, LANE), 0.5 / d, F32) * jnp.sum(err * err)

    return _rowwise(body, rows=x.shape[0], tl=tl, ins=[_row(x, tl), _par(g), _row(target, tl)],
                    outs=[("row", d, F32), ("acc", d), ("acc", LANE)], name=name)


def _rot(x, cos, sin, out_ref, col, scale=1.0, inverse=False):
    x1, x2 = x[:, :LANE], x[:, LANE:]
    if inverse:
        sin = -sin
    out_ref[:, col:col + LANE] = ((x1 * cos - x2 * sin) * scale).astype(out_ref.dtype)
    out_ref[:, col + LANE:col + 2 * LANE] = ((x1 * sin + x2 * cos) * scale).astype(out_ref.dtype)


def _ret_prep(proj, cos, sin, *, qkw, d, name, tl=256):
    dk = qkw // RET_HEADS
    assert dk == 2 * LANE and (2 * qkw) % d == 0

    def body(i, o):
        c, s = i[3][...], i[4][...]
        for h in range(RET_HEADS):
            _rot(i[0][:, h * dk:(h + 1) * dk], c, s, o[0], h * dk)
            _rot(i[1][:, h * dk:(h + 1) * dk], c, s, o[1], h * dk, scale=dk ** -0.5)
        o[2][...] = i[2][...].astype(BF16)

    return _rowwise(body, rows=proj.shape[0], tl=tl,
                    ins=[_row(proj, tl, qkw, 0), _row(proj, tl, qkw, 1), _row(proj, tl, d, 2 * qkw // d),
                         _row(cos, tl), _row(sin, tl)],
                    outs=[("row", qkw, BF16), ("row", qkw, BF16), ("row", d, BF16)], name=name)


def _ret_prep_bwd(dq_rot, dk_rot, cos, sin, *, name, tl=256):
    qkw = dq_rot.shape[1]
    dk = qkw // RET_HEADS

    def body(i, o):
        c, s = i[2][...], i[3][...]
        for h in range(RET_HEADS):
            _rot(i[0][:, h * dk:(h + 1) * dk], c, s, o[0], h * dk, inverse=True)
            _rot(i[1][:, h * dk:(h + 1) * dk], c, s, o[1], h * dk, scale=dk ** -0.5, inverse=True)

    return _rowwise(body, rows=dq_rot.shape[0], tl=tl,
                    ins=[_row(dq_rot, tl), _row(dk_rot, tl), _row(cos, tl), _row(sin, tl)],
                    outs=[("row", qkw, BF16), ("row", qkw, BF16)], name=name)


def _ret_weights(lgf, lgb):
    t = lax.broadcasted_iota(jnp.int32, (CHUNK, 1), 0).astype(F32)
    diff = (lax.broadcasted_iota(jnp.int32, (CHUNK, CHUNK), 0)
            - lax.broadcasted_iota(jnp.int32, (CHUNK, CHUNK), 1)).astype(F32)
    dmat = jnp.exp(jnp.where(diff >= 0, lgf * diff, -lgb * diff))
    return dict(dmat=dmat, wqf=jnp.exp(lgf * (t + 1.0)), wkf=jnp.exp(lgf * (CHUNK - 1.0 - t)),
                wqb=jnp.exp(lgb * (CHUNK - t)), wkb=jnp.exp(lgb * t))


def _ret_f_part(q, k, v, lgf, lgb, s_f):
    w = _ret_weights(lgf, lgb)
    y = _dnn(_dnt(q, k) * w["dmat"], v) + _dnn(q * w["wqf"], s_f)
    return y, _dtn(k * w["wkf"], v)


def _ret_b_part(q, k, v, lgb, s_b):
    w = _ret_weights(lgb, lgb)
    return _dnn(q * w["wqb"], s_b), _dtn(k * w["wkb"], v)


def _chunk(c):
    return pl.ds(pl.multiple_of(c * CHUNK, CHUNK), CHUNK)


def _ret_specs(l, qkw, d):
    dk, dv = qkw // RET_HEADS, d // RET_HEADS
    return dk, dv, [pl.BlockSpec(memory_space=pltpu.SMEM),
                    pl.BlockSpec((l, dk), lambda h: (0, h)), pl.BlockSpec((l, dk), lambda h: (0, h)),
                    pl.BlockSpec((l, dv), lambda h: (0, h))]


def _ret_fwd(lg, q, k, v, *, name):
    l, qkw = q.shape
    d = v.shape[1]
    nc = l // CHUNK
    dk, dv, in_specs = _ret_specs(l, qkw, d)

    def body(lg_ref, q_ref, k_ref, v_ref, y_ref, s_ref):
        h = pl.program_id(0)
        lgf = jnp.full((1, 1), lg_ref[0, h], F32)
        lgb = jnp.full((1, 1), lg_ref[1, h], F32)
        dec_f, dec_b = jnp.exp(lgf * CHUNK), jnp.exp(lgb * CHUNK)

        def load(c):
            r = _chunk(c)
            return r, q_ref[r, :].astype(F32), k_ref[r, :].astype(F32), v_ref[r, :].astype(F32)

        s_ref[...] = jnp.zeros_like(s_ref)

        def f_step(c, _):
            r, qc, kc, vc = load(c)
            y, kv = _ret_f_part(qc, kc, vc, lgf, lgb, s_ref[...])
            y_ref[r, :] = y
            s_ref[...] = dec_f * s_ref[...] + kv
            return 0

        lax.fori_loop(0, nc, f_step, 0)
        s_ref[...] = jnp.zeros_like(s_ref)

        def b_step(n, _):
            r, qc, kc, vc = load(nc - 1 - n)
            y, kv = _ret_b_part(qc, kc, vc, lgb, s_ref[...])
            y_ref[r, :] += y
            s_ref[...] = dec_b * s_ref[...] + kv
            return 0

        lax.fori_loop(0, nc, b_step, 0)

    return pl.pallas_call(
        body, name=name, grid=(RET_HEADS,), in_specs=in_specs,
        out_specs=pl.BlockSpec((l, dv), lambda h: (0, h)), out_shape=jax.ShapeDtypeStruct((l, d), F32),
        scratch_shapes=[pltpu.VMEM((dk, dv), F32)], compiler_params=_params("arbitrary"),
    )(lg, q, k, v)


def _ret_bwd(lg, q, k, v, dy, *, name):
    l, qkw = q.shape
    d = v.shape[1]
    nc = l // CHUNK
    dk, dv, in_specs = _ret_specs(l, qkw, d)

    def body(lg_ref, q_ref, k_ref, v_ref, dy_ref, dq_ref, dk_ref, dv_ref, dlg_ref, states, s_ref, sh_ref):
        h = pl.program_id(0)
        lgf = jnp.full((1, 1), lg_ref[0, h], F32)
        lgb = jnp.full((1, 1), lg_ref[1, h], F32)
        dec_f, dec_b = jnp.exp(lgf * CHUNK), jnp.exp(lgb * CHUNK)

        def load(c):
            r = _chunk(c)
            return (r, q_ref[r, :].astype(F32), k_ref[r, :].astype(F32), v_ref[r, :].astype(F32),
                    dy_ref[r, :].astype(F32))

        s_ref[...] = jnp.zeros_like(s_ref)

        def f_states(c, _):
            _, qc, kc, vc, _ = load(c)
            states[c] = s_ref[...]
            w = _ret_weights(lgf, lgb)
            s_ref[...] = dec_f * s_ref[...] + _dtn(kc * w["wkf"], vc)
            return 0

        lax.fori_loop(0, nc, f_states, 0)
        sh_ref[...] = jnp.zeros_like(sh_ref)

        def f_adj(n, carry):
            dlf, dlb, ddec = carry
            c = nc - 1 - n
            r, qc, kc, vc, dyc = load(c)
            sc = states[c]
            _, vjp = jax.vjp(_ret_f_part, qc, kc, vc, lgf, lgb, sc)
            dq, dkk, dvv, g_f, g_b, dsc = vjp((dyc, sh_ref[...]))
            dq_ref[r, :] = dq
            dk_ref[r, :] = dkk
            dv_ref[r, :] = dvv
            ddec = ddec + jnp.sum(sh_ref[...] * sc)
            sh_ref[...] = dsc + dec_f * sh_ref[...]
            return dlf + g_f, dlb + g_b, ddec

        z = jnp.zeros((1, 1), F32)
        dlf, dlb, ddec_f = lax.fori_loop(0, nc, f_adj, (z, z, z))

        s_ref[...] = jnp.zeros_like(s_ref)

        def b_states(n, _):
            c = nc - 1 - n
            _, qc, kc, vc, _ = load(c)
            states[c] = s_ref[...]
            w = _ret_weights(lgb, lgb)
            s_ref[...] = dec_b * s_ref[...] + _dtn(kc * w["wkb"], vc)
            return 0

        lax.fori_loop(0, nc, b_states, 0)
        sh_ref[...] = jnp.zeros_like(sh_ref)

        def b_adj(c, carry):
            dlb, ddec = carry
            r, qc, kc, vc, dyc = load(c)
            sc = states[c]
            _, vjp = jax.vjp(_ret_b_part, qc, kc, vc, lgb, sc)
            dq, dkk, dvv, g_b, dsc = vjp((dyc, sh_ref[...]))
            dq_ref[r, :] += dq
            dk_ref[r, :] += dkk
            dv_ref[r, :] += dvv
            ddec = ddec + jnp.sum(sh_ref[...] * sc)
            sh_ref[...] = dsc + dec_b * sh_ref[...]
            return dlb + g_b, ddec

        dlb, ddec_b = lax.fori_loop(0, nc, b_adj, (dlb, z))
        dlf = dlf + ddec_f * dec_f * CHUNK
        dlb = dlb + ddec_b * dec_b * CHUNK
        row = lax.broadcasted_iota(jnp.int32, (SUBLANE, LANE), 0)
        dlg_ref[...] = jnp.where(row == 0, dlf, jnp.where(row == 1, dlb, 0.0))

    head = lambda w: pl.BlockSpec((l, w), lambda h: (0, h))
    return pl.pallas_call(
        body, name=name, grid=(RET_HEADS,), in_specs=in_specs + [head(dv)],
        out_specs=[head(dk), head(dk), head(dv), pl.BlockSpec((None, SUBLANE, LANE), lambda h: (h, 0, 0))],
        out_shape=[jax.ShapeDtypeStruct((l, qkw), F32), jax.ShapeDtypeStruct((l, qkw), F32),
                   jax.ShapeDtypeStruct((l, d), F32), jax.ShapeDtypeStruct((RET_HEADS, SUBLANE, LANE), F32)],
        scratch_shapes=[pltpu.VMEM((nc, dk, dv), F32), pltpu.VMEM((dk, dv), F32), pltpu.VMEM((dk, dv), F32)],
        compiler_params=_params("arbitrary"),
    )(lg, q, k, v, dy)


def _s5_param_fn(a_re, a_im, log_dt, b_re, b_im, rep):
    dt = jnp.exp(log_dt)
    mag = jnp.exp(a_re * dt)
    lam_re, lam_im = mag * jnp.cos(a_im * dt), mag * jnp.sin(a_im * dt)
    n_re, n_im = lam_re - 1.0, lam_im
    den = a_re * a_re + a_im * a_im
    c_re = (n_re * a_re + n_im * a_im) / den
    c_im = (n_im * a_re - n_re * a_im) / den
    hi = lax.Precision.HIGHEST
    c_re = jnp.dot(c_re, rep, precision=hi, preferred_element_type=F32)
    c_im = jnp.dot(c_im, rep, precision=hi, preferred_element_type=F32)
    return lam_re, lam_im, c_re * b_re - c_im * b_im, c_re * b_im + c_im * b_re


def _s5_param_shapes(a_re, b_re):
    r, p = a_re.shape
    return [jax.ShapeDtypeStruct((r, p), F32)] * 2 + [jax.ShapeDtypeStruct(b_re.shape, F32)] * 2


def _s5_prep(a_re, a_im, log_dt, b_re, b_im, rep, *, name):
    def body(*refs):
        outs = _s5_param_fn(*[r[...] for r in refs[:6]])
        for o_ref, o in zip(refs[6:], outs):
            o_ref[...] = o

    return pl.pallas_call(body, name=name, out_shape=_s5_param_shapes(a_re, b_re),
                          compiler_params=_params())(a_re, a_im, log_dt, b_re, b_im, rep)


def _s5_prep_bwd(a_re, a_im, log_dt, b_re, b_im, rep, cts, *, name):
    def body(*refs):
        ins = [r[...] for r in refs[:6]]
        _, vjp = jax.vjp(lambda *p: _s5_param_fn(*p, ins[5]), *ins[:5])
        grads = vjp(tuple(r[...] for r in refs[6:10]))
        for o_ref, o in zip(refs[10:], grads):
            o_ref[...] = o

    shapes = [jax.ShapeDtypeStruct(t.shape, F32) for t in (a_re, a_im, log_dt, b_re, b_im)]
    return pl.pallas_call(body, name=name, out_shape=shapes,
                          compiler_params=_params())(a_re, a_im, log_dt, b_re, b_im, rep, *cts)


def _eye_tiles():
    return jnp.eye(TILE_GROUPS, dtype=F32)


def _b_tiles(bbar, tiles):
    t = bbar.reshape(2, tiles, TILE_GROUPS, SSM_STATE, SSM_GROUP).transpose(0, 1, 2, 4, 3)
    t = t[:, :, :, :, None, :] * _eye_tiles()[None, None, :, None, :, None]
    return t.reshape(2, tiles, TILE_U, TILE_N)


def _b_untile(dbt, tiles):
    t = dbt.reshape(2, tiles, TILE_GROUPS, SSM_GROUP, TILE_GROUPS, SSM_STATE)
    t = (t * _eye_tiles()[None, None, :, None, :, None]).sum(axis=4)
    return t.transpose(0, 1, 2, 4, 3).reshape(2 * tiles * TILE_GROUPS, SSM_STATE * SSM_GROUP)


def _c_tiles(c, tiles):
    t = c.reshape(2, tiles, TILE_GROUPS, SSM_GROUP, SSM_STATE).transpose(0, 1, 2, 4, 3)
    t = t[:, :, :, :, None, :] * _eye_tiles()[None, None, :, None, :, None]
    return t.reshape(2, tiles, TILE_N, TILE_U)


def _c_untile(dct, tiles):
    t = dct.reshape(2, tiles, TILE_GROUPS, SSM_STATE, TILE_GROUPS, SSM_GROUP)
    t = (t * _eye_tiles()[None, None, :, None, :, None]).sum(axis=4)
    return t.transpose(0, 1, 2, 4, 3).reshape(2, tiles * TILE_GROUPS, SSM_GROUP, SSM_STATE)


def _s5_scan(xr, xi, a_re, a_im, *, length, off, reverse, shifted=None):
    ls = length // N_SEG
    assert ls * N_SEG == length and ls & (ls - 1) == 0
    ar = [jnp.broadcast_to(a_re[:, c * LANE:(c + 1) * LANE], (N_SEG, LANE)) for c in range(N_LT)]
    ai = [jnp.broadcast_to(a_im[:, c * LANE:(c + 1) * LANE], (N_SEG, LANE)) for c in range(N_LT)]
    zero = jnp.zeros((N_SEG, LANE), F32)

    def rows_at(n, base):
        j = (ls - 1 - n) if reverse else n
        return pl.ds(base + j, N_SEG, stride=ls)

    def local(n, carry):
        rows = rows_at(n, off)
        new = []
        for c in range(N_LT):
            cr, ci = carry[2 * c], carry[2 * c + 1]
            nr = ar[c] * cr - ai[c] * ci + xr[c, rows, :]
            ni = ar[c] * ci + ai[c] * cr + xi[c, rows, :]
            xr[c, rows, :] = nr
            xi[c, rows, :] = ni
            new += [nr, ni]
        return tuple(new)

    ends = lax.fori_loop(0, ls, local, (zero,) * (2 * N_LT))

    row = lax.broadcasted_iota(jnp.int32, (N_SEG, LANE), 0)
    init = []
    for c in range(N_LT):
        pr, pi = ar[c][0:1, :], ai[c][0:1, :]
        for _ in range(ls.bit_length() - 1):
            pr, pi = pr * pr - pi * pi, 2.0 * pr * pi
        cr = ci = jnp.zeros((1, LANE), F32)
        ir, ii = zero, zero
        for s in (range(N_SEG - 1, -1, -1) if reverse else range(N_SEG)):
            ir = jnp.where(row == s, cr, ir)
            ii = jnp.where(row == s, ci, ii)
            er, ei = ends[2 * c][s:s + 1, :], ends[2 * c + 1][s:s + 1, :]
            cr, ci = pr * cr - pi * ci + er, pr * ci + pi * cr + ei
        init += [ir, ii]

    def fix(n, carry):
        rows = rows_at(n, off)
        new, sums = [], []
        for c in range(N_LT):
            cr, ci = carry[2 * c], carry[2 * c + 1]
            nr = ar[c] * cr - ai[c] * ci
            ni = ar[c] * ci + ai[c] * cr
            fr = xr[c, rows, :] + nr
            fi = xi[c, rows, :] + ni
            xr[c, rows, :] = fr
            xi[c, rows, :] = fi
            new += [nr, ni]
            if shifted is not None:
                yr, yi, yoff, shift = shifted
                srows = rows_at(n, yoff + shift)
                sr, si = yr[c, srows, :], yi[c, srows, :]
                sums += [carry[2 * N_LT + 2 * c] + fr * sr + fi * si,
                         carry[2 * N_LT + 2 * c + 1] + fi * sr - fr * si]
        return tuple(new + sums)

    start = tuple(init) + ((zero,) * (2 * N_LT) if shifted is not None else ())
    out = lax.fori_loop(0, ls, fix, start)
    return out[2 * N_LT:]


def _s5_tile_specs(l, d):
    tile = lambda r, c: pl.BlockSpec((None, None, r, c), lambda t, d=d: (d, t, 0, 0))
    return [pl.BlockSpec((l, TILE_U), lambda t: (0, t)), tile(TILE_U, TILE_N), tile(TILE_U, TILE_N),
            tile(1, TILE_N), tile(1, TILE_N), tile(TILE_N, TILE_U), tile(TILE_N, TILE_U)]


def _lanes(c):
    return slice(c * LANE, (c + 1) * LANE)


def _s5_fwd(u, u_cb, bt_re, bt_im, lam_re, lam_im, ct_re, ct_im, *, d, name):
    l = u.shape[0]
    tiles = bt_re.shape[1]
    specs = _s5_tile_specs(l, d)
    specs[0] = pl.BlockSpec((l, TILE_U), lambda t: (0, u_cb * tiles + t))

    def body(u_ref, bre, bim, lre, lim, cre, cim, y_ref, xr, xi):
        uu = u_ref[...]
        bu_re, bu_im = _dg(uu, bre[...], 1, 0), _dg(uu, bim[...], 1, 0)
        for c in range(N_LT):
            xr[c] = bu_re[:, _lanes(c)]
            xi[c] = bu_im[:, _lanes(c)]
        _s5_scan(xr, xi, lre[...], lim[...], length=l, off=0, reverse=(d == 1))
        y = jnp.zeros((l, TILE_U), F32)
        for c in range(N_LT):
            y = y + _dg(xr[c], cre[_lanes(c), :], 1, 0) - _dg(xi[c], cim[_lanes(c), :], 1, 0)
        y_ref[...] = y

    return pl.pallas_call(
        body, name=name, grid=(tiles,), in_specs=specs, out_specs=pl.BlockSpec((l, TILE_U), lambda t: (0, t)),
        out_shape=jax.ShapeDtypeStruct((l, tiles * TILE_U), F32),
        scratch_shapes=[pltpu.VMEM((N_LT, l, LANE), F32)] * 2, compiler_params=_params("arbitrary"),
    )(u, bt_re, bt_im, lam_re, lam_im, ct_re, ct_im)


def _s5_bwd(u, u_cb, dy, bt_re, bt_im, lam_re, lam_im, ct_re, ct_im, *, d, name):
    l = u.shape[0]
    tiles = bt_re.shape[1]
    specs = _s5_tile_specs(l, d)
    specs[0] = pl.BlockSpec((l, TILE_U), lambda t: (0, u_cb * tiles + t))
    specs.insert(1, pl.BlockSpec((l, TILE_U), lambda t: (0, t)))
    reverse = d == 1

    def body(u_ref, dy_ref, bre, bim, lre, lim, cre, cim, du_ref, dbre, dbim, dcre, dcim, dlre, dlim,
             xr, xi, gr, gi):
        uu, dyy = u_ref[...], dy_ref[...]
        bu_re, bu_im = _dg(uu, bre[...], 1, 0), _dg(uu, bim[...], 1, 0)
        edge = jnp.zeros((PAD, LANE), F32)
        for c in range(N_LT):
            for ref, val in ((xr, bu_re), (xi, bu_im)):
                ref[c, 0:PAD, :] = edge
                ref[c, PAD + l:PAD + l + PAD, :] = edge
                ref[c, PAD:PAD + l, :] = val[:, _lanes(c)]
        _s5_scan(xr, xi, lre[...], lim[...], length=l, off=PAD, reverse=reverse)
        gy_re, gy_im = _dg(dyy, cre[...], 1, 1), -_dg(dyy, cim[...], 1, 1)
        for c in range(N_LT):
            gr[c] = gy_re[:, _lanes(c)]
            gi[c] = gy_im[:, _lanes(c)]
        sums = _s5_scan(gr, gi, lre[...], -lim[...], length=l, off=0, reverse=not reverse,
                        shifted=(xr, xi, PAD, 1 if reverse else -1))
        du = jnp.zeros((l, TILE_U), F32)
        for c in range(N_LT):
            dlre[:, _lanes(c)] = jnp.sum(sums[2 * c], axis=0, keepdims=True)
            dlim[:, _lanes(c)] = jnp.sum(sums[2 * c + 1], axis=0, keepdims=True)
            g_re, g_im = gr[c], gi[c]
            du = du + _dg(g_re, bre[:, _lanes(c)], 1, 1) + _dg(g_im, bim[:, _lanes(c)], 1, 1)
            dbre[:, _lanes(c)] = _dg(uu, g_re, 0, 0)
            dbim[:, _lanes(c)] = _dg(uu, g_im, 0, 0)
            dcre[_lanes(c), :] = _dg(xr[c, PAD:PAD + l, :], dyy, 0, 0)
            dcim[_lanes(c), :] = -_dg(xi[c, PAD:PAD + l, :], dyy, 0, 0)
        du_ref[...] = du

    out3 = lambda r, c: pl.BlockSpec((None, r, c), lambda t: (t, 0, 0))
    f = lambda *s: jax.ShapeDtypeStruct(s, F32)
    return pl.pallas_call(
        body, name=name, grid=(tiles,), in_specs=specs,
        out_specs=[pl.BlockSpec((l, TILE_U), lambda t: (0, t)), out3(TILE_U, TILE_N), out3(TILE_U, TILE_N),
                   out3(TILE_N, TILE_U), out3(TILE_N, TILE_U), out3(1, TILE_N), out3(1, TILE_N)],
        out_shape=[f(l, tiles * TILE_U), f(tiles, TILE_U, TILE_N), f(tiles, TILE_U, TILE_N),
                   f(tiles, TILE_N, TILE_U), f(tiles, TILE_N, TILE_U), f(tiles, 1, TILE_N), f(tiles, 1, TILE_N)],
        scratch_shapes=[pltpu.VMEM((N_LT, l + 2 * PAD, LANE), F32)] * 2 + [pltpu.VMEM((N_LT, l, LANE), F32)] * 2,
        compiler_params=_params("arbitrary"),
    )(u, dy, bt_re, bt_im, lam_re, lam_im, ct_re, ct_im)


def _s5_post_fn(yf, yb, u, dskip):
    return jax.nn.gelu(yf + yb + dskip * u)


def _s5_post(yf, yb, proj, u_cb, dskip, *, name, tl=256):
    d = yf.shape[1]

    def body(i, o):
        ys = _s5_post_fn(i[0][...], i[1][...], i[2][...], i[3][...])
        o[0][...] = ys
        o[1][...] = ys.astype(BF16)

    return _rowwise(body, rows=yf.shape[0], tl=tl, ins=[_row(yf, tl), _row(yb, tl), _row(proj, tl, d, u_cb), _par(dskip)],
                    outs=[("row", d, F32), ("row", d, BF16)], name=name)


def _s5_post_bwd(yf, yb, proj, u_cb, dskip, dys, *, name, tl=256):
    d = yf.shape[1]

    def body(i, o):
        yf_, yb_, u_, ds_ = i[0][...], i[1][...], i[2][...], i[3][...]
        _, vjp = jax.vjp(lambda s: jax.nn.gelu(s), yf_ + yb_ + ds_ * u_)
        (dpre,) = vjp(i[4][...])
        o[0][...] = dpre
        o[1][...] += jnp.sum(dpre * u_, axis=0, keepdims=True)

    return _rowwise(body, rows=yf.shape[0], tl=tl,
                    ins=[_row(yf, tl), _row(yb, tl), _row(proj, tl, d, u_cb), _par(dskip), _row(dys, tl)],
                    outs=[("row", d, F32), ("acc", d)], name=name)


def _du_combine(dpre, dskip, du_f, du_b, *, name, tl=256):
    d = dpre.shape[1]

    def body(i, o):
        o[0][...] = (i[0][...] * i[1][...] + i[2][...] + i[3][...]).astype(BF16)

    return _rowwise(body, rows=dpre.shape[0], tl=tl, ins=[_row(dpre, tl), _par(dskip), _row(du_f, tl), _row(du_b, tl)],
                    outs=[("row", d, BF16)], name=name)[0]


def _merge_fn(y, g, gate_r, gate_s, ys, glu, b):
    ret = jax.nn.silu(g) * (y * lax.rsqrt(jnp.mean(y * y, axis=-1, keepdims=True) + EPS))
    ssm = ys * jax.nn.sigmoid(glu + b)
    return jax.nn.sigmoid(gate_r) * ret + jax.nn.sigmoid(gate_s) * ssm


def _merge_ins(y_raw, proj, ys, glu, b_glu, cb0, tl):
    d = y_raw.shape[1]
    return [_row(y_raw, tl), _row(proj, tl, d, cb0 + 1), _row(proj, tl, d, cb0 + 3), _row(proj, tl, d, cb0 + 4),
            _row(ys, tl), _row(glu, tl), _par(b_glu)]


def _merge(y_raw, proj, ys, glu, b_glu, *, cb0, name, tl=128):
    d = y_raw.shape[1]
    dv = d // RET_HEADS

    def body(i, o):
        for h in range(RET_HEADS):
            cs = slice(h * dv, (h + 1) * dv)
            o[0][:, cs] = _merge_fn(*[r[:, cs] for r in i]).astype(BF16)

    return _rowwise(body, rows=y_raw.shape[0], tl=tl, ins=_merge_ins(y_raw, proj, ys, glu, b_glu, cb0, tl),
                    outs=[("row", d, BF16)], name=name)[0]


def _merge_bwd(y_raw, proj, ys, glu, b_glu, dmerged, *, cb0, name, tl=128):
    d = y_raw.shape[1]
    dv = d // RET_HEADS

    def body(i, o):
        for h in range(RET_HEADS):
            cs = slice(h * dv, (h + 1) * dv)
            _, vjp = jax.vjp(_merge_fn, *[r[:, cs] for r in i[:7]])
            dy, dg, dgr, dgs, dys, dglu, db = vjp(i[7][:, cs])
            o[0][:, cs] = dy.astype(BF16)
            o[1][:, cs] = dg.astype(BF16)
            o[2][:, cs] = dgr.astype(BF16)
            o[3][:, cs] = dgs.astype(BF16)
            o[4][:, cs] = dglu.astype(BF16)
            o[5][:, cs] = dys
            o[6][:, cs] += db

    return _rowwise(body, rows=y_raw.shape[0], tl=tl,
                    ins=_merge_ins(y_raw, proj, ys, glu, b_glu, cb0, tl) + [_row(dmerged, tl)],
                    outs=[("row", d, BF16)] * 5 + [("row", d, F32), ("acc", d)], name=name)


def _ffn_act_fn(gate, up):
    return jax.nn.silu(gate) * up


def _ffn_act(gate, up, *, name, tl=128):
    def body(i, o):
        o[0][...] = _ffn_act_fn(i[0][...], i[1][...]).astype(BF16)

    return _rowwise(body, rows=gate.shape[0], tl=tl, ins=[_row(gate, tl), _row(up, tl)],
                    outs=[("row", gate.shape[1], BF16)], name=name)[0]


def _ffn_act_bwd(gate, up, dact, *, name, tl=128):
    def body(i, o):
        _, vjp = jax.vjp(_ffn_act_fn, i[0][...], i[1][...])
        dgate, dup = vjp(i[2][...])
        o[0][...] = dgate.astype(BF16)
        o[1][...] = dup.astype(BF16)

    w = gate.shape[1]
    return _rowwise(body, rows=gate.shape[0], tl=tl, ins=[_row(gate, tl), _row(up, tl), _row(dact, tl)],
                    outs=[("row", w, BF16), ("row", w, BF16)], name=name)


def _adamw(w, g, m, v):
    m = ADAM_B1 * m + (1.0 - ADAM_B1) * g
    v = ADAM_B2 * v + (1.0 - ADAM_B2) * (g * g)
    m_hat = m / (1.0 - ADAM_B1 ** ADAM_STEP)
    v_hat = v / (1.0 - ADAM_B2 ** ADAM_STEP)
    return -ADAM_LR * (m_hat / (jnp.sqrt(v_hat) + ADAM_EPS) + ADAM_WD * w), m, v


def _adam_flat(g, w, m, v, *, name, tr=FLAT_ROWS):
    def body(i, o):
        for o_ref, val in zip(o, _adamw(i[1][...], i[0][...], i[2][...], i[3][...])):
            o_ref[...] = val

    return _rowwise(body, rows=g.shape[0], tl=tr, ins=[_row(a, tr) for a in (g, w, m, v)],
                    outs=[("row", LANE, F32)] * 3, name=name)


def _adam_shard(recv, w, m, v, *, name, tr):
    depth, r, c = w.shape
    assert r % tr == 0 and recv.shape[3] == c
    blk = pl.BlockSpec((None, tr, c), lambda l, i: (l, i, 0))

    def body(recv_ref, w_ref, m_ref, v_ref, g_ref, d_ref, nm_ref, nv_ref):
        g = recv_ref[0].astype(F32)
        for p in range(1, N_DEV):
            g = g + recv_ref[p].astype(F32)
        g_ref[...] = g
        d_ref[...], nm_ref[...], nv_ref[...] = _adamw(w_ref[...], g, m_ref[...], v_ref[...])

    return pl.pallas_call(
        body, name=name, grid=(depth, r // tr),
        in_specs=[pl.BlockSpec((None, N_DEV, tr, c), lambda l, i: (l, 0, i, 0)), blk, blk, blk],
        out_specs=[blk] * 4, out_shape=[jax.ShapeDtypeStruct(w.shape, F32)] * 4,
        compiler_params=_params("parallel", "parallel"),
    )(recv, w, m, v)


def _position():
    x, y, c = lax.axis_index("x"), lax.axis_index("y"), lax.axis_index("c")
    return x, y, c, 4 * x + 2 * y + c


def _coords(p):
    return p // 4, (p // 2) % 2, p % 2


def _block_of(ref, kind, p, nb):
    if kind == "col":
        return ref.at[:, pl.ds(pl.multiple_of(p * nb, LANE), nb)]
    return ref.at[pl.ds(pl.multiple_of(p * nb, SUBLANE), nb), :]


def _all_gather(shards, kinds, *, name):
    n = len(shards)
    out_shape = []
    for s, kind in zip(shards, kinds):
        r, c = s.shape
        out_shape.append(jax.ShapeDtypeStruct((r, c * N_DEV) if kind == "col" else (r * N_DEV, c), s.dtype))

    def body(*refs):
        shard_refs, full_refs = refs[:n], refs[n:2 * n]
        send_sems, recv_sems, local_sems = refs[2 * n:]
        x, y, c, me = _position()
        sibling = (x, y, 1 - c)
        chips = [(1 - x, y), (x, 1 - y), (1 - x, 1 - y)]

        def block(t, dev):
            nb = shards[t].shape[1 if kinds[t] == "col" else 0]
            return _block_of(full_refs[t], kinds[t], 4 * dev[0] + 2 * dev[1] + dev[2], nb)

        def copy(t, k, dev, to, src=None):
            return pltpu.make_async_remote_copy(
                src_ref=block(t, dev) if src is None else src, dst_ref=block(t, dev),
                send_sem=send_sems.at[t, k], recv_sem=recv_sems.at[t, k], device_id=to, device_id_type=MESH)

        mine, first, passed = [], [], []
        for t in range(n):
            mine.append(pltpu.make_async_copy(shard_refs[t], block(t, (x, y, c)), local_sems.at[t]))
            mine[-1].start()
            first.append(copy(t, 0, (x, y, c), sibling, src=shard_refs[t]))
            first += [copy(t, 1 + j, (x, y, c), (*chip, c), src=shard_refs[t]) for j, chip in enumerate(chips)]
        for cp in first:
            cp.start()
        for j, chip in enumerate(chips):
            for t in range(n):
                copy(t, 1 + j, (*chip, c), (x, y, c)).wait_recv()
                passed.append(copy(t, 4 + j, (*chip, c), sibling))
                passed[-1].start()
        for t in range(n):
            copy(t, 0, sibling, (x, y, c)).wait_recv()
            for j, chip in enumerate(chips):
                copy(t, 4 + j, (*chip, 1 - c), (x, y, c)).wait_recv()
        for cp in first + passed:
            cp.wait_send()
        for cp in mine:
            cp.wait()

    any_spec = pl.BlockSpec(memory_space=pl.ANY)
    return pl.pallas_call(
        body, name=name, in_specs=[any_spec] * n, out_specs=[any_spec] * n, out_shape=out_shape,
        scratch_shapes=[pltpu.SemaphoreType.DMA((n, N_DEV - 1)), pltpu.SemaphoreType.DMA((n, N_DEV - 1)),
                        pltpu.SemaphoreType.DMA((n,))],
        compiler_params=pltpu.CompilerParams(has_side_effects=True),
    )(*shards)


def _scatter_partials(grads, kinds, *, name):
    n, depth = len(grads), len(grads[0])
    blocks, out_shape = [], []
    for g, kind in zip(grads, kinds):
        r, c = g[0].shape
        blocks.append((r, c // N_DEV) if kind == "col" else (r // N_DEV, c))
        out_shape.append(jax.ShapeDtypeStruct((depth, N_DEV) + blocks[-1], g[0].dtype))

    def body(*refs):
        g_refs = [refs[t * depth:(t + 1) * depth] for t in range(n)]
        recv_refs = refs[n * depth:n * depth + n]
        send_sems, recv_sems, local_sems = refs[n * depth + n:]
        _, _, _, me = _position()

        def copy(t, l, k):
            p = (me + k) % N_DEV
            nb = blocks[t][1 if kinds[t] == "col" else 0]
            return pltpu.make_async_remote_copy(
                src_ref=_block_of(g_refs[t][l], kinds[t], p, nb), dst_ref=recv_refs[t].at[l, me],
                send_sem=send_sems.at[t, l, k], recv_sem=recv_sems.at[t, l, k], device_id=_coords(p),
                device_id_type=MESH)

        def arrival(t, l, k):
            q = (me + N_DEV - k) % N_DEV
            nb = blocks[t][1 if kinds[t] == "col" else 0]
            return pltpu.make_async_remote_copy(
                src_ref=_block_of(g_refs[t][l], kinds[t], me, nb), dst_ref=recv_refs[t].at[l, q],
                send_sem=send_sems.at[t, l, k], recv_sem=recv_sems.at[t, l, k], device_id=_coords(q),
                device_id_type=MESH)

        mine, sent = [], []
        for t in range(n):
            nb = blocks[t][1 if kinds[t] == "col" else 0]
            for l in range(depth):
                mine.append(pltpu.make_async_copy(_block_of(g_refs[t][l], kinds[t], me, nb), recv_refs[t].at[l, me],
                                                  local_sems.at[t, l]))
                mine[-1].start()
                for k in range(1, N_DEV):
                    sent.append(copy(t, l, k))
                    sent[-1].start()
        for t in range(n):
            for l in range(depth):
                for k in range(1, N_DEV):
                    arrival(t, l, k).wait_recv()
        for cp in sent:
            cp.wait_send()
        for cp in mine:
            cp.wait()

    any_spec = pl.BlockSpec(memory_space=pl.ANY)
    return pl.pallas_call(
        body, name=name, in_specs=[any_spec] * (n * depth), out_specs=[any_spec] * n, out_shape=out_shape,
        scratch_shapes=[pltpu.SemaphoreType.DMA((n, depth, N_DEV)), pltpu.SemaphoreType.DMA((n, depth, N_DEV)),
                        pltpu.SemaphoreType.DMA((n, depth))],
        compiler_params=pltpu.CompilerParams(has_side_effects=True),
    )(*[g for per in grads for g in per])


def _all_reduce(part, *, name):
    _, r, _ = part.shape

    def body(part_ref, tot_ref, recv_ref, send1, recv1, send2, recv2):
        _, _, _, me = _position()

        def scatter(k, to_me=False):
            p = (me + N_DEV - k) % N_DEV if to_me else (me + k) % N_DEV
            return pltpu.make_async_remote_copy(
                src_ref=part_ref.at[me if to_me else p], dst_ref=recv_ref.at[p if to_me else me],
                send_sem=send1.at[k], recv_sem=recv1.at[k], device_id=_coords(p), device_id_type=MESH)

        def gather(k, to_me=False):
            p = (me + N_DEV - k) % N_DEV if to_me else (me + k) % N_DEV
            return pltpu.make_async_remote_copy(
                src_ref=tot_ref.at[me], dst_ref=tot_ref.at[p if to_me else me],
                send_sem=send2.at[k], recv_sem=recv2.at[k], device_id=_coords(p), device_id_type=MESH)

        for k in range(1, N_DEV):
            scatter(k).start()
        recv_ref[me] = part_ref[me]
        for k in range(1, N_DEV):
            scatter(k, to_me=True).wait_recv()
        total = recv_ref[0]
        for q in range(1, N_DEV):
            total = total + recv_ref[q]
        tot_ref[me] = total
        for k in range(1, N_DEV):
            gather(k).start()
        for k in range(1, N_DEV):
            gather(k, to_me=True).wait_recv()
        for k in range(1, N_DEV):
            scatter(k).wait_send()
            gather(k).wait_send()

    vmem = pl.BlockSpec(memory_space=pltpu.VMEM)
    return pl.pallas_call(
        body, name=name, in_specs=[vmem], out_specs=vmem, out_shape=jax.ShapeDtypeStruct(part.shape, F32),
        scratch_shapes=[pltpu.VMEM(part.shape, F32)] + [pltpu.SemaphoreType.DMA((N_DEV,))] * 4,
        compiler_params=pltpu.CompilerParams(has_side_effects=True, vmem_limit_bytes=VMEM_LIMIT),
    )(part)


def _round_up(n, m):
    return (n + m - 1) // m * m


def _row_tile(rows):
    return next(t for t in (256, 128, 64, 32, 16) if rows % t == 0)


def _local_step(x, target, small, full, *, qkw):
    l, d = x.shape
    depth = len(full)
    groups = d // SSM_GROUP
    tiles = groups // TILE_GROUPS
    half = qkw // RET_HEADS // 2
    cb0 = 2 * qkw // d
    inv = 1.0 / (ROPE_BASE ** (jnp.arange(half, dtype=F32) / half))
    ang = jnp.arange(l, dtype=F32)[:, None] * inv[None, :]
    cos, sin = jnp.cos(ang), jnp.sin(ang)
    rep = jnp.repeat(jnp.eye(SSM_STATE, dtype=F32), SSM_GROUP, axis=1)
    row2 = lambda a: a.reshape(1, -1)

    saved = []
    for i in range(depth):
        w_in, w_glu, w_out, w_gate, w_up, w_down = full[i]
        n = f"l{i}_"
        g_mix, g_ffn = row2(small["ln_mix_g"][i]), row2(small["ln_ffn_g"][i])
        dskip, b_glu = row2(small["ssm_d"][i]), row2(small["b_glu"][i])
        lg = small["ret_log_gamma"][i]
        h = _norm_fwd(x, g_mix, name=n + "norm_mix")
        proj = _matmul(h, w_in, mode="nn", out_dtype=F32, name=n + "proj")
        q_rot, k_rot, v_bf = _ret_prep(proj, cos, sin, qkw=qkw, d=d, name=n + "ret_prep")
        y_raw = _ret_fwd(lg, q_rot, k_rot, v_bf, name=n + "ret_fwd")
        par = [small["ssm_a_re"][i].reshape(2 * groups, SSM_STATE), small["ssm_a_im"][i].reshape(2 * groups, SSM_STATE),
               small["ssm_log_dt"][i].reshape(2 * groups, 1),
               small["ssm_b_re"][i].reshape(2 * groups, SSM_STATE * SSM_GROUP),
               small["ssm_b_im"][i].reshape(2 * groups, SSM_STATE * SSM_GROUP), rep]
        lam_re, lam_im, bbar_re, bbar_im = _s5_prep(*par, name=n + "s5_prep")
        s5 = [_b_tiles(bbar_re, tiles).astype(BF16), _b_tiles(bbar_im, tiles).astype(BF16),
              lam_re.reshape(2, tiles, 1, TILE_N), lam_im.reshape(2, tiles, 1, TILE_N),
              _c_tiles(small["ssm_c_re"][i], tiles).astype(BF16), _c_tiles(small["ssm_c_im"][i], tiles).astype(BF16)]
        y_f = _s5_fwd(proj, cb0 + 2, *s5, d=0, name=n + "s5_fwd_f")
        y_b = _s5_fwd(proj, cb0 + 2, *s5, d=1, name=n + "s5_fwd_b")
        ys, ys_bf = _s5_post(y_f, y_b, proj, cb0 + 2, dskip, name=n + "s5_post")
        glu = _matmul(ys_bf, w_glu, mode="nn", out_dtype=F32, name=n + "glu")
        merged = _merge(y_raw, proj, ys, glu, b_glu, cb0=cb0, name=n + "merge")
        x1 = _matmul(merged, w_out, mode="nn", out_dtype=F32, res=x, name=n + "out")
        h2 = _norm_fwd(x1, g_ffn, name=n + "norm_ffn")
        gate = _matmul(h2, w_gate, mode="nn", out_dtype=F32, name=n + "gate")
        up = _matmul(h2, w_up, mode="nn", out_dtype=F32, name=n + "up")
        act = _ffn_act(gate, up, name=n + "act")
        x2 = _matmul(act, w_down, mode="nn", out_dtype=F32, res=x1, name=n + "down")
        saved.append(dict(x=x, h=h, proj=proj, q_rot=q_rot, k_rot=k_rot, v_bf=v_bf, y_raw=y_raw, par=par, s5=s5,
                          y_f=y_f, y_b=y_b, ys=ys, ys_bf=ys_bf, glu=glu, merged=merged, x1=x1, h2=h2, gate=gate,
                          up=up, act=act))
        x = x2

    dx, dg_final, loss = _final(x, row2(small["ln_final_g"]), target, name="final")

    big = [[None] * depth for _ in range(6)]
    sg = {k: [None] * depth for k in ("ln_mix_g", "ret_log_gamma", "ssm_a_re", "ssm_a_im", "ssm_log_dt", "ssm_b_re",
                                      "ssm_b_im", "ssm_c_re", "ssm_c_im", "ssm_d", "b_glu", "ln_ffn_g")}
    for i in reversed(range(depth)):
        s = saved[i]
        w_in, w_glu, w_out, w_gate, w_up, w_down = full[i]
        n = f"l{i}_b_"
        g_mix, g_ffn = row2(small["ln_mix_g"][i]), row2(small["ln_ffn_g"][i])
        dskip, b_glu = row2(small["ssm_d"][i]), row2(small["b_glu"][i])
        lg = small["ret_log_gamma"][i]
        dact = _matmul(dx, w_down, mode="nt", out_dtype=F32, name=n + "dact")
        big[5][i] = _matmul(s["act"], dx, mode="tn", out_dtype=BF16, name=n + "dw_down")
        dgate, dup = _ffn_act_bwd(s["gate"], s["up"], dact, name=n + "act")
        dh2 = _matmul(dgate, w_gate, mode="nt", out_dtype=F32, name=n + "dh2_gate")
        dh2 = _matmul(dup, w_up, mode="nt", out_dtype=F32, res=dh2, name=n + "dh2_up")
        big[3][i] = _matmul(s["h2"], dgate, mode="tn", out_dtype=BF16, name=n + "dw_gate")
        big[4][i] = _matmul(s["h2"], dup, mode="tn", out_dtype=BF16, name=n + "dw_up")
        dx1, dgf = _norm_bwd(s["x1"], g_ffn, dh2, dx, name=n + "norm_ffn")
        sg["ln_ffn_g"][i] = dgf[0]
        dmerged = _matmul(dx1, w_out, mode="nt", out_dtype=F32, name=n + "dmerged")
        big[2][i] = _matmul(s["merged"], dx1, mode="tn", out_dtype=BF16, name=n + "dw_out")
        dy_raw, dg, dgate_r, dgate_s, dglu, dys_a, db_glu = _merge_bwd(
            s["y_raw"], s["proj"], s["ys"], s["glu"], b_glu, dmerged, cb0=cb0, name=n + "merge")
        sg["b_glu"][i] = db_glu[0]
        dys = _matmul(dglu, w_glu, mode="nt", out_dtype=F32, res=dys_a, name=n + "dys")
        big[1][i] = _matmul(s["ys_bf"], dglu, mode="tn", out_dtype=BF16, name=n + "dw_glu")
        dpre, dd = _s5_post_bwd(s["y_f"], s["y_b"], s["proj"], cb0 + 2, dskip, dys, name=n + "s5_post")
        sg["ssm_d"][i] = dd[0]
        r_f = _s5_bwd(s["proj"], cb0 + 2, dpre, *s["s5"], d=0, name=n + "s5_bwd_f")
        r_b = _s5_bwd(s["proj"], cb0 + 2, dpre, *s["s5"], d=1, name=n + "s5_bwd_b")
        du = _du_combine(dpre, dskip, r_f[0], r_b[0], name=n + "du")
        both = lambda k: jnp.stack([r_f[k], r_b[k]])
        cts = [both(5).reshape(2 * groups, SSM_STATE), both(6).reshape(2 * groups, SSM_STATE),
               _b_untile(both(1), tiles), _b_untile(both(2), tiles)]
        da_re, da_im, dldt, db_re, db_im = _s5_prep_bwd(*s["par"], cts, name=n + "s5_prep")
        sg["ssm_a_re"][i] = da_re.reshape(2, groups, SSM_STATE)
        sg["ssm_a_im"][i] = da_im.reshape(2, groups, SSM_STATE)
        sg["ssm_log_dt"][i] = dldt.reshape(2, groups)
        sg["ssm_b_re"][i] = db_re.reshape(2, groups, SSM_STATE, SSM_GROUP)
        sg["ssm_b_im"][i] = db_im.reshape(2, groups, SSM_STATE, SSM_GROUP)
        sg["ssm_c_re"][i] = _c_untile(both(3), tiles)
        sg["ssm_c_im"][i] = _c_untile(both(4), tiles)
        dq_rot, dk_rot, dv, dlg = _ret_bwd(lg, s["q_rot"], s["k_rot"], s["v_bf"], dy_raw, name=n + "ret_bwd")
        sg["ret_log_gamma"][i] = dlg[:, :2, 0].T
        dq, dk = _ret_prep_bwd(dq_rot, dk_rot, cos, sin, name=n + "ret_prep")
        dproj = jnp.concatenate([dq, dk, dv.astype(BF16), dg, du, dgate_r, dgate_s], axis=1)
        dh = _matmul(dproj, w_in, mode="nt", out_dtype=F32, name=n + "dh")
        big[0][i] = _matmul(s["h"], dproj, mode="tn", out_dtype=BF16, name=n + "dw_in")
        dx, dgm = _norm_bwd(s["x"], g_mix, dh, dx1, name=n + "norm_mix")
        sg["ln_mix_g"][i] = dgm[0]

    small_grads = {k: jnp.stack(v) for k, v in sg.items()}
    small_grads["ln_final_g"] = dg_final[0]
    return loss, dx, big, small_grads


BIG = ("w_in", "w_glu", "w_out", "w_ffn_gate", "w_ffn_up", "w_ffn_down")
BIG_KINDS = ("col", "row", "row", "col", "col", "row")
SMALL = ("ln_mix_g", "ret_log_gamma", "ssm_a_re", "ssm_a_im", "ssm_log_dt", "ssm_b_re", "ssm_b_im", "ssm_c_re",
         "ssm_c_im", "ssm_d", "b_glu", "ln_ffn_g", "ln_final_g")
WEIGHTS = ("ln_mix_g", "w_in", "ret_log_gamma", "ssm_a_re", "ssm_a_im", "ssm_log_dt", "ssm_b_re", "ssm_b_im",
           "ssm_c_re", "ssm_c_im", "ssm_d", "w_glu", "b_glu", "w_out", "ln_ffn_g", "w_ffn_gate", "w_ffn_up",
           "w_ffn_down", "ln_final_g")


def _pad_to(a, axis, size):
    pad = [(0, 0)] * a.ndim
    pad[axis] = (0, size - a.shape[axis])
    return jnp.pad(a, pad)


def _flatten_small(tree, extra):
    flat = jnp.concatenate([tree[k].reshape(-1).astype(F32) for k in SMALL] + [extra.reshape(-1)])
    size = _round_up(flat.shape[0], FLAT_ROWS * LANE)
    return _pad_to(flat, 0, size).reshape(size // LANE, LANE)


def _unflatten_small(flat, like):
    flat = flat.reshape(-1)
    out, at = {}, 0
    for k in SMALL:
        n = like[k].size
        out[k] = flat[at:at + n].reshape(like[k].shape)
        at += n
    return out, flat[at]


def kernel(x, ln_mix_g, w_in, ret_log_gamma, ssm_a_re, ssm_a_im, ssm_log_dt, ssm_b_re, ssm_b_im, ssm_c_re, ssm_c_im, ssm_d, w_glu, b_glu, w_out, ln_ffn_g, w_ffn_gate, w_ffn_up, w_ffn_down, ln_final_g, loss_target, m_ln_mix_g, m_w_in, m_ret_log_gamma, m_ssm_a_re, m_ssm_a_im, m_ssm_log_dt, m_ssm_b_re, m_ssm_b_im, m_ssm_c_re, m_ssm_c_im, m_ssm_d, m_w_glu, m_b_glu, m_w_out, m_ln_ffn_g, m_w_ffn_gate, m_w_ffn_up, m_w_ffn_down, m_ln_final_g, v_ln_mix_g, v_w_in, v_ret_log_gamma, v_ssm_a_re, v_ssm_a_im, v_ssm_log_dt, v_ssm_b_re, v_ssm_b_im, v_ssm_c_re, v_ssm_c_im, v_ssm_d, v_w_glu, v_b_glu, v_w_out, v_ln_ffn_g, v_w_ffn_gate, v_w_ffn_up, v_w_ffn_down, v_ln_final_g):
    w = dict(ln_mix_g=ln_mix_g, w_in=w_in, ret_log_gamma=ret_log_gamma, ssm_a_re=ssm_a_re, ssm_a_im=ssm_a_im, ssm_log_dt=ssm_log_dt, ssm_b_re=ssm_b_re, ssm_b_im=ssm_b_im, ssm_c_re=ssm_c_re, ssm_c_im=ssm_c_im, ssm_d=ssm_d, w_glu=w_glu, b_glu=b_glu, w_out=w_out, ln_ffn_g=ln_ffn_g, w_ffn_gate=w_ffn_gate, w_ffn_up=w_ffn_up, w_ffn_down=w_ffn_down, ln_final_g=ln_final_g)
    m = dict(ln_mix_g=m_ln_mix_g, w_in=m_w_in, ret_log_gamma=m_ret_log_gamma, ssm_a_re=m_ssm_a_re, ssm_a_im=m_ssm_a_im, ssm_log_dt=m_ssm_log_dt, ssm_b_re=m_ssm_b_re, ssm_b_im=m_ssm_b_im, ssm_c_re=m_ssm_c_re, ssm_c_im=m_ssm_c_im, ssm_d=m_ssm_d, w_glu=m_w_glu, b_glu=m_b_glu, w_out=m_w_out, ln_ffn_g=m_ln_ffn_g, w_ffn_gate=m_w_ffn_gate, w_ffn_up=m_w_ffn_up, w_ffn_down=m_w_ffn_down, ln_final_g=m_ln_final_g)
    v = dict(ln_mix_g=v_ln_mix_g, w_in=v_w_in, ret_log_gamma=v_ret_log_gamma, ssm_a_re=v_ssm_a_re, ssm_a_im=v_ssm_a_im, ssm_log_dt=v_ssm_log_dt, ssm_b_re=v_ssm_b_re, ssm_b_im=v_ssm_b_im, ssm_c_re=v_ssm_c_re, ssm_c_im=v_ssm_c_im, ssm_d=v_ssm_d, w_glu=v_w_glu, b_glu=v_b_glu, w_out=v_w_out, ln_ffn_g=v_ln_ffn_g, w_ffn_gate=v_w_ffn_gate, w_ffn_up=v_w_ffn_up, w_ffn_down=v_w_ffn_down, ln_final_g=v_ln_final_g)
    depth, d, nb_in = w_in.shape
    qkw = (nb_in * N_DEV - 5 * d) // 2
    nb_ffn = w_ffn_gate.shape[2]
    nb_pad = _round_up(nb_ffn, LANE)
    pad_axis = {"w_ffn_gate": 2, "w_ffn_up": 2, "w_ffn_down": 1}

    shards = []
    for i in range(depth):
        for k in BIG:
            s = w[k] if k not in pad_axis else _pad_to(w[k], pad_axis[k], nb_pad)
            shards.append(s[i].astype(BF16))
    gathered = _all_gather(shards, BIG_KINDS * depth, name="gather_weights")
    full = [gathered[i * len(BIG):(i + 1) * len(BIG)] for i in range(depth)]

    small = {k: w[k] for k in SMALL}
    loss, dx, big, small_grads = _local_step(x[0], loss_target[0], small, full, qkw=qkw)

    recv = _scatter_partials(big, BIG_KINDS, name="scatter_grads")
    grads, delta, new_m, new_v = {}, {}, {}, {}
    for t, k in enumerate(BIG):
        if k in ("w_ffn_gate", "w_ffn_up"):
            ops = [_pad_to(a[k], 2, nb_pad) for a in (w, m, v)]
            res = [r[:, :, :nb_ffn] for r in _adam_shard(recv[t], *ops, name="adam_" + k, tr=256)]
        else:
            res = _adam_shard(recv[t], w[k], m[k], v[k], name="adam_" + k, tr=_row_tile(w[k].shape[1]))
        grads[k], delta[k], new_m[k], new_v[k] = res

    part = _flatten_small(small_grads, loss[0, :1])
    rows = part.shape[0]
    total = _all_reduce(part.reshape(N_DEV, rows // N_DEV, LANE), name="reduce_small").reshape(rows, LANE)
    zero = jnp.zeros((1,), F32)
    flat = [_flatten_small({k: a[k] for k in SMALL}, zero) for a in (w, m, v)]
    upd = _adam_flat(total, *flat, name="adam_small")
    g_small, loss_total = _unflatten_small(total, small)
    grads.update(g_small)
    for dst, u in zip((delta, new_m, new_v), upd):
        dst.update(_unflatten_small(u, small)[0])

    return (loss_total, dx[None], *[grads[k] for k in WEIGHTS], *[delta[k] for k in WEIGHTS],
            *[new_m[k] for k in WEIGHTS], *[new_v[k] for k in WEIGHTS])
```

```python
import math

import jax
import jax.numpy as jnp
from jax import lax
from jax.experimental import pallas as pl
from jax.experimental.pallas import tpu as pltpu

F32 = jnp.float32
BF16 = jnp.bfloat16
MESH = pl.DeviceIdType.MESH

N_DEV = 8
RET_HEADS = 4
CHUNK = 128
ROPE_BASE = 10000.0
SSM_GROUP = 16
SSM_STATE = 64
TILE_GROUPS = 8
TILE_U = TILE_GROUPS * SSM_GROUP
TILE_N = TILE_GROUPS * SSM_STATE
LANE = 128
SUBLANE = 8
N_SEG = SUBLANE
N_LT = TILE_N // LANE
SCAN_UNROLL = 4
FLAT_ROWS = 1024
EPS = 1e-6
ADAM_LR = 0.001
ADAM_B1 = 0.9
ADAM_B2 = 0.999
ADAM_EPS = 1e-08
ADAM_WD = 0.01
ADAM_STEP = 10
VMEM_LIMIT = 56 * 1024 * 1024


def _params(*sem):
    return pltpu.CompilerParams(dimension_semantics=sem or None, vmem_limit_bytes=VMEM_LIMIT)


def _dg(a, b, ca, cb):
    return lax.dot_general(a.astype(BF16), b.astype(BF16), (((ca,), (cb,)), ((), ())),
                           preferred_element_type=F32)


@jax.custom_vjp
def _dnn(a, b):
    return _dg(a, b, 1, 0)


@jax.custom_vjp
def _dnt(a, b):
    return _dg(a, b, 1, 1)


@jax.custom_vjp
def _dtn(a, b):
    return _dg(a, b, 0, 0)


_dnn.defvjp(lambda a, b: (_dnn(a, b), (a, b)), lambda r, g: (_dnt(g, r[1]), _dtn(r[0], g)))
_dnt.defvjp(lambda a, b: (_dnt(a, b), (a, b)), lambda r, g: (_dnn(g, r[1]), _dtn(g, r[0])))
_dtn.defvjp(lambda a, b: (_dtn(a, b), (a, b)), lambda r, g: (_dnt(r[1], g), _dnn(r[0], g)))


def _matmul(a, b, *, mode, out_dtype, name, res=None, tm=1024, tn=1024, tk=512):
    if mode == "nn":
        (m, k), n = a.shape, b.shape[1]
    elif mode == "nt":
        (m, k), n = a.shape, b.shape[0]
    else:
        (k, m), n = a.shape, b.shape[1]
    tm, tn, tk = min(tm, m), min(tn, n), min(tk, k)
    assert m % tm == 0 and n % tn == 0 and k % tk == 0, (name, m, n, k)
    nk = k // tk
    if mode == "tn":
        a_spec = pl.BlockSpec((tk, tm), lambda i, j, kk: (kk, i))
    else:
        a_spec = pl.BlockSpec((tm, tk), lambda i, j, kk: (i, kk))
    if mode == "nt":
        b_spec = pl.BlockSpec((tn, tk), lambda i, j, kk: (j, kk))
    else:
        b_spec = pl.BlockSpec((tk, tn), lambda i, j, kk: (kk, j))
    ca, cb = {"nn": (1, 0), "nt": (1, 1), "tn": (0, 0)}[mode]
    o_spec = pl.BlockSpec((tm, tn), lambda i, j, kk: (i, j))
    has_res = res is not None

    def body(*refs):
        if has_res:
            a_ref, b_ref, r_ref, o_ref, acc = refs
        else:
            a_ref, b_ref, o_ref, acc = refs
        kk = pl.program_id(2)

        @pl.when(kk == 0)
        def _():
            acc[...] = r_ref[...] if has_res else jnp.zeros_like(acc)

        acc[...] += _dg(a_ref[...], b_ref[...], ca, cb)

        @pl.when(kk == nk - 1)
        def _():
            o_ref[...] = acc[...].astype(out_dtype)

    return pl.pallas_call(
        body, name=name, grid=(m // tm, n // tn, nk),
        in_specs=[a_spec, b_spec] + ([o_spec] if has_res else []),
        out_specs=o_spec, out_shape=jax.ShapeDtypeStruct((m, n), out_dtype),
        scratch_shapes=[pltpu.VMEM((tm, tn), F32)],
        compiler_params=_params("parallel", "parallel", "arbitrary"),
    )(*((a, b, res) if has_res else (a, b)))


def _row(arr, tl, width=None, cb=0):
    width = arr.shape[1] if width is None else width
    return arr, pl.BlockSpec((tl, width), lambda i, cb=cb: (i, cb))


def _par(arr):
    return arr, pl.BlockSpec(arr.shape, lambda i: (0,) * arr.ndim)


def _rowwise(body, *, rows, tl, ins, outs, name):
    arrays = [a for a, _ in ins]
    in_specs = [s for _, s in ins]
    out_shape, out_specs, acc_ids = [], [], []
    for n, o in enumerate(outs):
        if o[0] == "row":
            out_shape.append(jax.ShapeDtypeStruct((rows, o[1]), o[2]))
            out_specs.append(pl.BlockSpec((tl, o[1]), lambda i: (i, 0)))
        else:
            out_shape.append(jax.ShapeDtypeStruct((1, o[1]), F32))
            out_specs.append(pl.BlockSpec((1, o[1]), lambda i: (0, 0)))
            acc_ids.append(n)
    n_in = len(arrays)
    assert rows % tl == 0, (name, rows, tl)

    def wrapped(*refs):
        in_refs, out_refs = refs[:n_in], refs[n_in:]

        @pl.when(pl.program_id(0) == 0)
        def _():
            for n in acc_ids:
                out_refs[n][...] = jnp.zeros_like(out_refs[n])

        body(in_refs, out_refs)

    return pl.pallas_call(
        wrapped, name=name, grid=(rows // tl,), in_specs=in_specs, out_specs=out_specs,
        out_shape=out_shape, compiler_params=_params("arbitrary"),
    )(*arrays)


def _rms(x, g):
    return x * lax.rsqrt(jnp.mean(x * x, axis=-1, keepdims=True) + EPS) * g


def _norm_fwd(x, g, *, name, tl=256):
    def body(i, o):
        o[0][...] = _rms(i[0][...], i[1][...]).astype(BF16)

    return _rowwise(body, rows=x.shape[0], tl=tl, ins=[_row(x, tl), _par(g)],
                    outs=[("row", x.shape[1], BF16)], name=name)[0]


def _norm_bwd(x, g, dh, dres, *, name, tl=256):
    def body(i, o):
        _, vjp = jax.vjp(_rms, i[0][...], i[1][...])
        dx, dg = vjp(i[2][...])
        o[0][...] = i[3][...] + dx
        o[1][...] += dg

    d = x.shape[1]
    return _rowwise(body, rows=x.shape[0], tl=tl, ins=[_row(x, tl), _par(g), _row(dh, tl), _row(dres, tl)],
                    outs=[("row", d, F32), ("acc", d)], name=name)


def _final(x, g, target, *, name, tl=256):
    d = x.shape[1]

    def body(i, o):
        y, vjp = jax.vjp(_rms, i[0][...], i[1][...])
        err = y - i[2][...]
        dx, dg = vjp(err * (1.0 / d))
        o[0][...] = dx
        o[1][...] += dg
        o[2][...] += jnp.full((1, LANE), 0.5 / d, F32) * jnp.sum(err * err)

    return _rowwise(body, rows=x.shape[0], tl=tl, ins=[_row(x, tl), _par(g), _row(target, tl)],
                    outs=[("row", d, F32), ("acc", d), ("acc", LANE)], name=name)


def _rot(x, cos, sin, out_ref, col, scale=1.0, inverse=False):
    x1, x2 = x[:, :LANE], x[:, LANE:]
    if inverse:
        sin = -sin
    out_ref[:, col:col + LANE] = ((x1 * cos - x2 * sin) * scale).astype(out_ref.dtype)
    out_ref[:, col + LANE:col + 2 * LANE] = ((x1 * sin + x2 * cos) * scale).astype(out_ref.dtype)


def _ret_prep(proj, cos, sin, *, qkw, d, name, tl=256):
    dk = qkw // RET_HEADS
    assert dk == 2 * LANE and (2 * qkw) % d == 0

    def body(i, o):
        c, s = i[3][...], i[4][...]
        for h in range(RET_HEADS):
            _rot(i[0][:, h * dk:(h + 1) * dk], c, s, o[0], h * dk)
            _rot(i[1][:, h * dk:(h + 1) * dk], c, s, o[1], h * dk, scale=dk ** -0.5)
        o[2][...] = i[2][...].astype(BF16)

    return _rowwise(body, rows=proj.shape[0], tl=tl,
                    ins=[_row(proj, tl, qkw, 0), _row(proj, tl, qkw, 1), _row(proj, tl, d, 2 * qkw // d),
                         _row(cos, tl), _row(sin, tl)],
                    outs=[("row", qkw, BF16), ("row", qkw, BF16), ("row", d, BF16)], name=name)


def _ret_prep_bwd(dq_rot, dk_rot, cos, sin, *, name, tl=256):
    qkw = dq_rot.shape[1]
    dk = qkw // RET_HEADS

    def body(i, o):
        c, s = i[2][...], i[3][...]
        for h in range(RET_HEADS):
            _rot(i[0][:, h * dk:(h + 1) * dk], c, s, o[0], h * dk, inverse=True)
            _rot(i[1][:, h * dk:(h + 1) * dk], c, s, o[1], h * dk, scale=dk ** -0.5, inverse=True)

    return _rowwise(body, rows=dq_rot.shape[0], tl=tl,
                    ins=[_row(dq_rot, tl), _row(dk_rot, tl), _row(cos, tl), _row(sin, tl)],
                    outs=[("row", qkw, BF16), ("row", qkw, BF16)], name=name)


def _ret_weights(lgf, lgb):
    t = lax.broadcasted_iota(jnp.int32, (CHUNK, 1), 0).astype(F32)
    diff = (lax.broadcasted_iota(jnp.int32, (CHUNK, CHUNK), 0)
            - lax.broadcasted_iota(jnp.int32, (CHUNK, CHUNK), 1)).astype(F32)
    dmat = jnp.exp(jnp.where(diff >= 0, lgf * diff, -lgb * diff))
    return dict(dmat=dmat, wqf=jnp.exp(lgf * (t + 1.0)), wkf=jnp.exp(lgf * (CHUNK - 1.0 - t)),
                wqb=jnp.exp(lgb * (CHUNK - t)), wkb=jnp.exp(lgb * t))


def _ret_f_part(q, k, v, lgf, lgb, s_f):
    w = _ret_weights(lgf, lgb)
    y = _dnn(_dnt(q, k) * w["dmat"], v) + _dnn(q * w["wqf"], s_f)
    return y, _dtn(k * w["wkf"], v)


def _ret_b_part(q, k, v, lgb, s_b):
    w = _ret_weights(lgb, lgb)
    return _dnn(q * w["wqb"], s_b), _dtn(k * w["wkb"], v)


def _chunk(c):
    return pl.ds(pl.multiple_of(c * CHUNK, CHUNK), CHUNK)


def _ret_specs(l, qkw, d):
    dk, dv = qkw // RET_HEADS, d // RET_HEADS
    return dk, dv, [pl.BlockSpec(memory_space=pltpu.SMEM),
                    pl.BlockSpec((l, dk), lambda h: (0, h)), pl.BlockSpec((l, dk), lambda h: (0, h)),
                    pl.BlockSpec((l, dv), lambda h: (0, h))]


def _ret_fwd(lg, q, k, v, *, name):
    l, qkw = q.shape
    d = v.shape[1]
    nc = l // CHUNK
    dk, dv, in_specs = _ret_specs(l, qkw, d)

    def body(lg_ref, q_ref, k_ref, v_ref, y_ref, s_ref):
        h = pl.program_id(0)
        lgf = jnp.full((1, 1), lg_ref[0, h], F32)
        lgb = jnp.full((1, 1), lg_ref[1, h], F32)
        dec_f, dec_b = jnp.exp(lgf * CHUNK), jnp.exp(lgb * CHUNK)

        def load(c):
            r = _chunk(c)
            return r, q_ref[r, :].astype(F32), k_ref[r, :].astype(F32), v_ref[r, :].astype(F32)

        s_ref[...] = jnp.zeros_like(s_ref)

        def f_step(c, _):
            r, qc, kc, vc = load(c)
            y, kv = _ret_f_part(qc, kc, vc, lgf, lgb, s_ref[...])
            y_ref[r, :] = y
            s_ref[...] = dec_f * s_ref[...] + kv
            return 0

        lax.fori_loop(0, nc, f_step, 0)
        s_ref[...] = jnp.zeros_like(s_ref)

        def b_step(n, _):
            r, qc, kc, vc = load(nc - 1 - n)
            y, kv = _ret_b_part(qc, kc, vc, lgb, s_ref[...])
            y_ref[r, :] += y
            s_ref[...] = dec_b * s_ref[...] + kv
            return 0

        lax.fori_loop(0, nc, b_step, 0)

    return pl.pallas_call(
        body, name=name, grid=(RET_HEADS,), in_specs=in_specs,
        out_specs=pl.BlockSpec((l, dv), lambda h: (0, h)), out_shape=jax.ShapeDtypeStruct((l, d), F32),
        scratch_shapes=[pltpu.VMEM((dk, dv), F32)], compiler_params=_params("arbitrary"),
    )(lg, q, k, v)


def _ret_bwd(lg, q, k, v, dy, *, name):
    l, qkw = q.shape
    d = v.shape[1]
    nc = l // CHUNK
    dk, dv, in_specs = _ret_specs(l, qkw, d)

    def body(lg_ref, q_ref, k_ref, v_ref, dy_ref, dq_ref, dk_ref, dv_ref, dlg_ref, states, s_ref, sh_ref):
        h = pl.program_id(0)
        lgf = jnp.full((1, 1), lg_ref[0, h], F32)
        lgb = jnp.full((1, 1), lg_ref[1, h], F32)
        dec_f, dec_b = jnp.exp(lgf * CHUNK), jnp.exp(lgb * CHUNK)

        def load(c):
            r = _chunk(c)
            return (r, q_ref[r, :].astype(F32), k_ref[r, :].astype(F32), v_ref[r, :].astype(F32),
                    dy_ref[r, :].astype(F32))

        s_ref[...] = jnp.zeros_like(s_ref)

        def f_states(c, _):
            _, qc, kc, vc, _ = load(c)
            states[c] = s_ref[...]
            w = _ret_weights(lgf, lgb)
            s_ref[...] = dec_f * s_ref[...] + _dtn(kc * w["wkf"], vc)
            return 0

        lax.fori_loop(0, nc, f_states, 0)
        sh_ref[...] = jnp.zeros_like(sh_ref)

        def f_adj(n, carry):
            dlf, dlb, ddec = carry
            c = nc - 1 - n
            r, qc, kc, vc, dyc = load(c)
            sc = states[c]
            _, vjp = jax.vjp(_ret_f_part, qc, kc, vc, lgf, lgb, sc)
            dq, dkk, dvv, g_f, g_b, dsc = vjp((dyc, sh_ref[...]))
            dq_ref[r, :] = dq
            dk_ref[r, :] = dkk
            dv_ref[r, :] = dvv
            ddec = ddec + jnp.sum(sh_ref[...] * sc)
            sh_ref[...] = dsc + dec_f * sh_ref[...]
            return dlf + g_f, dlb + g_b, ddec

        z = jnp.zeros((1, 1), F32)
        dlf, dlb, ddec_f = lax.fori_loop(0, nc, f_adj, (z, z, z))

        s_ref[...] = jnp.zeros_like(s_ref)

        def b_states(n, _):
            c = nc - 1 - n
            _, qc, kc, vc, _ = load(c)
            states[c] = s_ref[...]
            w = _ret_weights(lgb, lgb)
            s_ref[...] = dec_b * s_ref[...] + _dtn(kc * w["wkb"], vc)
            return 0

        lax.fori_loop(0, nc, b_states, 0)
        sh_ref[...] = jnp.zeros_like(sh_ref)

        def b_adj(c, carry):
            dlb, ddec = carry
            r, qc, kc, vc, dyc = load(c)
            sc = states[c]
            _, vjp = jax.vjp(_ret_b_part, qc, kc, vc, lgb, sc)
            dq, dkk, dvv, g_b, dsc = vjp((dyc, sh_ref[...]))
            dq_ref[r, :] += dq
            dk_ref[r, :] += dkk
            dv_ref[r, :] += dvv
            ddec = ddec + jnp.sum(sh_ref[...] * sc)
            sh_ref[...] = dsc + dec_b * sh_ref[...]
            return dlb + g_b, ddec

        dlb, ddec_b = lax.fori_loop(0, nc, b_adj, (dlb, z))
        dlf = dlf + ddec_f * dec_f * CHUNK
        dlb = dlb + ddec_b * dec_b * CHUNK
        row = lax.broadcasted_iota(jnp.int32, (SUBLANE, LANE), 0)
        dlg_ref[...] = jnp.where(row == 0, dlf, jnp.where(row == 1, dlb, 0.0))

    head = lambda w: pl.BlockSpec((l, w), lambda h: (0, h))
    return pl.pallas_call(
        body, name=name, grid=(RET_HEADS,), in_specs=in_specs + [head(dv)],
        out_specs=[head(dk), head(dk), head(dv), pl.BlockSpec((None, SUBLANE, LANE), lambda h: (h, 0, 0))],
        out_shape=[jax.ShapeDtypeStruct((l, qkw), F32), jax.ShapeDtypeStruct((l, qkw), F32),
                   jax.ShapeDtypeStruct((l, d), F32), jax.ShapeDtypeStruct((RET_HEADS, SUBLANE, LANE), F32)],
        scratch_shapes=[pltpu.VMEM((nc, dk, dv), F32), pltpu.VMEM((dk, dv), F32), pltpu.VMEM((dk, dv), F32)],
        compiler_params=_params("arbitrary"),
    )(lg, q, k, v, dy)


def _s5_param_fn(a_re, a_im, log_dt, b_re, b_im, rep):
    dt = jnp.exp(log_dt)
    mag = jnp.exp(a_re * dt)
    lam_re, lam_im = mag * jnp.cos(a_im * dt), mag * jnp.sin(a_im * dt)
    n_re, n_im = lam_re - 1.0, lam_im
    den = a_re * a_re + a_im * a_im
    c_re = (n_re * a_re + n_im * a_im) / den
    c_im = (n_im * a_re - n_re * a_im) / den
    hi = lax.Precision.HIGHEST
    c_re = jnp.dot(c_re, rep, precision=hi, preferred_element_type=F32)
    c_im = jnp.dot(c_im, rep, precision=hi, preferred_element_type=F32)
    return lam_re, lam_im, c_re * b_re - c_im * b_im, c_re * b_im + c_im * b_re


def _s5_param_shapes(a_re, b_re):
    r, p = a_re.shape
    return [jax.ShapeDtypeStruct((r, p), F32)] * 2 + [jax.ShapeDtypeStruct(b_re.shape, F32)] * 2


def _s5_prep(a_re, a_im, log_dt, b_re, b_im, rep, *, name):
    def body(*refs):
        outs = _s5_param_fn(*[r[...] for r in refs[:6]])
        for o_ref, o in zip(refs[6:], outs):
            o_ref[...] = o

    return pl.pallas_call(body, name=name, out_shape=_s5_param_shapes(a_re, b_re),
                          compiler_params=_params())(a_re, a_im, log_dt, b_re, b_im, rep)


def _s5_prep_bwd(a_re, a_im, log_dt, b_re, b_im, rep, cts, *, name):
    def body(*refs):
        ins = [r[...] for r in refs[:6]]
        _, vjp = jax.vjp(lambda *p: _s5_param_fn(*p, ins[5]), *ins[:5])
        grads = vjp(tuple(r[...] for r in refs[6:10]))
        for o_ref, o in zip(refs[10:], grads):
            o_ref[...] = o

    shapes = [jax.ShapeDtypeStruct(t.shape, F32) for t in (a_re, a_im, log_dt, b_re, b_im)]
    return pl.pallas_call(body, name=name, out_shape=shapes,
                          compiler_params=_params())(a_re, a_im, log_dt, b_re, b_im, rep, *cts)


def _eye_tiles():
    return jnp.eye(TILE_GROUPS, dtype=F32)


def _b_tiles(bbar, tiles):
    t = bbar.reshape(2, tiles, TILE_GROUPS, SSM_STATE, SSM_GROUP).transpose(0, 1, 2, 4, 3)
    t = t[:, :, :, :, None, :] * _eye_tiles()[None, None, :, None, :, None]
    return t.reshape(2, tiles, TILE_U, TILE_N)


def _b_untile(dbt, tiles):
    t = dbt.reshape(2, tiles, TILE_GROUPS, SSM_GROUP, TILE_GROUPS, SSM_STATE)
    t = (t * _eye_tiles()[None, None, :, None, :, None]).sum(axis=4)
    return t.transpose(0, 1, 2, 4, 3).reshape(2 * tiles * TILE_GROUPS, SSM_STATE * SSM_GROUP)


def _c_tiles(c, tiles):
    t = c.reshape(2, tiles, TILE_GROUPS, SSM_GROUP, SSM_STATE).transpose(0, 1, 2, 4, 3)
    t = t[:, :, :, :, None, :] * _eye_tiles()[None, None, :, None, :, None]
    return t.reshape(2, tiles, TILE_N, TILE_U)


def _c_untile(dct, tiles):
    t = dct.reshape(2, tiles, TILE_GROUPS, SSM_STATE, TILE_GROUPS, SSM_GROUP)
    t = (t * _eye_tiles()[None, None, :, None, :, None]).sum(axis=4)
    return t.transpose(0, 1, 2, 4, 3).reshape(2, tiles * TILE_GROUPS, SSM_GROUP, SSM_STATE)


def _to_segments(a):
    l, w = a.shape
    return a.reshape(N_SEG, l // N_SEG, w).transpose(1, 0, 2).reshape(l, w)


def _from_segments(a):
    l, w = a.shape
    return a.reshape(l // N_SEG, N_SEG, w).transpose(1, 0, 2).reshape(l, w)


def _s5_scan(xr, xi, a_re, a_im, *, length, reverse, shifted=None):
    ls = length // N_SEG
    assert ls * N_SEG == length and ls & (ls - 1) == 0
    ar = [jnp.broadcast_to(a_re[:, c * LANE:(c + 1) * LANE], (N_SEG, LANE)) for c in range(N_LT)]
    ai = [jnp.broadcast_to(a_im[:, c * LANE:(c + 1) * LANE], (N_SEG, LANE)) for c in range(N_LT)]
    zero = jnp.zeros((N_SEG, LANE), F32)
    row = lax.broadcasted_iota(jnp.int32, (N_SEG, LANE), 0)

    def step_of(n):
        return (ls - 1 - n) if reverse else n

    def block(j):
        return pl.ds(j * N_SEG, N_SEG) if isinstance(j, int) else pl.ds(pl.multiple_of(j * N_SEG, N_SEG), N_SEG)

    def local(n, carry):
        rows = block(step_of(n))
        new = []
        for c in range(N_LT):
            cr, ci = carry[2 * c], carry[2 * c + 1]
            nr = ar[c] * cr - ai[c] * ci + xr[c, rows, :]
            ni = ar[c] * ci + ai[c] * cr + xi[c, rows, :]
            xr[c, rows, :] = nr
            xi[c, rows, :] = ni
            new += [nr, ni]
        return tuple(new)

    ends = lax.fori_loop(0, ls, local, (zero,) * (2 * N_LT), unroll=SCAN_UNROLL)

    init = []
    for c in range(N_LT):
        pr, pi = ar[c][0:1, :], ai[c][0:1, :]
        for _ in range(ls.bit_length() - 1):
            pr, pi = pr * pr - pi * pi, 2.0 * pr * pi
        cr = ci = jnp.zeros((1, LANE), F32)
        ir, ii = zero, zero
        for s in (range(N_SEG - 1, -1, -1) if reverse else range(N_SEG)):
            ir = jnp.where(row == s, cr, ir)
            ii = jnp.where(row == s, ci, ii)
            er, ei = ends[2 * c][s:s + 1, :], ends[2 * c + 1][s:s + 1, :]
            cr, ci = pr * cr - pi * ci + er, pr * ci + pi * cr + ei
        init += [ir, ii]

    def fix(n, carry, last=False):
        j = step_of(n)
        rows = block(j)
        new, sums = [], []
        for c in range(N_LT):
            cr, ci = carry[2 * c], carry[2 * c + 1]
            nr = ar[c] * cr - ai[c] * ci
            ni = ar[c] * ci + ai[c] * cr
            fr = xr[c, rows, :] + nr
            fi = xi[c, rows, :] + ni
            xr[c, rows, :] = fr
            xi[c, rows, :] = fi
            new += [nr, ni]
            if shifted is not None:
                yr, yi, shift = shifted
                if not last:
                    srows = block(j + shift)
                    sr, si = yr[c, srows, :], yi[c, srows, :]
                else:
                    edge = block(ls - 1 if shift < 0 else 0)
                    move, gone = (1, 0) if shift < 0 else (N_SEG - 1, N_SEG - 1)
                    sr = jnp.where(row == gone, 0.0, pltpu.roll(yr[c, edge, :], move, 0))
                    si = jnp.where(row == gone, 0.0, pltpu.roll(yi[c, edge, :], move, 0))
                sums += [carry[2 * N_LT + 2 * c] + fr * sr + fi * si,
                         carry[2 * N_LT + 2 * c + 1] + fi * sr - fr * si]
        return tuple(new + sums)

    if shifted is None:
        lax.fori_loop(0, ls, fix, tuple(init), unroll=SCAN_UNROLL)
        return ()
    assert shifted[2] == (-1 if reverse else 1)
    out = lax.fori_loop(0, ls - 1, fix, tuple(init) + (zero,) * (2 * N_LT), unroll=SCAN_UNROLL)
    return fix(ls - 1, out, last=True)[2 * N_LT:]


def _s5_tile_specs(l, d):
    tile = lambda r, c: pl.BlockSpec((None, None, r, c), lambda t, d=d: (d, t, 0, 0))
    return [pl.BlockSpec((l, TILE_U), lambda t: (0, t)), tile(TILE_U, TILE_N), tile(TILE_U, TILE_N),
            tile(1, TILE_N), tile(1, TILE_N), tile(TILE_N, TILE_U), tile(TILE_N, TILE_U)]


def _lanes(c):
    return slice(c * LANE, (c + 1) * LANE)


def _s5_fwd(u, bt_re, bt_im, lam_re, lam_im, ct_re, ct_im, *, d, name, add=None):
    l = u.shape[0]
    tiles = bt_re.shape[1]
    col = pl.BlockSpec((l, TILE_U), lambda t: (0, t))
    has_add = add is not None

    def body(*refs):
        u_ref, bre, bim, lre, lim, cre, cim = refs[:7]
        y_ref, xr, xi = refs[-3:]
        uu = u_ref[...]
        bu_re, bu_im = _dg(uu, bre[...], 1, 0), _dg(uu, bim[...], 1, 0)
        for c in range(N_LT):
            xr[c] = bu_re[:, _lanes(c)]
            xi[c] = bu_im[:, _lanes(c)]
        _s5_scan(xr, xi, lre[...], lim[...], length=l, reverse=(d == 1))
        y = refs[7][...] if has_add else jnp.zeros((l, TILE_U), F32)
        for c in range(N_LT):
            y = y + _dg(xr[c], cre[_lanes(c), :], 1, 0) - _dg(xi[c], cim[_lanes(c), :], 1, 0)
        y_ref[...] = y

    return pl.pallas_call(
        body, name=name, grid=(tiles,), in_specs=_s5_tile_specs(l, d) + [col] * has_add, out_specs=col,
        out_shape=jax.ShapeDtypeStruct((l, tiles * TILE_U), F32),
        scratch_shapes=[pltpu.VMEM((N_LT, l, LANE), F32)] * 2, compiler_params=_params("arbitrary"),
    )(u, bt_re, bt_im, lam_re, lam_im, ct_re, ct_im, *([add] if has_add else []))


def _s5_bwd(u, dy, bt_re, bt_im, lam_re, lam_im, ct_re, ct_im, *, d, name, add=None):
    l = u.shape[0]
    tiles = bt_re.shape[1]
    col = pl.BlockSpec((l, TILE_U), lambda t: (0, t))
    reverse = d == 1
    has_add = add is not None

    def body(*refs):
        u_ref, bre, bim, lre, lim, cre, cim, dy_ref = refs[:8]
        du_ref, dbre, dbim, dcre, dcim, dlre, dlim, xr, xi, gr, gi = refs[-11:]
        uu, dyy = u_ref[...], dy_ref[...]
        bu_re, bu_im = _dg(uu, bre[...], 1, 0), _dg(uu, bim[...], 1, 0)
        for c in range(N_LT):
            xr[c] = bu_re[:, _lanes(c)]
            xi[c] = bu_im[:, _lanes(c)]
        _s5_scan(xr, xi, lre[...], lim[...], length=l, reverse=reverse)
        gy_re, gy_im = _dg(dyy, cre[...], 1, 1), -_dg(dyy, cim[...], 1, 1)
        for c in range(N_LT):
            gr[c] = gy_re[:, _lanes(c)]
            gi[c] = gy_im[:, _lanes(c)]
        sums = _s5_scan(gr, gi, lre[...], -lim[...], length=l, reverse=not reverse,
                        shifted=(xr, xi, 1 if reverse else -1))
        du = refs[8][...] if has_add else jnp.zeros((l, TILE_U), F32)
        for c in range(N_LT):
            dlre[:, _lanes(c)] = jnp.sum(sums[2 * c], axis=0, keepdims=True)
            dlim[:, _lanes(c)] = jnp.sum(sums[2 * c + 1], axis=0, keepdims=True)
            g_re, g_im = gr[c], gi[c]
            du = du + _dg(g_re, bre[:, _lanes(c)], 1, 1) + _dg(g_im, bim[:, _lanes(c)], 1, 1)
            dbre[:, _lanes(c)] = _dg(uu, g_re, 0, 0)
            dbim[:, _lanes(c)] = _dg(uu, g_im, 0, 0)
            dcre[_lanes(c), :] = _dg(xr[c], dyy, 0, 0)
            dcim[_lanes(c), :] = -_dg(xi[c], dyy, 0, 0)
        du_ref[...] = du

    out3 = lambda r, c: pl.BlockSpec((None, r, c), lambda t: (t, 0, 0))
    f = lambda *s: jax.ShapeDtypeStruct(s, F32)
    return pl.pallas_call(
        body, name=name, grid=(tiles,), in_specs=_s5_tile_specs(l, d) + [col] + [col] * has_add,
        out_specs=[col, out3(TILE_U, TILE_N), out3(TILE_U, TILE_N),
                   out3(TILE_N, TILE_U), out3(TILE_N, TILE_U), out3(1, TILE_N), out3(1, TILE_N)],
        out_shape=[f(l, tiles * TILE_U), f(tiles, TILE_U, TILE_N), f(tiles, TILE_U, TILE_N),
                   f(tiles, TILE_N, TILE_U), f(tiles, TILE_N, TILE_U), f(tiles, 1, TILE_N), f(tiles, 1, TILE_N)],
        scratch_shapes=[pltpu.VMEM((N_LT, l, LANE), F32)] * 4, compiler_params=_params("arbitrary"),
    )(u, bt_re, bt_im, lam_re, lam_im, ct_re, ct_im, dy, *([add] if has_add else []))


def _s5_post(y, proj, u_cb, dskip, *, name, tl=256):
    d = y.shape[1]

    def body(i, o):
        ys = jax.nn.gelu(i[0][...] + i[2][...] * i[1][...])
        o[0][...] = ys
        o[1][...] = ys.astype(BF16)

    return _rowwise(body, rows=y.shape[0], tl=tl, ins=[_row(y, tl), _row(proj, tl, d, u_cb), _par(dskip)],
                    outs=[("row", d, F32), ("row", d, BF16)], name=name)


def _s5_post_bwd(y, proj, u_cb, dskip, dys, *, name, tl=256):
    d = y.shape[1]

    def body(i, o):
        u_ = i[1][...]
        _, vjp = jax.vjp(jax.nn.gelu, i[0][...] + i[2][...] * u_)
        (dpre,) = vjp(i[3][...])
        o[0][...] = dpre
        o[1][...] += jnp.sum(dpre * u_, axis=0, keepdims=True)

    return _rowwise(body, rows=y.shape[0], tl=tl,
                    ins=[_row(y, tl), _row(proj, tl, d, u_cb), _par(dskip), _row(dys, tl)],
                    outs=[("row", d, F32), ("acc", d)], name=name)


def _du_combine(dpre, dskip, du_s5, *, name, tl=256):
    d = dpre.shape[1]

    def body(i, o):
        o[0][...] = (i[0][...] * i[1][...] + i[2][...]).astype(BF16)

    return _rowwise(body, rows=dpre.shape[0], tl=tl, ins=[_row(dpre, tl), _par(dskip), _row(du_s5, tl)],
                    outs=[("row", d, BF16)], name=name)[0]


def _merge_fn(y, g, gate_r, gate_s, ys, glu, b):
    ret = jax.nn.silu(g) * (y * lax.rsqrt(jnp.mean(y * y, axis=-1, keepdims=True) + EPS))
    ssm = ys * jax.nn.sigmoid(glu + b)
    return jax.nn.sigmoid(gate_r) * ret + jax.nn.sigmoid(gate_s) * ssm


def _merge_ins(y_raw, proj, ys, glu, b_glu, cb0, tl):
    d = y_raw.shape[1]
    return [_row(y_raw, tl), _row(proj, tl, d, cb0 + 1), _row(proj, tl, d, cb0 + 3), _row(proj, tl, d, cb0 + 4),
            _row(ys, tl), _row(glu, tl), _par(b_glu)]


def _merge(y_raw, proj, ys, glu, b_glu, *, cb0, name, tl=128):
    d = y_raw.shape[1]
    dv = d // RET_HEADS

    def body(i, o):
        for h in range(RET_HEADS):
            cs = slice(h * dv, (h + 1) * dv)
            o[0][:, cs] = _merge_fn(*[r[:, cs] for r in i]).astype(BF16)

    return _rowwise(body, rows=y_raw.shape[0], tl=tl, ins=_merge_ins(y_raw, proj, ys, glu, b_glu, cb0, tl),
                    outs=[("row", d, BF16)], name=name)[0]


def _merge_bwd(y_raw, proj, ys, glu, b_glu, dmerged, *, cb0, name, tl=128):
    d = y_raw.shape[1]
    dv = d // RET_HEADS

    def body(i, o):
        for h in range(RET_HEADS):
            cs = slice(h * dv, (h + 1) * dv)
            _, vjp = jax.vjp(_merge_fn, *[r[:, cs] for r in i[:7]])
            dy, dg, dgr, dgs, dys, dglu, db = vjp(i[7][:, cs])
            o[0][:, cs] = dy.astype(BF16)
            o[1][:, cs] = dg.astype(BF16)
            o[2][:, cs] = dgr.astype(BF16)
            o[3][:, cs] = dgs.astype(BF16)
            o[4][:, cs] = dglu.astype(BF16)
            o[5][:, cs] = dys
            o[6][:, cs] += db

    return _rowwise(body, rows=y_raw.shape[0], tl=tl,
                    ins=_merge_ins(y_raw, proj, ys, glu, b_glu, cb0, tl) + [_row(dmerged, tl)],
                    outs=[("row", d, BF16)] * 5 + [("row", d, F32), ("acc", d)], name=name)


def _ffn_act_fn(gate, up):
    return jax.nn.silu(gate) * up


def _ffn_act(gate, up, *, name, tl=128):
    def body(i, o):
        o[0][...] = _ffn_act_fn(i[0][...], i[1][...]).astype(BF16)

    return _rowwise(body, rows=gate.shape[0], tl=tl, ins=[_row(gate, tl), _row(up, tl)],
                    outs=[("row", gate.shape[1], BF16)], name=name)[0]


def _ffn_act_bwd(gate, up, dact, *, name, tl=128):
    def body(i, o):
        _, vjp = jax.vjp(_ffn_act_fn, i[0][...], i[1][...])
        dgate, dup = vjp(i[2][...])
        o[0][...] = dgate.astype(BF16)
        o[1][...] = dup.astype(BF16)

    w = gate.shape[1]
    return _rowwise(body, rows=gate.shape[0], tl=tl, ins=[_row(gate, tl), _row(up, tl), _row(dact, tl)],
                    outs=[("row", w, BF16), ("row", w, BF16)], name=name)


def _adamw(w, g, m, v):
    m = ADAM_B1 * m + (1.0 - ADAM_B1) * g
    v = ADAM_B2 * v + (1.0 - ADAM_B2) * (g * g)
    m_hat = m / (1.0 - ADAM_B1 ** ADAM_STEP)
    v_hat = v / (1.0 - ADAM_B2 ** ADAM_STEP)
    return -ADAM_LR * (m_hat / (jnp.sqrt(v_hat) + ADAM_EPS) + ADAM_WD * w), m, v


def _adam_flat(g, w, m, v, *, name, tr=FLAT_ROWS):
    def body(i, o):
        for o_ref, val in zip(o, _adamw(i[1][...], i[0][...], i[2][...], i[3][...])):
            o_ref[...] = val

    return _rowwise(body, rows=g.shape[0], tl=tr, ins=[_row(a, tr) for a in (g, w, m, v)],
                    outs=[("row", LANE, F32)] * 3, name=name)


def _adam_shard(recv, w, m, v, *, name, tr):
    depth, r, c = w.shape
    assert r % tr == 0 and recv.shape[3] == c
    blk = pl.BlockSpec((None, tr, c), lambda l, i: (l, i, 0))

    def body(recv_ref, w_ref, m_ref, v_ref, g_ref, d_ref, nm_ref, nv_ref):
        g = recv_ref[0].astype(F32)
        for p in range(1, N_DEV):
            g = g + recv_ref[p].astype(F32)
        g_ref[...] = g
        d_ref[...], nm_ref[...], nv_ref[...] = _adamw(w_ref[...], g, m_ref[...], v_ref[...])

    return pl.pallas_call(
        body, name=name, grid=(depth, r // tr),
        in_specs=[pl.BlockSpec((None, N_DEV, tr, c), lambda l, i: (l, 0, i, 0)), blk, blk, blk],
        out_specs=[blk] * 4, out_shape=[jax.ShapeDtypeStruct(w.shape, F32)] * 4,
        compiler_params=_params("parallel", "parallel"),
    )(recv, w, m, v)


def _position():
    x, y, c = lax.axis_index("x"), lax.axis_index("y"), lax.axis_index("c")
    return x, y, c, 4 * x + 2 * y + c


def _coords(p):
    return p // 4, (p // 2) % 2, p % 2


def _block_of(ref, kind, p, nb):
    if kind == "col":
        return ref.at[:, pl.ds(pl.multiple_of(p * nb, LANE), nb)]
    return ref.at[pl.ds(pl.multiple_of(p * nb, SUBLANE), nb), :]


def _all_gather(shards, kinds, *, name):
    n = len(shards)
    out_shape = []
    for s, kind in zip(shards, kinds):
        r, c = s.shape
        out_shape.append(jax.ShapeDtypeStruct((r, c * N_DEV) if kind == "col" else (r * N_DEV, c), s.dtype))

    def body(*refs):
        shard_refs, full_refs = refs[:n], refs[n:2 * n]
        send_sems, recv_sems, local_sems = refs[2 * n:]
        x, y, c, me = _position()
        sibling = (x, y, 1 - c)
        chips = [(1 - x, y), (x, 1 - y), (1 - x, 1 - y)]

        def block(t, dev):
            nb = shards[t].shape[1 if kinds[t] == "col" else 0]
            return _block_of(full_refs[t], kinds[t], 4 * dev[0] + 2 * dev[1] + dev[2], nb)

        def copy(t, k, dev, to, src=None):
            return pltpu.make_async_remote_copy(
                src_ref=block(t, dev) if src is None else src, dst_ref=block(t, dev),
                send_sem=send_sems.at[t, k], recv_sem=recv_sems.at[t, k], device_id=to, device_id_type=MESH)

        mine, first, passed = [], [], []
        for t in range(n):
            mine.append(pltpu.make_async_copy(shard_refs[t], block(t, (x, y, c)), local_sems.at[t]))
            mine[-1].start()
            first.append(copy(t, 0, (x, y, c), sibling, src=shard_refs[t]))
            first += [copy(t, 1 + j, (x, y, c), (*chip, c), src=shard_refs[t]) for j, chip in enumerate(chips)]
        for cp in first:
            cp.start()
        for j, chip in enumerate(chips):
            for t in range(n):
                copy(t, 1 + j, (*chip, c), (x, y, c)).wait_recv()
                passed.append(copy(t, 4 + j, (*chip, c), sibling))
                passed[-1].start()
        for t in range(n):
            copy(t, 0, sibling, (x, y, c)).wait_recv()
            for j, chip in enumerate(chips):
                copy(t, 4 + j, (*chip, 1 - c), (x, y, c)).wait_recv()
        for cp in first + passed:
            cp.wait_send()
        for cp in mine:
            cp.wait()

    any_spec = pl.BlockSpec(memory_space=pl.ANY)
    return pl.pallas_call(
        body, name=name, in_specs=[any_spec] * n, out_specs=[any_spec] * n, out_shape=out_shape,
        scratch_shapes=[pltpu.SemaphoreType.DMA((n, N_DEV - 1)), pltpu.SemaphoreType.DMA((n, N_DEV - 1)),
                        pltpu.SemaphoreType.DMA((n,))],
        compiler_params=pltpu.CompilerParams(has_side_effects=True),
    )(*shards)


def _scatter_partials(grads, kinds, *, name):
    n, depth = len(grads), len(grads[0])
    blocks, out_shape = [], []
    for g, kind in zip(grads, kinds):
        r, c = g[0].shape
        blocks.append((r, c // N_DEV) if kind == "col" else (r // N_DEV, c))
        out_shape.append(jax.ShapeDtypeStruct((depth, N_DEV) + blocks[-1], g[0].dtype))

    def body(*refs):
        g_refs = [refs[t * depth:(t + 1) * depth] for t in range(n)]
        recv_refs = refs[n * depth:n * depth + n]
        send_sems, recv_sems, local_sems = refs[n * depth + n:]
        _, _, _, me = _position()

        def copy(t, l, k):
            p = (me + k) % N_DEV
            nb = blocks[t][1 if kinds[t] == "col" else 0]
            return pltpu.make_async_remote_copy(
                src_ref=_block_of(g_refs[t][l], kinds[t], p, nb), dst_ref=recv_refs[t].at[l, me],
                send_sem=send_sems.at[t, l, k], recv_sem=recv_sems.at[t, l, k], device_id=_coords(p),
                device_id_type=MESH)

        def arrival(t, l, k):
            q = (me + N_DEV - k) % N_DEV
            nb = blocks[t][1 if kinds[t] == "col" else 0]
            return pltpu.make_async_remote_copy(
                src_ref=_block_of(g_refs[t][l], kinds[t], me, nb), dst_ref=recv_refs[t].at[l, q],
                send_sem=send_sems.at[t, l, k], recv_sem=recv_sems.at[t, l, k], device_id=_coords(q),
                device_id_type=MESH)

        mine, sent = [], []
        for t in range(n):
            nb = blocks[t][1 if kinds[t] == "col" else 0]
            for l in range(depth):
                mine.append(pltpu.make_async_copy(_block_of(g_refs[t][l], kinds[t], me, nb), recv_refs[t].at[l, me],
                                                  local_sems.at[t, l]))
                mine[-1].start()
                for k in range(1, N_DEV):
                    sent.append(copy(t, l, k))
                    sent[-1].start()
        for t in range(n):
            for l in range(depth):
                for k in range(1, N_DEV):
                    arrival(t, l, k).wait_recv()
        for cp in sent:
            cp.wait_send()
        for cp in mine:
            cp.wait()

    any_spec = pl.BlockSpec(memory_space=pl.ANY)
    return pl.pallas_call(
        body, name=name, in_specs=[any_spec] * (n * depth), out_specs=[any_spec] * n, out_shape=out_shape,
        scratch_shapes=[pltpu.SemaphoreType.DMA((n, depth, N_DEV)), pltpu.SemaphoreType.DMA((n, depth, N_DEV)),
                        pltpu.SemaphoreType.DMA((n, depth))],
        compiler_params=pltpu.CompilerParams(has_side_effects=True),
    )(*[g for per in grads for g in per])


def _all_reduce(part, *, name):
    _, r, _ = part.shape

    def body(part_ref, tot_ref, recv_ref, send1, recv1, send2, recv2):
        _, _, _, me = _position()

        def scatter(k, to_me=False):
            p = (me + N_DEV - k) % N_DEV if to_me else (me + k) % N_DEV
            return pltpu.make_async_remote_copy(
                src_ref=part_ref.at[me if to_me else p], dst_ref=recv_ref.at[p if to_me else me],
                send_sem=send1.at[k], recv_sem=recv1.at[k], device_id=_coords(p), device_id_type=MESH)

        def gather(k, to_me=False):
            p = (me + N_DEV - k) % N_DEV if to_me else (me + k) % N_DEV
            return pltpu.make_async_remote_copy(
                src_ref=tot_ref.at[me], dst_ref=tot_ref.at[p if to_me else me],
                send_sem=send2.at[k], recv_sem=recv2.at[k], device_id=_coords(p), device_id_type=MESH)

        for k in range(1, N_DEV):
            scatter(k).start()
        recv_ref[me] = part_ref[me]
        for k in range(1, N_DEV):
            scatter(k, to_me=True).wait_recv()
        total = recv_ref[0]
        for q in range(1, N_DEV):
            total = total + recv_ref[q]
        tot_ref[me] = total
        for k in range(1, N_DEV):
            gather(k).start()
        for k in range(1, N_DEV):
            gather(k, to_me=True).wait_recv()
        for k in range(1, N_DEV):
            scatter(k).wait_send()
            gather(k).wait_send()

    vmem = pl.BlockSpec(memory_space=pltpu.VMEM)
    return pl.pallas_call(
        body, name=name, in_specs=[vmem], out_specs=vmem, out_shape=jax.ShapeDtypeStruct(part.shape, F32),
        scratch_shapes=[pltpu.VMEM(part.shape, F32)] + [pltpu.SemaphoreType.DMA((N_DEV,))] * 4,
        compiler_params=pltpu.CompilerParams(has_side_effects=True, vmem_limit_bytes=VMEM_LIMIT),
    )(part)


def _round_up(n, m):
    return (n + m - 1) // m * m


def _row_tile(rows):
    return next(t for t in (256, 128, 64, 32, 16) if rows % t == 0)


def _local_step(x, target, small, full, *, qkw):
    l, d = x.shape
    depth = len(full)
    groups = d // SSM_GROUP
    tiles = groups // TILE_GROUPS
    half = qkw // RET_HEADS // 2
    cb0 = 2 * qkw // d
    inv = 1.0 / (ROPE_BASE ** (jnp.arange(half, dtype=F32) / half))
    ang = jnp.arange(l, dtype=F32)[:, None] * inv[None, :]
    cos, sin = jnp.cos(ang), jnp.sin(ang)
    rep = jnp.repeat(jnp.eye(SSM_STATE, dtype=F32), SSM_GROUP, axis=1)
    row2 = lambda a: a.reshape(1, -1)

    saved = []
    for i in range(depth):
        w_in, w_glu, w_out, w_gate, w_up, w_down = full[i]
        n = f"l{i}_"
        g_mix, g_ffn = row2(small["ln_mix_g"][i]), row2(small["ln_ffn_g"][i])
        dskip, b_glu = row2(small["ssm_d"][i]), row2(small["b_glu"][i])
        lg = small["ret_log_gamma"][i]
        h = _norm_fwd(x, g_mix, name=n + "norm_mix")
        proj = _matmul(h, w_in, mode="nn", out_dtype=F32, name=n + "proj")
        q_rot, k_rot, v_bf = _ret_prep(proj, cos, sin, qkw=qkw, d=d, name=n + "ret_prep")
        y_raw = _ret_fwd(lg, q_rot, k_rot, v_bf, name=n + "ret_fwd")
        par = [small["ssm_a_re"][i].reshape(2 * groups, SSM_STATE), small["ssm_a_im"][i].reshape(2 * groups, SSM_STATE),
               small["ssm_log_dt"][i].reshape(2 * groups, 1),
               small["ssm_b_re"][i].reshape(2 * groups, SSM_STATE * SSM_GROUP),
               small["ssm_b_im"][i].reshape(2 * groups, SSM_STATE * SSM_GROUP), rep]
        lam_re, lam_im, bbar_re, bbar_im = _s5_prep(*par, name=n + "s5_prep")
        s5 = [_b_tiles(bbar_re, tiles).astype(BF16), _b_tiles(bbar_im, tiles).astype(BF16),
              lam_re.reshape(2, tiles, 1, TILE_N), lam_im.reshape(2, tiles, 1, TILE_N),
              _c_tiles(small["ssm_c_re"][i], tiles).astype(BF16), _c_tiles(small["ssm_c_im"][i], tiles).astype(BF16)]
        u_seg = _to_segments(proj[:, (cb0 + 2) * d:(cb0 + 3) * d])
        y_seg = _s5_fwd(u_seg, *s5, d=0, name=n + "s5_fwd_f")
        y_seg = _s5_fwd(u_seg, *s5, d=1, add=y_seg, name=n + "s5_fwd_b")
        y_s5 = _from_segments(y_seg)
        ys, ys_bf = _s5_post(y_s5, proj, cb0 + 2, dskip, name=n + "s5_post")
        glu = _matmul(ys_bf, w_glu, mode="nn", out_dtype=F32, name=n + "glu")
        merged = _merge(y_raw, proj, ys, glu, b_glu, cb0=cb0, name=n + "merge")
        x1 = _matmul(merged, w_out, mode="nn", out_dtype=F32, res=x, name=n + "out")
        h2 = _norm_fwd(x1, g_ffn, name=n + "norm_ffn")
        gate = _matmul(h2, w_gate, mode="nn", out_dtype=F32, name=n + "gate")
        up = _matmul(h2, w_up, mode="nn", out_dtype=F32, name=n + "up")
        act = _ffn_act(gate, up, name=n + "act")
        x2 = _matmul(act, w_down, mode="nn", out_dtype=F32, res=x1, name=n + "down")
        saved.append(dict(x=x, h=h, proj=proj, q_rot=q_rot, k_rot=k_rot, v_bf=v_bf, y_raw=y_raw, par=par, s5=s5,
                          u_seg=u_seg, y_s5=y_s5, ys=ys, ys_bf=ys_bf, glu=glu, merged=merged, x1=x1, h2=h2, gate=gate,
                          up=up, act=act))
        x = x2

    dx, dg_final, loss = _final(x, row2(small["ln_final_g"]), target, name="final")

    big = [[None] * depth for _ in range(6)]
    sg = {k: [None] * depth for k in ("ln_mix_g", "ret_log_gamma", "ssm_a_re", "ssm_a_im", "ssm_log_dt", "ssm_b_re",
                                      "ssm_b_im", "ssm_c_re", "ssm_c_im", "ssm_d", "b_glu", "ln_ffn_g")}
    for i in reversed(range(depth)):
        s = saved[i]
        w_in, w_glu, w_out, w_gate, w_up, w_down = full[i]
        n = f"l{i}_b_"
        g_mix, g_ffn = row2(small["ln_mix_g"][i]), row2(small["ln_ffn_g"][i])
        dskip, b_glu = row2(small["ssm_d"][i]), row2(small["b_glu"][i])
        lg = small["ret_log_gamma"][i]
        dact = _matmul(dx, w_down, mode="nt", out_dtype=F32, name=n + "dact")
        big[5][i] = _matmul(s["act"], dx, mode="tn", out_dtype=BF16, name=n + "dw_down")
        dgate, dup = _ffn_act_bwd(s["gate"], s["up"], dact, name=n + "act")
        dh2 = _matmul(dgate, w_gate, mode="nt", out_dtype=F32, name=n + "dh2_gate")
        dh2 = _matmul(dup, w_up, mode="nt", out_dtype=F32, res=dh2, name=n + "dh2_up")
        big[3][i] = _matmul(s["h2"], dgate, mode="tn", out_dtype=BF16, name=n + "dw_gate")
        big[4][i] = _matmul(s["h2"], dup, mode="tn", out_dtype=BF16, name=n + "dw_up")
        dx1, dgf = _norm_bwd(s["x1"], g_ffn, dh2, dx, name=n + "norm_ffn")
        sg["ln_ffn_g"][i] = dgf[0]
        dmerged = _matmul(dx1, w_out, mode="nt", out_dtype=F32, name=n + "dmerged")
        big[2][i] = _matmul(s["merged"], dx1, mode="tn", out_dtype=BF16, name=n + "dw_out")
        dy_raw, dg, dgate_r, dgate_s, dglu, dys_a, db_glu = _merge_bwd(
            s["y_raw"], s["proj"], s["ys"], s["glu"], b_glu, dmerged, cb0=cb0, name=n + "merge")
        sg["b_glu"][i] = db_glu[0]
        dys = _matmul(dglu, w_glu, mode="nt", out_dtype=F32, res=dys_a, name=n + "dys")
        big[1][i] = _matmul(s["ys_bf"], dglu, mode="tn", out_dtype=BF16, name=n + "dw_glu")
        dpre, dd = _s5_post_bwd(s["y_s5"], s["proj"], cb0 + 2, dskip, dys, name=n + "s5_post")
        sg["ssm_d"][i] = dd[0]
        dpre_seg = _to_segments(dpre)
        r_f = _s5_bwd(s["u_seg"], dpre_seg, *s["s5"], d=0, name=n + "s5_bwd_f")
        r_b = _s5_bwd(s["u_seg"], dpre_seg, *s["s5"], d=1, add=r_f[0], name=n + "s5_bwd_b")
        du = _du_combine(dpre, dskip, _from_segments(r_b[0]), name=n + "du")
        both = lambda k: jnp.stack([r_f[k], r_b[k]])
        cts = [both(5).reshape(2 * groups, SSM_STATE), both(6).reshape(2 * groups, SSM_STATE),
               _b_untile(both(1), tiles), _b_untile(both(2), tiles)]
        da_re, da_im, dldt, db_re, db_im = _s5_prep_bwd(*s["par"], cts, name=n + "s5_prep")
        sg["ssm_a_re"][i] = da_re.reshape(2, groups, SSM_STATE)
        sg["ssm_a_im"][i] = da_im.reshape(2, groups, SSM_STATE)
        sg["ssm_log_dt"][i] = dldt.reshape(2, groups)
        sg["ssm_b_re"][i] = db_re.reshape(2, groups, SSM_STATE, SSM_GROUP)
        sg["ssm_b_im"][i] = db_im.reshape(2, groups, SSM_STATE, SSM_GROUP)
        sg["ssm_c_re"][i] = _c_untile(both(3), tiles)
        sg["ssm_c_im"][i] = _c_untile(both(4), tiles)
        dq_rot, dk_rot, dv, dlg = _ret_bwd(lg, s["q_rot"], s["k_rot"], s["v_bf"], dy_raw, name=n + "ret_bwd")
        sg["ret_log_gamma"][i] = dlg[:, :2, 0].T
        dq, dk = _ret_prep_bwd(dq_rot, dk_rot, cos, sin, name=n + "ret_prep")
        dproj = jnp.concatenate([dq, dk, dv.astype(BF16), dg, du, dgate_r, dgate_s], axis=1)
        dh = _matmul(dproj, w_in, mode="nt", out_dtype=F32, name=n + "dh")
        big[0][i] = _matmul(s["h"], dproj, mode="tn", out_dtype=BF16, name=n + "dw_in")
        dx, dgm = _norm_bwd(s["x"], g_mix, dh, dx1, name=n + "norm_mix")
        sg["ln_mix_g"][i] = dgm[0]

    small_grads = {k: jnp.stack(v) for k, v in sg.items()}
    small_grads["ln_final_g"] = dg_final[0]
    return loss, dx, big, small_grads


BIG = ("w_in", "w_glu", "w_out", "w_ffn_gate", "w_ffn_up", "w_ffn_down")
BIG_KINDS = ("col", "row", "row", "col", "col", "row")
SMALL = ("ln_mix_g", "ret_log_gamma", "ssm_a_re", "ssm_a_im", "ssm_log_dt", "ssm_b_re", "ssm_b_im", "ssm_c_re",
         "ssm_c_im", "ssm_d", "b_glu", "ln_ffn_g", "ln_final_g")
WEIGHTS = ("ln_mix_g", "w_in", "ret_log_gamma", "ssm_a_re", "ssm_a_im", "ssm_log_dt", "ssm_b_re", "ssm_b_im",
           "ssm_c_re", "ssm_c_im", "ssm_d", "w_glu", "b_glu", "w_out", "ln_ffn_g", "w_ffn_gate", "w_ffn_up",
           "w_ffn_down", "ln_final_g")


def _pad_to(a, axis, size):
    pad = [(0, 0)] * a.ndim
    pad[axis] = (0, size - a.shape[axis])
    return jnp.pad(a, pad)


def _flatten_small(tree, extra):
    def as_rows(a):
        a = a.reshape(-1).astype(F32)
        return _pad_to(a, 0, _round_up(a.shape[0], LANE)).reshape(-1, LANE)

    flat = jnp.concatenate([as_rows(tree[k]) for k in SMALL] + [as_rows(extra)])
    return _pad_to(flat, 0, _round_up(flat.shape[0], FLAT_ROWS))


def _unflatten_small(flat, like):
    out, at = {}, 0
    for k in SMALL:
        n = like[k].size
        rows = _round_up(n, LANE) // LANE
        out[k] = flat[at:at + rows].reshape(-1)[:n].reshape(like[k].shape)
        at += rows
    return out, flat[at, 0]


def kernel(x, ln_mix_g, w_in, ret_log_gamma, ssm_a_re, ssm_a_im, ssm_log_dt, ssm_b_re, ssm_b_im, ssm_c_re, ssm_c_im, ssm_d, w_glu, b_glu, w_out, ln_ffn_g, w_ffn_gate, w_ffn_up, w_ffn_down, ln_final_g, loss_target, m_ln_mix_g, m_w_in, m_ret_log_gamma, m_ssm_a_re, m_ssm_a_im, m_ssm_log_dt, m_ssm_b_re, m_ssm_b_im, m_ssm_c_re, m_ssm_c_im, m_ssm_d, m_w_glu, m_b_glu, m_w_out, m_ln_ffn_g, m_w_ffn_gate, m_w_ffn_up, m_w_ffn_down, m_ln_final_g, v_ln_mix_g, v_w_in, v_ret_log_gamma, v_ssm_a_re, v_ssm_a_im, v_ssm_log_dt, v_ssm_b_re, v_ssm_b_im, v_ssm_c_re, v_ssm_c_im, v_ssm_d, v_w_glu, v_b_glu, v_w_out, v_ln_ffn_g, v_w_ffn_gate, v_w_ffn_up, v_w_ffn_down, v_ln_final_g):
    w = dict(ln_mix_g=ln_mix_g, w_in=w_in, ret_log_gamma=ret_log_gamma, ssm_a_re=ssm_a_re, ssm_a_im=ssm_a_im, ssm_log_dt=ssm_log_dt, ssm_b_re=ssm_b_re, ssm_b_im=ssm_b_im, ssm_c_re=ssm_c_re, ssm_c_im=ssm_c_im, ssm_d=ssm_d, w_glu=w_glu, b_glu=b_glu, w_out=w_out, ln_ffn_g=ln_ffn_g, w_ffn_gate=w_ffn_gate, w_ffn_up=w_ffn_up, w_ffn_down=w_ffn_down, ln_final_g=ln_final_g)
    m = dict(ln_mix_g=m_ln_mix_g, w_in=m_w_in, ret_log_gamma=m_ret_log_gamma, ssm_a_re=m_ssm_a_re, ssm_a_im=m_ssm_a_im, ssm_log_dt=m_ssm_log_dt, ssm_b_re=m_ssm_b_re, ssm_b_im=m_ssm_b_im, ssm_c_re=m_ssm_c_re, ssm_c_im=m_ssm_c_im, ssm_d=m_ssm_d, w_glu=m_w_glu, b_glu=m_b_glu, w_out=m_w_out, ln_ffn_g=m_ln_ffn_g, w_ffn_gate=m_w_ffn_gate, w_ffn_up=m_w_ffn_up, w_ffn_down=m_w_ffn_down, ln_final_g=m_ln_final_g)
    v = dict(ln_mix_g=v_ln_mix_g, w_in=v_w_in, ret_log_gamma=v_ret_log_gamma, ssm_a_re=v_ssm_a_re, ssm_a_im=v_ssm_a_im, ssm_log_dt=v_ssm_log_dt, ssm_b_re=v_ssm_b_re, ssm_b_im=v_ssm_b_im, ssm_c_re=v_ssm_c_re, ssm_c_im=v_ssm_c_im, ssm_d=v_ssm_d, w_glu=v_w_glu, b_glu=v_b_glu, w_out=v_w_out, ln_ffn_g=v_ln_ffn_g, w_ffn_gate=v_w_ffn_gate, w_ffn_up=v_w_ffn_up, w_ffn_down=v_w_ffn_down, ln_final_g=v_ln_final_g)
    depth, d, nb_in = w_in.shape
    qkw = (nb_in * N_DEV - 5 * d) // 2
    nb_ffn = w_ffn_gate.shape[2]
    nb_pad = _round_up(nb_ffn, LANE)
    pad_axis = {"w_ffn_gate": 2, "w_ffn_up": 2, "w_ffn_down": 1}

    shards = []
    for i in range(depth):
        for k in BIG:
            s = w[k] if k not in pad_axis else _pad_to(w[k], pad_axis[k], nb_pad)
            shards.append(s[i].astype(BF16))
    gathered = _all_gather(shards, BIG_KINDS * depth, name="gather_weights")
    full = [gathered[i * len(BIG):(i + 1) * len(BIG)] for i in range(depth)]

    small = {k: w[k] for k in SMALL}
    loss, dx, big, small_grads = _local_step(x[0], loss_target[0], small, full, qkw=qkw)

    recv = _scatter_partials(big, BIG_KINDS, name="scatter_grads")
    grads, delta, new_m, new_v = {}, {}, {}, {}
    for t, k in enumerate(BIG):
        if k in ("w_ffn_gate", "w_ffn_up"):
            ops = [_pad_to(a[k], 2, nb_pad) for a in (w, m, v)]
            res = [r[:, :, :nb_ffn] for r in _adam_shard(recv[t], *ops, name="adam_" + k, tr=256)]
        else:
            res = _adam_shard(recv[t], w[k], m[k], v[k], name="adam_" + k, tr=_row_tile(w[k].shape[1]))
        grads[k], delta[k], new_m[k], new_v[k] = res

    part = _flatten_small(small_grads, loss[0, :1])
    rows = part.shape[0]
    total = _all_reduce(part.reshape(N_DEV, rows // N_DEV, LANE), name="reduce_small").reshape(rows, LANE)
    zero = jnp.zeros((1,), F32)
    flat = [_flatten_small({k: a[k] for k in SMALL}, zero) for a in (w, m, v)]
    upd = _adam_flat(total, *flat, name="adam_small")
    g_small, loss_total = _unflatten_small(total, small)
    grads.update(g_small)
    for dst, u in zip((delta, new_m, new_v), upd):
        dst.update(_unflatten_small(u, small)[0])

    return (loss_total, dx[None], *[grads[k] for k in WEIGHTS], *[delta[k] for k in WEIGHTS],
            *[new_m[k] for k in WEIGHTS], *[new_v[k] for k in WEIGHTS])
```

```python
import math

import jax
import jax.numpy as jnp
from jax import lax
from jax.experimental import pallas as pl
from jax.experimental.pallas import tpu as pltpu

F32 = jnp.float32
BF16 = jnp.bfloat16
MESH = pl.DeviceIdType.MESH

N_DEV = 8
RET_HEADS = 4
CHUNK = 128
ROPE_BASE = 10000.0
SSM_GROUP = 16
SSM_STATE = 64
TILE_GROUPS = 8
TILE_U = TILE_GROUPS * SSM_GROUP
TILE_N = TILE_GROUPS * SSM_STATE
LANE = 128
SUBLANE = 8
N_SEG = SUBLANE
N_LT = TILE_N // LANE
SCAN_UNROLL = 4
FLAT_ROWS = 1024
EPS = 1e-6
ADAM_LR = 0.001
ADAM_B1 = 0.9
ADAM_B2 = 0.999
ADAM_EPS = 1e-08
ADAM_WD = 0.01
ADAM_STEP = 10
VMEM_LIMIT = 56 * 1024 * 1024


def _params(*sem):
    return pltpu.CompilerParams(dimension_semantics=sem or None, vmem_limit_bytes=VMEM_LIMIT)


def _dg(a, b, ca, cb):
    return lax.dot_general(a.astype(BF16), b.astype(BF16), (((ca,), (cb,)), ((), ())),
                           preferred_element_type=F32)


@jax.custom_vjp
def _dnn(a, b):
    return _dg(a, b, 1, 0)


@jax.custom_vjp
def _dnt(a, b):
    return _dg(a, b, 1, 1)


@jax.custom_vjp
def _dtn(a, b):
    return _dg(a, b, 0, 0)


_dnn.defvjp(lambda a, b: (_dnn(a, b), (a, b)), lambda r, g: (_dnt(g, r[1]), _dtn(r[0], g)))
_dnt.defvjp(lambda a, b: (_dnt(a, b), (a, b)), lambda r, g: (_dnn(g, r[1]), _dtn(g, r[0])))
_dtn.defvjp(lambda a, b: (_dtn(a, b), (a, b)), lambda r, g: (_dnt(r[1], g), _dnn(r[0], g)))


def _matmul(a, b, *, mode, out_dtype, name, res=None, tm=1024, tn=1024, tk=512):
    if mode == "nn":
        (m, k), n = a.shape, b.shape[1]
    elif mode == "nt":
        (m, k), n = a.shape, b.shape[0]
    else:
        (k, m), n = a.shape, b.shape[1]
    tm, tn, tk = min(tm, m), min(tn, n), min(tk, k)
    assert m % tm == 0 and n % tn == 0 and k % tk == 0, (name, m, n, k)
    nk = k // tk
    if mode == "tn":
        a_spec = pl.BlockSpec((tk, tm), lambda i, j, kk: (kk, i))
    else:
        a_spec = pl.BlockSpec((tm, tk), lambda i, j, kk: (i, kk))
    if mode == "nt":
        b_spec = pl.BlockSpec((tn, tk), lambda i, j, kk: (j, kk))
    else:
        b_spec = pl.BlockSpec((tk, tn), lambda i, j, kk: (kk, j))
    ca, cb = {"nn": (1, 0), "nt": (1, 1), "tn": (0, 0)}[mode]
    o_spec = pl.BlockSpec((tm, tn), lambda i, j, kk: (i, j))
    has_res = res is not None

    def body(*refs):
        if has_res:
            a_ref, b_ref, r_ref, o_ref, acc = refs
        else:
            a_ref, b_ref, o_ref, acc = refs
        kk = pl.program_id(2)

        @pl.when(kk == 0)
        def _():
            acc[...] = r_ref[...] if has_res else jnp.zeros_like(acc)

        acc[...] += _dg(a_ref[...], b_ref[...], ca, cb)

        @pl.when(kk == nk - 1)
        def _():
            o_ref[...] = acc[...].astype(out_dtype)

    return pl.pallas_call(
        body, name=name, grid=(m // tm, n // tn, nk),
        in_specs=[a_spec, b_spec] + ([o_spec] if has_res else []),
        out_specs=o_spec, out_shape=jax.ShapeDtypeStruct((m, n), out_dtype),
        scratch_shapes=[pltpu.VMEM((tm, tn), F32)],
        compiler_params=_params("parallel", "parallel", "arbitrary"),
    )(*((a, b, res) if has_res else (a, b)))


def _row(arr, tl, width=None, cb=0):
    width = arr.shape[1] if width is None else width
    return arr, pl.BlockSpec((tl, width), lambda i, cb=cb: (i, cb))


def _par(arr):
    return arr, pl.BlockSpec(arr.shape, lambda i: (0,) * arr.ndim)


def _rowwise(body, *, rows, tl, ins, outs, name):
    arrays = [a for a, _ in ins]
    in_specs = [s for _, s in ins]
    out_shape, out_specs, acc_ids = [], [], []
    for n, o in enumerate(outs):
        if o[0] == "row":
            out_shape.append(jax.ShapeDtypeStruct((rows, o[1]), o[2]))
            out_specs.append(pl.BlockSpec((tl, o[1]), lambda i: (i, 0)))
        else:
            out_shape.append(jax.ShapeDtypeStruct((1, o[1]), F32))
            out_specs.append(pl.BlockSpec((1, o[1]), lambda i: (0, 0)))
            acc_ids.append(n)
    n_in = len(arrays)
    assert rows % tl == 0, (name, rows, tl)

    def wrapped(*refs):
        in_refs, out_refs = refs[:n_in], refs[n_in:]

        @pl.when(pl.program_id(0) == 0)
        def _():
            for n in acc_ids:
                out_refs[n][...] = jnp.zeros_like(out_refs[n])

        body(in_refs, out_refs)

    return pl.pallas_call(
        wrapped, name=name, grid=(rows // tl,), in_specs=in_specs, out_specs=out_specs,
        out_shape=out_shape, compiler_params=_params("arbitrary"),
    )(*arrays)


def _rms(x, g):
    return x * lax.rsqrt(jnp.mean(x * x, axis=-1, keepdims=True) + EPS) * g


def _norm_fwd(x, g, *, name, tl=256):
    def body(i, o):
        o[0][...] = _rms(i[0][...], i[1][...]).astype(BF16)

    return _rowwise(body, rows=x.shape[0], tl=tl, ins=[_row(x, tl), _par(g)],
                    outs=[("row", x.shape[1], BF16)], name=name)[0]


def _norm_bwd(x, g, dh, dres, *, name, tl=256):
    def body(i, o):
        _, vjp = jax.vjp(_rms, i[0][...], i[1][...])
        dx, dg = vjp(i[2][...])
        o[0][...] = i[3][...] + dx
        o[1][...] += dg

    d = x.shape[1]
    return _rowwise(body, rows=x.shape[0], tl=tl, ins=[_row(x, tl), _par(g), _row(dh, tl), _row(dres, tl)],
                    outs=[("row", d, F32), ("acc", d)], name=name)


def _final(x, g, target, *, name, tl=256):
    d = x.shape[1]

    def body(i, o):
        y, vjp = jax.vjp(_rms, i[0][...], i[1][...])
        err = y - i[2][...]
        dx, dg = vjp(err * (1.0 / d))
        o[0][...] = dx
        o[1][...] += dg
        o[2][...] += jnp.full((1, LANE), 0.5 / d, F32) * jnp.sum(err * err)

    return _rowwise(body, rows=x.shape[0], tl=tl, ins=[_row(x, tl), _par(g), _row(target, tl)],
                    outs=[("row", d, F32), ("acc", d), ("acc", LANE)], name=name)


def _rot(x, cos, sin, out_ref, col, scale=1.0, inverse=False):
    x1, x2 = x[:, :LANE], x[:, LANE:]
    if inverse:
        sin = -sin
    out_ref[:, col:col + LANE] = ((x1 * cos - x2 * sin) * scale).astype(out_ref.dtype)
    out_ref[:, col + LANE:col + 2 * LANE] = ((x1 * sin + x2 * cos) * scale).astype(out_ref.dtype)


def _ret_prep(proj, cos, sin, *, qkw, d, name, tl=256):
    dk = qkw // RET_HEADS
    assert dk == 2 * LANE and (2 * qkw) % d == 0

    def body(i, o):
        c, s = i[3][...], i[4][...]
        for h in range(RET_HEADS):
            _rot(i[0][:, h * dk:(h + 1) * dk], c, s, o[0], h * dk)
            _rot(i[1][:, h * dk:(h + 1) * dk], c, s, o[1], h * dk, scale=dk ** -0.5)
        o[2][...] = i[2][...].astype(BF16)

    return _rowwise(body, rows=proj.shape[0], tl=tl,
                    ins=[_row(proj, tl, qkw, 0), _row(proj, tl, qkw, 1), _row(proj, tl, d, 2 * qkw // d),
                         _row(cos, tl), _row(sin, tl)],
                    outs=[("row", qkw, BF16), ("row", qkw, BF16), ("row", d, BF16)], name=name)


def _ret_prep_bwd(dq_rot, dk_rot, cos, sin, *, name, tl=256):
    qkw = dq_rot.shape[1]
    dk = qkw // RET_HEADS

    def body(i, o):
        c, s = i[2][...], i[3][...]
        for h in range(RET_HEADS):
            _rot(i[0][:, h * dk:(h + 1) * dk], c, s, o[0], h * dk, inverse=True)
            _rot(i[1][:, h * dk:(h + 1) * dk], c, s, o[1], h * dk, scale=dk ** -0.5, inverse=True)

    return _rowwise(body, rows=dq_rot.shape[0], tl=tl,
                    ins=[_row(dq_rot, tl), _row(dk_rot, tl), _row(cos, tl), _row(sin, tl)],
                    outs=[("row", qkw, BF16), ("row", qkw, BF16)], name=name)


def _ret_weights(lgf, lgb):
    t = lax.broadcasted_iota(jnp.int32, (CHUNK, 1), 0).astype(F32)
    diff = (lax.broadcasted_iota(jnp.int32, (CHUNK, CHUNK), 0)
            - lax.broadcasted_iota(jnp.int32, (CHUNK, CHUNK), 1)).astype(F32)
    dmat = jnp.exp(jnp.where(diff >= 0, lgf * diff, -lgb * diff))
    return dict(dmat=dmat, wqf=jnp.exp(lgf * (t + 1.0)), wkf=jnp.exp(lgf * (CHUNK - 1.0 - t)),
                wqb=jnp.exp(lgb * (CHUNK - t)), wkb=jnp.exp(lgb * t))


def _ret_f_part(q, k, v, lgf, lgb, s_f):
    w = _ret_weights(lgf, lgb)
    y = _dnn(_dnt(q, k) * w["dmat"], v) + _dnn(q * w["wqf"], s_f)
    return y, _dtn(k * w["wkf"], v)


def _ret_b_part(q, k, v, lgb, s_b):
    w = _ret_weights(lgb, lgb)
    return _dnn(q * w["wqb"], s_b), _dtn(k * w["wkb"], v)


def _chunk(c):
    return pl.ds(pl.multiple_of(c * CHUNK, CHUNK), CHUNK)


def _ret_specs(l, qkw, d):
    dk, dv = qkw // RET_HEADS, d // RET_HEADS
    return dk, dv, [pl.BlockSpec(memory_space=pltpu.SMEM),
                    pl.BlockSpec((l, dk), lambda h: (0, h)), pl.BlockSpec((l, dk), lambda h: (0, h)),
                    pl.BlockSpec((l, dv), lambda h: (0, h))]


def _ret_fwd(lg, q, k, v, *, name):
    l, qkw = q.shape
    d = v.shape[1]
    nc = l // CHUNK
    dk, dv, in_specs = _ret_specs(l, qkw, d)

    def body(lg_ref, q_ref, k_ref, v_ref, y_ref, s_ref):
        h = pl.program_id(0)
        lgf = jnp.full((1, 1), lg_ref[0, h], F32)
        lgb = jnp.full((1, 1), lg_ref[1, h], F32)
        dec_f, dec_b = jnp.exp(lgf * CHUNK), jnp.exp(lgb * CHUNK)

        def load(c):
            r = _chunk(c)
            return r, q_ref[r, :].astype(F32), k_ref[r, :].astype(F32), v_ref[r, :].astype(F32)

        s_ref[...] = jnp.zeros_like(s_ref)

        def f_step(c, _):
            r, qc, kc, vc = load(c)
            y, kv = _ret_f_part(qc, kc, vc, lgf, lgb, s_ref[...])
            y_ref[r, :] = y
            s_ref[...] = dec_f * s_ref[...] + kv
            return 0

        lax.fori_loop(0, nc, f_step, 0)
        s_ref[...] = jnp.zeros_like(s_ref)

        def b_step(n, _):
            r, qc, kc, vc = load(nc - 1 - n)
            y, kv = _ret_b_part(qc, kc, vc, lgb, s_ref[...])
            y_ref[r, :] += y
            s_ref[...] = dec_b * s_ref[...] + kv
            return 0

        lax.fori_loop(0, nc, b_step, 0)

    return pl.pallas_call(
        body, name=name, grid=(RET_HEADS,), in_specs=in_specs,
        out_specs=pl.BlockSpec((l, dv), lambda h: (0, h)), out_shape=jax.ShapeDtypeStruct((l, d), F32),
        scratch_shapes=[pltpu.VMEM((dk, dv), F32)], compiler_params=_params("arbitrary"),
    )(lg, q, k, v)


def _ret_bwd(lg, q, k, v, dy, *, name):
    l, qkw = q.shape
    d = v.shape[1]
    nc = l // CHUNK
    dk, dv, in_specs = _ret_specs(l, qkw, d)

    def body(lg_ref, q_ref, k_ref, v_ref, dy_ref, dq_ref, dk_ref, dv_ref, dlg_ref, states, s_ref, sh_ref):
        h = pl.program_id(0)
        lgf = jnp.full((1, 1), lg_ref[0, h], F32)
        lgb = jnp.full((1, 1), lg_ref[1, h], F32)
        dec_f, dec_b = jnp.exp(lgf * CHUNK), jnp.exp(lgb * CHUNK)

        def load(c):
            r = _chunk(c)
            return (r, q_ref[r, :].astype(F32), k_ref[r, :].astype(F32), v_ref[r, :].astype(F32),
                    dy_ref[r, :].astype(F32))

        s_ref[...] = jnp.zeros_like(s_ref)

        def f_states(c, _):
            _, qc, kc, vc, _ = load(c)
            states[c] = s_ref[...]
            w = _ret_weights(lgf, lgb)
            s_ref[...] = dec_f * s_ref[...] + _dtn(kc * w["wkf"], vc)
            return 0

        lax.fori_loop(0, nc, f_states, 0)
        sh_ref[...] = jnp.zeros_like(sh_ref)

        def f_adj(n, carry):
            dlf, dlb, ddec = carry
            c = nc - 1 - n
            r, qc, kc, vc, dyc = load(c)
            sc = states[c]
            _, vjp = jax.vjp(_ret_f_part, qc, kc, vc, lgf, lgb, sc)
            dq, dkk, dvv, g_f, g_b, dsc = vjp((dyc, sh_ref[...]))
            dq_ref[r, :] = dq
            dk_ref[r, :] = dkk
            dv_ref[r, :] = dvv
            ddec = ddec + jnp.sum(sh_ref[...] * sc)
            sh_ref[...] = dsc + dec_f * sh_ref[...]
            return dlf + g_f, dlb + g_b, ddec

        z = jnp.zeros((1, 1), F32)
        dlf, dlb, ddec_f = lax.fori_loop(0, nc, f_adj, (z, z, z))

        s_ref[...] = jnp.zeros_like(s_ref)

        def b_states(n, _):
            c = nc - 1 - n
            _, qc, kc, vc, _ = load(c)
            states[c] = s_ref[...]
            w = _ret_weights(lgb, lgb)
            s_ref[...] = dec_b * s_ref[...] + _dtn(kc * w["wkb"], vc)
            return 0

        lax.fori_loop(0, nc, b_states, 0)
        sh_ref[...] = jnp.zeros_like(sh_ref)

        def b_adj(c, carry):
            dlb, ddec = carry
            r, qc, kc, vc, dyc = load(c)
            sc = states[c]
            _, vjp = jax.vjp(_ret_b_part, qc, kc, vc, lgb, sc)
            dq, dkk, dvv, g_b, dsc = vjp((dyc, sh_ref[...]))
            dq_ref[r, :] += dq
            dk_ref[r, :] += dkk
            dv_ref[r, :] += dvv
            ddec = ddec + jnp.sum(sh_ref[...] * sc)
            sh_ref[...] = dsc + dec_b * sh_ref[...]
            return dlb + g_b, ddec

        dlb, ddec_b = lax.fori_loop(0, nc, b_adj, (dlb, z))
        dlf = dlf + ddec_f * dec_f * CHUNK
        dlb = dlb + ddec_b * dec_b * CHUNK
        row = lax.broadcasted_iota(jnp.int32, (SUBLANE, LANE), 0)
        dlg_ref[...] = jnp.where(row == 0, dlf, jnp.where(row == 1, dlb, 0.0))

    head = lambda w: pl.BlockSpec((l, w), lambda h: (0, h))
    return pl.pallas_call(
        body, name=name, grid=(RET_HEADS,), in_specs=in_specs + [head(dv)],
        out_specs=[head(dk), head(dk), head(dv), pl.BlockSpec((None, SUBLANE, LANE), lambda h: (h, 0, 0))],
        out_shape=[jax.ShapeDtypeStruct((l, qkw), F32), jax.ShapeDtypeStruct((l, qkw), F32),
                   jax.ShapeDtypeStruct((l, d), F32), jax.ShapeDtypeStruct((RET_HEADS, SUBLANE, LANE), F32)],
        scratch_shapes=[pltpu.VMEM((nc, dk, dv), F32), pltpu.VMEM((dk, dv), F32), pltpu.VMEM((dk, dv), F32)],
        compiler_params=_params("arbitrary"),
    )(lg, q, k, v, dy)


def _s5_param_fn(a_re, a_im, log_dt, b_re, b_im, rep):
    dt = jnp.exp(log_dt)
    mag = jnp.exp(a_re * dt)
    lam_re, lam_im = mag * jnp.cos(a_im * dt), mag * jnp.sin(a_im * dt)
    n_re, n_im = lam_re - 1.0, lam_im
    den = a_re * a_re + a_im * a_im
    c_re = (n_re * a_re + n_im * a_im) / den
    c_im = (n_im * a_re - n_re * a_im) / den
    hi = lax.Precision.HIGHEST
    c_re = jnp.dot(c_re, rep, precision=hi, preferred_element_type=F32)
    c_im = jnp.dot(c_im, rep, precision=hi, preferred_element_type=F32)
    return lam_re, lam_im, c_re * b_re - c_im * b_im, c_re * b_im + c_im * b_re


def _s5_param_shapes(a_re, b_re):
    r, p = a_re.shape
    return [jax.ShapeDtypeStruct((r, p), F32)] * 2 + [jax.ShapeDtypeStruct(b_re.shape, F32)] * 2


def _s5_prep(a_re, a_im, log_dt, b_re, b_im, rep, *, name):
    def body(*refs):
        outs = _s5_param_fn(*[r[...] for r in refs[:6]])
        for o_ref, o in zip(refs[6:], outs):
            o_ref[...] = o

    return pl.pallas_call(body, name=name, out_shape=_s5_param_shapes(a_re, b_re),
                          compiler_params=_params())(a_re, a_im, log_dt, b_re, b_im, rep)


def _s5_prep_bwd(a_re, a_im, log_dt, b_re, b_im, rep, cts, *, name):
    def body(*refs):
        ins = [r[...] for r in refs[:6]]
        _, vjp = jax.vjp(lambda *p: _s5_param_fn(*p, ins[5]), *ins[:5])
        grads = vjp(tuple(r[...] for r in refs[6:10]))
        for o_ref, o in zip(refs[10:], grads):
            o_ref[...] = o

    shapes = [jax.ShapeDtypeStruct(t.shape, F32) for t in (a_re, a_im, log_dt, b_re, b_im)]
    return pl.pallas_call(body, name=name, out_shape=shapes,
                          compiler_params=_params())(a_re, a_im, log_dt, b_re, b_im, rep, *cts)


def _eye_tiles():
    return jnp.eye(TILE_GROUPS, dtype=F32)


def _b_tiles(bbar, tiles):
    t = bbar.reshape(2, tiles, TILE_GROUPS, SSM_STATE, SSM_GROUP).transpose(0, 1, 2, 4, 3)
    t = t[:, :, :, :, None, :] * _eye_tiles()[None, None, :, None, :, None]
    return t.reshape(2, tiles, TILE_U, TILE_N)


def _b_untile(dbt, tiles):
    t = dbt.reshape(2, tiles, TILE_GROUPS, SSM_GROUP, TILE_GROUPS, SSM_STATE)
    t = (t * _eye_tiles()[None, None, :, None, :, None]).sum(axis=4)
    return t.transpose(0, 1, 2, 4, 3).reshape(2 * tiles * TILE_GROUPS, SSM_STATE * SSM_GROUP)


def _c_tiles(c, tiles):
    t = c.reshape(2, tiles, TILE_GROUPS, SSM_GROUP, SSM_STATE).transpose(0, 1, 2, 4, 3)
    t = t[:, :, :, :, None, :] * _eye_tiles()[None, None, :, None, :, None]
    return t.reshape(2, tiles, TILE_N, TILE_U)


def _c_untile(dct, tiles):
    t = dct.reshape(2, tiles, TILE_GROUPS, SSM_STATE, TILE_GROUPS, SSM_GROUP)
    t = (t * _eye_tiles()[None, None, :, None, :, None]).sum(axis=4)
    return t.transpose(0, 1, 2, 4, 3).reshape(2, tiles * TILE_GROUPS, SSM_GROUP, SSM_STATE)


def _to_segments(a):
    l, w = a.shape
    return a.reshape(N_SEG, l // N_SEG, w).transpose(1, 0, 2).reshape(l, w)


def _from_segments(a):
    l, w = a.shape
    return a.reshape(l // N_SEG, N_SEG, w).transpose(1, 0, 2).reshape(l, w)


def _s5_scan(xr, xi, a_re, a_im, *, length, reverse, shifted=None):
    ls = length // N_SEG
    assert ls * N_SEG == length and ls & (ls - 1) == 0
    ar = [jnp.broadcast_to(a_re[:, c * LANE:(c + 1) * LANE], (N_SEG, LANE)) for c in range(N_LT)]
    ai = [jnp.broadcast_to(a_im[:, c * LANE:(c + 1) * LANE], (N_SEG, LANE)) for c in range(N_LT)]
    zero = jnp.zeros((N_SEG, LANE), F32)
    row = lax.broadcasted_iota(jnp.int32, (N_SEG, LANE), 0)

    def step_of(n):
        return (ls - 1 - n) if reverse else n

    def block(j):
        return pl.ds(j * N_SEG, N_SEG) if isinstance(j, int) else pl.ds(pl.multiple_of(j * N_SEG, N_SEG), N_SEG)

    def local(n, carry):
        rows = block(step_of(n))
        new = []
        for c in range(N_LT):
            cr, ci = carry[2 * c], carry[2 * c + 1]
            nr = ar[c] * cr - ai[c] * ci + xr[c, rows, :]
            ni = ar[c] * ci + ai[c] * cr + xi[c, rows, :]
            xr[c, rows, :] = nr
            xi[c, rows, :] = ni
            new += [nr, ni]
        return tuple(new)

    ends = lax.fori_loop(0, ls, local, (zero,) * (2 * N_LT), unroll=SCAN_UNROLL)

    init = []
    for c in range(N_LT):
        pr, pi = ar[c][0:1, :], ai[c][0:1, :]
        for _ in range(ls.bit_length() - 1):
            pr, pi = pr * pr - pi * pi, 2.0 * pr * pi
        cr = ci = jnp.zeros((1, LANE), F32)
        ir, ii = zero, zero
        for s in (range(N_SEG - 1, -1, -1) if reverse else range(N_SEG)):
            ir = jnp.where(row == s, cr, ir)
            ii = jnp.where(row == s, ci, ii)
            er, ei = ends[2 * c][s:s + 1, :], ends[2 * c + 1][s:s + 1, :]
            cr, ci = pr * cr - pi * ci + er, pr * ci + pi * cr + ei
        init += [ir, ii]

    def fix(n, carry, last=False):
        j = step_of(n)
        rows = block(j)
        new, sums = [], []
        for c in range(N_LT):
            cr, ci = carry[2 * c], carry[2 * c + 1]
            nr = ar[c] * cr - ai[c] * ci
            ni = ar[c] * ci + ai[c] * cr
            fr = xr[c, rows, :] + nr
            fi = xi[c, rows, :] + ni
            xr[c, rows, :] = fr
            xi[c, rows, :] = fi
            new += [nr, ni]
            if shifted is not None:
                yr, yi, shift = shifted
                if not last:
                    srows = block(j + shift)
                    sr, si = yr[c, srows, :], yi[c, srows, :]
                else:
                    edge = block(ls - 1 if shift < 0 else 0)
                    move, gone = (1, 0) if shift < 0 else (N_SEG - 1, N_SEG - 1)
                    sr = jnp.where(row == gone, 0.0, pltpu.roll(yr[c, edge, :], move, 0))
                    si = jnp.where(row == gone, 0.0, pltpu.roll(yi[c, edge, :], move, 0))
                sums += [carry[2 * N_LT + 2 * c] + fr * sr + fi * si,
                         carry[2 * N_LT + 2 * c + 1] + fi * sr - fr * si]
        return tuple(new + sums)

    if shifted is None:
        lax.fori_loop(0, ls, fix, tuple(init), unroll=SCAN_UNROLL)
        return ()
    assert shifted[2] == (-1 if reverse else 1)
    out = lax.fori_loop(0, ls - 1, fix, tuple(init) + (zero,) * (2 * N_LT), unroll=SCAN_UNROLL)
    return fix(ls - 1, out, last=True)[2 * N_LT:]


def _s5_tile_specs(l, d):
    tile = lambda r, c: pl.BlockSpec((None, None, r, c), lambda t, d=d: (d, t, 0, 0))
    return [pl.BlockSpec((l, TILE_U), lambda t: (0, t)), tile(TILE_U, TILE_N), tile(TILE_U, TILE_N),
            tile(1, TILE_N), tile(1, TILE_N), tile(TILE_N, TILE_U), tile(TILE_N, TILE_U)]


def _lanes(c):
    return slice(c * LANE, (c + 1) * LANE)


def _s5_fwd(u, bt_re, bt_im, lam_re, lam_im, ct_re, ct_im, *, d, name, add=None):
    l = u.shape[0]
    tiles = bt_re.shape[1]
    col = pl.BlockSpec((l, TILE_U), lambda t: (0, t))
    has_add = add is not None

    def body(*refs):
        u_ref, bre, bim, lre, lim, cre, cim = refs[:7]
        y_ref, xr, xi = refs[-3:]
        uu = u_ref[...]
        bu_re, bu_im = _dg(uu, bre[...], 1, 0), _dg(uu, bim[...], 1, 0)
        for c in range(N_LT):
            xr[c] = bu_re[:, _lanes(c)]
            xi[c] = bu_im[:, _lanes(c)]
        _s5_scan(xr, xi, lre[...], lim[...], length=l, reverse=(d == 1))
        y = refs[7][...] if has_add else jnp.zeros((l, TILE_U), F32)
        for c in range(N_LT):
            y = y + _dg(xr[c], cre[_lanes(c), :], 1, 0) - _dg(xi[c], cim[_lanes(c), :], 1, 0)
        y_ref[...] = y

    return pl.pallas_call(
        body, name=name, grid=(tiles,), in_specs=_s5_tile_specs(l, d) + [col] * has_add, out_specs=col,
        out_shape=jax.ShapeDtypeStruct((l, tiles * TILE_U), F32),
        scratch_shapes=[pltpu.VMEM((N_LT, l, LANE), F32)] * 2, compiler_params=_params("arbitrary"),
    )(u, bt_re, bt_im, lam_re, lam_im, ct_re, ct_im, *([add] if has_add else []))


def _s5_bwd(u, dy, bt_re, bt_im, lam_re, lam_im, ct_re, ct_im, *, d, name, add=None):
    l = u.shape[0]
    tiles = bt_re.shape[1]
    col = pl.BlockSpec((l, TILE_U), lambda t: (0, t))
    reverse = d == 1
    has_add = add is not None

    def body(*refs):
        u_ref, bre, bim, lre, lim, cre, cim, dy_ref = refs[:8]
        du_ref, dbre, dbim, dcre, dcim, dlre, dlim, xr, xi, gr, gi = refs[-11:]
        uu, dyy = u_ref[...], dy_ref[...]
        bu_re, bu_im = _dg(uu, bre[...], 1, 0), _dg(uu, bim[...], 1, 0)
        for c in range(N_LT):
            xr[c] = bu_re[:, _lanes(c)]
            xi[c] = bu_im[:, _lanes(c)]
        _s5_scan(xr, xi, lre[...], lim[...], length=l, reverse=reverse)
        gy_re, gy_im = _dg(dyy, cre[...], 1, 1), -_dg(dyy, cim[...], 1, 1)
        for c in range(N_LT):
            gr[c] = gy_re[:, _lanes(c)]
            gi[c] = gy_im[:, _lanes(c)]
        sums = _s5_scan(gr, gi, lre[...], -lim[...], length=l, reverse=not reverse,
                        shifted=(xr, xi, 1 if reverse else -1))
        du = refs[8][...] if has_add else jnp.zeros((l, TILE_U), F32)
        for c in range(N_LT):
            dlre[:, _lanes(c)] = jnp.sum(sums[2 * c], axis=0, keepdims=True)
            dlim[:, _lanes(c)] = jnp.sum(sums[2 * c + 1], axis=0, keepdims=True)
            g_re, g_im = gr[c], gi[c]
            du = du + _dg(g_re, bre[:, _lanes(c)], 1, 1) + _dg(g_im, bim[:, _lanes(c)], 1, 1)
            dbre[:, _lanes(c)] = _dg(uu, g_re, 0, 0)
            dbim[:, _lanes(c)] = _dg(uu, g_im, 0, 0)
            dcre[_lanes(c), :] = _dg(xr[c], dyy, 0, 0)
            dcim[_lanes(c), :] = -_dg(xi[c], dyy, 0, 0)
        du_ref[...] = du

    out3 = lambda r, c: pl.BlockSpec((None, r, c), lambda t: (t, 0, 0))
    f = lambda *s: jax.ShapeDtypeStruct(s, F32)
    return pl.pallas_call(
        body, name=name, grid=(tiles,), in_specs=_s5_tile_specs(l, d) + [col] + [col] * has_add,
        out_specs=[col, out3(TILE_U, TILE_N), out3(TILE_U, TILE_N),
                   out3(TILE_N, TILE_U), out3(TILE_N, TILE_U), out3(1, TILE_N), out3(1, TILE_N)],
        out_shape=[f(l, tiles * TILE_U), f(tiles, TILE_U, TILE_N), f(tiles, TILE_U, TILE_N),
                   f(tiles, TILE_N, TILE_U), f(tiles, TILE_N, TILE_U), f(tiles, 1, TILE_N), f(tiles, 1, TILE_N)],
        scratch_shapes=[pltpu.VMEM((N_LT, l, LANE), F32)] * 4, compiler_params=_params("arbitrary"),
    )(u, bt_re, bt_im, lam_re, lam_im, ct_re, ct_im, dy, *([add] if has_add else []))


def _s5_post(y, proj, u_cb, dskip, *, name, tl=256):
    d = y.shape[1]

    def body(i, o):
        ys = jax.nn.gelu(i[0][...] + i[2][...] * i[1][...])
        o[0][...] = ys
        o[1][...] = ys.astype(BF16)

    return _rowwise(body, rows=y.shape[0], tl=tl, ins=[_row(y, tl), _row(proj, tl, d, u_cb), _par(dskip)],
                    outs=[("row", d, F32), ("row", d, BF16)], name=name)


def _s5_post_bwd(y, proj, u_cb, dskip, dys, *, name, tl=256):
    d = y.shape[1]

    def body(i, o):
        u_ = i[1][...]
        _, vjp = jax.vjp(jax.nn.gelu, i[0][...] + i[2][...] * u_)
        (dpre,) = vjp(i[3][...])
        o[0][...] = dpre
        o[1][...] += jnp.sum(dpre * u_, axis=0, keepdims=True)

    return _rowwise(body, rows=y.shape[0], tl=tl,
                    ins=[_row(y, tl), _row(proj, tl, d, u_cb), _par(dskip), _row(dys, tl)],
                    outs=[("row", d, F32), ("acc", d)], name=name)


def _du_combine(dpre, dskip, du_s5, *, name, tl=256):
    d = dpre.shape[1]

    def body(i, o):
        o[0][...] = (i[0][...] * i[1][...] + i[2][...]).astype(BF16)

    return _rowwise(body, rows=dpre.shape[0], tl=tl, ins=[_row(dpre, tl), _par(dskip), _row(du_s5, tl)],
                    outs=[("row", d, BF16)], name=name)[0]


def _merge_fn(y, g, gate_r, gate_s, ys, glu, b):
    ret = jax.nn.silu(g) * (y * lax.rsqrt(jnp.mean(y * y, axis=-1, keepdims=True) + EPS))
    ssm = ys * jax.nn.sigmoid(glu + b)
    return jax.nn.sigmoid(gate_r) * ret + jax.nn.sigmoid(gate_s) * ssm


def _merge_ins(y_raw, proj, ys, glu, b_glu, cb0, tl):
    d = y_raw.shape[1]
    return [_row(y_raw, tl), _row(proj, tl, d, cb0 + 1), _row(proj, tl, d, cb0 + 3), _row(proj, tl, d, cb0 + 4),
            _row(ys, tl), _row(glu, tl), _par(b_glu)]


def _merge(y_raw, proj, ys, glu, b_glu, *, cb0, name, tl=128):
    d = y_raw.shape[1]
    dv = d // RET_HEADS

    def body(i, o):
        for h in range(RET_HEADS):
            cs = slice(h * dv, (h + 1) * dv)
            o[0][:, cs] = _merge_fn(*[r[:, cs] for r in i]).astype(BF16)

    return _rowwise(body, rows=y_raw.shape[0], tl=tl, ins=_merge_ins(y_raw, proj, ys, glu, b_glu, cb0, tl),
                    outs=[("row", d, BF16)], name=name)[0]


def _merge_bwd(y_raw, proj, ys, glu, b_glu, dmerged, *, cb0, name, tl=128):
    d = y_raw.shape[1]
    dv = d // RET_HEADS

    def body(i, o):
        for h in range(RET_HEADS):
            cs = slice(h * dv, (h + 1) * dv)
            _, vjp = jax.vjp(_merge_fn, *[r[:, cs] for r in i[:7]])
            dy, dg, dgr, dgs, dys, dglu, db = vjp(i[7][:, cs])
            o[0][:, cs] = dy.astype(BF16)
            o[1][:, cs] = dg.astype(BF16)
            o[2][:, cs] = dgr.astype(BF16)
            o[3][:, cs] = dgs.astype(BF16)
            o[4][:, cs] = dglu.astype(BF16)
            o[5][:, cs] = dys
            o[6][:, cs] += db

    return _rowwise(body, rows=y_raw.shape[0], tl=tl,
                    ins=_merge_ins(y_raw, proj, ys, glu, b_glu, cb0, tl) + [_row(dmerged, tl)],
                    outs=[("row", d, BF16)] * 5 + [("row", d, F32), ("acc", d)], name=name)


def _ffn_act_fn(gate, up):
    return jax.nn.silu(gate) * up


def _ffn_act(gate, up, *, name, tl=128):
    def body(i, o):
        o[0][...] = _ffn_act_fn(i[0][...], i[1][...]).astype(BF16)

    return _rowwise(body, rows=gate.shape[0], tl=tl, ins=[_row(gate, tl), _row(up, tl)],
                    outs=[("row", gate.shape[1], BF16)], name=name)[0]


def _ffn_act_bwd(gate, up, dact, *, name, tl=128):
    def body(i, o):
        _, vjp = jax.vjp(_ffn_act_fn, i[0][...], i[1][...])
        dgate, dup = vjp(i[2][...])
        o[0][...] = dgate.astype(BF16)
        o[1][...] = dup.astype(BF16)

    w = gate.shape[1]
    return _rowwise(body, rows=gate.shape[0], tl=tl, ins=[_row(gate, tl), _row(up, tl), _row(dact, tl)],
                    outs=[("row", w, BF16), ("row", w, BF16)], name=name)


def _adamw(w, g, m, v):
    m = ADAM_B1 * m + (1.0 - ADAM_B1) * g
    v = ADAM_B2 * v + (1.0 - ADAM_B2) * (g * g)
    m_hat = m / (1.0 - ADAM_B1 ** ADAM_STEP)
    v_hat = v / (1.0 - ADAM_B2 ** ADAM_STEP)
    return -ADAM_LR * (m_hat / (jnp.sqrt(v_hat) + ADAM_EPS) + ADAM_WD * w), m, v


def _adam_flat(g, w, m, v, *, name, tr=FLAT_ROWS):
    def body(i, o):
        for o_ref, val in zip(o, _adamw(i[1][...], i[0][...], i[2][...], i[3][...])):
            o_ref[...] = val

    return _rowwise(body, rows=g.shape[0], tl=tr, ins=[_row(a, tr) for a in (g, w, m, v)],
                    outs=[("row", LANE, F32)] * 3, name=name)


def _adam_shard(recv, w, m, v, *, name, tr):
    depth, r, c = w.shape
    assert r % tr == 0 and recv.shape[3] == c
    blk = pl.BlockSpec((None, tr, c), lambda l, i: (l, i, 0))

    def body(recv_ref, w_ref, m_ref, v_ref, g_ref, d_ref, nm_ref, nv_ref):
        g = recv_ref[0].astype(F32)
        for p in range(1, N_DEV):
            g = g + recv_ref[p].astype(F32)
        g_ref[...] = g
        d_ref[...], nm_ref[...], nv_ref[...] = _adamw(w_ref[...], g, m_ref[...], v_ref[...])

    return pl.pallas_call(
        body, name=name, grid=(depth, r // tr),
        in_specs=[pl.BlockSpec((None, N_DEV, tr, c), lambda l, i: (l, 0, i, 0)), blk, blk, blk],
        out_specs=[blk] * 4, out_shape=[jax.ShapeDtypeStruct(w.shape, F32)] * 4,
        compiler_params=_params("parallel", "parallel"),
    )(recv, w, m, v)


def _position():
    x, y, c = lax.axis_index("x"), lax.axis_index("y"), lax.axis_index("c")
    return x, y, c, 4 * x + 2 * y + c


def _coords(p):
    return p // 4, (p // 2) % 2, p % 2


def _block_of(ref, kind, p, nb):
    if kind == "col":
        return ref.at[:, pl.ds(pl.multiple_of(p * nb, LANE), nb)]
    return ref.at[pl.ds(pl.multiple_of(p * nb, SUBLANE), nb), :]


def _all_gather(shards, kinds, *, name):
    n = len(shards)
    out_shape = []
    for s, kind in zip(shards, kinds):
        r, c = s.shape
        out_shape.append(jax.ShapeDtypeStruct((r, c * N_DEV) if kind == "col" else (r * N_DEV, c), s.dtype))

    def body(*refs):
        shard_refs, full_refs = refs[:n], refs[n:2 * n]
        send_sems, recv_sems, local_sems = refs[2 * n:]
        x, y, c, me = _position()
        sibling = (x, y, 1 - c)
        chips = [(1 - x, y), (x, 1 - y), (1 - x, 1 - y)]

        def block(t, dev):
            nb = shards[t].shape[1 if kinds[t] == "col" else 0]
            return _block_of(full_refs[t], kinds[t], 4 * dev[0] + 2 * dev[1] + dev[2], nb)

        def copy(t, k, dev, to, src=None):
            return pltpu.make_async_remote_copy(
                src_ref=block(t, dev) if src is None else src, dst_ref=block(t, dev),
                send_sem=send_sems.at[t, k], recv_sem=recv_sems.at[t, k], device_id=to, device_id_type=MESH)

        mine, first, passed = [], [], []
        for t in range(n):
            mine.append(pltpu.make_async_copy(shard_refs[t], block(t, (x, y, c)), local_sems.at[t]))
            mine[-1].start()
            first.append(copy(t, 0, (x, y, c), sibling, src=shard_refs[t]))
            first += [copy(t, 1 + j, (x, y, c), (*chip, c), src=shard_refs[t]) for j, chip in enumerate(chips)]
        for cp in first:
            cp.start()
        for j, chip in enumerate(chips):
            for t in range(n):
                copy(t, 1 + j, (*chip, c), (x, y, c)).wait_recv()
                passed.append(copy(t, 4 + j, (*chip, c), sibling))
                passed[-1].start()
        for t in range(n):
            copy(t, 0, sibling, (x, y, c)).wait_recv()
            for j, chip in enumerate(chips):
                copy(t, 4 + j, (*chip, 1 - c), (x, y, c)).wait_recv()
        for cp in first + passed:
            cp.wait_send()
        for cp in mine:
            cp.wait()

    any_spec = pl.BlockSpec(memory_space=pl.ANY)
    return pl.pallas_call(
        body, name=name, in_specs=[any_spec] * n, out_specs=[any_spec] * n, out_shape=out_shape,
        scratch_shapes=[pltpu.SemaphoreType.DMA((n, N_DEV - 1)), pltpu.SemaphoreType.DMA((n, N_DEV - 1)),
                        pltpu.SemaphoreType.DMA((n,))],
        compiler_params=pltpu.CompilerParams(has_side_effects=True),
    )(*shards)


class _GatherRoute:
    def __init__(self, shards, kinds):
        self.kinds = kinds
        self.nb = [s.shape[1 if k == "col" else 0] for s, k in zip(shards, kinds)]

    def lands(self, shards):
        return [lax.empty((s.shape[0], s.shape[1] * N_DEV) if k == "col" else (s.shape[0] * N_DEV, s.shape[1]), s.dtype)
                for s, k in zip(shards, self.kinds)]

    def src(self, t, src_refs, me, p):
        return src_refs[t]

    def dst(self, t, land_refs, me, origin):
        return _block_of(land_refs[t], self.kinds[t], origin, self.nb[t])


class _ScatterRoute:
    def __init__(self, grads, kinds, layer):
        self.kinds, self.layer = kinds, layer
        self.nb = [g.shape[1 if k == "col" else 0] // N_DEV for g, k in zip(grads, kinds)]

    def src(self, t, src_refs, me, p):
        return _block_of(src_refs[t], self.kinds[t], p, self.nb[t])

    def dst(self, t, land_refs, me, origin):
        return land_refs[t].at[self.layer, origin]


_HBM = pl.BlockSpec(memory_space=pltpu.HBM)
_SEM = pl.BlockSpec(memory_space=pltpu.SEMAPHORE)
_FLOWING = pltpu.SideEffectType.DATAFLOW_SIDE_EFFECTING


def _exchange_start(srcs, lands, route, after, *, name):
    n = len(srcs)

    def body(*refs):
        src_refs, land_refs = refs[:n], refs[n:2 * n]
        send_sems, recv_sems = refs[2 * n + 1:2 * n + 3]
        token, own_sems = refs[-2:]
        _, _, _, me = _position()
        own = [pltpu.make_async_copy(route.src(t, src_refs, me, me), route.dst(t, land_refs, me, me), own_sems.at[t])
               for t in range(n)]
        for cp in own:
            cp.start()
        for t in range(n):
            for k in range(1, N_DEV):
                p = (me + k) % N_DEV
                pltpu.make_async_remote_copy(
                    src_ref=route.src(t, src_refs, me, p), dst_ref=route.dst(t, land_refs, me, me),
                    send_sem=send_sems.at[t * N_DEV + k], recv_sem=recv_sems.at[t * N_DEV + k], device_id=_coords(p),
                    device_id_type=MESH).start()
        for cp in own:
            cp.wait()
        token[...] = jnp.zeros_like(token)

    hbm = lambda a: pltpu.HBM(a.shape, a.dtype)
    sems = pltpu.SemaphoreType.DMA((n * N_DEV,))
    out = pl.pallas_call(
        body, name=name,
        out_shape=(sems, sems, *[hbm(a) for a in srcs], *[hbm(a) for a in lands],
                   jax.ShapeDtypeStruct((SUBLANE, LANE), F32)),
        in_specs=[_HBM] * (2 * n) + [pl.BlockSpec(memory_space=pl.ANY)],
        out_specs=(_SEM, _SEM, *[_HBM] * (2 * n), pl.BlockSpec(memory_space=pltpu.VMEM)),
        input_output_aliases={i: 2 + i for i in range(2 * n)},
        scratch_shapes=[pltpu.SemaphoreType.DMA((n,))],
        compiler_params=pltpu.CompilerParams(has_side_effects=_FLOWING),
    )(*[pltpu.with_memory_space_constraint(a, pltpu.HBM) for a in (*srcs, *lands)], after)
    return out[0], out[1], out[2:2 + n], out[2 + n:2 + 2 * n], out[-1]


def _exchange_wait(started, route, after, *, name):
    send_sems, recv_sems, srcs, lands, _ = started
    n = len(srcs)

    def body(*refs):
        src_refs, land_refs = refs[:n], refs[n:2 * n]
        send_ref, recv_ref = refs[2 * n:2 * n + 2]
        _, _, _, me = _position()
        for t in range(n):
            for k in range(1, N_DEV):
                p, q = (me + k) % N_DEV, (me + N_DEV - k) % N_DEV
                cp = pltpu.make_async_remote_copy(
                    src_ref=route.src(t, src_refs, me, p), dst_ref=route.dst(t, land_refs, me, q),
                    send_sem=send_ref.at[t * N_DEV + k], recv_sem=recv_ref.at[t * N_DEV + k], device_id=_coords(q),
                    device_id_type=MESH)
                cp.wait_send()
                cp.wait_recv()

    hbm = lambda a: pltpu.HBM(a.shape, a.dtype)
    out = pl.pallas_call(
        body, name=name, out_shape=(*[hbm(a) for a in srcs], *[hbm(a) for a in lands]),
        in_specs=[_HBM] * (2 * n) + [_SEM, _SEM, pl.BlockSpec(memory_space=pl.ANY)],
        out_specs=tuple([_HBM] * (2 * n)), input_output_aliases={i: i for i in range(2 * n)},
        compiler_params=pltpu.CompilerParams(has_side_effects=_FLOWING),
    )(*srcs, *lands, send_sems, recv_sems, after)
    return list(out[n:])


def _all_reduce(part, *, name):
    _, r, _ = part.shape

    def body(part_ref, tot_ref, recv_ref, send1, recv1, send2, recv2):
        _, _, _, me = _position()

        def scatter(k, to_me=False):
            p = (me + N_DEV - k) % N_DEV if to_me else (me + k) % N_DEV
            return pltpu.make_async_remote_copy(
                src_ref=part_ref.at[me if to_me else p], dst_ref=recv_ref.at[p if to_me else me],
                send_sem=send1.at[k], recv_sem=recv1.at[k], device_id=_coords(p), device_id_type=MESH)

        def gather(k, to_me=False):
            p = (me + N_DEV - k) % N_DEV if to_me else (me + k) % N_DEV
            return pltpu.make_async_remote_copy(
                src_ref=tot_ref.at[me], dst_ref=tot_ref.at[p if to_me else me],
                send_sem=send2.at[k], recv_sem=recv2.at[k], device_id=_coords(p), device_id_type=MESH)

        for k in range(1, N_DEV):
            scatter(k).start()
        recv_ref[me] = part_ref[me]
        for k in range(1, N_DEV):
            scatter(k, to_me=True).wait_recv()
        total = recv_ref[0]
        for q in range(1, N_DEV):
            total = total + recv_ref[q]
        tot_ref[me] = total
        for k in range(1, N_DEV):
            gather(k).start()
        for k in range(1, N_DEV):
            gather(k, to_me=True).wait_recv()
        for k in range(1, N_DEV):
            scatter(k).wait_send()
            gather(k).wait_send()

    vmem = pl.BlockSpec(memory_space=pltpu.VMEM)
    return pl.pallas_call(
        body, name=name, in_specs=[vmem], out_specs=vmem, out_shape=jax.ShapeDtypeStruct(part.shape, F32),
        scratch_shapes=[pltpu.VMEM(part.shape, F32)] + [pltpu.SemaphoreType.DMA((N_DEV,))] * 4,
        compiler_params=pltpu.CompilerParams(has_side_effects=True, vmem_limit_bytes=VMEM_LIMIT),
    )(part)


def _round_up(n, m):
    return (n + m - 1) // m * m


def _row_tile(rows):
    return next(t for t in (256, 128, 64, 32, 16) if rows % t == 0)


def _local_step(x, target, small, depth, weights_of, grads_done, *, qkw):
    l, d = x.shape
    groups = d // SSM_GROUP
    tiles = groups // TILE_GROUPS
    half = qkw // RET_HEADS // 2
    cb0 = 2 * qkw // d
    inv = 1.0 / (ROPE_BASE ** (jnp.arange(half, dtype=F32) / half))
    ang = jnp.arange(l, dtype=F32)[:, None] * inv[None, :]
    cos, sin = jnp.cos(ang), jnp.sin(ang)
    rep = jnp.repeat(jnp.eye(SSM_STATE, dtype=F32), SSM_GROUP, axis=1)
    row2 = lambda a: a.reshape(1, -1)

    saved = []
    for i in range(depth):
        full_i, x = weights_of(i, x)
        w_in, w_glu, w_out, w_gate, w_up, w_down = full_i
        n = f"l{i}_"
        g_mix, g_ffn = row2(small["ln_mix_g"][i]), row2(small["ln_ffn_g"][i])
        dskip, b_glu = row2(small["ssm_d"][i]), row2(small["b_glu"][i])
        lg = small["ret_log_gamma"][i]
        h = _norm_fwd(x, g_mix, name=n + "norm_mix")
        proj = _matmul(h, w_in, mode="nn", out_dtype=F32, name=n + "proj")
        q_rot, k_rot, v_bf = _ret_prep(proj, cos, sin, qkw=qkw, d=d, name=n + "ret_prep")
        y_raw = _ret_fwd(lg, q_rot, k_rot, v_bf, name=n + "ret_fwd")
        par = [small["ssm_a_re"][i].reshape(2 * groups, SSM_STATE), small["ssm_a_im"][i].reshape(2 * groups, SSM_STATE),
               small["ssm_log_dt"][i].reshape(2 * groups, 1),
               small["ssm_b_re"][i].reshape(2 * groups, SSM_STATE * SSM_GROUP),
               small["ssm_b_im"][i].reshape(2 * groups, SSM_STATE * SSM_GROUP), rep]
        lam_re, lam_im, bbar_re, bbar_im = _s5_prep(*par, name=n + "s5_prep")
        s5 = [_b_tiles(bbar_re, tiles).astype(BF16), _b_tiles(bbar_im, tiles).astype(BF16),
              lam_re.reshape(2, tiles, 1, TILE_N), lam_im.reshape(2, tiles, 1, TILE_N),
              _c_tiles(small["ssm_c_re"][i], tiles).astype(BF16), _c_tiles(small["ssm_c_im"][i], tiles).astype(BF16)]
        u_seg = _to_segments(proj[:, (cb0 + 2) * d:(cb0 + 3) * d])
        y_seg = _s5_fwd(u_seg, *s5, d=0, name=n + "s5_fwd_f")
        y_seg = _s5_fwd(u_seg, *s5, d=1, add=y_seg, name=n + "s5_fwd_b")
        y_s5 = _from_segments(y_seg)
        ys, ys_bf = _s5_post(y_s5, proj, cb0 + 2, dskip, name=n + "s5_post")
        glu = _matmul(ys_bf, w_glu, mode="nn", out_dtype=F32, name=n + "glu")
        merged = _merge(y_raw, proj, ys, glu, b_glu, cb0=cb0, name=n + "merge")
        x1 = _matmul(merged, w_out, mode="nn", out_dtype=F32, res=x, name=n + "out")
        h2 = _norm_fwd(x1, g_ffn, name=n + "norm_ffn")
        gate = _matmul(h2, w_gate, mode="nn", out_dtype=F32, name=n + "gate")
        up = _matmul(h2, w_up, mode="nn", out_dtype=F32, name=n + "up")
        act = _ffn_act(gate, up, name=n + "act")
        x2 = _matmul(act, w_down, mode="nn", out_dtype=F32, res=x1, name=n + "down")
        saved.append(dict(full=full_i, x=x, h=h, proj=proj, q_rot=q_rot, k_rot=k_rot, v_bf=v_bf, y_raw=y_raw, par=par, s5=s5,
                          u_seg=u_seg, y_s5=y_s5, ys=ys, ys_bf=ys_bf, glu=glu, merged=merged, x1=x1, h2=h2, gate=gate,
                          up=up, act=act))
        x = x2

    dx, dg_final, loss = _final(x, row2(small["ln_final_g"]), target, name="final")

    sg = {k: [None] * depth for k in ("ln_mix_g", "ret_log_gamma", "ssm_a_re", "ssm_a_im", "ssm_log_dt", "ssm_b_re",
                                      "ssm_b_im", "ssm_c_re", "ssm_c_im", "ssm_d", "b_glu", "ln_ffn_g")}
    for i in reversed(range(depth)):
        s = saved[i]
        big = [None] * len(BIG)
        w_in, w_glu, w_out, w_gate, w_up, w_down = s["full"]
        n = f"l{i}_b_"
        g_mix, g_ffn = row2(small["ln_mix_g"][i]), row2(small["ln_ffn_g"][i])
        dskip, b_glu = row2(small["ssm_d"][i]), row2(small["b_glu"][i])
        lg = small["ret_log_gamma"][i]
        dact = _matmul(dx, w_down, mode="nt", out_dtype=F32, name=n + "dact")
        big[5] = _matmul(s["act"], dx, mode="tn", out_dtype=BF16, name=n + "dw_down")
        dgate, dup = _ffn_act_bwd(s["gate"], s["up"], dact, name=n + "act")
        dh2 = _matmul(dgate, w_gate, mode="nt", out_dtype=F32, name=n + "dh2_gate")
        dh2 = _matmul(dup, w_up, mode="nt", out_dtype=F32, res=dh2, name=n + "dh2_up")
        big[3] = _matmul(s["h2"], dgate, mode="tn", out_dtype=BF16, name=n + "dw_gate")
        big[4] = _matmul(s["h2"], dup, mode="tn", out_dtype=BF16, name=n + "dw_up")
        dx1, dgf = _norm_bwd(s["x1"], g_ffn, dh2, dx, name=n + "norm_ffn")
        sg["ln_ffn_g"][i] = dgf[0]
        dmerged = _matmul(dx1, w_out, mode="nt", out_dtype=F32, name=n + "dmerged")
        big[2] = _matmul(s["merged"], dx1, mode="tn", out_dtype=BF16, name=n + "dw_out")
        dy_raw, dg, dgate_r, dgate_s, dglu, dys_a, db_glu = _merge_bwd(
            s["y_raw"], s["proj"], s["ys"], s["glu"], b_glu, dmerged, cb0=cb0, name=n + "merge")
        sg["b_glu"][i] = db_glu[0]
        dys = _matmul(dglu, w_glu, mode="nt", out_dtype=F32, res=dys_a, name=n + "dys")
        big[1] = _matmul(s["ys_bf"], dglu, mode="tn", out_dtype=BF16, name=n + "dw_glu")
        dpre, dd = _s5_post_bwd(s["y_s5"], s["proj"], cb0 + 2, dskip, dys, name=n + "s5_post")
        sg["ssm_d"][i] = dd[0]
        dpre_seg = _to_segments(dpre)
        r_f = _s5_bwd(s["u_seg"], dpre_seg, *s["s5"], d=0, name=n + "s5_bwd_f")
        r_b = _s5_bwd(s["u_seg"], dpre_seg, *s["s5"], d=1, add=r_f[0], name=n + "s5_bwd_b")
        du = _du_combine(dpre, dskip, _from_segments(r_b[0]), name=n + "du")
        both = lambda k: jnp.stack([r_f[k], r_b[k]])
        cts = [both(5).reshape(2 * groups, SSM_STATE), both(6).reshape(2 * groups, SSM_STATE),
               _b_untile(both(1), tiles), _b_untile(both(2), tiles)]
        da_re, da_im, dldt, db_re, db_im = _s5_prep_bwd(*s["par"], cts, name=n + "s5_prep")
        sg["ssm_a_re"][i] = da_re.reshape(2, groups, SSM_STATE)
        sg["ssm_a_im"][i] = da_im.reshape(2, groups, SSM_STATE)
        sg["ssm_log_dt"][i] = dldt.reshape(2, groups)
        sg["ssm_b_re"][i] = db_re.reshape(2, groups, SSM_STATE, SSM_GROUP)
        sg["ssm_b_im"][i] = db_im.reshape(2, groups, SSM_STATE, SSM_GROUP)
        sg["ssm_c_re"][i] = _c_untile(both(3), tiles)
        sg["ssm_c_im"][i] = _c_untile(both(4), tiles)
        dq_rot, dk_rot, dv, dlg = _ret_bwd(lg, s["q_rot"], s["k_rot"], s["v_bf"], dy_raw, name=n + "ret_bwd")
        sg["ret_log_gamma"][i] = dlg[:, :2, 0].T
        dq, dk = _ret_prep_bwd(dq_rot, dk_rot, cos, sin, name=n + "ret_prep")
        dproj = jnp.concatenate([dq, dk, dv.astype(BF16), dg, du, dgate_r, dgate_s], axis=1)
        dh = _matmul(dproj, w_in, mode="nt", out_dtype=F32, name=n + "dh")
        big[0] = _matmul(s["h"], dproj, mode="tn", out_dtype=BF16, name=n + "dw_in")
        dx, dgm = _norm_bwd(s["x"], g_mix, dh, dx1, name=n + "norm_mix")
        sg["ln_mix_g"][i] = dgm[0]
        dx = grads_done(i, big, dx)

    small_grads = {k: jnp.stack(v) for k, v in sg.items()}
    small_grads["ln_final_g"] = dg_final[0]
    return loss, dx, small_grads


BIG = ("w_in", "w_glu", "w_out", "w_ffn_gate", "w_ffn_up", "w_ffn_down")
BIG_KINDS = ("col", "row", "row", "col", "col", "row")
SMALL = ("ln_mix_g", "ret_log_gamma", "ssm_a_re", "ssm_a_im", "ssm_log_dt", "ssm_b_re", "ssm_b_im", "ssm_c_re",
         "ssm_c_im", "ssm_d", "b_glu", "ln_ffn_g", "ln_final_g")
WEIGHTS = ("ln_mix_g", "w_in", "ret_log_gamma", "ssm_a_re", "ssm_a_im", "ssm_log_dt", "ssm_b_re", "ssm_b_im",
           "ssm_c_re", "ssm_c_im", "ssm_d", "w_glu", "b_glu", "w_out", "ln_ffn_g", "w_ffn_gate", "w_ffn_up",
           "w_ffn_down", "ln_final_g")


def _pad_to(a, axis, size):
    pad = [(0, 0)] * a.ndim
    pad[axis] = (0, size - a.shape[axis])
    return jnp.pad(a, pad)


def _flatten_small(tree, extra):
    def as_rows(a):
        a = a.reshape(-1).astype(F32)
        return _pad_to(a, 0, _round_up(a.shape[0], SUBLANE * LANE)).reshape(-1, LANE)

    flat = jnp.concatenate([as_rows(tree[k]) for k in SMALL] + [as_rows(extra)])
    return _pad_to(flat, 0, _round_up(flat.shape[0], FLAT_ROWS))


def _unflatten_small(flat, like):
    out, at = {}, 0
    for k in SMALL:
        n = like[k].size
        rows = _round_up(n, SUBLANE * LANE) // LANE
        out[k] = flat[at:at + rows].reshape(-1)[:n].reshape(like[k].shape)
        at += rows
    return out, flat[at, 0]


def kernel(x, ln_mix_g, w_in, ret_log_gamma, ssm_a_re, ssm_a_im, ssm_log_dt, ssm_b_re, ssm_b_im, ssm_c_re, ssm_c_im, ssm_d, w_glu, b_glu, w_out, ln_ffn_g, w_ffn_gate, w_ffn_up, w_ffn_down, ln_final_g, loss_target, m_ln_mix_g, m_w_in, m_ret_log_gamma, m_ssm_a_re, m_ssm_a_im, m_ssm_log_dt, m_ssm_b_re, m_ssm_b_im, m_ssm_c_re, m_ssm_c_im, m_ssm_d, m_w_glu, m_b_glu, m_w_out, m_ln_ffn_g, m_w_ffn_gate, m_w_ffn_up, m_w_ffn_down, m_ln_final_g, v_ln_mix_g, v_w_in, v_ret_log_gamma, v_ssm_a_re, v_ssm_a_im, v_ssm_log_dt, v_ssm_b_re, v_ssm_b_im, v_ssm_c_re, v_ssm_c_im, v_ssm_d, v_w_glu, v_b_glu, v_w_out, v_ln_ffn_g, v_w_ffn_gate, v_w_ffn_up, v_w_ffn_down, v_ln_final_g):
    w = dict(ln_mix_g=ln_mix_g, w_in=w_in, ret_log_gamma=ret_log_gamma, ssm_a_re=ssm_a_re, ssm_a_im=ssm_a_im, ssm_log_dt=ssm_log_dt, ssm_b_re=ssm_b_re, ssm_b_im=ssm_b_im, ssm_c_re=ssm_c_re, ssm_c_im=ssm_c_im, ssm_d=ssm_d, w_glu=w_glu, b_glu=b_glu, w_out=w_out, ln_ffn_g=ln_ffn_g, w_ffn_gate=w_ffn_gate, w_ffn_up=w_ffn_up, w_ffn_down=w_ffn_down, ln_final_g=ln_final_g)
    m = dict(ln_mix_g=m_ln_mix_g, w_in=m_w_in, ret_log_gamma=m_ret_log_gamma, ssm_a_re=m_ssm_a_re, ssm_a_im=m_ssm_a_im, ssm_log_dt=m_ssm_log_dt, ssm_b_re=m_ssm_b_re, ssm_b_im=m_ssm_b_im, ssm_c_re=m_ssm_c_re, ssm_c_im=m_ssm_c_im, ssm_d=m_ssm_d, w_glu=m_w_glu, b_glu=m_b_glu, w_out=m_w_out, ln_ffn_g=m_ln_ffn_g, w_ffn_gate=m_w_ffn_gate, w_ffn_up=m_w_ffn_up, w_ffn_down=m_w_ffn_down, ln_final_g=m_ln_final_g)
    v = dict(ln_mix_g=v_ln_mix_g, w_in=v_w_in, ret_log_gamma=v_ret_log_gamma, ssm_a_re=v_ssm_a_re, ssm_a_im=v_ssm_a_im, ssm_log_dt=v_ssm_log_dt, ssm_b_re=v_ssm_b_re, ssm_b_im=v_ssm_b_im, ssm_c_re=v_ssm_c_re, ssm_c_im=v_ssm_c_im, ssm_d=v_ssm_d, w_glu=v_w_glu, b_glu=v_b_glu, w_out=v_w_out, ln_ffn_g=v_ln_ffn_g, w_ffn_gate=v_w_ffn_gate, w_ffn_up=v_w_ffn_up, w_ffn_down=v_w_ffn_down, ln_final_g=v_ln_final_g)
    depth, d, nb_in = w_in.shape
    qkw = (nb_in * N_DEV - 5 * d) // 2
    nb_ffn = w_ffn_gate.shape[2]
    nb_pad = _round_up(nb_ffn, LANE)
    pad_axis = {"w_ffn_gate": 2, "w_ffn_up": 2, "w_ffn_down": 1}

    assert depth == 2
    padded = {k: w[k] if k not in pad_axis else _pad_to(w[k], pad_axis[k], nb_pad) for k in BIG}
    shards = [[padded[k][i].astype(BF16) for k in BIG] for i in range(depth)]
    full0 = _all_gather(shards[0], BIG_KINDS, name="gather_l0")
    gather1 = _GatherRoute(shards[1], BIG_KINDS)
    gathering = _exchange_start(shards[1], gather1.lands(shards[1]), gather1, full0[0], name="gather_l1_start")

    def weights_of(i, act):
        if i == 0:
            return full0, act + gathering[4][0, 0]
        return _exchange_wait(gathering, gather1, act, name="gather_l1_wait"), act

    pending = {}

    def grads_done(i, big, dact):
        route = _ScatterRoute(big, BIG_KINDS, i)
        if i == depth - 1:
            lands = [lax.empty((depth, N_DEV, g.shape[0], g.shape[1] // N_DEV) if kind == "col"
                               else (depth, N_DEV, g.shape[0] // N_DEV, g.shape[1]), BF16)
                     for g, kind in zip(big, BIG_KINDS)]
            pending["last"] = _exchange_start(big, lands, route, dact, name="scatter_l1_start"), route
            return dact + pending["last"][0][4][0, 0]
        started, route_last = pending["last"]
        lands = _exchange_wait(started, route_last, dact, name="scatter_l1_wait")
        started = _exchange_start(big, lands, route, dact, name="scatter_l0_start")
        pending["recv"] = _exchange_wait(started, route, started[4], name="scatter_l0_wait")
        return dact

    small = {k: w[k] for k in SMALL}
    loss, dx, small_grads = _local_step(x[0], loss_target[0], small, depth, weights_of, grads_done, qkw=qkw)
    recv = pending["recv"]
    grads, delta, new_m, new_v = {}, {}, {}, {}
    for t, k in enumerate(BIG):
        if k in ("w_ffn_gate", "w_ffn_up"):
            ops = [_pad_to(a[k], 2, nb_pad) for a in (w, m, v)]
            res = [r[:, :, :nb_ffn] for r in _adam_shard(recv[t], *ops, name="adam_" + k, tr=256)]
        else:
            res = _adam_shard(recv[t], w[k], m[k], v[k], name="adam_" + k, tr=_row_tile(w[k].shape[1]))
        grads[k], delta[k], new_m[k], new_v[k] = res

    part = _flatten_small(small_grads, loss[0, :1])
    rows = part.shape[0]
    total = _all_reduce(part.reshape(N_DEV, rows // N_DEV, LANE), name="reduce_small").reshape(rows, LANE)
    zero = jnp.zeros((1,), F32)
    flat = [_flatten_small({k: a[k] for k in SMALL}, zero) for a in (w, m, v)]
    upd = _adam_flat(total, *flat, name="adam_small")
    g_small, loss_total = _unflatten_small(total, small)
    grads.update(g_small)
    for dst, u in zip((delta, new_m, new_v), upd):
        dst.update(_unflatten_small(u, small)[0])

    return (loss_total, dx[None], *[grads[k] for k in WEIGHTS], *[delta[k] for k in WEIGHTS],
            *[new_m[k] for k in WEIGHTS], *[new_v[k] for k in WEIGHTS])
```

```python
import math

import jax
import jax.numpy as jnp
from jax import lax
from jax.experimental import pallas as pl
from jax.experimental.pallas import tpu as pltpu

F32 = jnp.float32
BF16 = jnp.bfloat16
MESH = pl.DeviceIdType.MESH

N_DEV = 8
RET_HEADS = 4
CHUNK = 128
ROPE_BASE = 10000.0
SSM_GROUP = 16
SSM_STATE = 64
TILE_GROUPS = 8
TILE_U = TILE_GROUPS * SSM_GROUP
TILE_N = TILE_GROUPS * SSM_STATE
LANE = 128
SUBLANE = 8
N_SEG = SUBLANE
N_LT = TILE_N // LANE
SCAN_UNROLL = 4
FLAT_ROWS = 1024
EPS = 1e-6
ADAM_LR = 0.001
ADAM_B1 = 0.9
ADAM_B2 = 0.999
ADAM_EPS = 1e-08
ADAM_WD = 0.01
ADAM_STEP = 10
VMEM_LIMIT = 56 * 1024 * 1024


def _params(*sem):
    return pltpu.CompilerParams(dimension_semantics=sem or None, vmem_limit_bytes=VMEM_LIMIT)


def _dg(a, b, ca, cb):
    return lax.dot_general(a.astype(BF16), b.astype(BF16), (((ca,), (cb,)), ((), ())),
                           preferred_element_type=F32)


@jax.custom_vjp
def _dnn(a, b):
    return _dg(a, b, 1, 0)


@jax.custom_vjp
def _dnt(a, b):
    return _dg(a, b, 1, 1)


@jax.custom_vjp
def _dtn(a, b):
    return _dg(a, b, 0, 0)


_dnn.defvjp(lambda a, b: (_dnn(a, b), (a, b)), lambda r, g: (_dnt(g, r[1]), _dtn(r[0], g)))
_dnt.defvjp(lambda a, b: (_dnt(a, b), (a, b)), lambda r, g: (_dnn(g, r[1]), _dtn(g, r[0])))
_dtn.defvjp(lambda a, b: (_dtn(a, b), (a, b)), lambda r, g: (_dnt(r[1], g), _dnn(r[0], g)))


def _matmul(a, b, *, mode, out_dtype, name, res=None, tm=1024, tn=1024, tk=512):
    if mode == "nn":
        (m, k), n = a.shape, b.shape[1]
    elif mode == "nt":
        (m, k), n = a.shape, b.shape[0]
    else:
        (k, m), n = a.shape, b.shape[1]
    tm, tn, tk = min(tm, m), min(tn, n), min(tk, k)
    assert m % tm == 0 and n % tn == 0 and k % tk == 0, (name, m, n, k)
    nk = k // tk
    if mode == "tn":
        a_spec = pl.BlockSpec((tk, tm), lambda i, j, kk: (kk, i))
    else:
        a_spec = pl.BlockSpec((tm, tk), lambda i, j, kk: (i, kk))
    if mode == "nt":
        b_spec = pl.BlockSpec((tn, tk), lambda i, j, kk: (j, kk))
    else:
        b_spec = pl.BlockSpec((tk, tn), lambda i, j, kk: (kk, j))
    ca, cb = {"nn": (1, 0), "nt": (1, 1), "tn": (0, 0)}[mode]
    o_spec = pl.BlockSpec((tm, tn), lambda i, j, kk: (i, j))
    has_res = res is not None

    def body(*refs):
        if has_res:
            a_ref, b_ref, r_ref, o_ref, acc = refs
        else:
            a_ref, b_ref, o_ref, acc = refs
        kk = pl.program_id(2)

        @pl.when(kk == 0)
        def _():
            acc[...] = r_ref[...] if has_res else jnp.zeros_like(acc)

        acc[...] += _dg(a_ref[...], b_ref[...], ca, cb)

        @pl.when(kk == nk - 1)
        def _():
            o_ref[...] = acc[...].astype(out_dtype)

    return pl.pallas_call(
        body, name=name, grid=(m // tm, n // tn, nk),
        in_specs=[a_spec, b_spec] + ([o_spec] if has_res else []),
        out_specs=o_spec, out_shape=jax.ShapeDtypeStruct((m, n), out_dtype),
        scratch_shapes=[pltpu.VMEM((tm, tn), F32)],
        compiler_params=_params("parallel", "parallel", "arbitrary"),
    )(*((a, b, res) if has_res else (a, b)))


def _row(arr, tl, width=None, cb=0):
    width = arr.shape[1] if width is None else width
    return arr, pl.BlockSpec((tl, width), lambda i, cb=cb: (i, cb))


def _par(arr):
    return arr, pl.BlockSpec(arr.shape, lambda i: (0,) * arr.ndim)


def _rowwise(body, *, rows, tl, ins, outs, name):
    arrays = [a for a, _ in ins]
    in_specs = [s for _, s in ins]
    out_shape, out_specs, acc_ids = [], [], []
    for n, o in enumerate(outs):
        if o[0] == "row":
            out_shape.append(jax.ShapeDtypeStruct((rows, o[1]), o[2]))
            out_specs.append(pl.BlockSpec((tl, o[1]), lambda i: (i, 0)))
        else:
            out_shape.append(jax.ShapeDtypeStruct((1, o[1]), F32))
            out_specs.append(pl.BlockSpec((1, o[1]), lambda i: (0, 0)))
            acc_ids.append(n)
    n_in = len(arrays)
    assert rows % tl == 0, (name, rows, tl)

    def wrapped(*refs):
        in_refs, out_refs = refs[:n_in], refs[n_in:]

        @pl.when(pl.program_id(0) == 0)
        def _():
            for n in acc_ids:
                out_refs[n][...] = jnp.zeros_like(out_refs[n])

        body(in_refs, out_refs)

    return pl.pallas_call(
        wrapped, name=name, grid=(rows // tl,), in_specs=in_specs, out_specs=out_specs,
        out_shape=out_shape, compiler_params=_params("arbitrary"),
    )(*arrays)


def _rms(x, g):
    return x * lax.rsqrt(jnp.mean(x * x, axis=-1, keepdims=True) + EPS) * g


def _norm_fwd(x, g, *, name, tl=256):
    def body(i, o):
        o[0][...] = _rms(i[0][...], i[1][...]).astype(BF16)

    return _rowwise(body, rows=x.shape[0], tl=tl, ins=[_row(x, tl), _par(g)],
                    outs=[("row", x.shape[1], BF16)], name=name)[0]


def _norm_bwd(x, g, dh, dres, *, name, tl=256):
    def body(i, o):
        _, vjp = jax.vjp(_rms, i[0][...], i[1][...])
        dx, dg = vjp(i[2][...])
        o[0][...] = i[3][...] + dx
        o[1][...] += dg

    d = x.shape[1]
    return _rowwise(body, rows=x.shape[0], tl=tl, ins=[_row(x, tl), _par(g), _row(dh, tl), _row(dres, tl)],
                    outs=[("row", d, F32), ("acc", d)], name=name)


def _final(x, g, target, *, name, tl=256):
    d = x.shape[1]

    def body(i, o):
        y, vjp = jax.vjp(_rms, i[0][...], i[1][...])
        err = y - i[2][...]
        dx, dg = vjp(err * (1.0 / d))
        o[0][...] = dx
        o[1][...] += dg
        o[2][...] += jnp.full((1, LANE), 0.5 / d, F32) * jnp.sum(err * err)

    return _rowwise(body, rows=x.shape[0], tl=tl, ins=[_row(x, tl), _par(g), _row(target, tl)],
                    outs=[("row", d, F32), ("acc", d), ("acc", LANE)], name=name)


def _rot(x, cos, sin, out_ref, col, scale=1.0, inverse=False):
    x1, x2 = x[:, :LANE], x[:, LANE:]
    if inverse:
        sin = -sin
    out_ref[:, col:col + LANE] = ((x1 * cos - x2 * sin) * scale).astype(out_ref.dtype)
    out_ref[:, col + LANE:col + 2 * LANE] = ((x1 * sin + x2 * cos) * scale).astype(out_ref.dtype)


def _ret_prep(proj, cos, sin, *, qkw, d, name, tl=256):
    dk = qkw // RET_HEADS
    assert dk == 2 * LANE and (2 * qkw) % d == 0

    def body(i, o):
        c, s = i[3][...], i[4][...]
        for h in range(RET_HEADS):
            _rot(i[0][:, h * dk:(h + 1) * dk], c, s, o[0], h * dk)
            _rot(i[1][:, h * dk:(h + 1) * dk], c, s, o[1], h * dk, scale=dk ** -0.5)
        o[2][...] = i[2][...].astype(BF16)

    return _rowwise(body, rows=proj.shape[0], tl=tl,
                    ins=[_row(proj, tl, qkw, 0), _row(proj, tl, qkw, 1), _row(proj, tl, d, 2 * qkw // d),
                         _row(cos, tl), _row(sin, tl)],
                    outs=[("row", qkw, BF16), ("row", qkw, BF16), ("row", d, BF16)], name=name)


def _ret_prep_bwd(dq_rot, dk_rot, cos, sin, *, name, tl=256):
    qkw = dq_rot.shape[1]
    dk = qkw // RET_HEADS

    def body(i, o):
        c, s = i[2][...], i[3][...]
        for h in range(RET_HEADS):
            _rot(i[0][:, h * dk:(h + 1) * dk], c, s, o[0], h * dk, inverse=True)
            _rot(i[1][:, h * dk:(h + 1) * dk], c, s, o[1], h * dk, scale=dk ** -0.5, inverse=True)

    return _rowwise(body, rows=dq_rot.shape[0], tl=tl,
                    ins=[_row(dq_rot, tl), _row(dk_rot, tl), _row(cos, tl), _row(sin, tl)],
                    outs=[("row", qkw, BF16), ("row", qkw, BF16)], name=name)


def _ret_weights(lgf, lgb):
    t = lax.broadcasted_iota(jnp.int32, (CHUNK, 1), 0).astype(F32)
    diff = (lax.broadcasted_iota(jnp.int32, (CHUNK, CHUNK), 0)
            - lax.broadcasted_iota(jnp.int32, (CHUNK, CHUNK), 1)).astype(F32)
    dmat = jnp.exp(jnp.where(diff >= 0, lgf * diff, -lgb * diff))
    return dict(dmat=dmat, wqf=jnp.exp(lgf * (t + 1.0)), wkf=jnp.exp(lgf * (CHUNK - 1.0 - t)),
                wqb=jnp.exp(lgb * (CHUNK - t)), wkb=jnp.exp(lgb * t))


def _ret_f_part(q, k, v, lgf, lgb, s_f):
    w = _ret_weights(lgf, lgb)
    y = _dnn(_dnt(q, k) * w["dmat"], v) + _dnn(q * w["wqf"], s_f)
    return y, _dtn(k * w["wkf"], v)


def _ret_b_part(q, k, v, lgb, s_b):
    w = _ret_weights(lgb, lgb)
    return _dnn(q * w["wqb"], s_b), _dtn(k * w["wkb"], v)


def _chunk(c):
    return pl.ds(pl.multiple_of(c * CHUNK, CHUNK), CHUNK)


def _ret_specs(l, qkw, d):
    dk, dv = qkw // RET_HEADS, d // RET_HEADS
    return dk, dv, [pl.BlockSpec(memory_space=pltpu.SMEM),
                    pl.BlockSpec((l, dk), lambda h: (0, h)), pl.BlockSpec((l, dk), lambda h: (0, h)),
                    pl.BlockSpec((l, dv), lambda h: (0, h))]


def _ret_fwd(lg, q, k, v, *, name):
    l, qkw = q.shape
    d = v.shape[1]
    nc = l // CHUNK
    dk, dv, in_specs = _ret_specs(l, qkw, d)

    def body(lg_ref, q_ref, k_ref, v_ref, y_ref, s_ref):
        h = pl.program_id(0)
        lgf = jnp.full((1, 1), lg_ref[0, h], F32)
        lgb = jnp.full((1, 1), lg_ref[1, h], F32)
        dec_f, dec_b = jnp.exp(lgf * CHUNK), jnp.exp(lgb * CHUNK)

        def load(c):
            r = _chunk(c)
            return r, q_ref[r, :].astype(F32), k_ref[r, :].astype(F32), v_ref[r, :].astype(F32)

        s_ref[...] = jnp.zeros_like(s_ref)

        def f_step(c, _):
            r, qc, kc, vc = load(c)
            y, kv = _ret_f_part(qc, kc, vc, lgf, lgb, s_ref[...])
            y_ref[r, :] = y
            s_ref[...] = dec_f * s_ref[...] + kv
            return 0

        lax.fori_loop(0, nc, f_step, 0)
        s_ref[...] = jnp.zeros_like(s_ref)

        def b_step(n, _):
            r, qc, kc, vc = load(nc - 1 - n)
            y, kv = _ret_b_part(qc, kc, vc, lgb, s_ref[...])
            y_ref[r, :] += y
            s_ref[...] = dec_b * s_ref[...] + kv
            return 0

        lax.fori_loop(0, nc, b_step, 0)

    return pl.pallas_call(
        body, name=name, grid=(RET_HEADS,), in_specs=in_specs,
        out_specs=pl.BlockSpec((l, dv), lambda h: (0, h)), out_shape=jax.ShapeDtypeStruct((l, d), F32),
        scratch_shapes=[pltpu.VMEM((dk, dv), F32)], compiler_params=_params("arbitrary"),
    )(lg, q, k, v)


def _ret_bwd(lg, q, k, v, dy, *, name):
    l, qkw = q.shape
    d = v.shape[1]
    nc = l // CHUNK
    dk, dv, in_specs = _ret_specs(l, qkw, d)

    def body(lg_ref, q_ref, k_ref, v_ref, dy_ref, dq_ref, dk_ref, dv_ref, dlg_ref, states, s_ref, sh_ref):
        h = pl.program_id(0)
        lgf = jnp.full((1, 1), lg_ref[0, h], F32)
        lgb = jnp.full((1, 1), lg_ref[1, h], F32)
        dec_f, dec_b = jnp.exp(lgf * CHUNK), jnp.exp(lgb * CHUNK)

        def load(c):
            r = _chunk(c)
            return (r, q_ref[r, :].astype(F32), k_ref[r, :].astype(F32), v_ref[r, :].astype(F32),
                    dy_ref[r, :].astype(F32))

        s_ref[...] = jnp.zeros_like(s_ref)

        def f_states(c, _):
            _, qc, kc, vc, _ = load(c)
            states[c] = s_ref[...]
            w = _ret_weights(lgf, lgb)
            s_ref[...] = dec_f * s_ref[...] + _dtn(kc * w["wkf"], vc)
            return 0

        lax.fori_loop(0, nc, f_states, 0)
        sh_ref[...] = jnp.zeros_like(sh_ref)

        def f_adj(n, carry):
            dlf, dlb, ddec = carry
            c = nc - 1 - n
            r, qc, kc, vc, dyc = load(c)
            sc = states[c]
            _, vjp = jax.vjp(_ret_f_part, qc, kc, vc, lgf, lgb, sc)
            dq, dkk, dvv, g_f, g_b, dsc = vjp((dyc, sh_ref[...]))
            dq_ref[r, :] = dq
            dk_ref[r, :] = dkk
            dv_ref[r, :] = dvv
            ddec = ddec + jnp.sum(sh_ref[...] * sc)
            sh_ref[...] = dsc + dec_f * sh_ref[...]
            return dlf + g_f, dlb + g_b, ddec

        z = jnp.zeros((1, 1), F32)
        dlf, dlb, ddec_f = lax.fori_loop(0, nc, f_adj, (z, z, z))

        s_ref[...] = jnp.zeros_like(s_ref)

        def b_states(n, _):
            c = nc - 1 - n
            _, qc, kc, vc, _ = load(c)
            states[c] = s_ref[...]
            w = _ret_weights(lgb, lgb)
            s_ref[...] = dec_b * s_ref[...] + _dtn(kc * w["wkb"], vc)
            return 0

        lax.fori_loop(0, nc, b_states, 0)
        sh_ref[...] = jnp.zeros_like(sh_ref)

        def b_adj(c, carry):
            dlb, ddec = carry
            r, qc, kc, vc, dyc = load(c)
            sc = states[c]
            _, vjp = jax.vjp(_ret_b_part, qc, kc, vc, lgb, sc)
            dq, dkk, dvv, g_b, dsc = vjp((dyc, sh_ref[...]))
            dq_ref[r, :] += dq
            dk_ref[r, :] += dkk
            dv_ref[r, :] += dvv
            ddec = ddec + jnp.sum(sh_ref[...] * sc)
            sh_ref[...] = dsc + dec_b * sh_ref[...]
            return dlb + g_b, ddec

        dlb, ddec_b = lax.fori_loop(0, nc, b_adj, (dlb, z))
        dlf = dlf + ddec_f * dec_f * CHUNK
        dlb = dlb + ddec_b * dec_b * CHUNK
        row = lax.broadcasted_iota(jnp.int32, (SUBLANE, LANE), 0)
        dlg_ref[...] = jnp.where(row == 0, dlf, jnp.where(row == 1, dlb, 0.0))

    head = lambda w: pl.BlockSpec((l, w), lambda h: (0, h))
    return pl.pallas_call(
        body, name=name, grid=(RET_HEADS,), in_specs=in_specs + [head(dv)],
        out_specs=[head(dk), head(dk), head(dv), pl.BlockSpec((None, SUBLANE, LANE), lambda h: (h, 0, 0))],
        out_shape=[jax.ShapeDtypeStruct((l, qkw), F32), jax.ShapeDtypeStruct((l, qkw), F32),
                   jax.ShapeDtypeStruct((l, d), F32), jax.ShapeDtypeStruct((RET_HEADS, SUBLANE, LANE), F32)],
        scratch_shapes=[pltpu.VMEM((nc, dk, dv), F32), pltpu.VMEM((dk, dv), F32), pltpu.VMEM((dk, dv), F32)],
        compiler_params=_params("arbitrary"),
    )(lg, q, k, v, dy)


def _s5_param_fn(a_re, a_im, log_dt, b_re, b_im, rep):
    dt = jnp.exp(log_dt)
    mag = jnp.exp(a_re * dt)
    lam_re, lam_im = mag * jnp.cos(a_im * dt), mag * jnp.sin(a_im * dt)
    n_re, n_im = lam_re - 1.0, lam_im
    den = a_re * a_re + a_im * a_im
    c_re = (n_re * a_re + n_im * a_im) / den
    c_im = (n_im * a_re - n_re * a_im) / den
    hi = lax.Precision.HIGHEST
    c_re = jnp.dot(c_re, rep, precision=hi, preferred_element_type=F32)
    c_im = jnp.dot(c_im, rep, precision=hi, preferred_element_type=F32)
    return lam_re, lam_im, c_re * b_re - c_im * b_im, c_re * b_im + c_im * b_re


def _s5_param_shapes(a_re, b_re):
    r, p = a_re.shape
    return [jax.ShapeDtypeStruct((r, p), F32)] * 2 + [jax.ShapeDtypeStruct(b_re.shape, F32)] * 2


def _s5_prep(a_re, a_im, log_dt, b_re, b_im, rep, *, name):
    def body(*refs):
        outs = _s5_param_fn(*[r[...] for r in refs[:6]])
        for o_ref, o in zip(refs[6:], outs):
            o_ref[...] = o

    return pl.pallas_call(body, name=name, out_shape=_s5_param_shapes(a_re, b_re),
                          compiler_params=_params())(a_re, a_im, log_dt, b_re, b_im, rep)


def _s5_prep_bwd(a_re, a_im, log_dt, b_re, b_im, rep, cts, *, name):
    def body(*refs):
        ins = [r[...] for r in refs[:6]]
        _, vjp = jax.vjp(lambda *p: _s5_param_fn(*p, ins[5]), *ins[:5])
        grads = vjp(tuple(r[...] for r in refs[6:10]))
        for o_ref, o in zip(refs[10:], grads):
            o_ref[...] = o

    shapes = [jax.ShapeDtypeStruct(t.shape, F32) for t in (a_re, a_im, log_dt, b_re, b_im)]
    return pl.pallas_call(body, name=name, out_shape=shapes,
                          compiler_params=_params())(a_re, a_im, log_dt, b_re, b_im, rep, *cts)


def _eye_tiles():
    return jnp.eye(TILE_GROUPS, dtype=F32)


def _b_tiles(bbar, tiles):
    t = bbar.reshape(2, tiles, TILE_GROUPS, SSM_STATE, SSM_GROUP).transpose(0, 1, 2, 4, 3)
    t = t[:, :, :, :, None, :] * _eye_tiles()[None, None, :, None, :, None]
    return t.reshape(2, tiles, TILE_U, TILE_N)


def _b_untile(dbt, tiles):
    t = dbt.reshape(2, tiles, TILE_GROUPS, SSM_GROUP, TILE_GROUPS, SSM_STATE)
    t = (t * _eye_tiles()[None, None, :, None, :, None]).sum(axis=4)
    return t.transpose(0, 1, 2, 4, 3).reshape(2 * tiles * TILE_GROUPS, SSM_STATE * SSM_GROUP)


def _c_tiles(c, tiles):
    t = c.reshape(2, tiles, TILE_GROUPS, SSM_GROUP, SSM_STATE).transpose(0, 1, 2, 4, 3)
    t = t[:, :, :, :, None, :] * _eye_tiles()[None, None, :, None, :, None]
    return t.reshape(2, tiles, TILE_N, TILE_U)


def _c_untile(dct, tiles):
    t = dct.reshape(2, tiles, TILE_GROUPS, SSM_STATE, TILE_GROUPS, SSM_GROUP)
    t = (t * _eye_tiles()[None, None, :, None, :, None]).sum(axis=4)
    return t.transpose(0, 1, 2, 4, 3).reshape(2, tiles * TILE_GROUPS, SSM_GROUP, SSM_STATE)


def _to_segments(a):
    l, w = a.shape
    return a.reshape(N_SEG, l // N_SEG, w).transpose(1, 0, 2).reshape(l, w)


def _from_segments(a):
    l, w = a.shape
    return a.reshape(l // N_SEG, N_SEG, w).transpose(1, 0, 2).reshape(l, w)


def _s5_scan(xr, xi, a_re, a_im, *, length, reverse, shifted=None):
    ls = length // N_SEG
    assert ls * N_SEG == length and ls & (ls - 1) == 0
    ar = [jnp.broadcast_to(a_re[:, c * LANE:(c + 1) * LANE], (N_SEG, LANE)) for c in range(N_LT)]
    ai = [jnp.broadcast_to(a_im[:, c * LANE:(c + 1) * LANE], (N_SEG, LANE)) for c in range(N_LT)]
    zero = jnp.zeros((N_SEG, LANE), F32)
    row = lax.broadcasted_iota(jnp.int32, (N_SEG, LANE), 0)

    def step_of(n):
        return (ls - 1 - n) if reverse else n

    def block(j):
        return pl.ds(j * N_SEG, N_SEG) if isinstance(j, int) else pl.ds(pl.multiple_of(j * N_SEG, N_SEG), N_SEG)

    def local(n, carry):
        rows = block(step_of(n))
        new = []
        for c in range(N_LT):
            cr, ci = carry[2 * c], carry[2 * c + 1]
            nr = ar[c] * cr - ai[c] * ci + xr[c, rows, :]
            ni = ar[c] * ci + ai[c] * cr + xi[c, rows, :]
            xr[c, rows, :] = nr
            xi[c, rows, :] = ni
            new += [nr, ni]
        return tuple(new)

    ends = lax.fori_loop(0, ls, local, (zero,) * (2 * N_LT), unroll=SCAN_UNROLL)

    init = []
    for c in range(N_LT):
        pr, pi = ar[c][0:1, :], ai[c][0:1, :]
        for _ in range(ls.bit_length() - 1):
            pr, pi = pr * pr - pi * pi, 2.0 * pr * pi
        cr = ci = jnp.zeros((1, LANE), F32)
        ir, ii = zero, zero
        for s in (range(N_SEG - 1, -1, -1) if reverse else range(N_SEG)):
            ir = jnp.where(row == s, cr, ir)
            ii = jnp.where(row == s, ci, ii)
            er, ei = ends[2 * c][s:s + 1, :], ends[2 * c + 1][s:s + 1, :]
            cr, ci = pr * cr - pi * ci + er, pr * ci + pi * cr + ei
        init += [ir, ii]

    def fix(n, carry, last=False):
        j = step_of(n)
        rows = block(j)
        new, sums = [], []
        for c in range(N_LT):
            cr, ci = carry[2 * c], carry[2 * c + 1]
            nr = ar[c] * cr - ai[c] * ci
            ni = ar[c] * ci + ai[c] * cr
            fr = xr[c, rows, :] + nr
            fi = xi[c, rows, :] + ni
            xr[c, rows, :] = fr
            xi[c, rows, :] = fi
            new += [nr, ni]
            if shifted is not None:
                yr, yi, shift = shifted
                if not last:
                    srows = block(j + shift)
                    sr, si = yr[c, srows, :], yi[c, srows, :]
                else:
                    edge = block(ls - 1 if shift < 0 else 0)
                    move, gone = (1, 0) if shift < 0 else (N_SEG - 1, N_SEG - 1)
                    sr = jnp.where(row == gone, 0.0, pltpu.roll(yr[c, edge, :], move, 0))
                    si = jnp.where(row == gone, 0.0, pltpu.roll(yi[c, edge, :], move, 0))
                sums += [carry[2 * N_LT + 2 * c] + fr * sr + fi * si,
                         carry[2 * N_LT + 2 * c + 1] + fi * sr - fr * si]
        return tuple(new + sums)

    if shifted is None:
        lax.fori_loop(0, ls, fix, tuple(init), unroll=SCAN_UNROLL)
        return ()
    assert shifted[2] == (-1 if reverse else 1)
    out = lax.fori_loop(0, ls - 1, fix, tuple(init) + (zero,) * (2 * N_LT), unroll=SCAN_UNROLL)
    return fix(ls - 1, out, last=True)[2 * N_LT:]


def _s5_tile_specs(l, d):
    tile = lambda r, c: pl.BlockSpec((None, None, r, c), lambda t, d=d: (d, t, 0, 0))
    return [pl.BlockSpec((l, TILE_U), lambda t: (0, t)), tile(TILE_U, TILE_N), tile(TILE_U, TILE_N),
            tile(1, TILE_N), tile(1, TILE_N), tile(TILE_N, TILE_U), tile(TILE_N, TILE_U)]


def _lanes(c):
    return slice(c * LANE, (c + 1) * LANE)


def _s5_fwd(u, bt_re, bt_im, lam_re, lam_im, ct_re, ct_im, *, d, name, add=None):
    l = u.shape[0]
    tiles = bt_re.shape[1]
    col = pl.BlockSpec((l, TILE_U), lambda t: (0, t))
    has_add = add is not None

    def body(*refs):
        u_ref, bre, bim, lre, lim, cre, cim = refs[:7]
        y_ref, xr, xi = refs[-3:]
        uu = u_ref[...]
        bu_re, bu_im = _dg(uu, bre[...], 1, 0), _dg(uu, bim[...], 1, 0)
        for c in range(N_LT):
            xr[c] = bu_re[:, _lanes(c)]
            xi[c] = bu_im[:, _lanes(c)]
        _s5_scan(xr, xi, lre[...], lim[...], length=l, reverse=(d == 1))
        y = refs[7][...] if has_add else jnp.zeros((l, TILE_U), F32)
        for c in range(N_LT):
            y = y + _dg(xr[c], cre[_lanes(c), :], 1, 0) - _dg(xi[c], cim[_lanes(c), :], 1, 0)
        y_ref[...] = y

    return pl.pallas_call(
        body, name=name, grid=(tiles,), in_specs=_s5_tile_specs(l, d) + [col] * has_add, out_specs=col,
        out_shape=jax.ShapeDtypeStruct((l, tiles * TILE_U), F32),
        scratch_shapes=[pltpu.VMEM((N_LT, l, LANE), F32)] * 2, compiler_params=_params("arbitrary"),
    )(u, bt_re, bt_im, lam_re, lam_im, ct_re, ct_im, *([add] if has_add else []))


def _s5_bwd(u, dy, bt_re, bt_im, lam_re, lam_im, ct_re, ct_im, *, d, name, add=None):
    l = u.shape[0]
    tiles = bt_re.shape[1]
    col = pl.BlockSpec((l, TILE_U), lambda t: (0, t))
    reverse = d == 1
    has_add = add is not None

    def body(*refs):
        u_ref, bre, bim, lre, lim, cre, cim, dy_ref = refs[:8]
        du_ref, dbre, dbim, dcre, dcim, dlre, dlim, xr, xi, gr, gi = refs[-11:]
        uu, dyy = u_ref[...], dy_ref[...]
        bu_re, bu_im = _dg(uu, bre[...], 1, 0), _dg(uu, bim[...], 1, 0)
        for c in range(N_LT):
            xr[c] = bu_re[:, _lanes(c)]
            xi[c] = bu_im[:, _lanes(c)]
        _s5_scan(xr, xi, lre[...], lim[...], length=l, reverse=reverse)
        gy_re, gy_im = _dg(dyy, cre[...], 1, 1), -_dg(dyy, cim[...], 1, 1)
        for c in range(N_LT):
            gr[c] = gy_re[:, _lanes(c)]
            gi[c] = gy_im[:, _lanes(c)]
        sums = _s5_scan(gr, gi, lre[...], -lim[...], length=l, reverse=not reverse,
                        shifted=(xr, xi, 1 if reverse else -1))
        du = refs[8][...] if has_add else jnp.zeros((l, TILE_U), F32)
        for c in range(N_LT):
            dlre[:, _lanes(c)] = jnp.sum(sums[2 * c], axis=0, keepdims=True)
            dlim[:, _lanes(c)] = jnp.sum(sums[2 * c + 1], axis=0, keepdims=True)
            g_re, g_im = gr[c], gi[c]
            du = du + _dg(g_re, bre[:, _lanes(c)], 1, 1) + _dg(g_im, bim[:, _lanes(c)], 1, 1)
            dbre[:, _lanes(c)] = _dg(uu, g_re, 0, 0)
            dbim[:, _lanes(c)] = _dg(uu, g_im, 0, 0)
            dcre[_lanes(c), :] = _dg(xr[c], dyy, 0, 0)
            dcim[_lanes(c), :] = -_dg(xi[c], dyy, 0, 0)
        du_ref[...] = du

    out3 = lambda r, c: pl.BlockSpec((None, r, c), lambda t: (t, 0, 0))
    f = lambda *s: jax.ShapeDtypeStruct(s, F32)
    return pl.pallas_call(
        body, name=name, grid=(tiles,), in_specs=_s5_tile_specs(l, d) + [col] + [col] * has_add,
        out_specs=[col, out3(TILE_U, TILE_N), out3(TILE_U, TILE_N),
                   out3(TILE_N, TILE_U), out3(TILE_N, TILE_U), out3(1, TILE_N), out3(1, TILE_N)],
        out_shape=[f(l, tiles * TILE_U), f(tiles, TILE_U, TILE_N), f(tiles, TILE_U, TILE_N),
                   f(tiles, TILE_N, TILE_U), f(tiles, TILE_N, TILE_U), f(tiles, 1, TILE_N), f(tiles, 1, TILE_N)],
        scratch_shapes=[pltpu.VMEM((N_LT, l, LANE), F32)] * 4, compiler_params=_params("arbitrary"),
    )(u, bt_re, bt_im, lam_re, lam_im, ct_re, ct_im, dy, *([add] if has_add else []))


def _s5_post(y, proj, u_cb, dskip, *, name, tl=256):
    d = y.shape[1]

    def body(i, o):
        ys = jax.nn.gelu(i[0][...] + i[2][...] * i[1][...])
        o[0][...] = ys
        o[1][...] = ys.astype(BF16)

    return _rowwise(body, rows=y.shape[0], tl=tl, ins=[_row(y, tl), _row(proj, tl, d, u_cb), _par(dskip)],
                    outs=[("row", d, F32), ("row", d, BF16)], name=name)


def _s5_post_bwd(y, proj, u_cb, dskip, dys, *, name, tl=256):
    d = y.shape[1]

    def body(i, o):
        u_ = i[1][...]
        _, vjp = jax.vjp(jax.nn.gelu, i[0][...] + i[2][...] * u_)
        (dpre,) = vjp(i[3][...])
        o[0][...] = dpre
        o[1][...] += jnp.sum(dpre * u_, axis=0, keepdims=True)

    return _rowwise(body, rows=y.shape[0], tl=tl,
                    ins=[_row(y, tl), _row(proj, tl, d, u_cb), _par(dskip), _row(dys, tl)],
                    outs=[("row", d, F32), ("acc", d)], name=name)


def _du_combine(dpre, dskip, du_s5, *, name, tl=256):
    d = dpre.shape[1]

    def body(i, o):
        o[0][...] = (i[0][...] * i[1][...] + i[2][...]).astype(BF16)

    return _rowwise(body, rows=dpre.shape[0], tl=tl, ins=[_row(dpre, tl), _par(dskip), _row(du_s5, tl)],
                    outs=[("row", d, BF16)], name=name)[0]


def _merge_fn(y, g, gate_r, gate_s, ys, glu, b):
    ret = jax.nn.silu(g) * (y * lax.rsqrt(jnp.mean(y * y, axis=-1, keepdims=True) + EPS))
    ssm = ys * jax.nn.sigmoid(glu + b)
    return jax.nn.sigmoid(gate_r) * ret + jax.nn.sigmoid(gate_s) * ssm


def _merge_ins(y_raw, proj, ys, glu, b_glu, cb0, tl):
    d = y_raw.shape[1]
    return [_row(y_raw, tl), _row(proj, tl, d, cb0 + 1), _row(proj, tl, d, cb0 + 3), _row(proj, tl, d, cb0 + 4),
            _row(ys, tl), _row(glu, tl), _par(b_glu)]


def _merge(y_raw, proj, ys, glu, b_glu, *, cb0, name, tl=128):
    d = y_raw.shape[1]
    dv = d // RET_HEADS

    def body(i, o):
        for h in range(RET_HEADS):
            cs = slice(h * dv, (h + 1) * dv)
            o[0][:, cs] = _merge_fn(*[r[:, cs] for r in i]).astype(BF16)

    return _rowwise(body, rows=y_raw.shape[0], tl=tl, ins=_merge_ins(y_raw, proj, ys, glu, b_glu, cb0, tl),
                    outs=[("row", d, BF16)], name=name)[0]


def _merge_bwd(y_raw, proj, ys, glu, b_glu, dmerged, *, cb0, name, tl=128):
    d = y_raw.shape[1]
    dv = d // RET_HEADS

    def body(i, o):
        for h in range(RET_HEADS):
            cs = slice(h * dv, (h + 1) * dv)
            _, vjp = jax.vjp(_merge_fn, *[r[:, cs] for r in i[:7]])
            dy, dg, dgr, dgs, dys, dglu, db = vjp(i[7][:, cs])
            o[0][:, cs] = dy.astype(BF16)
            o[1][:, cs] = dg.astype(BF16)
            o[2][:, cs] = dgr.astype(BF16)
            o[3][:, cs] = dgs.astype(BF16)
            o[4][:, cs] = dglu.astype(BF16)
            o[5][:, cs] = dys
            o[6][:, cs] += db

    return _rowwise(body, rows=y_raw.shape[0], tl=tl,
                    ins=_merge_ins(y_raw, proj, ys, glu, b_glu, cb0, tl) + [_row(dmerged, tl)],
                    outs=[("row", d, BF16)] * 5 + [("row", d, F32), ("acc", d)], name=name)


def _ffn_act_fn(gate, up):
    return jax.nn.silu(gate) * up


def _ffn_act(gate, up, *, name, tl=128):
    def body(i, o):
        o[0][...] = _ffn_act_fn(i[0][...], i[1][...]).astype(BF16)

    return _rowwise(body, rows=gate.shape[0], tl=tl, ins=[_row(gate, tl), _row(up, tl)],
                    outs=[("row", gate.shape[1], BF16)], name=name)[0]


def _ffn_act_bwd(gate, up, dact, *, name, tl=128):
    def body(i, o):
        _, vjp = jax.vjp(_ffn_act_fn, i[0][...], i[1][...])
        dgate, dup = vjp(i[2][...])
        o[0][...] = dgate.astype(BF16)
        o[1][...] = dup.astype(BF16)

    w = gate.shape[1]
    return _rowwise(body, rows=gate.shape[0], tl=tl, ins=[_row(gate, tl), _row(up, tl), _row(dact, tl)],
                    outs=[("row", w, BF16), ("row", w, BF16)], name=name)


def _adamw(w, g, m, v):
    m = ADAM_B1 * m + (1.0 - ADAM_B1) * g
    v = ADAM_B2 * v + (1.0 - ADAM_B2) * (g * g)
    m_hat = m / (1.0 - ADAM_B1 ** ADAM_STEP)
    v_hat = v / (1.0 - ADAM_B2 ** ADAM_STEP)
    return -ADAM_LR * (m_hat / (jnp.sqrt(v_hat) + ADAM_EPS) + ADAM_WD * w), m, v


def _adam_flat(g, w, m, v, *, name, tr=FLAT_ROWS):
    def body(i, o):
        for o_ref, val in zip(o, _adamw(i[1][...], i[0][...], i[2][...], i[3][...])):
            o_ref[...] = val

    return _rowwise(body, rows=g.shape[0], tl=tr, ins=[_row(a, tr) for a in (g, w, m, v)],
                    outs=[("row", LANE, F32)] * 3, name=name)


def _adam_shard(recv, w, m, v, *, layer, name, tr, tie=None, others=None):
    depth, r, c = w.shape
    assert r % tr == 0 and recv.shape[2] == c
    blk = pl.BlockSpec((None, tr, c), lambda i: (layer, i, 0))
    extra = ([] if tie is None else [tie]) + list(others or [])

    def body(recv_ref, w_ref, m_ref, v_ref, *refs):
        g_ref, d_ref, nm_ref, nv_ref = refs[len(extra):]
        g = recv_ref[0].astype(F32)
        for p in range(1, N_DEV):
            g = g + recv_ref[p].astype(F32)
        g_ref[...] = g
        d_ref[...], nm_ref[...], nv_ref[...] = _adamw(w_ref[...], g, m_ref[...], v_ref[...])

    first = 4 + (tie is not None)
    return pl.pallas_call(
        body, name=name, grid=(r // tr,),
        in_specs=[pl.BlockSpec((N_DEV, tr, c), lambda i: (0, i, 0)), blk, blk, blk]
        + [pl.BlockSpec(memory_space=pl.ANY)] * len(extra),
        out_specs=[blk] * 4, out_shape=[jax.ShapeDtypeStruct(w.shape, F32)] * 4,
        input_output_aliases={first + j: j for j in range(4)} if others else {},
        compiler_params=_params("parallel"),
    )(recv, w, m, v, *extra)


def _position():
    x, y, c = lax.axis_index("x"), lax.axis_index("y"), lax.axis_index("c")
    return x, y, c, 4 * x + 2 * y + c


def _coords(p):
    return p // 4, (p // 2) % 2, p % 2


def _block_of(ref, kind, p, nb):
    if kind == "col":
        return ref.at[:, pl.ds(pl.multiple_of(p * nb, LANE), nb)]
    return ref.at[pl.ds(pl.multiple_of(p * nb, SUBLANE), nb), :]


def _all_gather(shards, kinds, *, name):
    n = len(shards)
    out_shape = []
    for s, kind in zip(shards, kinds):
        r, c = s.shape
        out_shape.append(jax.ShapeDtypeStruct((r, c * N_DEV) if kind == "col" else (r * N_DEV, c), s.dtype))

    def body(*refs):
        shard_refs, full_refs = refs[:n], refs[n:2 * n]
        send_sems, recv_sems, local_sems = refs[2 * n:]
        x, y, c, me = _position()
        sibling = (x, y, 1 - c)
        chips = [(1 - x, y), (x, 1 - y), (1 - x, 1 - y)]

        def block(t, dev):
            nb = shards[t].shape[1 if kinds[t] == "col" else 0]
            return _block_of(full_refs[t], kinds[t], 4 * dev[0] + 2 * dev[1] + dev[2], nb)

        def copy(t, k, dev, to, src=None):
            return pltpu.make_async_remote_copy(
                src_ref=block(t, dev) if src is None else src, dst_ref=block(t, dev),
                send_sem=send_sems.at[t, k], recv_sem=recv_sems.at[t, k], device_id=to, device_id_type=MESH)

        mine, first, passed = [], [], []
        for t in range(n):
            mine.append(pltpu.make_async_copy(shard_refs[t], block(t, (x, y, c)), local_sems.at[t]))
            mine[-1].start()
            first.append(copy(t, 0, (x, y, c), sibling, src=shard_refs[t]))
            first += [copy(t, 1 + j, (x, y, c), (*chip, c), src=shard_refs[t]) for j, chip in enumerate(chips)]
        for cp in first:
            cp.start()
        for j, chip in enumerate(chips):
            for t in range(n):
                copy(t, 1 + j, (*chip, c), (x, y, c)).wait_recv()
                passed.append(copy(t, 4 + j, (*chip, c), sibling))
                passed[-1].start()
        for t in range(n):
            copy(t, 0, sibling, (x, y, c)).wait_recv()
            for j, chip in enumerate(chips):
                copy(t, 4 + j, (*chip, 1 - c), (x, y, c)).wait_recv()
        for cp in first + passed:
            cp.wait_send()
        for cp in mine:
            cp.wait()

    any_spec = pl.BlockSpec(memory_space=pl.ANY)
    return pl.pallas_call(
        body, name=name, in_specs=[any_spec] * n, out_specs=[any_spec] * n, out_shape=out_shape,
        scratch_shapes=[pltpu.SemaphoreType.DMA((n, N_DEV - 1)), pltpu.SemaphoreType.DMA((n, N_DEV - 1)),
                        pltpu.SemaphoreType.DMA((n,))],
        compiler_params=pltpu.CompilerParams(has_side_effects=True),
    )(*shards)


class _GatherRoute:
    def __init__(self, shards, kinds):
        self.kinds = kinds
        self.nb = [s.shape[1 if k == "col" else 0] for s, k in zip(shards, kinds)]

    def lands(self, shards):
        return [lax.empty((s.shape[0], s.shape[1] * N_DEV) if k == "col" else (s.shape[0] * N_DEV, s.shape[1]), s.dtype)
                for s, k in zip(shards, self.kinds)]

    def src(self, t, src_refs, me, p):
        return src_refs[t]

    def dst(self, t, land_refs, me, origin):
        return _block_of(land_refs[t], self.kinds[t], origin, self.nb[t])


class _ScatterRoute:
    def __init__(self, grads, kinds):
        self.kinds = kinds
        self.nb = [g.shape[1 if k == "col" else 0] // N_DEV for g, k in zip(grads, kinds)]

    def lands(self, grads):
        return [lax.empty((N_DEV, g.shape[0], nb) if k == "col" else (N_DEV, nb, g.shape[1]), g.dtype)
                for g, k, nb in zip(grads, self.kinds, self.nb)]

    def src(self, t, src_refs, me, p):
        return _block_of(src_refs[t], self.kinds[t], p, self.nb[t])

    def dst(self, t, land_refs, me, origin):
        return land_refs[t].at[origin]


_HBM = pl.BlockSpec(memory_space=pltpu.HBM)
_SEM = pl.BlockSpec(memory_space=pltpu.SEMAPHORE)
_FLOWING = pltpu.SideEffectType.DATAFLOW_SIDE_EFFECTING


def _exchange_start(srcs, lands, route, after, *, name):
    n = len(srcs)

    def body(*refs):
        src_refs, land_refs = refs[:n], refs[n:2 * n]
        send_sems, recv_sems = refs[2 * n + 1:2 * n + 3]
        token = refs[-1]
        _, _, _, me = _position()
        for t in range(n):
            for k in range(1, N_DEV):
                p = (me + k) % N_DEV
                pltpu.make_async_remote_copy(
                    src_ref=route.src(t, src_refs, me, p), dst_ref=route.dst(t, land_refs, me, me),
                    send_sem=send_sems.at[t * N_DEV + k], recv_sem=recv_sems.at[t * N_DEV + k], device_id=_coords(p),
                    device_id_type=MESH).start()
        token[...] = jnp.zeros_like(token)

    hbm = lambda a: pltpu.HBM(a.shape, a.dtype)
    sems = pltpu.SemaphoreType.DMA((n * N_DEV,))
    out = pl.pallas_call(
        body, name=name,
        out_shape=(sems, sems, *[hbm(a) for a in srcs], *[hbm(a) for a in lands],
                   jax.ShapeDtypeStruct((SUBLANE, LANE), F32)),
        in_specs=[_HBM] * (2 * n) + [pl.BlockSpec(memory_space=pl.ANY)],
        out_specs=(_SEM, _SEM, *[_HBM] * (2 * n), pl.BlockSpec(memory_space=pltpu.VMEM)),
        input_output_aliases={i: 2 + i for i in range(2 * n)},
        compiler_params=pltpu.CompilerParams(has_side_effects=_FLOWING),
    )(*[pltpu.with_memory_space_constraint(a, pltpu.HBM) for a in (*srcs, *lands)], after)
    return out[0], out[1], out[2:2 + n], out[2 + n:2 + 2 * n], out[-1]


def _exchange_wait(started, route, after, *, name):
    send_sems, recv_sems, srcs, lands, _ = started
    n = len(srcs)

    def body(*refs):
        src_refs, land_refs = refs[:n], refs[n:2 * n]
        send_ref, recv_ref = refs[2 * n:2 * n + 2]
        own_sems = refs[-1]
        _, _, _, me = _position()
        for t in range(n):
            for k in range(1, N_DEV):
                p, q = (me + k) % N_DEV, (me + N_DEV - k) % N_DEV
                cp = pltpu.make_async_remote_copy(
                    src_ref=route.src(t, src_refs, me, p), dst_ref=route.dst(t, land_refs, me, q),
                    send_sem=send_ref.at[t * N_DEV + k], recv_sem=recv_ref.at[t * N_DEV + k], device_id=_coords(q),
                    device_id_type=MESH)
                cp.wait_send()
                cp.wait_recv()
        own = [pltpu.make_async_copy(route.src(t, src_refs, me, me), route.dst(t, land_refs, me, me), own_sems.at[t])
               for t in range(n)]
        for cp in own:
            cp.start()
        for cp in own:
            cp.wait()

    hbm = lambda a: pltpu.HBM(a.shape, a.dtype)
    out = pl.pallas_call(
        body, name=name, out_shape=(*[hbm(a) for a in srcs], *[hbm(a) for a in lands]),
        in_specs=[_HBM] * (2 * n) + [_SEM, _SEM, pl.BlockSpec(memory_space=pl.ANY)],
        out_specs=tuple([_HBM] * (2 * n)), input_output_aliases={i: i for i in range(2 * n)},
        scratch_shapes=[pltpu.SemaphoreType.DMA((n,))],
        compiler_params=pltpu.CompilerParams(has_side_effects=_FLOWING),
    )(*srcs, *lands, send_sems, recv_sems, after)
    return list(out[n:])


def _all_reduce(part, *, name):
    _, r, _ = part.shape

    def body(part_ref, tot_ref, recv_ref, send1, recv1, send2, recv2):
        _, _, _, me = _position()

        def scatter(k, to_me=False):
            p = (me + N_DEV - k) % N_DEV if to_me else (me + k) % N_DEV
            return pltpu.make_async_remote_copy(
                src_ref=part_ref.at[me if to_me else p], dst_ref=recv_ref.at[p if to_me else me],
                send_sem=send1.at[k], recv_sem=recv1.at[k], device_id=_coords(p), device_id_type=MESH)

        def gather(k, to_me=False):
            p = (me + N_DEV - k) % N_DEV if to_me else (me + k) % N_DEV
            return pltpu.make_async_remote_copy(
                src_ref=tot_ref.at[me], dst_ref=tot_ref.at[p if to_me else me],
                send_sem=send2.at[k], recv_sem=recv2.at[k], device_id=_coords(p), device_id_type=MESH)

        for k in range(1, N_DEV):
            scatter(k).start()
        recv_ref[me] = part_ref[me]
        for k in range(1, N_DEV):
            scatter(k, to_me=True).wait_recv()
        total = recv_ref[0]
        for q in range(1, N_DEV):
            total = total + recv_ref[q]
        tot_ref[me] = total
        for k in range(1, N_DEV):
            gather(k).start()
        for k in range(1, N_DEV):
            gather(k, to_me=True).wait_recv()
        for k in range(1, N_DEV):
            scatter(k).wait_send()
            gather(k).wait_send()

    vmem = pl.BlockSpec(memory_space=pltpu.VMEM)
    return pl.pallas_call(
        body, name=name, in_specs=[vmem], out_specs=vmem, out_shape=jax.ShapeDtypeStruct(part.shape, F32),
        scratch_shapes=[pltpu.VMEM(part.shape, F32)] + [pltpu.SemaphoreType.DMA((N_DEV,))] * 4,
        compiler_params=pltpu.CompilerParams(has_side_effects=True, vmem_limit_bytes=VMEM_LIMIT),
    )(part)


def _round_up(n, m):
    return (n + m - 1) // m * m


def _row_tile(rows):
    return next(t for t in (256, 128, 64, 32, 16) if rows % t == 0)


def _local_step(x, target, small, depth, weights_of, grads_done, *, qkw):
    l, d = x.shape
    groups = d // SSM_GROUP
    tiles = groups // TILE_GROUPS
    half = qkw // RET_HEADS // 2
    cb0 = 2 * qkw // d
    inv = 1.0 / (ROPE_BASE ** (jnp.arange(half, dtype=F32) / half))
    ang = jnp.arange(l, dtype=F32)[:, None] * inv[None, :]
    cos, sin = jnp.cos(ang), jnp.sin(ang)
    rep = jnp.repeat(jnp.eye(SSM_STATE, dtype=F32), SSM_GROUP, axis=1)
    row2 = lambda a: a.reshape(1, -1)

    saved = []
    for i in range(depth):
        full_i, x = weights_of(i, x)
        w_in, w_glu, w_out, w_gate, w_up, w_down = full_i
        n = f"l{i}_"
        g_mix, g_ffn = row2(small["ln_mix_g"][i]), row2(small["ln_ffn_g"][i])
        dskip, b_glu = row2(small["ssm_d"][i]), row2(small["b_glu"][i])
        lg = small["ret_log_gamma"][i]
        h = _norm_fwd(x, g_mix, name=n + "norm_mix")
        proj = _matmul(h, w_in, mode="nn", out_dtype=F32, name=n + "proj")
        q_rot, k_rot, v_bf = _ret_prep(proj, cos, sin, qkw=qkw, d=d, name=n + "ret_prep")
        y_raw = _ret_fwd(lg, q_rot, k_rot, v_bf, name=n + "ret_fwd")
        par = [small["ssm_a_re"][i].reshape(2 * groups, SSM_STATE), small["ssm_a_im"][i].reshape(2 * groups, SSM_STATE),
               small["ssm_log_dt"][i].reshape(2 * groups, 1),
               small["ssm_b_re"][i].reshape(2 * groups, SSM_STATE * SSM_GROUP),
               small["ssm_b_im"][i].reshape(2 * groups, SSM_STATE * SSM_GROUP), rep]
        lam_re, lam_im, bbar_re, bbar_im = _s5_prep(*par, name=n + "s5_prep")
        s5 = [_b_tiles(bbar_re, tiles).astype(BF16), _b_tiles(bbar_im, tiles).astype(BF16),
              lam_re.reshape(2, tiles, 1, TILE_N), lam_im.reshape(2, tiles, 1, TILE_N),
              _c_tiles(small["ssm_c_re"][i], tiles).astype(BF16), _c_tiles(small["ssm_c_im"][i], tiles).astype(BF16)]
        u_seg = _to_segments(proj[:, (cb0 + 2) * d:(cb0 + 3) * d])
        y_seg = _s5_fwd(u_seg, *s5, d=0, name=n + "s5_fwd_f")
        y_seg = _s5_fwd(u_seg, *s5, d=1, add=y_seg, name=n + "s5_fwd_b")
        y_s5 = _from_segments(y_seg)
        ys, ys_bf = _s5_post(y_s5, proj, cb0 + 2, dskip, name=n + "s5_post")
        glu = _matmul(ys_bf, w_glu, mode="nn", out_dtype=F32, name=n + "glu")
        merged = _merge(y_raw, proj, ys, glu, b_glu, cb0=cb0, name=n + "merge")
        x1 = _matmul(merged, w_out, mode="nn", out_dtype=F32, res=x, name=n + "out")
        h2 = _norm_fwd(x1, g_ffn, name=n + "norm_ffn")
        gate = _matmul(h2, w_gate, mode="nn", out_dtype=F32, name=n + "gate")
        up = _matmul(h2, w_up, mode="nn", out_dtype=F32, name=n + "up")
        act = _ffn_act(gate, up, name=n + "act")
        x2 = _matmul(act, w_down, mode="nn", out_dtype=F32, res=x1, name=n + "down")
        saved.append(dict(full=full_i, x=x, h=h, proj=proj, q_rot=q_rot, k_rot=k_rot, v_bf=v_bf, y_raw=y_raw, par=par, s5=s5,
                          u_seg=u_seg, y_s5=y_s5, ys=ys, ys_bf=ys_bf, glu=glu, merged=merged, x1=x1, h2=h2, gate=gate,
                          up=up, act=act))
        x = x2

    dx, dg_final, loss = _final(x, row2(small["ln_final_g"]), target, name="final")

    sg = {k: [None] * depth for k in ("ln_mix_g", "ret_log_gamma", "ssm_a_re", "ssm_a_im", "ssm_log_dt", "ssm_b_re",
                                      "ssm_b_im", "ssm_c_re", "ssm_c_im", "ssm_d", "b_glu", "ln_ffn_g")}
    for i in reversed(range(depth)):
        s = saved[i]
        big = [None] * len(BIG)
        w_in, w_glu, w_out, w_gate, w_up, w_down = s["full"]
        n = f"l{i}_b_"
        g_mix, g_ffn = row2(small["ln_mix_g"][i]), row2(small["ln_ffn_g"][i])
        dskip, b_glu = row2(small["ssm_d"][i]), row2(small["b_glu"][i])
        lg = small["ret_log_gamma"][i]
        dact = _matmul(dx, w_down, mode="nt", out_dtype=F32, name=n + "dact")
        big[5] = _matmul(s["act"], dx, mode="tn", out_dtype=BF16, name=n + "dw_down")
        dgate, dup = _ffn_act_bwd(s["gate"], s["up"], dact, name=n + "act")
        dh2 = _matmul(dgate, w_gate, mode="nt", out_dtype=F32, name=n + "dh2_gate")
        dh2 = _matmul(dup, w_up, mode="nt", out_dtype=F32, res=dh2, name=n + "dh2_up")
        big[3] = _matmul(s["h2"], dgate, mode="tn", out_dtype=BF16, name=n + "dw_gate")
        big[4] = _matmul(s["h2"], dup, mode="tn", out_dtype=BF16, name=n + "dw_up")
        dx1, dgf = _norm_bwd(s["x1"], g_ffn, dh2, dx, name=n + "norm_ffn")
        sg["ln_ffn_g"][i] = dgf[0]
        dmerged = _matmul(dx1, w_out, mode="nt", out_dtype=F32, name=n + "dmerged")
        big[2] = _matmul(s["merged"], dx1, mode="tn", out_dtype=BF16, name=n + "dw_out")
        dy_raw, dg, dgate_r, dgate_s, dglu, dys_a, db_glu = _merge_bwd(
            s["y_raw"], s["proj"], s["ys"], s["glu"], b_glu, dmerged, cb0=cb0, name=n + "merge")
        sg["b_glu"][i] = db_glu[0]
        dys = _matmul(dglu, w_glu, mode="nt", out_dtype=F32, res=dys_a, name=n + "dys")
        big[1] = _matmul(s["ys_bf"], dglu, mode="tn", out_dtype=BF16, name=n + "dw_glu")
        dpre, dd = _s5_post_bwd(s["y_s5"], s["proj"], cb0 + 2, dskip, dys, name=n + "s5_post")
        sg["ssm_d"][i] = dd[0]
        dpre_seg = _to_segments(dpre)
        r_f = _s5_bwd(s["u_seg"], dpre_seg, *s["s5"], d=0, name=n + "s5_bwd_f")
        r_b = _s5_bwd(s["u_seg"], dpre_seg, *s["s5"], d=1, add=r_f[0], name=n + "s5_bwd_b")
        du = _du_combine(dpre, dskip, _from_segments(r_b[0]), name=n + "du")
        both = lambda k: jnp.stack([r_f[k], r_b[k]])
        cts = [both(5).reshape(2 * groups, SSM_STATE), both(6).reshape(2 * groups, SSM_STATE),
               _b_untile(both(1), tiles), _b_untile(both(2), tiles)]
        da_re, da_im, dldt, db_re, db_im = _s5_prep_bwd(*s["par"], cts, name=n + "s5_prep")
        sg["ssm_a_re"][i] = da_re.reshape(2, groups, SSM_STATE)
        sg["ssm_a_im"][i] = da_im.reshape(2, groups, SSM_STATE)
        sg["ssm_log_dt"][i] = dldt.reshape(2, groups)
        sg["ssm_b_re"][i] = db_re.reshape(2, groups, SSM_STATE, SSM_GROUP)
        sg["ssm_b_im"][i] = db_im.reshape(2, groups, SSM_STATE, SSM_GROUP)
        sg["ssm_c_re"][i] = _c_untile(both(3), tiles)
        sg["ssm_c_im"][i] = _c_untile(both(4), tiles)
        dq_rot, dk_rot, dv, dlg = _ret_bwd(lg, s["q_rot"], s["k_rot"], s["v_bf"], dy_raw, name=n + "ret_bwd")
        sg["ret_log_gamma"][i] = dlg[:, :2, 0].T
        dq, dk = _ret_prep_bwd(dq_rot, dk_rot, cos, sin, name=n + "ret_prep")
        dproj = jnp.concatenate([dq, dk, dv.astype(BF16), dg, du, dgate_r, dgate_s], axis=1)
        dh = _matmul(dproj, w_in, mode="nt", out_dtype=F32, name=n + "dh")
        big[0] = _matmul(s["h"], dproj, mode="tn", out_dtype=BF16, name=n + "dw_in")
        dx, dgm = _norm_bwd(s["x"], g_mix, dh, dx1, name=n + "norm_mix")
        sg["ln_mix_g"][i] = dgm[0]
        dx = grads_done(i, big, dx)

    small_grads = {k: jnp.stack(v) for k, v in sg.items()}
    small_grads["ln_final_g"] = dg_final[0]
    return loss, dx, small_grads


BIG = ("w_in", "w_glu", "w_out", "w_ffn_gate", "w_ffn_up", "w_ffn_down")
BIG_KINDS = ("col", "row", "row", "col", "col", "row")
SMALL = ("ln_mix_g", "ret_log_gamma", "ssm_a_re", "ssm_a_im", "ssm_log_dt", "ssm_b_re", "ssm_b_im", "ssm_c_re",
         "ssm_c_im", "ssm_d", "b_glu", "ln_ffn_g", "ln_final_g")
WEIGHTS = ("ln_mix_g", "w_in", "ret_log_gamma", "ssm_a_re", "ssm_a_im", "ssm_log_dt", "ssm_b_re", "ssm_b_im",
           "ssm_c_re", "ssm_c_im", "ssm_d", "w_glu", "b_glu", "w_out", "ln_ffn_g", "w_ffn_gate", "w_ffn_up",
           "w_ffn_down", "ln_final_g")


def _pad_to(a, axis, size):
    pad = [(0, 0)] * a.ndim
    pad[axis] = (0, size - a.shape[axis])
    return jnp.pad(a, pad)


def _flatten_small(tree, extra):
    def as_rows(a):
        a = a.reshape(-1).astype(F32)
        return _pad_to(a, 0, _round_up(a.shape[0], SUBLANE * LANE)).reshape(-1, LANE)

    flat = jnp.concatenate([as_rows(tree[k]) for k in SMALL] + [as_rows(extra)])
    return _pad_to(flat, 0, _round_up(flat.shape[0], FLAT_ROWS))


def _unflatten_small(flat, like):
    out, at = {}, 0
    for k in SMALL:
        n = like[k].size
        rows = _round_up(n, SUBLANE * LANE) // LANE
        out[k] = flat[at:at + rows].reshape(-1)[:n].reshape(like[k].shape)
        at += rows
    return out, flat[at, 0]


def kernel(x, ln_mix_g, w_in, ret_log_gamma, ssm_a_re, ssm_a_im, ssm_log_dt, ssm_b_re, ssm_b_im, ssm_c_re, ssm_c_im, ssm_d, w_glu, b_glu, w_out, ln_ffn_g, w_ffn_gate, w_ffn_up, w_ffn_down, ln_final_g, loss_target, m_ln_mix_g, m_w_in, m_ret_log_gamma, m_ssm_a_re, m_ssm_a_im, m_ssm_log_dt, m_ssm_b_re, m_ssm_b_im, m_ssm_c_re, m_ssm_c_im, m_ssm_d, m_w_glu, m_b_glu, m_w_out, m_ln_ffn_g, m_w_ffn_gate, m_w_ffn_up, m_w_ffn_down, m_ln_final_g, v_ln_mix_g, v_w_in, v_ret_log_gamma, v_ssm_a_re, v_ssm_a_im, v_ssm_log_dt, v_ssm_b_re, v_ssm_b_im, v_ssm_c_re, v_ssm_c_im, v_ssm_d, v_w_glu, v_b_glu, v_w_out, v_ln_ffn_g, v_w_ffn_gate, v_w_ffn_up, v_w_ffn_down, v_ln_final_g):
    w = dict(ln_mix_g=ln_mix_g, w_in=w_in, ret_log_gamma=ret_log_gamma, ssm_a_re=ssm_a_re, ssm_a_im=ssm_a_im, ssm_log_dt=ssm_log_dt, ssm_b_re=ssm_b_re, ssm_b_im=ssm_b_im, ssm_c_re=ssm_c_re, ssm_c_im=ssm_c_im, ssm_d=ssm_d, w_glu=w_glu, b_glu=b_glu, w_out=w_out, ln_ffn_g=ln_ffn_g, w_ffn_gate=w_ffn_gate, w_ffn_up=w_ffn_up, w_ffn_down=w_ffn_down, ln_final_g=ln_final_g)
    m = dict(ln_mix_g=m_ln_mix_g, w_in=m_w_in, ret_log_gamma=m_ret_log_gamma, ssm_a_re=m_ssm_a_re, ssm_a_im=m_ssm_a_im, ssm_log_dt=m_ssm_log_dt, ssm_b_re=m_ssm_b_re, ssm_b_im=m_ssm_b_im, ssm_c_re=m_ssm_c_re, ssm_c_im=m_ssm_c_im, ssm_d=m_ssm_d, w_glu=m_w_glu, b_glu=m_b_glu, w_out=m_w_out, ln_ffn_g=m_ln_ffn_g, w_ffn_gate=m_w_ffn_gate, w_ffn_up=m_w_ffn_up, w_ffn_down=m_w_ffn_down, ln_final_g=m_ln_final_g)
    v = dict(ln_mix_g=v_ln_mix_g, w_in=v_w_in, ret_log_gamma=v_ret_log_gamma, ssm_a_re=v_ssm_a_re, ssm_a_im=v_ssm_a_im, ssm_log_dt=v_ssm_log_dt, ssm_b_re=v_ssm_b_re, ssm_b_im=v_ssm_b_im, ssm_c_re=v_ssm_c_re, ssm_c_im=v_ssm_c_im, ssm_d=v_ssm_d, w_glu=v_w_glu, b_glu=v_b_glu, w_out=v_w_out, ln_ffn_g=v_ln_ffn_g, w_ffn_gate=v_w_ffn_gate, w_ffn_up=v_w_ffn_up, w_ffn_down=v_w_ffn_down, ln_final_g=v_ln_final_g)
    depth, d, nb_in = w_in.shape
    qkw = (nb_in * N_DEV - 5 * d) // 2
    nb_ffn = w_ffn_gate.shape[2]
    nb_pad = _round_up(nb_ffn, LANE)
    pad_axis = {"w_ffn_gate": 2, "w_ffn_up": 2, "w_ffn_down": 1}

    assert depth == 2
    padded = {k: w[k] if k not in pad_axis else _pad_to(w[k], pad_axis[k], nb_pad) for k in BIG}
    shards = [[padded[k][i].astype(BF16) for k in BIG] for i in range(depth)]
    full0 = _all_gather(shards[0], BIG_KINDS, name="gather_l0")
    gather1 = _GatherRoute(shards[1], BIG_KINDS)
    gathering = _exchange_start(shards[1], gather1.lands(shards[1]), gather1, full0[0], name="gather_l1_start")

    def weights_of(i, act):
        if i == 0:
            return full0, act + gathering[4][0, 0]
        return _exchange_wait(gathering, gather1, act, name="gather_l1_wait"), act

    pending = {}

    def grads_done(i, big, dact):
        route = _ScatterRoute(big, BIG_KINDS)
        started = _exchange_start(big, route.lands(big), route, dact, name=f"scatter_l{i}_start")
        if i == depth - 1:
            pending[i] = started, route
            return dact + started[4][0, 0]
        last, route_last = pending.pop(i + 1)
        pending["recv_last"] = _exchange_wait(last, route_last, dact, name=f"scatter_l{i + 1}_wait")
        pending[i] = started, route
        return dact

    small = {k: w[k] for k in SMALL}
    loss, dx, small_grads = _local_step(x[0], loss_target[0], small, depth, weights_of, grads_done, qkw=qkw)

    ops = {k: [_pad_to(a[k], 2, nb_pad) for a in (w, m, v)] if k in ("w_ffn_gate", "w_ffn_up") else [a[k] for a in (w, m, v)]
           for k in BIG}
    tiles = {k: _row_tile(ops[k][0].shape[1]) for k in BIG}
    started, route = pending.pop(0)
    half = {k: _adam_shard(pending["recv_last"][t], *ops[k], layer=1, name="adam_l1_" + k, tr=tiles[k], tie=started[4])
            for t, k in enumerate(BIG)}

    part = _flatten_small(small_grads, loss[0, :1])
    rows = part.shape[0]
    total = _all_reduce(part.reshape(N_DEV, rows // N_DEV, LANE), name="reduce_small").reshape(rows, LANE)
    zero = jnp.zeros((1,), F32)
    flat = [_flatten_small({k: a[k] for k in SMALL}, zero) for a in (w, m, v)]
    upd = _adam_flat(total, *flat, name="adam_small")
    grads, delta, new_m, new_v = {}, {}, {}, {}
    g_small, loss_total = _unflatten_small(total, small)
    grads.update(g_small)
    for dst, u in zip((delta, new_m, new_v), upd):
        dst.update(_unflatten_small(u, small)[0])

    recv = _exchange_wait(started, route, upd[0], name="scatter_l0_wait")
    for t, k in enumerate(BIG):
        res = _adam_shard(recv[t], *ops[k], layer=0, name="adam_l0_" + k, tr=tiles[k], others=half[k])
        if k in ("w_ffn_gate", "w_ffn_up"):
            res = [r[:, :, :nb_ffn] for r in res]
        grads[k], delta[k], new_m[k], new_v[k] = res

    return (loss_total, dx[None], *[grads[k] for k in WEIGHTS], *[delta[k] for k in WEIGHTS],
            *[new_m[k] for k in WEIGHTS], *[new_v[k] for k in WEIGHTS])
```

```python
import math

import jax
import jax.numpy as jnp
from jax import lax
from jax.experimental import pallas as pl
from jax.experimental.pallas import tpu as pltpu

F32 = jnp.float32
BF16 = jnp.bfloat16
MESH = pl.DeviceIdType.MESH

N_DEV = 8
RET_HEADS = 4
CHUNK = 128
ROPE_BASE = 10000.0
SSM_GROUP = 16
SSM_STATE = 64
TILE_GROUPS = 8
TILE_U = TILE_GROUPS * SSM_GROUP
TILE_N = TILE_GROUPS * SSM_STATE
LANE = 128
SUBLANE = 8
N_SEG = SUBLANE
N_LT = TILE_N // LANE
SCAN_UNROLL = 4
FLAT_ROWS = 1024
EPS = 1e-6
ADAM_LR = 0.001
ADAM_B1 = 0.9
ADAM_B2 = 0.999
ADAM_EPS = 1e-08
ADAM_WD = 0.01
ADAM_STEP = 10
VMEM_LIMIT = 56 * 1024 * 1024


def _params(*sem):
    return pltpu.CompilerParams(dimension_semantics=sem or None, vmem_limit_bytes=VMEM_LIMIT)


def _dg(a, b, ca, cb):
    return lax.dot_general(a.astype(BF16), b.astype(BF16), (((ca,), (cb,)), ((), ())),
                           preferred_element_type=F32)


@jax.custom_vjp
def _dnn(a, b):
    return _dg(a, b, 1, 0)


@jax.custom_vjp
def _dnt(a, b):
    return _dg(a, b, 1, 1)


@jax.custom_vjp
def _dtn(a, b):
    return _dg(a, b, 0, 0)


_dnn.defvjp(lambda a, b: (_dnn(a, b), (a, b)), lambda r, g: (_dnt(g, r[1]), _dtn(r[0], g)))
_dnt.defvjp(lambda a, b: (_dnt(a, b), (a, b)), lambda r, g: (_dnn(g, r[1]), _dtn(g, r[0])))
_dtn.defvjp(lambda a, b: (_dtn(a, b), (a, b)), lambda r, g: (_dnt(r[1], g), _dnn(r[0], g)))


def _matmul(a, b, *, mode, out_dtype, name, res=None, tm=1024, tn=1024, tk=512):
    if mode == "nn":
        (m, k), n = a.shape, b.shape[1]
    elif mode == "nt":
        (m, k), n = a.shape, b.shape[0]
    else:
        (k, m), n = a.shape, b.shape[1]
    tm, tn, tk = min(tm, m), min(tn, n), min(tk, k)
    assert m % tm == 0 and n % tn == 0 and k % tk == 0, (name, m, n, k)
    nk = k // tk
    if mode == "tn":
        a_spec = pl.BlockSpec((tk, tm), lambda i, j, kk: (kk, i))
    else:
        a_spec = pl.BlockSpec((tm, tk), lambda i, j, kk: (i, kk))
    if mode == "nt":
        b_spec = pl.BlockSpec((tn, tk), lambda i, j, kk: (j, kk))
    else:
        b_spec = pl.BlockSpec((tk, tn), lambda i, j, kk: (kk, j))
    ca, cb = {"nn": (1, 0), "nt": (1, 1), "tn": (0, 0)}[mode]
    o_spec = pl.BlockSpec((tm, tn), lambda i, j, kk: (i, j))
    has_res = res is not None

    def body(*refs):
        if has_res:
            a_ref, b_ref, r_ref, o_ref, acc = refs
        else:
            a_ref, b_ref, o_ref, acc = refs
        kk = pl.program_id(2)

        @pl.when(kk == 0)
        def _():
            acc[...] = r_ref[...] if has_res else jnp.zeros_like(acc)

        acc[...] += _dg(a_ref[...], b_ref[...], ca, cb)

        @pl.when(kk == nk - 1)
        def _():
            o_ref[...] = acc[...].astype(out_dtype)

    return pl.pallas_call(
        body, name=name, grid=(m // tm, n // tn, nk),
        in_specs=[a_spec, b_spec] + ([o_spec] if has_res else []),
        out_specs=o_spec, out_shape=jax.ShapeDtypeStruct((m, n), out_dtype),
        scratch_shapes=[pltpu.VMEM((tm, tn), F32)],
        compiler_params=_params("parallel", "parallel", "arbitrary"),
    )(*((a, b, res) if has_res else (a, b)))


def _row(arr, tl, width=None, cb=0):
    width = arr.shape[1] if width is None else width
    return arr, pl.BlockSpec((tl, width), lambda i, cb=cb: (i, cb))


def _par(arr):
    return arr, pl.BlockSpec(arr.shape, lambda i: (0,) * arr.ndim)


def _rowwise(body, *, rows, tl, ins, outs, name):
    arrays = [a for a, _ in ins]
    in_specs = [s for _, s in ins]
    out_shape, out_specs, acc_ids = [], [], []
    for n, o in enumerate(outs):
        if o[0] == "row":
            out_shape.append(jax.ShapeDtypeStruct((rows, o[1]), o[2]))
            out_specs.append(pl.BlockSpec((tl, o[1]), lambda i: (i, 0)))
        else:
            out_shape.append(jax.ShapeDtypeStruct((1, o[1]), F32))
            out_specs.append(pl.BlockSpec((1, o[1]), lambda i: (0, 0)))
            acc_ids.append(n)
    n_in = len(arrays)
    assert rows % tl == 0, (name, rows, tl)

    def wrapped(*refs):
        in_refs, out_refs = refs[:n_in], refs[n_in:]

        @pl.when(pl.program_id(0) == 0)
        def _():
            for n in acc_ids:
                out_refs[n][...] = jnp.zeros_like(out_refs[n])

        body(in_refs, out_refs)

    return pl.pallas_call(
        wrapped, name=name, grid=(rows // tl,), in_specs=in_specs, out_specs=out_specs,
        out_shape=out_shape, compiler_params=_params("arbitrary"),
    )(*arrays)


def _rms(x, g):
    return x * lax.rsqrt(jnp.mean(x * x, axis=-1, keepdims=True) + EPS) * g


def _norm_fwd(x, g, *, name, tl=256):
    def body(i, o):
        o[0][...] = _rms(i[0][...], i[1][...]).astype(BF16)

    return _rowwise(body, rows=x.shape[0], tl=tl, ins=[_row(x, tl), _par(g)],
                    outs=[("row", x.shape[1], BF16)], name=name)[0]


def _norm_bwd(x, g, dh, dres, *, name, tl=256):
    def body(i, o):
        _, vjp = jax.vjp(_rms, i[0][...], i[1][...])
        dx, dg = vjp(i[2][...])
        o[0][...] = i[3][...] + dx
        o[1][...] += dg

    d = x.shape[1]
    return _rowwise(body, rows=x.shape[0], tl=tl, ins=[_row(x, tl), _par(g), _row(dh, tl), _row(dres, tl)],
                    outs=[("row", d, F32), ("acc", d)], name=name)


def _final(x, g, target, *, name, tl=256):
    d = x.shape[1]

    def body(i, o):
        y, vjp = jax.vjp(_rms, i[0][...], i[1][...])
        err = y - i[2][...]
        dx, dg = vjp(err * (1.0 / d))
        o[0][...] = dx
        o[1][...] += dg
        o[2][...] += jnp.full((1, LANE), 0.5 / d, F32) * jnp.sum(err * err)

    return _rowwise(body, rows=x.shape[0], tl=tl, ins=[_row(x, tl), _par(g), _row(target, tl)],
                    outs=[("row", d, F32), ("acc", d), ("acc", LANE)], name=name)


def _rot(x, cos, sin, out_ref, col, scale=1.0, inverse=False):
    x1, x2 = x[:, :LANE], x[:, LANE:]
    if inverse:
        sin = -sin
    out_ref[:, col:col + LANE] = ((x1 * cos - x2 * sin) * scale).astype(out_ref.dtype)
    out_ref[:, col + LANE:col + 2 * LANE] = ((x1 * sin + x2 * cos) * scale).astype(out_ref.dtype)


def _ret_prep(proj, cos, sin, *, qkw, d, name, tl=256):
    dk = qkw // RET_HEADS
    assert dk == 2 * LANE and (2 * qkw) % d == 0

    def body(i, o):
        c, s = i[3][...], i[4][...]
        for h in range(RET_HEADS):
            _rot(i[0][:, h * dk:(h + 1) * dk], c, s, o[0], h * dk)
            _rot(i[1][:, h * dk:(h + 1) * dk], c, s, o[1], h * dk, scale=dk ** -0.5)
        o[2][...] = i[2][...].astype(BF16)

    return _rowwise(body, rows=proj.shape[0], tl=tl,
                    ins=[_row(proj, tl, qkw, 0), _row(proj, tl, qkw, 1), _row(proj, tl, d, 2 * qkw // d),
                         _row(cos, tl), _row(sin, tl)],
                    outs=[("row", qkw, BF16), ("row", qkw, BF16), ("row", d, BF16)], name=name)


def _ret_prep_bwd(dq_rot, dk_rot, cos, sin, *, name, tl=256):
    qkw = dq_rot.shape[1]
    dk = qkw // RET_HEADS

    def body(i, o):
        c, s = i[2][...], i[3][...]
        for h in range(RET_HEADS):
            _rot(i[0][:, h * dk:(h + 1) * dk], c, s, o[0], h * dk, inverse=True)
            _rot(i[1][:, h * dk:(h + 1) * dk], c, s, o[1], h * dk, scale=dk ** -0.5, inverse=True)

    return _rowwise(body, rows=dq_rot.shape[0], tl=tl,
                    ins=[_row(dq_rot, tl), _row(dk_rot, tl), _row(cos, tl), _row(sin, tl)],
                    outs=[("row", qkw, BF16), ("row", qkw, BF16)], name=name)


def _ret_weights(lgf, lgb):
    t = lax.broadcasted_iota(jnp.int32, (CHUNK, 1), 0).astype(F32)
    diff = (lax.broadcasted_iota(jnp.int32, (CHUNK, CHUNK), 0)
            - lax.broadcasted_iota(jnp.int32, (CHUNK, CHUNK), 1)).astype(F32)
    dmat = jnp.exp(jnp.where(diff >= 0, lgf * diff, -lgb * diff))
    return dict(dmat=dmat, wqf=jnp.exp(lgf * (t + 1.0)), wkf=jnp.exp(lgf * (CHUNK - 1.0 - t)),
                wqb=jnp.exp(lgb * (CHUNK - t)), wkb=jnp.exp(lgb * t))


def _ret_f_part(q, k, v, lgf, lgb, s_f):
    w = _ret_weights(lgf, lgb)
    y = _dnn(_dnt(q, k) * w["dmat"], v) + _dnn(q * w["wqf"], s_f)
    return y, _dtn(k * w["wkf"], v)


def _ret_b_part(q, k, v, lgb, s_b):
    w = _ret_weights(lgb, lgb)
    return _dnn(q * w["wqb"], s_b), _dtn(k * w["wkb"], v)


def _chunk(c):
    return pl.ds(pl.multiple_of(c * CHUNK, CHUNK), CHUNK)


def _ret_specs(l, qkw, d):
    dk, dv = qkw // RET_HEADS, d // RET_HEADS
    return dk, dv, [pl.BlockSpec(memory_space=pltpu.SMEM),
                    pl.BlockSpec((l, dk), lambda h: (0, h)), pl.BlockSpec((l, dk), lambda h: (0, h)),
                    pl.BlockSpec((l, dv), lambda h: (0, h))]


def _ret_fwd(lg, q, k, v, *, name):
    l, qkw = q.shape
    d = v.shape[1]
    nc = l // CHUNK
    dk, dv, in_specs = _ret_specs(l, qkw, d)

    def body(lg_ref, q_ref, k_ref, v_ref, y_ref, s_ref):
        h = pl.program_id(0)
        lgf = jnp.full((1, 1), lg_ref[0, h], F32)
        lgb = jnp.full((1, 1), lg_ref[1, h], F32)
        dec_f, dec_b = jnp.exp(lgf * CHUNK), jnp.exp(lgb * CHUNK)

        def load(c):
            r = _chunk(c)
            return r, q_ref[r, :].astype(F32), k_ref[r, :].astype(F32), v_ref[r, :].astype(F32)

        s_ref[...] = jnp.zeros_like(s_ref)

        def f_step(c, _):
            r, qc, kc, vc = load(c)
            y, kv = _ret_f_part(qc, kc, vc, lgf, lgb, s_ref[...])
            y_ref[r, :] = y
            s_ref[...] = dec_f * s_ref[...] + kv
            return 0

        lax.fori_loop(0, nc, f_step, 0)
        s_ref[...] = jnp.zeros_like(s_ref)

        def b_step(n, _):
            r, qc, kc, vc = load(nc - 1 - n)
            y, kv = _ret_b_part(qc, kc, vc, lgb, s_ref[...])
            y_ref[r, :] += y
            s_ref[...] = dec_b * s_ref[...] + kv
            return 0

        lax.fori_loop(0, nc, b_step, 0)

    return pl.pallas_call(
        body, name=name, grid=(RET_HEADS,), in_specs=in_specs,
        out_specs=pl.BlockSpec((l, dv), lambda h: (0, h)), out_shape=jax.ShapeDtypeStruct((l, d), F32),
        scratch_shapes=[pltpu.VMEM((dk, dv), F32)], compiler_params=_params("arbitrary"),
    )(lg, q, k, v)


def _ret_bwd(lg, q, k, v, dy, *, name):
    l, qkw = q.shape
    d = v.shape[1]
    nc = l // CHUNK
    dk, dv, in_specs = _ret_specs(l, qkw, d)

    def body(lg_ref, q_ref, k_ref, v_ref, dy_ref, dq_ref, dk_ref, dv_ref, dlg_ref, states, s_ref, sh_ref):
        h = pl.program_id(0)
        lgf = jnp.full((1, 1), lg_ref[0, h], F32)
        lgb = jnp.full((1, 1), lg_ref[1, h], F32)
        dec_f, dec_b = jnp.exp(lgf * CHUNK), jnp.exp(lgb * CHUNK)

        def load(c):
            r = _chunk(c)
            return (r, q_ref[r, :].astype(F32), k_ref[r, :].astype(F32), v_ref[r, :].astype(F32),
                    dy_ref[r, :].astype(F32))

        s_ref[...] = jnp.zeros_like(s_ref)

        def f_states(c, _):
            _, qc, kc, vc, _ = load(c)
            states[c] = s_ref[...]
            w = _ret_weights(lgf, lgb)
            s_ref[...] = dec_f * s_ref[...] + _dtn(kc * w["wkf"], vc)
            return 0

        lax.fori_loop(0, nc, f_states, 0)
        sh_ref[...] = jnp.zeros_like(sh_ref)

        def f_adj(n, carry):
            dlf, dlb, ddec = carry
            c = nc - 1 - n
            r, qc, kc, vc, dyc = load(c)
            sc = states[c]
            _, vjp = jax.vjp(_ret_f_part, qc, kc, vc, lgf, lgb, sc)
            dq, dkk, dvv, g_f, g_b, dsc = vjp((dyc, sh_ref[...]))
            dq_ref[r, :] = dq
            dk_ref[r, :] = dkk
            dv_ref[r, :] = dvv
            ddec = ddec + jnp.sum(sh_ref[...] * sc)
            sh_ref[...] = dsc + dec_f * sh_ref[...]
            return dlf + g_f, dlb + g_b, ddec

        z = jnp.zeros((1, 1), F32)
        dlf, dlb, ddec_f = lax.fori_loop(0, nc, f_adj, (z, z, z))

        s_ref[...] = jnp.zeros_like(s_ref)

        def b_states(n, _):
            c = nc - 1 - n
            _, qc, kc, vc, _ = load(c)
            states[c] = s_ref[...]
            w = _ret_weights(lgb, lgb)
            s_ref[...] = dec_b * s_ref[...] + _dtn(kc * w["wkb"], vc)
            return 0

        lax.fori_loop(0, nc, b_states, 0)
        sh_ref[...] = jnp.zeros_like(sh_ref)

        def b_adj(c, carry):
            dlb, ddec = carry
            r, qc, kc, vc, dyc = load(c)
            sc = states[c]
            _, vjp = jax.vjp(_ret_b_part, qc, kc, vc, lgb, sc)
            dq, dkk, dvv, g_b, dsc = vjp((dyc, sh_ref[...]))
            dq_ref[r, :] += dq
            dk_ref[r, :] += dkk
            dv_ref[r, :] += dvv
            ddec = ddec + jnp.sum(sh_ref[...] * sc)
            sh_ref[...] = dsc + dec_b * sh_ref[...]
            return dlb + g_b, ddec

        dlb, ddec_b = lax.fori_loop(0, nc, b_adj, (dlb, z))
        dlf = dlf + ddec_f * dec_f * CHUNK
        dlb = dlb + ddec_b * dec_b * CHUNK
        row = lax.broadcasted_iota(jnp.int32, (SUBLANE, LANE), 0)
        dlg_ref[...] = jnp.where(row == 0, dlf, jnp.where(row == 1, dlb, 0.0))

    head = lambda w: pl.BlockSpec((l, w), lambda h: (0, h))
    return pl.pallas_call(
        body, name=name, grid=(RET_HEADS,), in_specs=in_specs + [head(dv)],
        out_specs=[head(dk), head(dk), head(dv), pl.BlockSpec((None, SUBLANE, LANE), lambda h: (h, 0, 0))],
        out_shape=[jax.ShapeDtypeStruct((l, qkw), F32), jax.ShapeDtypeStruct((l, qkw), F32),
                   jax.ShapeDtypeStruct((l, d), F32), jax.ShapeDtypeStruct((RET_HEADS, SUBLANE, LANE), F32)],
        scratch_shapes=[pltpu.VMEM((nc, dk, dv), F32), pltpu.VMEM((dk, dv), F32), pltpu.VMEM((dk, dv), F32)],
        compiler_params=_params("arbitrary"),
    )(lg, q, k, v, dy)


def _s5_param_fn(a_re, a_im, log_dt, b_re, b_im, rep):
    dt = jnp.exp(log_dt)
    mag = jnp.exp(a_re * dt)
    lam_re, lam_im = mag * jnp.cos(a_im * dt), mag * jnp.sin(a_im * dt)
    n_re, n_im = lam_re - 1.0, lam_im
    den = a_re * a_re + a_im * a_im
    c_re = (n_re * a_re + n_im * a_im) / den
    c_im = (n_im * a_re - n_re * a_im) / den
    hi = lax.Precision.HIGHEST
    c_re = jnp.dot(c_re, rep, precision=hi, preferred_element_type=F32)
    c_im = jnp.dot(c_im, rep, precision=hi, preferred_element_type=F32)
    return lam_re, lam_im, c_re * b_re - c_im * b_im, c_re * b_im + c_im * b_re


def _s5_param_shapes(a_re, b_re):
    r, p = a_re.shape
    return [jax.ShapeDtypeStruct((r, p), F32)] * 2 + [jax.ShapeDtypeStruct(b_re.shape, F32)] * 2


def _s5_prep(a_re, a_im, log_dt, b_re, b_im, rep, *, name):
    def body(*refs):
        outs = _s5_param_fn(*[r[...] for r in refs[:6]])
        for o_ref, o in zip(refs[6:], outs):
            o_ref[...] = o

    return pl.pallas_call(body, name=name, out_shape=_s5_param_shapes(a_re, b_re),
                          compiler_params=_params())(a_re, a_im, log_dt, b_re, b_im, rep)


def _s5_prep_bwd(a_re, a_im, log_dt, b_re, b_im, rep, cts, *, name):
    def body(*refs):
        ins = [r[...] for r in refs[:6]]
        _, vjp = jax.vjp(lambda *p: _s5_param_fn(*p, ins[5]), *ins[:5])
        grads = vjp(tuple(r[...] for r in refs[6:10]))
        for o_ref, o in zip(refs[10:], grads):
            o_ref[...] = o

    shapes = [jax.ShapeDtypeStruct(t.shape, F32) for t in (a_re, a_im, log_dt, b_re, b_im)]
    return pl.pallas_call(body, name=name, out_shape=shapes,
                          compiler_params=_params())(a_re, a_im, log_dt, b_re, b_im, rep, *cts)


def _eye_tiles():
    return jnp.eye(TILE_GROUPS, dtype=F32)


def _b_tiles(bbar, tiles):
    t = bbar.reshape(2, tiles, TILE_GROUPS, SSM_STATE, SSM_GROUP).transpose(0, 1, 2, 4, 3)
    t = t[:, :, :, :, None, :] * _eye_tiles()[None, None, :, None, :, None]
    return t.reshape(2, tiles, TILE_U, TILE_N)


def _b_untile(dbt, tiles):
    t = dbt.reshape(2, tiles, TILE_GROUPS, SSM_GROUP, TILE_GROUPS, SSM_STATE)
    t = (t * _eye_tiles()[None, None, :, None, :, None]).sum(axis=4)
    return t.transpose(0, 1, 2, 4, 3).reshape(2 * tiles * TILE_GROUPS, SSM_STATE * SSM_GROUP)


def _c_tiles(c, tiles):
    t = c.reshape(2, tiles, TILE_GROUPS, SSM_GROUP, SSM_STATE).transpose(0, 1, 2, 4, 3)
    t = t[:, :, :, :, None, :] * _eye_tiles()[None, None, :, None, :, None]
    return t.reshape(2, tiles, TILE_N, TILE_U)


def _c_untile(dct, tiles):
    t = dct.reshape(2, tiles, TILE_GROUPS, SSM_STATE, TILE_GROUPS, SSM_GROUP)
    t = (t * _eye_tiles()[None, None, :, None, :, None]).sum(axis=4)
    return t.transpose(0, 1, 2, 4, 3).reshape(2, tiles * TILE_GROUPS, SSM_GROUP, SSM_STATE)


def _to_segments(a):
    l, w = a.shape
    return a.reshape(N_SEG, l // N_SEG, w).transpose(1, 0, 2).reshape(l, w)


def _from_segments(a):
    l, w = a.shape
    return a.reshape(l // N_SEG, N_SEG, w).transpose(1, 0, 2).reshape(l, w)


def _s5_scan(xr, xi, a_re, a_im, *, length, reverse, shifted=None):
    ls = length // N_SEG
    assert ls * N_SEG == length and ls & (ls - 1) == 0
    ar = [jnp.broadcast_to(a_re[:, c * LANE:(c + 1) * LANE], (N_SEG, LANE)) for c in range(N_LT)]
    ai = [jnp.broadcast_to(a_im[:, c * LANE:(c + 1) * LANE], (N_SEG, LANE)) for c in range(N_LT)]
    zero = jnp.zeros((N_SEG, LANE), F32)
    row = lax.broadcasted_iota(jnp.int32, (N_SEG, LANE), 0)

    def step_of(n):
        return (ls - 1 - n) if reverse else n

    def block(j):
        return pl.ds(j * N_SEG, N_SEG) if isinstance(j, int) else pl.ds(pl.multiple_of(j * N_SEG, N_SEG), N_SEG)

    def local(n, carry):
        rows = block(step_of(n))
        new = []
        for c in range(N_LT):
            cr, ci = carry[2 * c], carry[2 * c + 1]
            nr = ar[c] * cr - ai[c] * ci + xr[c, rows, :]
            ni = ar[c] * ci + ai[c] * cr + xi[c, rows, :]
            xr[c, rows, :] = nr
            xi[c, rows, :] = ni
            new += [nr, ni]
        return tuple(new)

    ends = lax.fori_loop(0, ls, local, (zero,) * (2 * N_LT), unroll=SCAN_UNROLL)

    init = []
    for c in range(N_LT):
        pr, pi = ar[c][0:1, :], ai[c][0:1, :]
        for _ in range(ls.bit_length() - 1):
            pr, pi = pr * pr - pi * pi, 2.0 * pr * pi
        cr = ci = jnp.zeros((1, LANE), F32)
        ir, ii = zero, zero
        for s in (range(N_SEG - 1, -1, -1) if reverse else range(N_SEG)):
            ir = jnp.where(row == s, cr, ir)
            ii = jnp.where(row == s, ci, ii)
            er, ei = ends[2 * c][s:s + 1, :], ends[2 * c + 1][s:s + 1, :]
            cr, ci = pr * cr - pi * ci + er, pr * ci + pi * cr + ei
        init += [ir, ii]

    def fix(n, carry, last=False):
        j = step_of(n)
        rows = block(j)
        new, sums = [], []
        for c in range(N_LT):
            cr, ci = carry[2 * c], carry[2 * c + 1]
            nr = ar[c] * cr - ai[c] * ci
            ni = ar[c] * ci + ai[c] * cr
            fr = xr[c, rows, :] + nr
            fi = xi[c, rows, :] + ni
            xr[c, rows, :] = fr
            xi[c, rows, :] = fi
            new += [nr, ni]
            if shifted is not None:
                yr, yi, shift = shifted
                if not last:
                    srows = block(j + shift)
                    sr, si = yr[c, srows, :], yi[c, srows, :]
                else:
                    edge = block(ls - 1 if shift < 0 else 0)
                    move, gone = (1, 0) if shift < 0 else (N_SEG - 1, N_SEG - 1)
                    sr = jnp.where(row == gone, 0.0, pltpu.roll(yr[c, edge, :], move, 0))
                    si = jnp.where(row == gone, 0.0, pltpu.roll(yi[c, edge, :], move, 0))
                sums += [carry[2 * N_LT + 2 * c] + fr * sr + fi * si,
                         carry[2 * N_LT + 2 * c + 1] + fi * sr - fr * si]
        return tuple(new + sums)

    if shifted is None:
        lax.fori_loop(0, ls, fix, tuple(init), unroll=SCAN_UNROLL)
        return ()
    assert shifted[2] == (-1 if reverse else 1)
    out = lax.fori_loop(0, ls - 1, fix, tuple(init) + (zero,) * (2 * N_LT), unroll=SCAN_UNROLL)
    return fix(ls - 1, out, last=True)[2 * N_LT:]


def _s5_tile_specs(l, d):
    tile = lambda r, c: pl.BlockSpec((None, None, r, c), lambda t, d=d: (d, t, 0, 0))
    return [pl.BlockSpec((l, TILE_U), lambda t: (0, t)), tile(TILE_U, TILE_N), tile(TILE_U, TILE_N),
            tile(1, TILE_N), tile(1, TILE_N), tile(TILE_N, TILE_U), tile(TILE_N, TILE_U)]


def _lanes(c):
    return slice(c * LANE, (c + 1) * LANE)


def _s5_fwd(u, bt_re, bt_im, lam_re, lam_im, ct_re, ct_im, *, d, name, add=None):
    l = u.shape[0]
    tiles = bt_re.shape[1]
    col = pl.BlockSpec((l, TILE_U), lambda t: (0, t))
    has_add = add is not None

    def body(*refs):
        u_ref, bre, bim, lre, lim, cre, cim = refs[:7]
        y_ref, xr, xi = refs[-3:]
        uu = u_ref[...]
        bu_re, bu_im = _dg(uu, bre[...], 1, 0), _dg(uu, bim[...], 1, 0)
        for c in range(N_LT):
            xr[c] = bu_re[:, _lanes(c)]
            xi[c] = bu_im[:, _lanes(c)]
        _s5_scan(xr, xi, lre[...], lim[...], length=l, reverse=(d == 1))
        y = refs[7][...] if has_add else jnp.zeros((l, TILE_U), F32)
        for c in range(N_LT):
            y = y + _dg(xr[c], cre[_lanes(c), :], 1, 0) - _dg(xi[c], cim[_lanes(c), :], 1, 0)
        y_ref[...] = y

    return pl.pallas_call(
        body, name=name, grid=(tiles,), in_specs=_s5_tile_specs(l, d) + [col] * has_add, out_specs=col,
        out_shape=jax.ShapeDtypeStruct((l, tiles * TILE_U), F32),
        scratch_shapes=[pltpu.VMEM((N_LT, l, LANE), F32)] * 2, compiler_params=_params("arbitrary"),
    )(u, bt_re, bt_im, lam_re, lam_im, ct_re, ct_im, *([add] if has_add else []))


def _s5_bwd(u, dy, bt_re, bt_im, lam_re, lam_im, ct_re, ct_im, *, d, name, add=None):
    l = u.shape[0]
    tiles = bt_re.shape[1]
    col = pl.BlockSpec((l, TILE_U), lambda t: (0, t))
    reverse = d == 1
    has_add = add is not None

    def body(*refs):
        u_ref, bre, bim, lre, lim, cre, cim, dy_ref = refs[:8]
        du_ref, dbre, dbim, dcre, dcim, dlre, dlim, xr, xi, gr, gi = refs[-11:]
        uu, dyy = u_ref[...], dy_ref[...]
        bu_re, bu_im = _dg(uu, bre[...], 1, 0), _dg(uu, bim[...], 1, 0)
        for c in range(N_LT):
            xr[c] = bu_re[:, _lanes(c)]
            xi[c] = bu_im[:, _lanes(c)]
        _s5_scan(xr, xi, lre[...], lim[...], length=l, reverse=reverse)
        gy_re, gy_im = _dg(dyy, cre[...], 1, 1), -_dg(dyy, cim[...], 1, 1)
        for c in range(N_LT):
            gr[c] = gy_re[:, _lanes(c)]
            gi[c] = gy_im[:, _lanes(c)]
        sums = _s5_scan(gr, gi, lre[...], -lim[...], length=l, reverse=not reverse,
                        shifted=(xr, xi, 1 if reverse else -1))
        du = refs[8][...] if has_add else jnp.zeros((l, TILE_U), F32)
        for c in range(N_LT):
            dlre[:, _lanes(c)] = jnp.sum(sums[2 * c], axis=0, keepdims=True)
            dlim[:, _lanes(c)] = jnp.sum(sums[2 * c + 1], axis=0, keepdims=True)
            g_re, g_im = gr[c], gi[c]
            du = du + _dg(g_re, bre[:, _lanes(c)], 1, 1) + _dg(g_im, bim[:, _lanes(c)], 1, 1)
            dbre[:, _lanes(c)] = _dg(uu, g_re, 0, 0)
            dbim[:, _lanes(c)] = _dg(uu, g_im, 0, 0)
            dcre[_lanes(c), :] = _dg(xr[c], dyy, 0, 0)
            dcim[_lanes(c), :] = -_dg(xi[c], dyy, 0, 0)
        du_ref[...] = du

    out3 = lambda r, c: pl.BlockSpec((None, r, c), lambda t: (t, 0, 0))
    f = lambda *s: jax.ShapeDtypeStruct(s, F32)
    return pl.pallas_call(
        body, name=name, grid=(tiles,), in_specs=_s5_tile_specs(l, d) + [col] + [col] * has_add,
        out_specs=[col, out3(TILE_U, TILE_N), out3(TILE_U, TILE_N),
                   out3(TILE_N, TILE_U), out3(TILE_N, TILE_U), out3(1, TILE_N), out3(1, TILE_N)],
        out_shape=[f(l, tiles * TILE_U), f(tiles, TILE_U, TILE_N), f(tiles, TILE_U, TILE_N),
                   f(tiles, TILE_N, TILE_U), f(tiles, TILE_N, TILE_U), f(tiles, 1, TILE_N), f(tiles, 1, TILE_N)],
        scratch_shapes=[pltpu.VMEM((N_LT, l, LANE), F32)] * 4, compiler_params=_params("arbitrary"),
    )(u, bt_re, bt_im, lam_re, lam_im, ct_re, ct_im, dy, *([add] if has_add else []))


def _s5_post(y, proj, u_cb, dskip, *, name, tl=256):
    d = y.shape[1]

    def body(i, o):
        ys = jax.nn.gelu(i[0][...] + i[2][...] * i[1][...])
        o[0][...] = ys
        o[1][...] = ys.astype(BF16)

    return _rowwise(body, rows=y.shape[0], tl=tl, ins=[_row(y, tl), _row(proj, tl, d, u_cb), _par(dskip)],
                    outs=[("row", d, F32), ("row", d, BF16)], name=name)


def _s5_post_bwd(y, proj, u_cb, dskip, dys, *, name, tl=256):
    d = y.shape[1]

    def body(i, o):
        u_ = i[1][...]
        _, vjp = jax.vjp(jax.nn.gelu, i[0][...] + i[2][...] * u_)
        (dpre,) = vjp(i[3][...])
        o[0][...] = dpre
        o[1][...] += jnp.sum(dpre * u_, axis=0, keepdims=True)

    return _rowwise(body, rows=y.shape[0], tl=tl,
                    ins=[_row(y, tl), _row(proj, tl, d, u_cb), _par(dskip), _row(dys, tl)],
                    outs=[("row", d, F32), ("acc", d)], name=name)


def _du_combine(dpre, dskip, du_s5, *, name, tl=256):
    d = dpre.shape[1]

    def body(i, o):
        o[0][...] = (i[0][...] * i[1][...] + i[2][...]).astype(BF16)

    return _rowwise(body, rows=dpre.shape[0], tl=tl, ins=[_row(dpre, tl), _par(dskip), _row(du_s5, tl)],
                    outs=[("row", d, BF16)], name=name)[0]


def _merge_fn(y, g, gate_r, gate_s, ys, glu, b):
    ret = jax.nn.silu(g) * (y * lax.rsqrt(jnp.mean(y * y, axis=-1, keepdims=True) + EPS))
    ssm = ys * jax.nn.sigmoid(glu + b)
    return jax.nn.sigmoid(gate_r) * ret + jax.nn.sigmoid(gate_s) * ssm


def _merge_ins(y_raw, proj, ys, glu, b_glu, cb0, tl):
    d = y_raw.shape[1]
    return [_row(y_raw, tl), _row(proj, tl, d, cb0 + 1), _row(proj, tl, d, cb0 + 3), _row(proj, tl, d, cb0 + 4),
            _row(ys, tl), _row(glu, tl), _par(b_glu)]


def _merge(y_raw, proj, ys, glu, b_glu, *, cb0, name, tl=128):
    d = y_raw.shape[1]
    dv = d // RET_HEADS

    def body(i, o):
        for h in range(RET_HEADS):
            cs = slice(h * dv, (h + 1) * dv)
            o[0][:, cs] = _merge_fn(*[r[:, cs] for r in i]).astype(BF16)

    return _rowwise(body, rows=y_raw.shape[0], tl=tl, ins=_merge_ins(y_raw, proj, ys, glu, b_glu, cb0, tl),
                    outs=[("row", d, BF16)], name=name)[0]


def _merge_bwd(y_raw, proj, ys, glu, b_glu, dmerged, *, cb0, name, tl=128):
    d = y_raw.shape[1]
    dv = d // RET_HEADS

    def body(i, o):
        for h in range(RET_HEADS):
            cs = slice(h * dv, (h + 1) * dv)
            _, vjp = jax.vjp(_merge_fn, *[r[:, cs] for r in i[:7]])
            dy, dg, dgr, dgs, dys, dglu, db = vjp(i[7][:, cs])
            o[0][:, cs] = dy.astype(BF16)
            o[1][:, cs] = dg.astype(BF16)
            o[2][:, cs] = dgr.astype(BF16)
            o[3][:, cs] = dgs.astype(BF16)
            o[4][:, cs] = dglu.astype(BF16)
            o[5][:, cs] = dys
            o[6][:, cs] += db

    return _rowwise(body, rows=y_raw.shape[0], tl=tl,
                    ins=_merge_ins(y_raw, proj, ys, glu, b_glu, cb0, tl) + [_row(dmerged, tl)],
                    outs=[("row", d, BF16)] * 5 + [("row", d, F32), ("acc", d)], name=name)


def _ffn_act_fn(gate, up):
    return jax.nn.silu(gate) * up


def _ffn_act(gate, up, *, name, tl=128):
    def body(i, o):
        o[0][...] = _ffn_act_fn(i[0][...], i[1][...]).astype(BF16)

    return _rowwise(body, rows=gate.shape[0], tl=tl, ins=[_row(gate, tl), _row(up, tl)],
                    outs=[("row", gate.shape[1], BF16)], name=name)[0]


def _ffn_act_bwd(gate, up, dact, *, name, tl=128):
    def body(i, o):
        _, vjp = jax.vjp(_ffn_act_fn, i[0][...], i[1][...])
        dgate, dup = vjp(i[2][...])
        o[0][...] = dgate.astype(BF16)
        o[1][...] = dup.astype(BF16)

    w = gate.shape[1]
    return _rowwise(body, rows=gate.shape[0], tl=tl, ins=[_row(gate, tl), _row(up, tl), _row(dact, tl)],
                    outs=[("row", w, BF16), ("row", w, BF16)], name=name)


def _adamw(w, g, m, v):
    m = ADAM_B1 * m + (1.0 - ADAM_B1) * g
    v = ADAM_B2 * v + (1.0 - ADAM_B2) * (g * g)
    m_hat = m / (1.0 - ADAM_B1 ** ADAM_STEP)
    v_hat = v / (1.0 - ADAM_B2 ** ADAM_STEP)
    return -ADAM_LR * (m_hat / (jnp.sqrt(v_hat) + ADAM_EPS) + ADAM_WD * w), m, v


def _adam_flat(g, w, m, v, *, name, tr=FLAT_ROWS):
    def body(i, o):
        for o_ref, val in zip(o, _adamw(i[1][...], i[0][...], i[2][...], i[3][...])):
            o_ref[...] = val

    return _rowwise(body, rows=g.shape[0], tl=tr, ins=[_row(a, tr) for a in (g, w, m, v)],
                    outs=[("row", LANE, F32)] * 3, name=name)


def _adam_shard(recv, own, me, w, m, v, *, kind, layer, name, tr, tie=None, others=None):
    depth, r, c = w.shape
    assert r % tr == 0 and recv.shape[2] == c
    blk = pl.BlockSpec((None, tr, c), lambda i, me_ref: (layer, i, 0))
    if kind == "col":
        own_spec = pl.BlockSpec((tr, c), lambda i, me_ref: (i, me_ref[0]))
    else:
        per = own.shape[0] // N_DEV // tr
        assert per * tr * N_DEV == own.shape[0]
        own_spec = pl.BlockSpec((tr, c), lambda i, me_ref: (me_ref[0] * per + i, 0))
    extra = ([] if tie is None else [tie]) + list(others or [])

    def body(me_ref, recv_ref, own_ref, w_ref, m_ref, v_ref, *refs):
        g_ref, d_ref, nm_ref, nv_ref = refs[len(extra):]
        g = own_ref[...].astype(F32)
        for k in range(N_DEV - 1):
            g = g + recv_ref[k].astype(F32)
        g_ref[...] = g
        d_ref[...], nm_ref[...], nv_ref[...] = _adamw(w_ref[...], g, m_ref[...], v_ref[...])

    first = 6 + (tie is not None)
    return pl.pallas_call(
        body, name=name, out_shape=[jax.ShapeDtypeStruct(w.shape, F32)] * 4,
        grid_spec=pltpu.PrefetchScalarGridSpec(
            num_scalar_prefetch=1, grid=(r // tr,),
            in_specs=[pl.BlockSpec((N_DEV - 1, tr, c), lambda i, me_ref: (0, i, 0)), own_spec, blk, blk, blk]
            + [pl.BlockSpec(memory_space=pl.ANY)] * len(extra),
            out_specs=[blk] * 4),
        input_output_aliases={first + j: j for j in range(4)} if others else {},
        compiler_params=_params("parallel"),
    )(me, recv, own, w, m, v, *extra)


def _position():
    x, y, c = lax.axis_index("x"), lax.axis_index("y"), lax.axis_index("c")
    return x, y, c, 4 * x + 2 * y + c


def _coords(p):
    return p // 4, (p // 2) % 2, p % 2


def _block_of(ref, kind, p, nb):
    if kind == "col":
        return ref.at[:, pl.ds(pl.multiple_of(p * nb, LANE), nb)]
    return ref.at[pl.ds(pl.multiple_of(p * nb, SUBLANE), nb), :]


def _all_gather(shards, kinds, *, name):
    n = len(shards)
    out_shape = []
    for s, kind in zip(shards, kinds):
        r, c = s.shape
        out_shape.append(jax.ShapeDtypeStruct((r, c * N_DEV) if kind == "col" else (r * N_DEV, c), s.dtype))

    def body(*refs):
        shard_refs, full_refs = refs[:n], refs[n:2 * n]
        send_sems, recv_sems, local_sems = refs[2 * n:]
        x, y, c, me = _position()
        sibling = (x, y, 1 - c)
        chips = [(1 - x, y), (x, 1 - y), (1 - x, 1 - y)]

        def block(t, dev):
            nb = shards[t].shape[1 if kinds[t] == "col" else 0]
            return _block_of(full_refs[t], kinds[t], 4 * dev[0] + 2 * dev[1] + dev[2], nb)

        def copy(t, k, dev, to, src=None):
            return pltpu.make_async_remote_copy(
                src_ref=block(t, dev) if src is None else src, dst_ref=block(t, dev),
                send_sem=send_sems.at[t, k], recv_sem=recv_sems.at[t, k], device_id=to, device_id_type=MESH)

        mine, first, passed = [], [], []
        for t in range(n):
            mine.append(pltpu.make_async_copy(shard_refs[t], block(t, (x, y, c)), local_sems.at[t]))
            mine[-1].start()
            first.append(copy(t, 0, (x, y, c), sibling, src=shard_refs[t]))
            first += [copy(t, 1 + j, (x, y, c), (*chip, c), src=shard_refs[t]) for j, chip in enumerate(chips)]
        for cp in first:
            cp.start()
        for j, chip in enumerate(chips):
            for t in range(n):
                copy(t, 1 + j, (*chip, c), (x, y, c)).wait_recv()
                passed.append(copy(t, 4 + j, (*chip, c), sibling))
                passed[-1].start()
        for t in range(n):
            copy(t, 0, sibling, (x, y, c)).wait_recv()
            for j, chip in enumerate(chips):
                copy(t, 4 + j, (*chip, 1 - c), (x, y, c)).wait_recv()
        for cp in first + passed:
            cp.wait_send()
        for cp in mine:
            cp.wait()

    any_spec = pl.BlockSpec(memory_space=pl.ANY)
    return pl.pallas_call(
        body, name=name, in_specs=[any_spec] * n, out_specs=[any_spec] * n, out_shape=out_shape,
        scratch_shapes=[pltpu.SemaphoreType.DMA((n, N_DEV - 1)), pltpu.SemaphoreType.DMA((n, N_DEV - 1)),
                        pltpu.SemaphoreType.DMA((n,))],
        compiler_params=pltpu.CompilerParams(has_side_effects=True),
    )(*shards)


class _GatherRoute:
    def __init__(self, shards, kinds):
        self.kinds = kinds
        self.nb = [s.shape[1 if k == "col" else 0] for s, k in zip(shards, kinds)]

    def lands(self, shards, me):
        out = []
        for t, (s, k) in enumerate(zip(shards, self.kinds)):
            full = lax.empty((s.shape[0], s.shape[1] * N_DEV) if k == "col" else (s.shape[0] * N_DEV, s.shape[1]), s.dtype)
            out.append(_place_own(s, full, me, k, name=f"own_block_{t}"))
        return out

    def sent(self, t, src_refs, me, k):
        return src_refs[t]

    def lands_at(self, t, land_refs, me, k):
        return _block_of(land_refs[t], self.kinds[t], (me + N_DEV - k) % N_DEV, self.nb[t])


class _ScatterRoute:
    def __init__(self, grads, kinds):
        self.kinds = kinds
        self.nb = [g.shape[1 if k == "col" else 0] // N_DEV for g, k in zip(grads, kinds)]

    def lands(self, grads):
        return [lax.empty((N_DEV - 1, g.shape[0], nb) if k == "col" else (N_DEV - 1, nb, g.shape[1]), g.dtype)
                for g, k, nb in zip(grads, self.kinds, self.nb)]

    def sent(self, t, src_refs, me, k):
        return _block_of(src_refs[t], self.kinds[t], (me + k) % N_DEV, self.nb[t])

    def lands_at(self, t, land_refs, me, k):
        return land_refs[t].at[k - 1]


def _place_own(shard, full, me, kind, *, name):
    r, c = shard.shape
    tr = _row_tile(r)
    if kind == "col":
        dst = pl.BlockSpec((tr, c), lambda i, me_ref: (i, me_ref[0]))
    else:
        dst = pl.BlockSpec((tr, c), lambda i, me_ref: (me_ref[0] * (r // tr) + i, 0))

    def body(me_ref, shard_ref, full_ref, out_ref):
        out_ref[...] = shard_ref[...]

    return pl.pallas_call(
        body, name=name, out_shape=jax.ShapeDtypeStruct(full.shape, full.dtype),
        grid_spec=pltpu.PrefetchScalarGridSpec(
            num_scalar_prefetch=1, grid=(r // tr,),
            in_specs=[pl.BlockSpec((tr, c), lambda i, me_ref: (i, 0)), pl.BlockSpec(memory_space=pl.ANY)],
            out_specs=dst),
        input_output_aliases={2: 0}, compiler_params=_params("parallel"),
    )(me, shard, full)


_HBM = pl.BlockSpec(memory_space=pltpu.HBM)
_SEM = pl.BlockSpec(memory_space=pltpu.SEMAPHORE)
_FLOWING = pltpu.SideEffectType.DATAFLOW_SIDE_EFFECTING


def _exchange_start(srcs, lands, route, after, *, name):
    n = len(srcs)

    def body(*refs):
        src_refs, land_refs = refs[:n], refs[n:2 * n]
        send_sems, recv_sems = refs[2 * n + 1:2 * n + 3]
        token = refs[-1]
        _, _, _, me = _position()
        for t in range(n):
            for k in range(1, N_DEV):
                p = (me + k) % N_DEV
                pltpu.make_async_remote_copy(
                    src_ref=route.sent(t, src_refs, me, k), dst_ref=route.lands_at(t, land_refs, p, k),
                    send_sem=send_sems.at[t * N_DEV + k], recv_sem=recv_sems.at[t * N_DEV + k], device_id=_coords(p),
                    device_id_type=MESH).start()
        token[...] = jnp.zeros_like(token)

    hbm = lambda a: pltpu.HBM(a.shape, a.dtype)
    sems = pltpu.SemaphoreType.DMA((n * N_DEV,))
    out = pl.pallas_call(
        body, name=name,
        out_shape=(sems, sems, *[hbm(a) for a in srcs], *[hbm(a) for a in lands],
                   jax.ShapeDtypeStruct((SUBLANE, LANE), F32)),
        in_specs=[_HBM] * (2 * n) + [pl.BlockSpec(memory_space=pl.ANY)],
        out_specs=(_SEM, _SEM, *[_HBM] * (2 * n), pl.BlockSpec(memory_space=pltpu.VMEM)),
        input_output_aliases={i: 2 + i for i in range(2 * n)},
        compiler_params=pltpu.CompilerParams(has_side_effects=_FLOWING),
    )(*[pltpu.with_memory_space_constraint(a, pltpu.HBM) for a in (*srcs, *lands)], after)
    return out[0], out[1], out[2:2 + n], out[2 + n:2 + 2 * n], out[-1]


def _exchange_wait(started, route, after, *, name):
    send_sems, recv_sems, srcs, lands, _ = started
    n = len(srcs)

    def body(*refs):
        src_refs, land_refs = refs[:n], refs[n:2 * n]
        send_ref, recv_ref = refs[2 * n:2 * n + 2]
        _, _, _, me = _position()
        for t in range(n):
            for k in range(1, N_DEV):
                cp = pltpu.make_async_remote_copy(
                    src_ref=route.sent(t, src_refs, me, k), dst_ref=route.lands_at(t, land_refs, me, k),
                    send_sem=send_ref.at[t * N_DEV + k], recv_sem=recv_ref.at[t * N_DEV + k],
                    device_id=_coords((me + N_DEV - k) % N_DEV), device_id_type=MESH)
                cp.wait_send()
                cp.wait_recv()

    hbm = lambda a: pltpu.HBM(a.shape, a.dtype)
    out = pl.pallas_call(
        body, name=name, out_shape=(*[hbm(a) for a in srcs], *[hbm(a) for a in lands]),
        in_specs=[_HBM] * (2 * n) + [_SEM, _SEM, pl.BlockSpec(memory_space=pl.ANY)],
        out_specs=tuple([_HBM] * (2 * n)), input_output_aliases={i: i for i in range(2 * n)},
        compiler_params=pltpu.CompilerParams(has_side_effects=_FLOWING),
    )(*srcs, *lands, send_sems, recv_sems, after)
    return list(out[:n]), list(out[n:])


def _all_reduce(part, *, name):
    _, r, _ = part.shape

    def body(part_ref, tot_ref, recv_ref, send1, recv1, send2, recv2):
        _, _, _, me = _position()

        def scatter(k, to_me=False):
            p = (me + N_DEV - k) % N_DEV if to_me else (me + k) % N_DEV
            return pltpu.make_async_remote_copy(
                src_ref=part_ref.at[me if to_me else p], dst_ref=recv_ref.at[p if to_me else me],
                send_sem=send1.at[k], recv_sem=recv1.at[k], device_id=_coords(p), device_id_type=MESH)

        def gather(k, to_me=False):
            p = (me + N_DEV - k) % N_DEV if to_me else (me + k) % N_DEV
            return pltpu.make_async_remote_copy(
                src_ref=tot_ref.at[me], dst_ref=tot_ref.at[p if to_me else me],
                send_sem=send2.at[k], recv_sem=recv2.at[k], device_id=_coords(p), device_id_type=MESH)

        for k in range(1, N_DEV):
            scatter(k).start()
        recv_ref[me] = part_ref[me]
        for k in range(1, N_DEV):
            scatter(k, to_me=True).wait_recv()
        total = recv_ref[0]
        for q in range(1, N_DEV):
            total = total + recv_ref[q]
        tot_ref[me] = total
        for k in range(1, N_DEV):
            gather(k).start()
        for k in range(1, N_DEV):
            gather(k, to_me=True).wait_recv()
        for k in range(1, N_DEV):
            scatter(k).wait_send()
            gather(k).wait_send()

    vmem = pl.BlockSpec(memory_space=pltpu.VMEM)
    return pl.pallas_call(
        body, name=name, in_specs=[vmem], out_specs=vmem, out_shape=jax.ShapeDtypeStruct(part.shape, F32),
        scratch_shapes=[pltpu.VMEM(part.shape, F32)] + [pltpu.SemaphoreType.DMA((N_DEV,))] * 4,
        compiler_params=pltpu.CompilerParams(has_side_effects=True, vmem_limit_bytes=VMEM_LIMIT),
    )(part)


def _round_up(n, m):
    return (n + m - 1) // m * m


def _row_tile(rows):
    return next(t for t in (256, 128, 64, 32, 16) if rows % t == 0)


def _local_step(x, target, small, depth, weights_of, grads_done, *, qkw):
    l, d = x.shape
    groups = d // SSM_GROUP
    tiles = groups // TILE_GROUPS
    half = qkw // RET_HEADS // 2
    cb0 = 2 * qkw // d
    inv = 1.0 / (ROPE_BASE ** (jnp.arange(half, dtype=F32) / half))
    ang = jnp.arange(l, dtype=F32)[:, None] * inv[None, :]
    cos, sin = jnp.cos(ang), jnp.sin(ang)
    rep = jnp.repeat(jnp.eye(SSM_STATE, dtype=F32), SSM_GROUP, axis=1)
    row2 = lambda a: a.reshape(1, -1)

    saved = []
    for i in range(depth):
        full_i, x = weights_of(i, x)
        w_in, w_glu, w_out, w_gate, w_up, w_down = full_i
        n = f"l{i}_"
        g_mix, g_ffn = row2(small["ln_mix_g"][i]), row2(small["ln_ffn_g"][i])
        dskip, b_glu = row2(small["ssm_d"][i]), row2(small["b_glu"][i])
        lg = small["ret_log_gamma"][i]
        h = _norm_fwd(x, g_mix, name=n + "norm_mix")
        proj = _matmul(h, w_in, mode="nn", out_dtype=F32, name=n + "proj")
        q_rot, k_rot, v_bf = _ret_prep(proj, cos, sin, qkw=qkw, d=d, name=n + "ret_prep")
        y_raw = _ret_fwd(lg, q_rot, k_rot, v_bf, name=n + "ret_fwd")
        par = [small["ssm_a_re"][i].reshape(2 * groups, SSM_STATE), small["ssm_a_im"][i].reshape(2 * groups, SSM_STATE),
               small["ssm_log_dt"][i].reshape(2 * groups, 1),
               small["ssm_b_re"][i].reshape(2 * groups, SSM_STATE * SSM_GROUP),
               small["ssm_b_im"][i].reshape(2 * groups, SSM_STATE * SSM_GROUP), rep]
        lam_re, lam_im, bbar_re, bbar_im = _s5_prep(*par, name=n + "s5_prep")
        s5 = [_b_tiles(bbar_re, tiles).astype(BF16), _b_tiles(bbar_im, tiles).astype(BF16),
              lam_re.reshape(2, tiles, 1, TILE_N), lam_im.reshape(2, tiles, 1, TILE_N),
              _c_tiles(small["ssm_c_re"][i], tiles).astype(BF16), _c_tiles(small["ssm_c_im"][i], tiles).astype(BF16)]
        u_seg = _to_segments(proj[:, (cb0 + 2) * d:(cb0 + 3) * d])
        y_seg = _s5_fwd(u_seg, *s5, d=0, name=n + "s5_fwd_f")
        y_seg = _s5_fwd(u_seg, *s5, d=1, add=y_seg, name=n + "s5_fwd_b")
        y_s5 = _from_segments(y_seg)
        ys, ys_bf = _s5_post(y_s5, proj, cb0 + 2, dskip, name=n + "s5_post")
        glu = _matmul(ys_bf, w_glu, mode="nn", out_dtype=F32, name=n + "glu")
        merged = _merge(y_raw, proj, ys, glu, b_glu, cb0=cb0, name=n + "merge")
        x1 = _matmul(merged, w_out, mode="nn", out_dtype=F32, res=x, name=n + "out")
        h2 = _norm_fwd(x1, g_ffn, name=n + "norm_ffn")
        gate = _matmul(h2, w_gate, mode="nn", out_dtype=F32, name=n + "gate")
        up = _matmul(h2, w_up, mode="nn", out_dtype=F32, name=n + "up")
        act = _ffn_act(gate, up, name=n + "act")
        x2 = _matmul(act, w_down, mode="nn", out_dtype=F32, res=x1, name=n + "down")
        saved.append(dict(full=full_i, x=x, h=h, proj=proj, q_rot=q_rot, k_rot=k_rot, v_bf=v_bf, y_raw=y_raw, par=par, s5=s5,
                          u_seg=u_seg, y_s5=y_s5, ys=ys, ys_bf=ys_bf, glu=glu, merged=merged, x1=x1, h2=h2, gate=gate,
                          up=up, act=act))
        x = x2

    dx, dg_final, loss = _final(x, row2(small["ln_final_g"]), target, name="final")

    sg = {k: [None] * depth for k in ("ln_mix_g", "ret_log_gamma", "ssm_a_re", "ssm_a_im", "ssm_log_dt", "ssm_b_re",
                                      "ssm_b_im", "ssm_c_re", "ssm_c_im", "ssm_d", "b_glu", "ln_ffn_g")}
    for i in reversed(range(depth)):
        s = saved[i]
        big = [None] * len(BIG)
        w_in, w_glu, w_out, w_gate, w_up, w_down = s["full"]
        n = f"l{i}_b_"
        g_mix, g_ffn = row2(small["ln_mix_g"][i]), row2(small["ln_ffn_g"][i])
        dskip, b_glu = row2(small["ssm_d"][i]), row2(small["b_glu"][i])
        lg = small["ret_log_gamma"][i]
        dact = _matmul(dx, w_down, mode="nt", out_dtype=F32, name=n + "dact")
        big[5] = _matmul(s["act"], dx, mode="tn", out_dtype=BF16, name=n + "dw_down")
        dgate, dup = _ffn_act_bwd(s["gate"], s["up"], dact, name=n + "act")
        dh2 = _matmul(dgate, w_gate, mode="nt", out_dtype=F32, name=n + "dh2_gate")
        dh2 = _matmul(dup, w_up, mode="nt", out_dtype=F32, res=dh2, name=n + "dh2_up")
        big[3] = _matmul(s["h2"], dgate, mode="tn", out_dtype=BF16, name=n + "dw_gate")
        big[4] = _matmul(s["h2"], dup, mode="tn", out_dtype=BF16, name=n + "dw_up")
        dx1, dgf = _norm_bwd(s["x1"], g_ffn, dh2, dx, name=n + "norm_ffn")
        sg["ln_ffn_g"][i] = dgf[0]
        dmerged = _matmul(dx1, w_out, mode="nt", out_dtype=F32, name=n + "dmerged")
        big[2] = _matmul(s["merged"], dx1, mode="tn", out_dtype=BF16, name=n + "dw_out")
        dy_raw, dg, dgate_r, dgate_s, dglu, dys_a, db_glu = _merge_bwd(
            s["y_raw"], s["proj"], s["ys"], s["glu"], b_glu, dmerged, cb0=cb0, name=n + "merge")
        sg["b_glu"][i] = db_glu[0]
        dys = _matmul(dglu, w_glu, mode="nt", out_dtype=F32, res=dys_a, name=n + "dys")
        big[1] = _matmul(s["ys_bf"], dglu, mode="tn", out_dtype=BF16, name=n + "dw_glu")
        dpre, dd = _s5_post_bwd(s["y_s5"], s["proj"], cb0 + 2, dskip, dys, name=n + "s5_post")
        sg["ssm_d"][i] = dd[0]
        dpre_seg = _to_segments(dpre)
        r_f = _s5_bwd(s["u_seg"], dpre_seg, *s["s5"], d=0, name=n + "s5_bwd_f")
        r_b = _s5_bwd(s["u_seg"], dpre_seg, *s["s5"], d=1, add=r_f[0], name=n + "s5_bwd_b")
        du = _du_combine(dpre, dskip, _from_segments(r_b[0]), name=n + "du")
        both = lambda k: jnp.stack([r_f[k], r_b[k]])
        cts = [both(5).reshape(2 * groups, SSM_STATE), both(6).reshape(2 * groups, SSM_STATE),
               _b_untile(both(1), tiles), _b_untile(both(2), tiles)]
        da_re, da_im, dldt, db_re, db_im = _s5_prep_bwd(*s["par"], cts, name=n + "s5_prep")
        sg["ssm_a_re"][i] = da_re.reshape(2, groups, SSM_STATE)
        sg["ssm_a_im"][i] = da_im.reshape(2, groups, SSM_STATE)
        sg["ssm_log_dt"][i] = dldt.reshape(2, groups)
        sg["ssm_b_re"][i] = db_re.reshape(2, groups, SSM_STATE, SSM_GROUP)
        sg["ssm_b_im"][i] = db_im.reshape(2, groups, SSM_STATE, SSM_GROUP)
        sg["ssm_c_re"][i] = _c_untile(both(3), tiles)
        sg["ssm_c_im"][i] = _c_untile(both(4), tiles)
        dq_rot, dk_rot, dv, dlg = _ret_bwd(lg, s["q_rot"], s["k_rot"], s["v_bf"], dy_raw, name=n + "ret_bwd")
        sg["ret_log_gamma"][i] = dlg[:, :2, 0].T
        dq, dk = _ret_prep_bwd(dq_rot, dk_rot, cos, sin, name=n + "ret_prep")
        dproj = jnp.concatenate([dq, dk, dv.astype(BF16), dg, du, dgate_r, dgate_s], axis=1)
        dh = _matmul(dproj, w_in, mode="nt", out_dtype=F32, name=n + "dh")
        big[0] = _matmul(s["h"], dproj, mode="tn", out_dtype=BF16, name=n + "dw_in")
        dx, dgm = _norm_bwd(s["x"], g_mix, dh, dx1, name=n + "norm_mix")
        sg["ln_mix_g"][i] = dgm[0]
        dx = grads_done(i, big, dx)

    small_grads = {k: jnp.stack(v) for k, v in sg.items()}
    small_grads["ln_final_g"] = dg_final[0]
    return loss, dx, small_grads


BIG = ("w_in", "w_glu", "w_out", "w_ffn_gate", "w_ffn_up", "w_ffn_down")
BIG_KINDS = ("col", "row", "row", "col", "col", "row")
SMALL = ("ln_mix_g", "ret_log_gamma", "ssm_a_re", "ssm_a_im", "ssm_log_dt", "ssm_b_re", "ssm_b_im", "ssm_c_re",
         "ssm_c_im", "ssm_d", "b_glu", "ln_ffn_g", "ln_final_g")
WEIGHTS = ("ln_mix_g", "w_in", "ret_log_gamma", "ssm_a_re", "ssm_a_im", "ssm_log_dt", "ssm_b_re", "ssm_b_im",
           "ssm_c_re", "ssm_c_im", "ssm_d", "w_glu", "b_glu", "w_out", "ln_ffn_g", "w_ffn_gate", "w_ffn_up",
           "w_ffn_down", "ln_final_g")


def _pad_to(a, axis, size):
    pad = [(0, 0)] * a.ndim
    pad[axis] = (0, size - a.shape[axis])
    return jnp.pad(a, pad)


def _flatten_small(tree, extra):
    def as_rows(a):
        a = a.reshape(-1).astype(F32)
        return _pad_to(a, 0, _round_up(a.shape[0], SUBLANE * LANE)).reshape(-1, LANE)

    flat = jnp.concatenate([as_rows(tree[k]) for k in SMALL] + [as_rows(extra)])
    return _pad_to(flat, 0, _round_up(flat.shape[0], FLAT_ROWS))


def _unflatten_small(flat, like):
    out, at = {}, 0
    for k in SMALL:
        n = like[k].size
        rows = _round_up(n, SUBLANE * LANE) // LANE
        out[k] = flat[at:at + rows].reshape(-1)[:n].reshape(like[k].shape)
        at += rows
    return out, flat[at, 0]


def kernel(x, ln_mix_g, w_in, ret_log_gamma, ssm_a_re, ssm_a_im, ssm_log_dt, ssm_b_re, ssm_b_im, ssm_c_re, ssm_c_im, ssm_d, w_glu, b_glu, w_out, ln_ffn_g, w_ffn_gate, w_ffn_up, w_ffn_down, ln_final_g, loss_target, m_ln_mix_g, m_w_in, m_ret_log_gamma, m_ssm_a_re, m_ssm_a_im, m_ssm_log_dt, m_ssm_b_re, m_ssm_b_im, m_ssm_c_re, m_ssm_c_im, m_ssm_d, m_w_glu, m_b_glu, m_w_out, m_ln_ffn_g, m_w_ffn_gate, m_w_ffn_up, m_w_ffn_down, m_ln_final_g, v_ln_mix_g, v_w_in, v_ret_log_gamma, v_ssm_a_re, v_ssm_a_im, v_ssm_log_dt, v_ssm_b_re, v_ssm_b_im, v_ssm_c_re, v_ssm_c_im, v_ssm_d, v_w_glu, v_b_glu, v_w_out, v_ln_ffn_g, v_w_ffn_gate, v_w_ffn_up, v_w_ffn_down, v_ln_final_g):
    w = dict(ln_mix_g=ln_mix_g, w_in=w_in, ret_log_gamma=ret_log_gamma, ssm_a_re=ssm_a_re, ssm_a_im=ssm_a_im, ssm_log_dt=ssm_log_dt, ssm_b_re=ssm_b_re, ssm_b_im=ssm_b_im, ssm_c_re=ssm_c_re, ssm_c_im=ssm_c_im, ssm_d=ssm_d, w_glu=w_glu, b_glu=b_glu, w_out=w_out, ln_ffn_g=ln_ffn_g, w_ffn_gate=w_ffn_gate, w_ffn_up=w_ffn_up, w_ffn_down=w_ffn_down, ln_final_g=ln_final_g)
    m = dict(ln_mix_g=m_ln_mix_g, w_in=m_w_in, ret_log_gamma=m_ret_log_gamma, ssm_a_re=m_ssm_a_re, ssm_a_im=m_ssm_a_im, ssm_log_dt=m_ssm_log_dt, ssm_b_re=m_ssm_b_re, ssm_b_im=m_ssm_b_im, ssm_c_re=m_ssm_c_re, ssm_c_im=m_ssm_c_im, ssm_d=m_ssm_d, w_glu=m_w_glu, b_glu=m_b_glu, w_out=m_w_out, ln_ffn_g=m_ln_ffn_g, w_ffn_gate=m_w_ffn_gate, w_ffn_up=m_w_ffn_up, w_ffn_down=m_w_ffn_down, ln_final_g=m_ln_final_g)
    v = dict(ln_mix_g=v_ln_mix_g, w_in=v_w_in, ret_log_gamma=v_ret_log_gamma, ssm_a_re=v_ssm_a_re, ssm_a_im=v_ssm_a_im, ssm_log_dt=v_ssm_log_dt, ssm_b_re=v_ssm_b_re, ssm_b_im=v_ssm_b_im, ssm_c_re=v_ssm_c_re, ssm_c_im=v_ssm_c_im, ssm_d=v_ssm_d, w_glu=v_w_glu, b_glu=v_b_glu, w_out=v_w_out, ln_ffn_g=v_ln_ffn_g, w_ffn_gate=v_w_ffn_gate, w_ffn_up=v_w_ffn_up, w_ffn_down=v_w_ffn_down, ln_final_g=v_ln_final_g)
    depth, d, nb_in = w_in.shape
    qkw = (nb_in * N_DEV - 5 * d) // 2
    nb_ffn = w_ffn_gate.shape[2]
    nb_pad = _round_up(nb_ffn, LANE)
    pad_axis = {"w_ffn_gate": 2, "w_ffn_up": 2, "w_ffn_down": 1}

    assert depth == 2
    padded = {k: w[k] if k not in pad_axis else _pad_to(w[k], pad_axis[k], nb_pad) for k in BIG}
    shards = [[padded[k][i].astype(BF16) for k in BIG] for i in range(depth)]
    full0 = _all_gather(shards[0], BIG_KINDS, name="gather_l0")
    me = (4 * lax.axis_index("x") + 2 * lax.axis_index("y") + lax.axis_index("c")).astype(jnp.int32).reshape(1)
    gather1 = _GatherRoute(shards[1], BIG_KINDS)
    gathering = _exchange_start(shards[1], gather1.lands(shards[1], me), gather1, full0[0], name="gather_l1_start")

    def weights_of(i, act):
        if i == 0:
            return full0, act + gathering[4][0, 0]
        return _exchange_wait(gathering, gather1, act, name="gather_l1_wait")[1], act

    pending = {}

    def grads_done(i, big, dact):
        route = _ScatterRoute(big, BIG_KINDS)
        started = _exchange_start(big, route.lands(big), route, dact, name=f"scatter_l{i}_start")
        if i == depth - 1:
            pending[i] = started, route
            return dact + started[4][0, 0]
        last, route_last = pending.pop(i + 1)
        pending["last"] = _exchange_wait(last, route_last, dact, name=f"scatter_l{i + 1}_wait")
        pending[i] = started, route
        return dact

    small = {k: w[k] for k in SMALL}
    loss, dx, small_grads = _local_step(x[0], loss_target[0], small, depth, weights_of, grads_done, qkw=qkw)
    started, route = pending.pop(0)

    part = _flatten_small(small_grads, loss[0, :1]) + started[4][0, 0]
    rows = part.shape[0]
    total = _all_reduce(part.reshape(N_DEV, rows // N_DEV, LANE), name="reduce_small").reshape(rows, LANE)
    zero = jnp.zeros((1,), F32)
    flat = [_flatten_small({k: a[k] for k in SMALL}, zero) for a in (w, m, v)]
    upd = _adam_flat(total, *flat, name="adam_small")
    grads, delta, new_m, new_v = {}, {}, {}, {}
    g_small, loss_total = _unflatten_small(total, small)
    grads.update(g_small)
    for dst, u in zip((delta, new_m, new_v), upd):
        dst.update(_unflatten_small(u, small)[0])

    ops = {k: [_pad_to(a[k], 2, nb_pad) for a in (w, m, v)] if k in ("w_ffn_gate", "w_ffn_up") else [a[k] for a in (w, m, v)]
           for k in BIG}
    tiles = {k: _row_tile(ops[k][0].shape[1]) for k in BIG}
    own, recv = pending["last"]
    half, tie = {}, upd[0]
    for t, (k, kind) in enumerate(zip(BIG, BIG_KINDS)):
        half[k] = _adam_shard(recv[t], own[t], me, *ops[k], kind=kind, layer=1, name="adam_l1_" + k, tr=tiles[k], tie=tie)
        tie = half[k][0]
    own, recv = _exchange_wait(started, route, tie, name="scatter_l0_wait")
    for t, (k, kind) in enumerate(zip(BIG, BIG_KINDS)):
        res = _adam_shard(recv[t], own[t], me, *ops[k], kind=kind, layer=0, name="adam_l0_" + k, tr=tiles[k],
                          others=half[k])
        if k in ("w_ffn_gate", "w_ffn_up"):
            res = [r[:, :, :nb_ffn] for r in res]
        grads[k], delta[k], new_m[k], new_v[k] = res

    return (loss_total, dx[None], *[grads[k] for k in WEIGHTS], *[delta[k] for k in WEIGHTS],
            *[new_m[k] for k in WEIGHTS], *[new_v[k] for k in WEIGHTS])
```

```python
import math

import jax
import jax.numpy as jnp
from jax import lax
from jax.experimental import pallas as pl
from jax.experimental.pallas import tpu as pltpu

F32 = jnp.float32
BF16 = jnp.bfloat16
MESH = pl.DeviceIdType.MESH

N_DEV = 8
RET_HEADS = 4
CHUNK = 128
ROPE_BASE = 10000.0
SSM_GROUP = 16
SSM_STATE = 64
TILE_GROUPS = 8
TILE_U = TILE_GROUPS * SSM_GROUP
TILE_N = TILE_GROUPS * SSM_STATE
LANE = 128
SUBLANE = 8
N_SEG = SUBLANE
N_LT = TILE_N // LANE
SCAN_UNROLL = 4
FLAT_ROWS = 1024
EPS = 1e-6
ADAM_LR = 0.001
ADAM_B1 = 0.9
ADAM_B2 = 0.999
ADAM_EPS = 1e-08
ADAM_WD = 0.01
ADAM_STEP = 10
VMEM_LIMIT = 56 * 1024 * 1024


def _params(*sem):
    return pltpu.CompilerParams(dimension_semantics=sem or None, vmem_limit_bytes=VMEM_LIMIT)


def _dg(a, b, ca, cb):
    return lax.dot_general(a.astype(BF16), b.astype(BF16), (((ca,), (cb,)), ((), ())),
                           preferred_element_type=F32)


@jax.custom_vjp
def _dnn(a, b):
    return _dg(a, b, 1, 0)


@jax.custom_vjp
def _dnt(a, b):
    return _dg(a, b, 1, 1)


@jax.custom_vjp
def _dtn(a, b):
    return _dg(a, b, 0, 0)


_dnn.defvjp(lambda a, b: (_dnn(a, b), (a, b)), lambda r, g: (_dnt(g, r[1]), _dtn(r[0], g)))
_dnt.defvjp(lambda a, b: (_dnt(a, b), (a, b)), lambda r, g: (_dnn(g, r[1]), _dtn(g, r[0])))
_dtn.defvjp(lambda a, b: (_dtn(a, b), (a, b)), lambda r, g: (_dnt(r[1], g), _dnn(r[0], g)))


def _matmul(a, b, *, mode, out_dtype, name, res=None, tm=1024, tn=1024, tk=512):
    if mode == "nn":
        (m, k), n = a.shape, b.shape[1]
    elif mode == "nt":
        (m, k), n = a.shape, b.shape[0]
    else:
        (k, m), n = a.shape, b.shape[1]
    tm, tn, tk = min(tm, m), min(tn, n), min(tk, k)
    assert m % tm == 0 and n % tn == 0 and k % tk == 0, (name, m, n, k)
    nk = k // tk
    if mode == "tn":
        a_spec = pl.BlockSpec((tk, tm), lambda i, j, kk: (kk, i))
    else:
        a_spec = pl.BlockSpec((tm, tk), lambda i, j, kk: (i, kk))
    if mode == "nt":
        b_spec = pl.BlockSpec((tn, tk), lambda i, j, kk: (j, kk))
    else:
        b_spec = pl.BlockSpec((tk, tn), lambda i, j, kk: (kk, j))
    ca, cb = {"nn": (1, 0), "nt": (1, 1), "tn": (0, 0)}[mode]
    o_spec = pl.BlockSpec((tm, tn), lambda i, j, kk: (i, j))
    has_res = res is not None

    def body(*refs):
        if has_res:
            a_ref, b_ref, r_ref, o_ref, acc = refs
        else:
            a_ref, b_ref, o_ref, acc = refs
        kk = pl.program_id(2)

        @pl.when(kk == 0)
        def _():
            acc[...] = r_ref[...] if has_res else jnp.zeros_like(acc)

        acc[...] += _dg(a_ref[...], b_ref[...], ca, cb)

        @pl.when(kk == nk - 1)
        def _():
            o_ref[...] = acc[...].astype(out_dtype)

    return pl.pallas_call(
        body, name=name, grid=(m // tm, n // tn, nk),
        in_specs=[a_spec, b_spec] + ([o_spec] if has_res else []),
        out_specs=o_spec, out_shape=jax.ShapeDtypeStruct((m, n), out_dtype),
        scratch_shapes=[pltpu.VMEM((tm, tn), F32)],
        compiler_params=_params("parallel", "parallel", "arbitrary"),
    )(*((a, b, res) if has_res else (a, b)))


def _row(arr, tl, width=None, cb=0):
    width = arr.shape[1] if width is None else width
    return arr, pl.BlockSpec((tl, width), lambda i, cb=cb: (i, cb))


def _par(arr):
    return arr, pl.BlockSpec(arr.shape, lambda i: (0,) * arr.ndim)


def _rowwise(body, *, rows, tl, ins, outs, name):
    arrays = [a for a, _ in ins]
    in_specs = [s for _, s in ins]
    out_shape, out_specs, acc_ids = [], [], []
    for n, o in enumerate(outs):
        if o[0] == "row":
            out_shape.append(jax.ShapeDtypeStruct((rows, o[1]), o[2]))
            out_specs.append(pl.BlockSpec((tl, o[1]), lambda i: (i, 0)))
        else:
            out_shape.append(jax.ShapeDtypeStruct((1, o[1]), F32))
            out_specs.append(pl.BlockSpec((1, o[1]), lambda i: (0, 0)))
            acc_ids.append(n)
    n_in = len(arrays)
    assert rows % tl == 0, (name, rows, tl)

    def wrapped(*refs):
        in_refs, out_refs = refs[:n_in], refs[n_in:]

        @pl.when(pl.program_id(0) == 0)
        def _():
            for n in acc_ids:
                out_refs[n][...] = jnp.zeros_like(out_refs[n])

        body(in_refs, out_refs)

    return pl.pallas_call(
        wrapped, name=name, grid=(rows // tl,), in_specs=in_specs, out_specs=out_specs,
        out_shape=out_shape, compiler_params=_params("arbitrary"),
    )(*arrays)


def _rms(x, g):
    return x * lax.rsqrt(jnp.mean(x * x, axis=-1, keepdims=True) + EPS) * g


def _norm_fwd(x, g, *, name, tl=256):
    def body(i, o):
        o[0][...] = _rms(i[0][...], i[1][...]).astype(BF16)

    return _rowwise(body, rows=x.shape[0], tl=tl, ins=[_row(x, tl), _par(g)],
                    outs=[("row", x.shape[1], BF16)], name=name)[0]


def _norm_bwd(x, g, dh, dres, *, name, tl=256):
    def body(i, o):
        _, vjp = jax.vjp(_rms, i[0][...], i[1][...])
        dx, dg = vjp(i[2][...])
        o[0][...] = i[3][...] + dx
        o[1][...] += dg

    d = x.shape[1]
    return _rowwise(body, rows=x.shape[0], tl=tl, ins=[_row(x, tl), _par(g), _row(dh, tl), _row(dres, tl)],
                    outs=[("row", d, F32), ("acc", d)], name=name)


def _final(x, g, target, *, name, tl=256):
    d = x.shape[1]

    def body(i, o):
        y, vjp = jax.vjp(_rms, i[0][...], i[1][...])
        err = y - i[2][...]
        dx, dg = vjp(err * (1.0 / d))
        o[0][...] = dx
        o[1][...] += dg
        o[2][...] += jnp.full((1, LANE), 0.5 / d, F32) * jnp.sum(err * err)

    return _rowwise(body, rows=x.shape[0], tl=tl, ins=[_row(x, tl), _par(g), _row(target, tl)],
                    outs=[("row", d, F32), ("acc", d), ("acc", LANE)], name=name)


def _rot(x, cos, sin, out_ref, col, scale=1.0, inverse=False):
    x1, x2 = x[:, :LANE], x[:, LANE:]
    if inverse:
        sin = -sin
    out_ref[:, col:col + LANE] = ((x1 * cos - x2 * sin) * scale).astype(out_ref.dtype)
    out_ref[:, col + LANE:col + 2 * LANE] = ((x1 * sin + x2 * cos) * scale).astype(out_ref.dtype)


def _ret_prep(proj, cos, sin, *, qkw, d, name, tl=256):
    dk = qkw // RET_HEADS
    assert dk == 2 * LANE and (2 * qkw) % d == 0

    def body(i, o):
        c, s = i[3][...], i[4][...]
        for h in range(RET_HEADS):
            _rot(i[0][:, h * dk:(h + 1) * dk], c, s, o[0], h * dk)
            _rot(i[1][:, h * dk:(h + 1) * dk], c, s, o[1], h * dk, scale=dk ** -0.5)
        o[2][...] = i[2][...].astype(BF16)

    return _rowwise(body, rows=proj.shape[0], tl=tl,
                    ins=[_row(proj, tl, qkw, 0), _row(proj, tl, qkw, 1), _row(proj, tl, d, 2 * qkw // d),
                         _row(cos, tl), _row(sin, tl)],
                    outs=[("row", qkw, BF16), ("row", qkw, BF16), ("row", d, BF16)], name=name)


def _ret_prep_bwd(dq_rot, dk_rot, cos, sin, *, name, tl=256):
    qkw = dq_rot.shape[1]
    dk = qkw // RET_HEADS

    def body(i, o):
        c, s = i[2][...], i[3][...]
        for h in range(RET_HEADS):
            _rot(i[0][:, h * dk:(h + 1) * dk], c, s, o[0], h * dk, inverse=True)
            _rot(i[1][:, h * dk:(h + 1) * dk], c, s, o[1], h * dk, scale=dk ** -0.5, inverse=True)

    return _rowwise(body, rows=dq_rot.shape[0], tl=tl,
                    ins=[_row(dq_rot, tl), _row(dk_rot, tl), _row(cos, tl), _row(sin, tl)],
                    outs=[("row", qkw, BF16), ("row", qkw, BF16)], name=name)


def _ret_weights(lgf, lgb):
    t = lax.broadcasted_iota(jnp.int32, (CHUNK, 1), 0).astype(F32)
    diff = (lax.broadcasted_iota(jnp.int32, (CHUNK, CHUNK), 0)
            - lax.broadcasted_iota(jnp.int32, (CHUNK, CHUNK), 1)).astype(F32)
    dmat = jnp.exp(jnp.where(diff >= 0, lgf * diff, -lgb * diff))
    return dict(dmat=dmat, wqf=jnp.exp(lgf * (t + 1.0)), wkf=jnp.exp(lgf * (CHUNK - 1.0 - t)),
                wqb=jnp.exp(lgb * (CHUNK - t)), wkb=jnp.exp(lgb * t))


def _ret_f_part(q, k, v, lgf, lgb, s_f):
    w = _ret_weights(lgf, lgb)
    y = _dnn(_dnt(q, k) * w["dmat"], v) + _dnn(q * w["wqf"], s_f)
    return y, _dtn(k * w["wkf"], v)


def _ret_b_part(q, k, v, lgb, s_b):
    w = _ret_weights(lgb, lgb)
    return _dnn(q * w["wqb"], s_b), _dtn(k * w["wkb"], v)


def _chunk(c):
    return pl.ds(pl.multiple_of(c * CHUNK, CHUNK), CHUNK)


def _ret_specs(l, qkw, d):
    dk, dv = qkw // RET_HEADS, d // RET_HEADS
    return dk, dv, [pl.BlockSpec(memory_space=pltpu.SMEM),
                    pl.BlockSpec((l, dk), lambda h: (0, h)), pl.BlockSpec((l, dk), lambda h: (0, h)),
                    pl.BlockSpec((l, dv), lambda h: (0, h))]


def _ret_fwd(lg, q, k, v, *, name):
    l, qkw = q.shape
    d = v.shape[1]
    nc = l // CHUNK
    dk, dv, in_specs = _ret_specs(l, qkw, d)

    def body(lg_ref, q_ref, k_ref, v_ref, y_ref, s_ref):
        h = pl.program_id(0)
        lgf = jnp.full((1, 1), lg_ref[0, h], F32)
        lgb = jnp.full((1, 1), lg_ref[1, h], F32)
        dec_f, dec_b = jnp.exp(lgf * CHUNK), jnp.exp(lgb * CHUNK)

        def load(c):
            r = _chunk(c)
            return r, q_ref[r, :].astype(F32), k_ref[r, :].astype(F32), v_ref[r, :].astype(F32)

        s_ref[...] = jnp.zeros_like(s_ref)

        def f_step(c, _):
            r, qc, kc, vc = load(c)
            y, kv = _ret_f_part(qc, kc, vc, lgf, lgb, s_ref[...])
            y_ref[r, :] = y
            s_ref[...] = dec_f * s_ref[...] + kv
            return 0

        lax.fori_loop(0, nc, f_step, 0)
        s_ref[...] = jnp.zeros_like(s_ref)

        def b_step(n, _):
            r, qc, kc, vc = load(nc - 1 - n)
            y, kv = _ret_b_part(qc, kc, vc, lgb, s_ref[...])
            y_ref[r, :] += y
            s_ref[...] = dec_b * s_ref[...] + kv
            return 0

        lax.fori_loop(0, nc, b_step, 0)

    return pl.pallas_call(
        body, name=name, grid=(RET_HEADS,), in_specs=in_specs,
        out_specs=pl.BlockSpec((l, dv), lambda h: (0, h)), out_shape=jax.ShapeDtypeStruct((l, d), F32),
        scratch_shapes=[pltpu.VMEM((dk, dv), F32)], compiler_params=_params("arbitrary"),
    )(lg, q, k, v)


def _ret_bwd(lg, q, k, v, dy, *, name):
    l, qkw = q.shape
    d = v.shape[1]
    nc = l // CHUNK
    dk, dv, in_specs = _ret_specs(l, qkw, d)

    def body(lg_ref, q_ref, k_ref, v_ref, dy_ref, dq_ref, dk_ref, dv_ref, dlg_ref, states, s_ref, sh_ref):
        h = pl.program_id(0)
        lgf = jnp.full((1, 1), lg_ref[0, h], F32)
        lgb = jnp.full((1, 1), lg_ref[1, h], F32)
        dec_f, dec_b = jnp.exp(lgf * CHUNK), jnp.exp(lgb * CHUNK)

        def load(c):
            r = _chunk(c)
            return (r, q_ref[r, :].astype(F32), k_ref[r, :].astype(F32), v_ref[r, :].astype(F32),
                    dy_ref[r, :].astype(F32))

        s_ref[...] = jnp.zeros_like(s_ref)

        def f_states(c, _):
            _, qc, kc, vc, _ = load(c)
            states[c] = s_ref[...]
            w = _ret_weights(lgf, lgb)
            s_ref[...] = dec_f * s_ref[...] + _dtn(kc * w["wkf"], vc)
            return 0

        lax.fori_loop(0, nc, f_states, 0)
        sh_ref[...] = jnp.zeros_like(sh_ref)

        def f_adj(n, carry):
            dlf, dlb, ddec = carry
            c = nc - 1 - n
            r, qc, kc, vc, dyc = load(c)
            sc = states[c]
            _, vjp = jax.vjp(_ret_f_part, qc, kc, vc, lgf, lgb, sc)
            dq, dkk, dvv, g_f, g_b, dsc = vjp((dyc, sh_ref[...]))
            dq_ref[r, :] = dq
            dk_ref[r, :] = dkk
            dv_ref[r, :] = dvv
            ddec = ddec + jnp.sum(sh_ref[...] * sc)
            sh_ref[...] = dsc + dec_f * sh_ref[...]
            return dlf + g_f, dlb + g_b, ddec

        z = jnp.zeros((1, 1), F32)
        dlf, dlb, ddec_f = lax.fori_loop(0, nc, f_adj, (z, z, z))

        s_ref[...] = jnp.zeros_like(s_ref)

        def b_states(n, _):
            c = nc - 1 - n
            _, qc, kc, vc, _ = load(c)
            states[c] = s_ref[...]
            w = _ret_weights(lgb, lgb)
            s_ref[...] = dec_b * s_ref[...] + _dtn(kc * w["wkb"], vc)
            return 0

        lax.fori_loop(0, nc, b_states, 0)
        sh_ref[...] = jnp.zeros_like(sh_ref)

        def b_adj(c, carry):
            dlb, ddec = carry
            r, qc, kc, vc, dyc = load(c)
            sc = states[c]
            _, vjp = jax.vjp(_ret_b_part, qc, kc, vc, lgb, sc)
            dq, dkk, dvv, g_b, dsc = vjp((dyc, sh_ref[...]))
            dq_ref[r, :] += dq
            dk_ref[r, :] += dkk
            dv_ref[r, :] += dvv
            ddec = ddec + jnp.sum(sh_ref[...] * sc)
            sh_ref[...] = dsc + dec_b * sh_ref[...]
            return dlb + g_b, ddec

        dlb, ddec_b = lax.fori_loop(0, nc, b_adj, (dlb, z))
        dlf = dlf + ddec_f * dec_f * CHUNK
        dlb = dlb + ddec_b * dec_b * CHUNK
        row = lax.broadcasted_iota(jnp.int32, (SUBLANE, LANE), 0)
        dlg_ref[...] = jnp.where(row == 0, dlf, jnp.where(row == 1, dlb, 0.0))

    head = lambda w: pl.BlockSpec((l, w), lambda h: (0, h))
    return pl.pallas_call(
        body, name=name, grid=(RET_HEADS,), in_specs=in_specs + [head(dv)],
        out_specs=[head(dk), head(dk), head(dv), pl.BlockSpec((None, SUBLANE, LANE), lambda h: (h, 0, 0))],
        out_shape=[jax.ShapeDtypeStruct((l, qkw), F32), jax.ShapeDtypeStruct((l, qkw), F32),
                   jax.ShapeDtypeStruct((l, d), F32), jax.ShapeDtypeStruct((RET_HEADS, SUBLANE, LANE), F32)],
        scratch_shapes=[pltpu.VMEM((nc, dk, dv), F32), pltpu.VMEM((dk, dv), F32), pltpu.VMEM((dk, dv), F32)],
        compiler_params=_params("arbitrary"),
    )(lg, q, k, v, dy)


def _s5_param_fn(a_re, a_im, log_dt, b_re, b_im, rep):
    dt = jnp.exp(log_dt)
    mag = jnp.exp(a_re * dt)
    lam_re, lam_im = mag * jnp.cos(a_im * dt), mag * jnp.sin(a_im * dt)
    n_re, n_im = lam_re - 1.0, lam_im
    den = a_re * a_re + a_im * a_im
    c_re = (n_re * a_re + n_im * a_im) / den
    c_im = (n_im * a_re - n_re * a_im) / den
    hi = lax.Precision.HIGHEST
    c_re = jnp.dot(c_re, rep, precision=hi, preferred_element_type=F32)
    c_im = jnp.dot(c_im, rep, precision=hi, preferred_element_type=F32)
    return lam_re, lam_im, c_re * b_re - c_im * b_im, c_re * b_im + c_im * b_re


def _s5_param_shapes(a_re, b_re):
    r, p = a_re.shape
    return [jax.ShapeDtypeStruct((r, p), F32)] * 2 + [jax.ShapeDtypeStruct(b_re.shape, F32)] * 2


def _s5_prep(a_re, a_im, log_dt, b_re, b_im, rep, *, name):
    def body(*refs):
        outs = _s5_param_fn(*[r[...] for r in refs[:6]])
        for o_ref, o in zip(refs[6:], outs):
            o_ref[...] = o

    return pl.pallas_call(body, name=name, out_shape=_s5_param_shapes(a_re, b_re),
                          compiler_params=_params())(a_re, a_im, log_dt, b_re, b_im, rep)


def _s5_prep_bwd(a_re, a_im, log_dt, b_re, b_im, rep, cts, *, name):
    def body(*refs):
        ins = [r[...] for r in refs[:6]]
        _, vjp = jax.vjp(lambda *p: _s5_param_fn(*p, ins[5]), *ins[:5])
        grads = vjp(tuple(r[...] for r in refs[6:10]))
        for o_ref, o in zip(refs[10:], grads):
            o_ref[...] = o

    shapes = [jax.ShapeDtypeStruct(t.shape, F32) for t in (a_re, a_im, log_dt, b_re, b_im)]
    return pl.pallas_call(body, name=name, out_shape=shapes,
                          compiler_params=_params())(a_re, a_im, log_dt, b_re, b_im, rep, *cts)


def _eye_tiles():
    return jnp.eye(TILE_GROUPS, dtype=F32)


def _b_tiles(bbar, tiles):
    t = bbar.reshape(2, tiles, TILE_GROUPS, SSM_STATE, SSM_GROUP).transpose(0, 1, 2, 4, 3)
    t = t[:, :, :, :, None, :] * _eye_tiles()[None, None, :, None, :, None]
    return t.reshape(2, tiles, TILE_U, TILE_N)


def _b_untile(dbt, tiles):
    t = dbt.reshape(2, tiles, TILE_GROUPS, SSM_GROUP, TILE_GROUPS, SSM_STATE)
    t = (t * _eye_tiles()[None, None, :, None, :, None]).sum(axis=4)
    return t.transpose(0, 1, 2, 4, 3).reshape(2 * tiles * TILE_GROUPS, SSM_STATE * SSM_GROUP)


def _c_tiles(c, tiles):
    t = c.reshape(2, tiles, TILE_GROUPS, SSM_GROUP, SSM_STATE).transpose(0, 1, 2, 4, 3)
    t = t[:, :, :, :, None, :] * _eye_tiles()[None, None, :, None, :, None]
    return t.reshape(2, tiles, TILE_N, TILE_U)


def _c_untile(dct, tiles):
    t = dct.reshape(2, tiles, TILE_GROUPS, SSM_STATE, TILE_GROUPS, SSM_GROUP)
    t = (t * _eye_tiles()[None, None, :, None, :, None]).sum(axis=4)
    return t.transpose(0, 1, 2, 4, 3).reshape(2, tiles * TILE_GROUPS, SSM_GROUP, SSM_STATE)


def _to_segments(a):
    l, w = a.shape
    return a.reshape(N_SEG, l // N_SEG, w).transpose(1, 0, 2).reshape(l, w)


def _from_segments(a):
    l, w = a.shape
    return a.reshape(l // N_SEG, N_SEG, w).transpose(1, 0, 2).reshape(l, w)


def _s5_scan(xr, xi, a_re, a_im, *, length, reverse, shifted=None):
    ls = length // N_SEG
    assert ls * N_SEG == length and ls & (ls - 1) == 0
    ar = [jnp.broadcast_to(a_re[:, c * LANE:(c + 1) * LANE], (N_SEG, LANE)) for c in range(N_LT)]
    ai = [jnp.broadcast_to(a_im[:, c * LANE:(c + 1) * LANE], (N_SEG, LANE)) for c in range(N_LT)]
    zero = jnp.zeros((N_SEG, LANE), F32)
    row = lax.broadcasted_iota(jnp.int32, (N_SEG, LANE), 0)

    def step_of(n):
        return (ls - 1 - n) if reverse else n

    def block(j):
        return pl.ds(j * N_SEG, N_SEG) if isinstance(j, int) else pl.ds(pl.multiple_of(j * N_SEG, N_SEG), N_SEG)

    def local(n, carry):
        rows = block(step_of(n))
        new = []
        for c in range(N_LT):
            cr, ci = carry[2 * c], carry[2 * c + 1]
            nr = ar[c] * cr - ai[c] * ci + xr[c, rows, :]
            ni = ar[c] * ci + ai[c] * cr + xi[c, rows, :]
            xr[c, rows, :] = nr
            xi[c, rows, :] = ni
            new += [nr, ni]
        return tuple(new)

    ends = lax.fori_loop(0, ls, local, (zero,) * (2 * N_LT), unroll=SCAN_UNROLL)

    init = []
    for c in range(N_LT):
        pr, pi = ar[c][0:1, :], ai[c][0:1, :]
        for _ in range(ls.bit_length() - 1):
            pr, pi = pr * pr - pi * pi, 2.0 * pr * pi
        cr = ci = jnp.zeros((1, LANE), F32)
        ir, ii = zero, zero
        for s in (range(N_SEG - 1, -1, -1) if reverse else range(N_SEG)):
            ir = jnp.where(row == s, cr, ir)
            ii = jnp.where(row == s, ci, ii)
            er, ei = ends[2 * c][s:s + 1, :], ends[2 * c + 1][s:s + 1, :]
            cr, ci = pr * cr - pi * ci + er, pr * ci + pi * cr + ei
        init += [ir, ii]

    def fix(n, carry, last=False):
        j = step_of(n)
        rows = block(j)
        new, sums = [], []
        for c in range(N_LT):
            cr, ci = carry[2 * c], carry[2 * c + 1]
            nr = ar[c] * cr - ai[c] * ci
            ni = ar[c] * ci + ai[c] * cr
            fr = xr[c, rows, :] + nr
            fi = xi[c, rows, :] + ni
            xr[c, rows, :] = fr
            xi[c, rows, :] = fi
            new += [nr, ni]
            if shifted is not None:
                yr, yi, shift = shifted
                if not last:
                    srows = block(j + shift)
                    sr, si = yr[c, srows, :], yi[c, srows, :]
                else:
                    edge = block(ls - 1 if shift < 0 else 0)
                    move, gone = (1, 0) if shift < 0 else (N_SEG - 1, N_SEG - 1)
                    sr = jnp.where(row == gone, 0.0, pltpu.roll(yr[c, edge, :], move, 0))
                    si = jnp.where(row == gone, 0.0, pltpu.roll(yi[c, edge, :], move, 0))
                sums += [carry[2 * N_LT + 2 * c] + fr * sr + fi * si,
                         carry[2 * N_LT + 2 * c + 1] + fi * sr - fr * si]
        return tuple(new + sums)

    if shifted is None:
        lax.fori_loop(0, ls, fix, tuple(init), unroll=SCAN_UNROLL)
        return ()
    assert shifted[2] == (-1 if reverse else 1)
    out = lax.fori_loop(0, ls - 1, fix, tuple(init) + (zero,) * (2 * N_LT), unroll=SCAN_UNROLL)
    return fix(ls - 1, out, last=True)[2 * N_LT:]


def _s5_tile_specs(l, d):
    tile = lambda r, c: pl.BlockSpec((None, None, r, c), lambda t, d=d: (d, t, 0, 0))
    return [pl.BlockSpec((l, TILE_U), lambda t: (0, t)), tile(TILE_U, TILE_N), tile(TILE_U, TILE_N),
            tile(1, TILE_N), tile(1, TILE_N), tile(TILE_N, TILE_U), tile(TILE_N, TILE_U)]


def _lanes(c):
    return slice(c * LANE, (c + 1) * LANE)


def _s5_fwd(u, bt_re, bt_im, lam_re, lam_im, ct_re, ct_im, *, d, name, add=None):
    l = u.shape[0]
    tiles = bt_re.shape[1]
    col = pl.BlockSpec((l, TILE_U), lambda t: (0, t))
    has_add = add is not None

    def body(*refs):
        u_ref, bre, bim, lre, lim, cre, cim = refs[:7]
        y_ref, xr, xi = refs[-3:]
        uu = u_ref[...]
        bu_re, bu_im = _dg(uu, bre[...], 1, 0), _dg(uu, bim[...], 1, 0)
        for c in range(N_LT):
            xr[c] = bu_re[:, _lanes(c)]
            xi[c] = bu_im[:, _lanes(c)]
        _s5_scan(xr, xi, lre[...], lim[...], length=l, reverse=(d == 1))
        y = refs[7][...] if has_add else jnp.zeros((l, TILE_U), F32)
        for c in range(N_LT):
            y = y + _dg(xr[c], cre[_lanes(c), :], 1, 0) - _dg(xi[c], cim[_lanes(c), :], 1, 0)
        y_ref[...] = y

    return pl.pallas_call(
        body, name=name, grid=(tiles,), in_specs=_s5_tile_specs(l, d) + [col] * has_add, out_specs=col,
        out_shape=jax.ShapeDtypeStruct((l, tiles * TILE_U), F32),
        scratch_shapes=[pltpu.VMEM((N_LT, l, LANE), F32)] * 2, compiler_params=_params("arbitrary"),
    )(u, bt_re, bt_im, lam_re, lam_im, ct_re, ct_im, *([add] if has_add else []))


def _s5_bwd(u, dy, bt_re, bt_im, lam_re, lam_im, ct_re, ct_im, *, d, name, add=None):
    l = u.shape[0]
    tiles = bt_re.shape[1]
    col = pl.BlockSpec((l, TILE_U), lambda t: (0, t))
    reverse = d == 1
    has_add = add is not None

    def body(*refs):
        u_ref, bre, bim, lre, lim, cre, cim, dy_ref = refs[:8]
        du_ref, dbre, dbim, dcre, dcim, dlre, dlim, xr, xi, gr, gi = refs[-11:]
        uu, dyy = u_ref[...], dy_ref[...]
        bu_re, bu_im = _dg(uu, bre[...], 1, 0), _dg(uu, bim[...], 1, 0)
        for c in range(N_LT):
            xr[c] = bu_re[:, _lanes(c)]
            xi[c] = bu_im[:, _lanes(c)]
        _s5_scan(xr, xi, lre[...], lim[...], length=l, reverse=reverse)
        gy_re, gy_im = _dg(dyy, cre[...], 1, 1), -_dg(dyy, cim[...], 1, 1)
        for c in range(N_LT):
            gr[c] = gy_re[:, _lanes(c)]
            gi[c] = gy_im[:, _lanes(c)]
        sums = _s5_scan(gr, gi, lre[...], -lim[...], length=l, reverse=not reverse,
                        shifted=(xr, xi, 1 if reverse else -1))
        du = refs[8][...] if has_add else jnp.zeros((l, TILE_U), F32)
        for c in range(N_LT):
            dlre[:, _lanes(c)] = jnp.sum(sums[2 * c], axis=0, keepdims=True)
            dlim[:, _lanes(c)] = jnp.sum(sums[2 * c + 1], axis=0, keepdims=True)
            g_re, g_im = gr[c], gi[c]
            du = du + _dg(g_re, bre[:, _lanes(c)], 1, 1) + _dg(g_im, bim[:, _lanes(c)], 1, 1)
            dbre[:, _lanes(c)] = _dg(uu, g_re, 0, 0)
            dbim[:, _lanes(c)] = _dg(uu, g_im, 0, 0)
            dcre[_lanes(c), :] = _dg(xr[c], dyy, 0, 0)
            dcim[_lanes(c), :] = -_dg(xi[c], dyy, 0, 0)
        du_ref[...] = du

    out3 = lambda r, c: pl.BlockSpec((None, r, c), lambda t: (t, 0, 0))
    f = lambda *s: jax.ShapeDtypeStruct(s, F32)
    return pl.pallas_call(
        body, name=name, grid=(tiles,), in_specs=_s5_tile_specs(l, d) + [col] + [col] * has_add,
        out_specs=[col, out3(TILE_U, TILE_N), out3(TILE_U, TILE_N),
                   out3(TILE_N, TILE_U), out3(TILE_N, TILE_U), out3(1, TILE_N), out3(1, TILE_N)],
        out_shape=[f(l, tiles * TILE_U), f(tiles, TILE_U, TILE_N), f(tiles, TILE_U, TILE_N),
                   f(tiles, TILE_N, TILE_U), f(tiles, TILE_N, TILE_U), f(tiles, 1, TILE_N), f(tiles, 1, TILE_N)],
        scratch_shapes=[pltpu.VMEM((N_LT, l, LANE), F32)] * 4, compiler_params=_params("arbitrary"),
    )(u, bt_re, bt_im, lam_re, lam_im, ct_re, ct_im, dy, *([add] if has_add else []))


def _s5_post(y, proj, u_cb, dskip, *, name, tl=256):
    d = y.shape[1]

    def body(i, o):
        ys = jax.nn.gelu(i[0][...] + i[2][...] * i[1][...])
        o[0][...] = ys
        o[1][...] = ys.astype(BF16)

    return _rowwise(body, rows=y.shape[0], tl=tl, ins=[_row(y, tl), _row(proj, tl, d, u_cb), _par(dskip)],
                    outs=[("row", d, F32), ("row", d, BF16)], name=name)


def _s5_post_bwd(y, proj, u_cb, dskip, dys, *, name, tl=256):
    d = y.shape[1]

    def body(i, o):
        u_ = i[1][...]
        _, vjp = jax.vjp(jax.nn.gelu, i[0][...] + i[2][...] * u_)
        (dpre,) = vjp(i[3][...])
        o[0][...] = dpre
        o[1][...] += jnp.sum(dpre * u_, axis=0, keepdims=True)

    return _rowwise(body, rows=y.shape[0], tl=tl,
                    ins=[_row(y, tl), _row(proj, tl, d, u_cb), _par(dskip), _row(dys, tl)],
                    outs=[("row", d, F32), ("acc", d)], name=name)


def _du_combine(dpre, dskip, du_s5, *, name, tl=256):
    d = dpre.shape[1]

    def body(i, o):
        o[0][...] = (i[0][...] * i[1][...] + i[2][...]).astype(BF16)

    return _rowwise(body, rows=dpre.shape[0], tl=tl, ins=[_row(dpre, tl), _par(dskip), _row(du_s5, tl)],
                    outs=[("row", d, BF16)], name=name)[0]


def _merge_fn(y, g, gate_r, gate_s, ys, glu, b):
    ret = jax.nn.silu(g) * (y * lax.rsqrt(jnp.mean(y * y, axis=-1, keepdims=True) + EPS))
    ssm = ys * jax.nn.sigmoid(glu + b)
    return jax.nn.sigmoid(gate_r) * ret + jax.nn.sigmoid(gate_s) * ssm


def _merge_ins(y_raw, proj, ys, glu, b_glu, cb0, tl):
    d = y_raw.shape[1]
    return [_row(y_raw, tl), _row(proj, tl, d, cb0 + 1), _row(proj, tl, d, cb0 + 3), _row(proj, tl, d, cb0 + 4),
            _row(ys, tl), _row(glu, tl), _par(b_glu)]


def _merge(y_raw, proj, ys, glu, b_glu, *, cb0, name, tl=128):
    d = y_raw.shape[1]
    dv = d // RET_HEADS

    def body(i, o):
        for h in range(RET_HEADS):
            cs = slice(h * dv, (h + 1) * dv)
            o[0][:, cs] = _merge_fn(*[r[:, cs] for r in i]).astype(BF16)

    return _rowwise(body, rows=y_raw.shape[0], tl=tl, ins=_merge_ins(y_raw, proj, ys, glu, b_glu, cb0, tl),
                    outs=[("row", d, BF16)], name=name)[0]


def _merge_bwd(y_raw, proj, ys, glu, b_glu, dmerged, *, cb0, name, tl=128):
    d = y_raw.shape[1]
    dv = d // RET_HEADS

    def body(i, o):
        for h in range(RET_HEADS):
            cs = slice(h * dv, (h + 1) * dv)
            _, vjp = jax.vjp(_merge_fn, *[r[:, cs] for r in i[:7]])
            dy, dg, dgr, dgs, dys, dglu, db = vjp(i[7][:, cs])
            o[0][:, cs] = dy.astype(BF16)
            o[1][:, cs] = dg.astype(BF16)
            o[2][:, cs] = dgr.astype(BF16)
            o[3][:, cs] = dgs.astype(BF16)
            o[4][:, cs] = dglu.astype(BF16)
            o[5][:, cs] = dys
            o[6][:, cs] += db

    return _rowwise(body, rows=y_raw.shape[0], tl=tl,
                    ins=_merge_ins(y_raw, proj, ys, glu, b_glu, cb0, tl) + [_row(dmerged, tl)],
                    outs=[("row", d, BF16)] * 5 + [("row", d, F32), ("acc", d)], name=name)


def _ffn_act_fn(gate, up):
    return jax.nn.silu(gate) * up


def _ffn_act(gate, up, *, name, tl=128):
    def body(i, o):
        o[0][...] = _ffn_act_fn(i[0][...], i[1][...]).astype(BF16)

    return _rowwise(body, rows=gate.shape[0], tl=tl, ins=[_row(gate, tl), _row(up, tl)],
                    outs=[("row", gate.shape[1], BF16)], name=name)[0]


def _ffn_act_bwd(gate, up, dact, *, name, tl=128):
    def body(i, o):
        _, vjp = jax.vjp(_ffn_act_fn, i[0][...], i[1][...])
        dgate, dup = vjp(i[2][...])
        o[0][...] = dgate.astype(BF16)
        o[1][...] = dup.astype(BF16)

    w = gate.shape[1]
    return _rowwise(body, rows=gate.shape[0], tl=tl, ins=[_row(gate, tl), _row(up, tl), _row(dact, tl)],
                    outs=[("row", w, BF16), ("row", w, BF16)], name=name)


def _adamw(w, g, m, v):
    m = ADAM_B1 * m + (1.0 - ADAM_B1) * g
    v = ADAM_B2 * v + (1.0 - ADAM_B2) * (g * g)
    m_hat = m / (1.0 - ADAM_B1 ** ADAM_STEP)
    v_hat = v / (1.0 - ADAM_B2 ** ADAM_STEP)
    return -ADAM_LR * (m_hat / (jnp.sqrt(v_hat) + ADAM_EPS) + ADAM_WD * w), m, v


def _adam_flat(g, w, m, v, *, name, tr=FLAT_ROWS):
    def body(i, o):
        for o_ref, val in zip(o, _adamw(i[1][...], i[0][...], i[2][...], i[3][...])):
            o_ref[...] = val

    return _rowwise(body, rows=g.shape[0], tl=tr, ins=[_row(a, tr) for a in (g, w, m, v)],
                    outs=[("row", LANE, F32)] * 3, name=name)


def _adam_shard(recv, own, me, w, m, v, *, kind, layer, name, tr, tie=None, others=None):
    depth, r, c = w.shape
    assert r % tr == 0 and recv.shape[2] == c
    blk = pl.BlockSpec((None, tr, c), lambda i, me_ref: (layer, i, 0))
    if kind == "col":
        own_spec = pl.BlockSpec((tr, c), lambda i, me_ref: (i, me_ref[0]))
    else:
        per = own.shape[0] // N_DEV // tr
        assert per * tr * N_DEV == own.shape[0]
        own_spec = pl.BlockSpec((tr, c), lambda i, me_ref: (me_ref[0] * per + i, 0))
    extra = ([] if tie is None else [tie]) + list(others or [])

    def body(me_ref, recv_ref, own_ref, w_ref, m_ref, v_ref, *refs):
        g_ref, d_ref, nm_ref, nv_ref = refs[len(extra):]
        g = own_ref[...].astype(F32)
        for k in range(N_DEV - 1):
            g = g + recv_ref[k].astype(F32)
        g_ref[...] = g
        d_ref[...], nm_ref[...], nv_ref[...] = _adamw(w_ref[...], g, m_ref[...], v_ref[...])

    first = 6 + (tie is not None)
    return pl.pallas_call(
        body, name=name, out_shape=[jax.ShapeDtypeStruct(w.shape, F32)] * 4,
        grid_spec=pltpu.PrefetchScalarGridSpec(
            num_scalar_prefetch=1, grid=(r // tr,),
            in_specs=[pl.BlockSpec((N_DEV - 1, tr, c), lambda i, me_ref: (0, i, 0)), own_spec, blk, blk, blk]
            + [pl.BlockSpec(memory_space=pl.ANY)] * len(extra),
            out_specs=[blk] * 4),
        input_output_aliases={first + j: j for j in range(4)} if others else {},
        compiler_params=_params("parallel"),
    )(me, recv, own, w, m, v, *extra)


def _position():
    x, y, c = lax.axis_index("x"), lax.axis_index("y"), lax.axis_index("c")
    return x, y, c, 4 * x + 2 * y + c


def _coords(p):
    return p // 4, (p // 2) % 2, p % 2


def _block_of(ref, kind, p, nb):
    if kind == "col":
        return ref.at[:, pl.ds(pl.multiple_of(p * nb, LANE), nb)]
    return ref.at[pl.ds(pl.multiple_of(p * nb, SUBLANE), nb), :]


def _all_gather(shards, kinds, *, name):
    n = len(shards)
    out_shape = []
    for s, kind in zip(shards, kinds):
        r, c = s.shape
        out_shape.append(jax.ShapeDtypeStruct((r, c * N_DEV) if kind == "col" else (r * N_DEV, c), s.dtype))

    def body(*refs):
        shard_refs, full_refs = refs[:n], refs[n:2 * n]
        send_sems, recv_sems, local_sems = refs[2 * n:]
        x, y, c, me = _position()
        sibling = (x, y, 1 - c)
        chips = [(1 - x, y), (x, 1 - y), (1 - x, 1 - y)]

        def block(t, dev):
            nb = shards[t].shape[1 if kinds[t] == "col" else 0]
            return _block_of(full_refs[t], kinds[t], 4 * dev[0] + 2 * dev[1] + dev[2], nb)

        def copy(t, k, dev, to, src=None):
            return pltpu.make_async_remote_copy(
                src_ref=block(t, dev) if src is None else src, dst_ref=block(t, dev),
                send_sem=send_sems.at[t, k], recv_sem=recv_sems.at[t, k], device_id=to, device_id_type=MESH)

        mine, first, passed = [], [], []
        for t in range(n):
            mine.append(pltpu.make_async_copy(shard_refs[t], block(t, (x, y, c)), local_sems.at[t]))
            mine[-1].start()
            first.append(copy(t, 0, (x, y, c), sibling, src=shard_refs[t]))
            first += [copy(t, 1 + j, (x, y, c), (*chip, c), src=shard_refs[t]) for j, chip in enumerate(chips)]
        for cp in first:
            cp.start()
        for j, chip in enumerate(chips):
            for t in range(n):
                copy(t, 1 + j, (*chip, c), (x, y, c)).wait_recv()
                passed.append(copy(t, 4 + j, (*chip, c), sibling))
                passed[-1].start()
        for t in range(n):
            copy(t, 0, sibling, (x, y, c)).wait_recv()
            for j, chip in enumerate(chips):
                copy(t, 4 + j, (*chip, 1 - c), (x, y, c)).wait_recv()
        for cp in first + passed:
            cp.wait_send()
        for cp in mine:
            cp.wait()

    any_spec = pl.BlockSpec(memory_space=pl.ANY)
    return pl.pallas_call(
        body, name=name, in_specs=[any_spec] * n, out_specs=[any_spec] * n, out_shape=out_shape,
        scratch_shapes=[pltpu.SemaphoreType.DMA((n, N_DEV - 1)), pltpu.SemaphoreType.DMA((n, N_DEV - 1)),
                        pltpu.SemaphoreType.DMA((n,))],
        compiler_params=pltpu.CompilerParams(has_side_effects=True),
    )(*shards)


class _GatherRoute:
    def __init__(self, shards, kinds):
        self.kinds = kinds
        self.nb = [s.shape[1 if k == "col" else 0] for s, k in zip(shards, kinds)]

    def lands(self, shards, me):
        out = []
        for t, (s, k) in enumerate(zip(shards, self.kinds)):
            full = lax.empty((s.shape[0], s.shape[1] * N_DEV) if k == "col" else (s.shape[0] * N_DEV, s.shape[1]), s.dtype)
            out.append(_place_own(s, full, me, k, name=f"own_block_{t}"))
        return out

    def sent(self, t, src_refs, me, k):
        return src_refs[t]

    def lands_at(self, t, land_refs, me, k):
        return _block_of(land_refs[t], self.kinds[t], (me + N_DEV - k) % N_DEV, self.nb[t])


class _ScatterRoute:
    def __init__(self, grads, kinds):
        self.kinds = kinds
        self.nb = [g.shape[1 if k == "col" else 0] // N_DEV for g, k in zip(grads, kinds)]

    def lands(self, grads):
        return [lax.empty((N_DEV - 1, g.shape[0], nb) if k == "col" else (N_DEV - 1, nb, g.shape[1]), g.dtype)
                for g, k, nb in zip(grads, self.kinds, self.nb)]

    def sent(self, t, src_refs, me, k):
        return _block_of(src_refs[t], self.kinds[t], (me + k) % N_DEV, self.nb[t])

    def lands_at(self, t, land_refs, me, k):
        return land_refs[t].at[k - 1]


def _place_own(shard, full, me, kind, *, name):
    r, c = shard.shape
    tr = _row_tile(r)
    if kind == "col":
        dst = pl.BlockSpec((tr, c), lambda i, me_ref: (i, me_ref[0]))
    else:
        dst = pl.BlockSpec((tr, c), lambda i, me_ref: (me_ref[0] * (r // tr) + i, 0))

    def body(me_ref, shard_ref, full_ref, out_ref):
        out_ref[...] = shard_ref[...]

    return pl.pallas_call(
        body, name=name, out_shape=jax.ShapeDtypeStruct(full.shape, full.dtype),
        grid_spec=pltpu.PrefetchScalarGridSpec(
            num_scalar_prefetch=1, grid=(r // tr,),
            in_specs=[pl.BlockSpec((tr, c), lambda i, me_ref: (i, 0)), pl.BlockSpec(memory_space=pl.ANY)],
            out_specs=dst),
        input_output_aliases={2: 0}, compiler_params=_params("parallel"),
    )(me, shard, full)


_HBM = pl.BlockSpec(memory_space=pltpu.HBM)
_SEM = pl.BlockSpec(memory_space=pltpu.SEMAPHORE)
_FLOWING = pltpu.SideEffectType.DATAFLOW_SIDE_EFFECTING


def _exchange_start(srcs, lands, route, after, *, name):
    n = len(srcs)

    def body(*refs):
        src_refs, land_refs = refs[:n], refs[n:2 * n]
        send_sems, recv_sems = refs[2 * n + 1:2 * n + 3]
        token = refs[-1]
        _, _, _, me = _position()
        for t in range(n):
            for k in range(1, N_DEV):
                p = (me + k) % N_DEV
                pltpu.make_async_remote_copy(
                    src_ref=route.sent(t, src_refs, me, k), dst_ref=route.lands_at(t, land_refs, p, k),
                    send_sem=send_sems.at[t * N_DEV + k], recv_sem=recv_sems.at[t * N_DEV + k], device_id=_coords(p),
                    device_id_type=MESH).start()
        token[...] = jnp.zeros_like(token)

    hbm = lambda a: pltpu.HBM(a.shape, a.dtype)
    sems = pltpu.SemaphoreType.DMA((n * N_DEV,))
    out = pl.pallas_call(
        body, name=name,
        out_shape=(sems, sems, *[hbm(a) for a in srcs], *[hbm(a) for a in lands],
                   jax.ShapeDtypeStruct((SUBLANE, LANE), F32)),
        in_specs=[_HBM] * (2 * n) + [pl.BlockSpec(memory_space=pl.ANY)],
        out_specs=(_SEM, _SEM, *[_HBM] * (2 * n), pl.BlockSpec(memory_space=pltpu.VMEM)),
        input_output_aliases={i: 2 + i for i in range(2 * n)},
        compiler_params=pltpu.CompilerParams(has_side_effects=_FLOWING),
    )(*[pltpu.with_memory_space_constraint(a, pltpu.HBM) for a in (*srcs, *lands)], after)
    return out[0], out[1], out[2:2 + n], out[2 + n:2 + 2 * n], out[-1]


def _exchange_wait(started, route, after, *, name):
    send_sems, recv_sems, srcs, lands, _ = started
    n = len(srcs)

    def body(*refs):
        src_refs, land_refs = refs[:n], refs[n:2 * n]
        send_ref, recv_ref = refs[2 * n:2 * n + 2]
        _, _, _, me = _position()
        for t in range(n):
            for k in range(1, N_DEV):
                cp = pltpu.make_async_remote_copy(
                    src_ref=route.sent(t, src_refs, me, k), dst_ref=route.lands_at(t, land_refs, me, k),
                    send_sem=send_ref.at[t * N_DEV + k], recv_sem=recv_ref.at[t * N_DEV + k],
                    device_id=_coords((me + N_DEV - k) % N_DEV), device_id_type=MESH)
                cp.wait_send()
                cp.wait_recv()

    hbm = lambda a: pltpu.HBM(a.shape, a.dtype)
    out = pl.pallas_call(
        body, name=name, out_shape=(*[hbm(a) for a in srcs], *[hbm(a) for a in lands]),
        in_specs=[_HBM] * (2 * n) + [_SEM, _SEM, pl.BlockSpec(memory_space=pl.ANY)],
        out_specs=tuple([_HBM] * (2 * n)), input_output_aliases={i: i for i in range(2 * n)},
        compiler_params=pltpu.CompilerParams(has_side_effects=_FLOWING),
    )(*srcs, *lands, send_sems, recv_sems, after)
    return list(out[:n]), list(out[n:])


def _all_reduce(part, *, name):
    _, r, _ = part.shape

    def body(part_ref, tot_ref, recv_ref, send1, recv1, send2, recv2):
        _, _, _, me = _position()

        def scatter(k, to_me=False):
            p = (me + N_DEV - k) % N_DEV if to_me else (me + k) % N_DEV
            return pltpu.make_async_remote_copy(
                src_ref=part_ref.at[me if to_me else p], dst_ref=recv_ref.at[p if to_me else me],
                send_sem=send1.at[k], recv_sem=recv1.at[k], device_id=_coords(p), device_id_type=MESH)

        def gather(k, to_me=False):
            p = (me + N_DEV - k) % N_DEV if to_me else (me + k) % N_DEV
            return pltpu.make_async_remote_copy(
                src_ref=tot_ref.at[me], dst_ref=tot_ref.at[p if to_me else me],
                send_sem=send2.at[k], recv_sem=recv2.at[k], device_id=_coords(p), device_id_type=MESH)

        for k in range(1, N_DEV):
            scatter(k).start()
        recv_ref[me] = part_ref[me]
        for k in range(1, N_DEV):
            scatter(k, to_me=True).wait_recv()
        total = recv_ref[0]
        for q in range(1, N_DEV):
            total = total + recv_ref[q]
        tot_ref[me] = total
        for k in range(1, N_DEV):
            gather(k).start()
        for k in range(1, N_DEV):
            gather(k, to_me=True).wait_recv()
        for k in range(1, N_DEV):
            scatter(k).wait_send()
            gather(k).wait_send()

    vmem = pl.BlockSpec(memory_space=pltpu.VMEM)
    return pl.pallas_call(
        body, name=name, in_specs=[vmem], out_specs=vmem, out_shape=jax.ShapeDtypeStruct(part.shape, F32),
        scratch_shapes=[pltpu.VMEM(part.shape, F32)] + [pltpu.SemaphoreType.DMA((N_DEV,))] * 4,
        compiler_params=pltpu.CompilerParams(has_side_effects=True, vmem_limit_bytes=VMEM_LIMIT),
    )(part)


def _round_up(n, m):
    return (n + m - 1) // m * m


def _row_tile(rows):
    return next(t for t in (256, 128, 64, 32, 16) if rows % t == 0)


def _local_step(x, target, small, depth, weights_of, grads_done, *, qkw):
    l, d = x.shape
    groups = d // SSM_GROUP
    tiles = groups // TILE_GROUPS
    half = qkw // RET_HEADS // 2
    cb0 = 2 * qkw // d
    inv = 1.0 / (ROPE_BASE ** (jnp.arange(half, dtype=F32) / half))
    ang = jnp.arange(l, dtype=F32)[:, None] * inv[None, :]
    cos, sin = jnp.cos(ang), jnp.sin(ang)
    rep = jnp.repeat(jnp.eye(SSM_STATE, dtype=F32), SSM_GROUP, axis=1)
    row2 = lambda a: a.reshape(1, -1)

    saved = []
    for i in range(depth):
        full_i, x = weights_of(i, x)
        w_in, w_glu, w_out, w_gate, w_up, w_down = full_i
        n = f"l{i}_"
        g_mix, g_ffn = row2(small["ln_mix_g"][i]), row2(small["ln_ffn_g"][i])
        dskip, b_glu = row2(small["ssm_d"][i]), row2(small["b_glu"][i])
        lg = small["ret_log_gamma"][i]
        h = _norm_fwd(x, g_mix, name=n + "norm_mix")
        proj = _matmul(h, w_in, mode="nn", out_dtype=F32, name=n + "proj")
        q_rot, k_rot, v_bf = _ret_prep(proj, cos, sin, qkw=qkw, d=d, name=n + "ret_prep")
        y_raw = _ret_fwd(lg, q_rot, k_rot, v_bf, name=n + "ret_fwd")
        par = [small["ssm_a_re"][i].reshape(2 * groups, SSM_STATE), small["ssm_a_im"][i].reshape(2 * groups, SSM_STATE),
               small["ssm_log_dt"][i].reshape(2 * groups, 1),
               small["ssm_b_re"][i].reshape(2 * groups, SSM_STATE * SSM_GROUP),
               small["ssm_b_im"][i].reshape(2 * groups, SSM_STATE * SSM_GROUP), rep]
        lam_re, lam_im, bbar_re, bbar_im = _s5_prep(*par, name=n + "s5_prep")
        s5 = [_b_tiles(bbar_re, tiles).astype(BF16), _b_tiles(bbar_im, tiles).astype(BF16),
              lam_re.reshape(2, tiles, 1, TILE_N), lam_im.reshape(2, tiles, 1, TILE_N),
              _c_tiles(small["ssm_c_re"][i], tiles).astype(BF16), _c_tiles(small["ssm_c_im"][i], tiles).astype(BF16)]
        u_seg = _to_segments(proj[:, (cb0 + 2) * d:(cb0 + 3) * d])
        y_seg = _s5_fwd(u_seg, *s5, d=0, name=n + "s5_fwd_f")
        y_seg = _s5_fwd(u_seg, *s5, d=1, add=y_seg, name=n + "s5_fwd_b")
        y_s5 = _from_segments(y_seg)
        ys, ys_bf = _s5_post(y_s5, proj, cb0 + 2, dskip, name=n + "s5_post")
        glu = _matmul(ys_bf, w_glu, mode="nn", out_dtype=F32, name=n + "glu")
        merged = _merge(y_raw, proj, ys, glu, b_glu, cb0=cb0, name=n + "merge")
        x1 = _matmul(merged, w_out, mode="nn", out_dtype=F32, res=x, name=n + "out")
        h2 = _norm_fwd(x1, g_ffn, name=n + "norm_ffn")
        gate = _matmul(h2, w_gate, mode="nn", out_dtype=F32, name=n + "gate")
        up = _matmul(h2, w_up, mode="nn", out_dtype=F32, name=n + "up")
        act = _ffn_act(gate, up, name=n + "act")
        x2 = _matmul(act, w_down, mode="nn", out_dtype=F32, res=x1, name=n + "down")
        saved.append(dict(full=full_i, x=x, h=h, proj=proj, q_rot=q_rot, k_rot=k_rot, v_bf=v_bf, y_raw=y_raw, par=par, s5=s5,
                          u_seg=u_seg, y_s5=y_s5, ys=ys, ys_bf=ys_bf, glu=glu, merged=merged, x1=x1, h2=h2, gate=gate,
                          up=up, act=act))
        x = x2

    dx, dg_final, loss = _final(x, row2(small["ln_final_g"]), target, name="final")

    sg = {k: [None] * depth for k in ("ln_mix_g", "ret_log_gamma", "ssm_a_re", "ssm_a_im", "ssm_log_dt", "ssm_b_re",
                                      "ssm_b_im", "ssm_c_re", "ssm_c_im", "ssm_d", "b_glu", "ln_ffn_g")}
    for i in reversed(range(depth)):
        s = saved[i]
        big = [None] * len(BIG)
        w_in, w_glu, w_out, w_gate, w_up, w_down = s["full"]
        n = f"l{i}_b_"
        g_mix, g_ffn = row2(small["ln_mix_g"][i]), row2(small["ln_ffn_g"][i])
        dskip, b_glu = row2(small["ssm_d"][i]), row2(small["b_glu"][i])
        lg = small["ret_log_gamma"][i]
        dact = _matmul(dx, w_down, mode="nt", out_dtype=F32, name=n + "dact")
        big[5] = _matmul(s["act"], dx, mode="tn", out_dtype=BF16, name=n + "dw_down")
        dgate, dup = _ffn_act_bwd(s["gate"], s["up"], dact, name=n + "act")
        dh2 = _matmul(dgate, w_gate, mode="nt", out_dtype=F32, name=n + "dh2_gate")
        dh2 = _matmul(dup, w_up, mode="nt", out_dtype=F32, res=dh2, name=n + "dh2_up")
        big[3] = _matmul(s["h2"], dgate, mode="tn", out_dtype=BF16, name=n + "dw_gate")
        big[4] = _matmul(s["h2"], dup, mode="tn", out_dtype=BF16, name=n + "dw_up")
        dh2 = grads_done(i, FFN_GROUP, [big[t] for t in FFN_GROUP], dh2)
        dx1, dgf = _norm_bwd(s["x1"], g_ffn, dh2, dx, name=n + "norm_ffn")
        sg["ln_ffn_g"][i] = dgf[0]
        dmerged = _matmul(dx1, w_out, mode="nt", out_dtype=F32, name=n + "dmerged")
        big[2] = _matmul(s["merged"], dx1, mode="tn", out_dtype=BF16, name=n + "dw_out")
        dy_raw, dg, dgate_r, dgate_s, dglu, dys_a, db_glu = _merge_bwd(
            s["y_raw"], s["proj"], s["ys"], s["glu"], b_glu, dmerged, cb0=cb0, name=n + "merge")
        sg["b_glu"][i] = db_glu[0]
        dys = _matmul(dglu, w_glu, mode="nt", out_dtype=F32, res=dys_a, name=n + "dys")
        big[1] = _matmul(s["ys_bf"], dglu, mode="tn", out_dtype=BF16, name=n + "dw_glu")
        dpre, dd = _s5_post_bwd(s["y_s5"], s["proj"], cb0 + 2, dskip, dys, name=n + "s5_post")
        sg["ssm_d"][i] = dd[0]
        dpre_seg = _to_segments(dpre)
        r_f = _s5_bwd(s["u_seg"], dpre_seg, *s["s5"], d=0, name=n + "s5_bwd_f")
        r_b = _s5_bwd(s["u_seg"], dpre_seg, *s["s5"], d=1, add=r_f[0], name=n + "s5_bwd_b")
        du = _du_combine(dpre, dskip, _from_segments(r_b[0]), name=n + "du")
        both = lambda k: jnp.stack([r_f[k], r_b[k]])
        cts = [both(5).reshape(2 * groups, SSM_STATE), both(6).reshape(2 * groups, SSM_STATE),
               _b_untile(both(1), tiles), _b_untile(both(2), tiles)]
        da_re, da_im, dldt, db_re, db_im = _s5_prep_bwd(*s["par"], cts, name=n + "s5_prep")
        sg["ssm_a_re"][i] = da_re.reshape(2, groups, SSM_STATE)
        sg["ssm_a_im"][i] = da_im.reshape(2, groups, SSM_STATE)
        sg["ssm_log_dt"][i] = dldt.reshape(2, groups)
        sg["ssm_b_re"][i] = db_re.reshape(2, groups, SSM_STATE, SSM_GROUP)
        sg["ssm_b_im"][i] = db_im.reshape(2, groups, SSM_STATE, SSM_GROUP)
        sg["ssm_c_re"][i] = _c_untile(both(3), tiles)
        sg["ssm_c_im"][i] = _c_untile(both(4), tiles)
        dq_rot, dk_rot, dv, dlg = _ret_bwd(lg, s["q_rot"], s["k_rot"], s["v_bf"], dy_raw, name=n + "ret_bwd")
        sg["ret_log_gamma"][i] = dlg[:, :2, 0].T
        dq, dk = _ret_prep_bwd(dq_rot, dk_rot, cos, sin, name=n + "ret_prep")
        dproj = jnp.concatenate([dq, dk, dv.astype(BF16), dg, du, dgate_r, dgate_s], axis=1)
        dh = _matmul(dproj, w_in, mode="nt", out_dtype=F32, name=n + "dh")
        big[0] = _matmul(s["h"], dproj, mode="tn", out_dtype=BF16, name=n + "dw_in")
        dx, dgm = _norm_bwd(s["x"], g_mix, dh, dx1, name=n + "norm_mix")
        sg["ln_mix_g"][i] = dgm[0]
        dx = grads_done(i, MIX_GROUP, [big[t] for t in MIX_GROUP], dx)

    small_grads = {k: jnp.stack(v) for k, v in sg.items()}
    small_grads["ln_final_g"] = dg_final[0]
    return loss, dx, small_grads


BIG = ("w_in", "w_glu", "w_out", "w_ffn_gate", "w_ffn_up", "w_ffn_down")
BIG_KINDS = ("col", "row", "row", "col", "col", "row")
MIX_GROUP = (0, 1, 2)
FFN_GROUP = (3, 4, 5)
SMALL = ("ln_mix_g", "ret_log_gamma", "ssm_a_re", "ssm_a_im", "ssm_log_dt", "ssm_b_re", "ssm_b_im", "ssm_c_re",
         "ssm_c_im", "ssm_d", "b_glu", "ln_ffn_g", "ln_final_g")
WEIGHTS = ("ln_mix_g", "w_in", "ret_log_gamma", "ssm_a_re", "ssm_a_im", "ssm_log_dt", "ssm_b_re", "ssm_b_im",
           "ssm_c_re", "ssm_c_im", "ssm_d", "w_glu", "b_glu", "w_out", "ln_ffn_g", "w_ffn_gate", "w_ffn_up",
           "w_ffn_down", "ln_final_g")


def _pad_to(a, axis, size):
    pad = [(0, 0)] * a.ndim
    pad[axis] = (0, size - a.shape[axis])
    return jnp.pad(a, pad)


def _flatten_small(tree, extra):
    def as_rows(a):
        a = a.reshape(-1).astype(F32)
        return _pad_to(a, 0, _round_up(a.shape[0], SUBLANE * LANE)).reshape(-1, LANE)

    flat = jnp.concatenate([as_rows(tree[k]) for k in SMALL] + [as_rows(extra)])
    return _pad_to(flat, 0, _round_up(flat.shape[0], FLAT_ROWS))


def _unflatten_small(flat, like):
    out, at = {}, 0
    for k in SMALL:
        n = like[k].size
        rows = _round_up(n, SUBLANE * LANE) // LANE
        out[k] = flat[at:at + rows].reshape(-1)[:n].reshape(like[k].shape)
        at += rows
    return out, flat[at, 0]


def kernel(x, ln_mix_g, w_in, ret_log_gamma, ssm_a_re, ssm_a_im, ssm_log_dt, ssm_b_re, ssm_b_im, ssm_c_re, ssm_c_im, ssm_d, w_glu, b_glu, w_out, ln_ffn_g, w_ffn_gate, w_ffn_up, w_ffn_down, ln_final_g, loss_target, m_ln_mix_g, m_w_in, m_ret_log_gamma, m_ssm_a_re, m_ssm_a_im, m_ssm_log_dt, m_ssm_b_re, m_ssm_b_im, m_ssm_c_re, m_ssm_c_im, m_ssm_d, m_w_glu, m_b_glu, m_w_out, m_ln_ffn_g, m_w_ffn_gate, m_w_ffn_up, m_w_ffn_down, m_ln_final_g, v_ln_mix_g, v_w_in, v_ret_log_gamma, v_ssm_a_re, v_ssm_a_im, v_ssm_log_dt, v_ssm_b_re, v_ssm_b_im, v_ssm_c_re, v_ssm_c_im, v_ssm_d, v_w_glu, v_b_glu, v_w_out, v_ln_ffn_g, v_w_ffn_gate, v_w_ffn_up, v_w_ffn_down, v_ln_final_g):
    w = dict(ln_mix_g=ln_mix_g, w_in=w_in, ret_log_gamma=ret_log_gamma, ssm_a_re=ssm_a_re, ssm_a_im=ssm_a_im, ssm_log_dt=ssm_log_dt, ssm_b_re=ssm_b_re, ssm_b_im=ssm_b_im, ssm_c_re=ssm_c_re, ssm_c_im=ssm_c_im, ssm_d=ssm_d, w_glu=w_glu, b_glu=b_glu, w_out=w_out, ln_ffn_g=ln_ffn_g, w_ffn_gate=w_ffn_gate, w_ffn_up=w_ffn_up, w_ffn_down=w_ffn_down, ln_final_g=ln_final_g)
    m = dict(ln_mix_g=m_ln_mix_g, w_in=m_w_in, ret_log_gamma=m_ret_log_gamma, ssm_a_re=m_ssm_a_re, ssm_a_im=m_ssm_a_im, ssm_log_dt=m_ssm_log_dt, ssm_b_re=m_ssm_b_re, ssm_b_im=m_ssm_b_im, ssm_c_re=m_ssm_c_re, ssm_c_im=m_ssm_c_im, ssm_d=m_ssm_d, w_glu=m_w_glu, b_glu=m_b_glu, w_out=m_w_out, ln_ffn_g=m_ln_ffn_g, w_ffn_gate=m_w_ffn_gate, w_ffn_up=m_w_ffn_up, w_ffn_down=m_w_ffn_down, ln_final_g=m_ln_final_g)
    v = dict(ln_mix_g=v_ln_mix_g, w_in=v_w_in, ret_log_gamma=v_ret_log_gamma, ssm_a_re=v_ssm_a_re, ssm_a_im=v_ssm_a_im, ssm_log_dt=v_ssm_log_dt, ssm_b_re=v_ssm_b_re, ssm_b_im=v_ssm_b_im, ssm_c_re=v_ssm_c_re, ssm_c_im=v_ssm_c_im, ssm_d=v_ssm_d, w_glu=v_w_glu, b_glu=v_b_glu, w_out=v_w_out, ln_ffn_g=v_ln_ffn_g, w_ffn_gate=v_w_ffn_gate, w_ffn_up=v_w_ffn_up, w_ffn_down=v_w_ffn_down, ln_final_g=v_ln_final_g)
    depth, d, nb_in = w_in.shape
    qkw = (nb_in * N_DEV - 5 * d) // 2
    nb_ffn = w_ffn_gate.shape[2]
    nb_pad = _round_up(nb_ffn, LANE)
    pad_axis = {"w_ffn_gate": 2, "w_ffn_up": 2, "w_ffn_down": 1}

    assert depth == 2
    padded = {k: w[k] if k not in pad_axis else _pad_to(w[k], pad_axis[k], nb_pad) for k in BIG}
    shards = [[padded[k][i].astype(BF16) for k in BIG] for i in range(depth)]
    full0 = _all_gather(shards[0], BIG_KINDS, name="gather_l0")
    me = (4 * lax.axis_index("x") + 2 * lax.axis_index("y") + lax.axis_index("c")).astype(jnp.int32).reshape(1)
    gather1 = _GatherRoute(shards[1], BIG_KINDS)
    gathering = _exchange_start(shards[1], gather1.lands(shards[1], me), gather1, full0[0], name="gather_l1_start")

    def weights_of(i, act):
        if i == 0:
            return full0, act + gathering[4][0, 0]
        return _exchange_wait(gathering, gather1, act, name="gather_l1_wait")[1], act

    exchanges, held = [], []

    def grads_done(i, group, grads, act):
        route = _ScatterRoute(grads, [BIG_KINDS[t] for t in group])
        if i == 0 and group == MIX_GROUP:
            held.append((i, group, grads, route))
            return act
        started = _exchange_start(grads, route.lands(grads), route, act, name=f"scatter_l{i}_{group[0]}_start")
        exchanges.append((i, group, started, route))
        return act + started[4][0, 0]

    small = {k: w[k] for k in SMALL}
    loss, dx, small_grads = _local_step(x[0], loss_target[0], small, depth, weights_of, grads_done, qkw=qkw)
    arrived, tie = {}, dx
    for i, group, started, route in exchanges:
        own, recv = _exchange_wait(started, route, tie, name=f"scatter_l{i}_{group[0]}_wait")
        arrived.update({(i, t): (own[j], recv[j]) for j, t in enumerate(group)})
        tie = recv[0]

    part = _flatten_small(small_grads, loss[0, :1])
    rows = part.shape[0]
    total = _all_reduce(part.reshape(N_DEV, rows // N_DEV, LANE), name="reduce_small").reshape(rows, LANE)
    (i, group, held_grads, route), = held
    started = _exchange_start(held_grads, route.lands(held_grads), route, total, name=f"scatter_l{i}_{group[0]}_start")
    zero = jnp.zeros((1,), F32)
    flat = [_flatten_small({k: a[k] for k in SMALL}, zero) for a in (w, m, v)]
    upd = _adam_flat(total + started[4][0, 0], *flat, name="adam_small")
    grads, delta, new_m, new_v = {}, {}, {}, {}
    g_small, loss_total = _unflatten_small(total, small)
    grads.update(g_small)
    for dst, u in zip((delta, new_m, new_v), upd):
        dst.update(_unflatten_small(u, small)[0])

    ops = {k: [_pad_to(a[k], 2, nb_pad) for a in (w, m, v)] if k in ("w_ffn_gate", "w_ffn_up") else [a[k] for a in (w, m, v)]
           for k in BIG}
    half, tie = {}, upd[0]

    def adam(layer, t, tie):
        k = BIG[t]
        own, recv = arrived[layer, t]
        return _adam_shard(recv, own, me, *ops[k], kind=BIG_KINDS[t], layer=layer, name=f"adam_l{layer}_{k}",
                           tr=_row_tile(ops[k][0].shape[1]), tie=tie, others=half.get(k))

    for t in range(len(BIG)):
        half[BIG[t]] = adam(1, t, tie)
        tie = half[BIG[t]][0]
    for t in FFN_GROUP:
        half[BIG[t]] = adam(0, t, tie)
        tie = half[BIG[t]][0]
    own, recv = _exchange_wait(started, route, tie, name=f"scatter_l{i}_{group[0]}_wait")
    arrived.update({(i, t): (own[j], recv[j]) for j, t in enumerate(group)})
    for t in MIX_GROUP:
        half[BIG[t]] = adam(0, t, None)
    for k in BIG:
        res = half[k]
        if k in ("w_ffn_gate", "w_ffn_up"):
            res = [r[:, :, :nb_ffn] for r in res]
        grads[k], delta[k], new_m[k], new_v[k] = res

    return (loss_total, dx[None], *[grads[k] for k in WEIGHTS], *[delta[k] for k in WEIGHTS],
            *[new_m[k] for k in WEIGHTS], *[new_v[k] for k in WEIGHTS])
```

```python
import math

import jax
import jax.numpy as jnp
from jax import lax
from jax.experimental import pallas as pl
from jax.experimental.pallas import tpu as pltpu

F32 = jnp.float32
BF16 = jnp.bfloat16
MESH = pl.DeviceIdType.MESH

N_DEV = 8
RET_HEADS = 4
CHUNK = 128
ROPE_BASE = 10000.0
SSM_GROUP = 16
SSM_STATE = 64
TILE_GROUPS = 8
TILE_U = TILE_GROUPS * SSM_GROUP
TILE_N = TILE_GROUPS * SSM_STATE
LANE = 128
SUBLANE = 8
N_SEG = SUBLANE
N_LT = TILE_N // LANE
SCAN_UNROLL = 4
FLAT_ROWS = 1024
EPS = 1e-6
ADAM_LR = 0.001
ADAM_B1 = 0.9
ADAM_B2 = 0.999
ADAM_EPS = 1e-08
ADAM_WD = 0.01
ADAM_STEP = 10
VMEM_LIMIT = 56 * 1024 * 1024


def _params(*sem):
    return pltpu.CompilerParams(dimension_semantics=sem or None, vmem_limit_bytes=VMEM_LIMIT)


def _dg(a, b, ca, cb):
    return lax.dot_general(a.astype(BF16), b.astype(BF16), (((ca,), (cb,)), ((), ())),
                           preferred_element_type=F32)


@jax.custom_vjp
def _dnn(a, b):
    return _dg(a, b, 1, 0)


@jax.custom_vjp
def _dnt(a, b):
    return _dg(a, b, 1, 1)


@jax.custom_vjp
def _dtn(a, b):
    return _dg(a, b, 0, 0)


_dnn.defvjp(lambda a, b: (_dnn(a, b), (a, b)), lambda r, g: (_dnt(g, r[1]), _dtn(r[0], g)))
_dnt.defvjp(lambda a, b: (_dnt(a, b), (a, b)), lambda r, g: (_dnn(g, r[1]), _dtn(g, r[0])))
_dtn.defvjp(lambda a, b: (_dtn(a, b), (a, b)), lambda r, g: (_dnt(r[1], g), _dnn(r[0], g)))


def _matmul(a, b, *, mode, out_dtype, name, res=None, tm=1024, tn=1024, tk=2048):
    if mode == "nn":
        (m, k), n = a.shape, b.shape[1]
    elif mode == "nt":
        (m, k), n = a.shape, b.shape[0]
    else:
        (k, m), n = a.shape, b.shape[1]
    tm, tn, tk = min(tm, m), min(tn, n), min(tk, k)
    assert m % tm == 0 and n % tn == 0 and k % tk == 0, (name, m, n, k)
    nk = k // tk
    if mode == "tn":
        a_spec = pl.BlockSpec((tk, tm), lambda i, j, kk: (kk, i))
    else:
        a_spec = pl.BlockSpec((tm, tk), lambda i, j, kk: (i, kk))
    if mode == "nt":
        b_spec = pl.BlockSpec((tn, tk), lambda i, j, kk: (j, kk))
    else:
        b_spec = pl.BlockSpec((tk, tn), lambda i, j, kk: (kk, j))
    ca, cb = {"nn": (1, 0), "nt": (1, 1), "tn": (0, 0)}[mode]
    o_spec = pl.BlockSpec((tm, tn), lambda i, j, kk: (i, j))
    has_res = res is not None

    def body(*refs):
        a_ref, b_ref = refs[:2]
        r_ref = refs[2] if has_res else None
        o_ref = refs[2 + has_res]
        part = _dg(a_ref[...], b_ref[...], ca, cb)
        if nk == 1:
            o_ref[...] = (part + r_ref[...] if has_res else part).astype(out_dtype)
            return
        acc = refs[-1]
        kk = pl.program_id(2)

        @pl.when(kk == 0)
        def _():
            acc[...] = part + r_ref[...] if has_res else part

        @pl.when(kk > 0)
        def _():
            acc[...] += part

        @pl.when(kk == nk - 1)
        def _():
            o_ref[...] = acc[...].astype(out_dtype)

    return pl.pallas_call(
        body, name=name, grid=(m // tm, n // tn, nk),
        in_specs=[a_spec, b_spec] + ([o_spec] if has_res else []),
        out_specs=o_spec, out_shape=jax.ShapeDtypeStruct((m, n), out_dtype),
        scratch_shapes=[pltpu.VMEM((tm, tn), F32)] if nk > 1 else [],
        compiler_params=_params("parallel", "parallel", "arbitrary"),
    )(*((a, b, res) if has_res else (a, b)))


def _row(arr, tl, width=None, cb=0):
    width = arr.shape[1] if width is None else width
    return arr, pl.BlockSpec((tl, width), lambda i, cb=cb: (i, cb))


def _par(arr):
    return arr, pl.BlockSpec(arr.shape, lambda i: (0,) * arr.ndim)


def _rowwise(body, *, rows, tl, ins, outs, name):
    arrays = [a for a, _ in ins]
    in_specs = [s for _, s in ins]
    out_shape, out_specs, acc_ids = [], [], []
    for n, o in enumerate(outs):
        if o[0] == "row":
            out_shape.append(jax.ShapeDtypeStruct((rows, o[1]), o[2]))
            out_specs.append(pl.BlockSpec((tl, o[1]), lambda i: (i, 0)))
        else:
            out_shape.append(jax.ShapeDtypeStruct((1, o[1]), F32))
            out_specs.append(pl.BlockSpec((1, o[1]), lambda i: (0, 0)))
            acc_ids.append(n)
    n_in = len(arrays)
    assert rows % tl == 0, (name, rows, tl)

    def wrapped(*refs):
        in_refs, out_refs = refs[:n_in], refs[n_in:]

        @pl.when(pl.program_id(0) == 0)
        def _():
            for n in acc_ids:
                out_refs[n][...] = jnp.zeros_like(out_refs[n])

        body(in_refs, out_refs)

    return pl.pallas_call(
        wrapped, name=name, grid=(rows // tl,), in_specs=in_specs, out_specs=out_specs,
        out_shape=out_shape, compiler_params=_params("arbitrary"),
    )(*arrays)


def _rms(x, g):
    return x * lax.rsqrt(jnp.mean(x * x, axis=-1, keepdims=True) + EPS) * g


def _norm_fwd(x, g, *, name, tl=256):
    def body(i, o):
        o[0][...] = _rms(i[0][...], i[1][...]).astype(BF16)

    return _rowwise(body, rows=x.shape[0], tl=tl, ins=[_row(x, tl), _par(g)],
                    outs=[("row", x.shape[1], BF16)], name=name)[0]


def _norm_bwd(x, g, dh, dres, *, name, tl=256):
    def body(i, o):
        _, vjp = jax.vjp(_rms, i[0][...], i[1][...])
        dx, dg = vjp(i[2][...])
        o[0][...] = i[3][...] + dx
        o[1][...] += dg

    d = x.shape[1]
    return _rowwise(body, rows=x.shape[0], tl=tl, ins=[_row(x, tl), _par(g), _row(dh, tl), _row(dres, tl)],
                    outs=[("row", d, F32), ("acc", d)], name=name)


def _final(x, g, target, *, name, tl=256):
    d = x.shape[1]

    def body(i, o):
        y, vjp = jax.vjp(_rms, i[0][...], i[1][...])
        err = y - i[2][...]
        dx, dg = vjp(err * (1.0 / d))
        o[0][...] = dx
        o[1][...] += dg
        o[2][...] += jnp.full((1, LANE), 0.5 / d, F32) * jnp.sum(err * err)

    return _rowwise(body, rows=x.shape[0], tl=tl, ins=[_row(x, tl), _par(g), _row(target, tl)],
                    outs=[("row", d, F32), ("acc", d), ("acc", LANE)], name=name)


def _rot(x, cos, sin, out_ref, col, scale=1.0, inverse=False):
    x1, x2 = x[:, :LANE], x[:, LANE:]
    if inverse:
        sin = -sin
    out_ref[:, col:col + LANE] = ((x1 * cos - x2 * sin) * scale).astype(out_ref.dtype)
    out_ref[:, col + LANE:col + 2 * LANE] = ((x1 * sin + x2 * cos) * scale).astype(out_ref.dtype)


def _ret_prep(proj, cos, sin, *, qkw, d, name, tl=256):
    dk = qkw // RET_HEADS
    assert dk == 2 * LANE and (2 * qkw) % d == 0

    def body(i, o):
        c, s = i[3][...], i[4][...]
        for h in range(RET_HEADS):
            _rot(i[0][:, h * dk:(h + 1) * dk], c, s, o[0], h * dk)
            _rot(i[1][:, h * dk:(h + 1) * dk], c, s, o[1], h * dk, scale=dk ** -0.5)
        o[2][...] = i[2][...].astype(BF16)

    return _rowwise(body, rows=proj.shape[0], tl=tl,
                    ins=[_row(proj, tl, qkw, 0), _row(proj, tl, qkw, 1), _row(proj, tl, d, 2 * qkw // d),
                         _row(cos, tl), _row(sin, tl)],
                    outs=[("row", qkw, BF16), ("row", qkw, BF16), ("row", d, BF16)], name=name)


def _ret_prep_bwd(dq_rot, dk_rot, cos, sin, *, name, tl=256):
    qkw = dq_rot.shape[1]
    dk = qkw // RET_HEADS

    def body(i, o):
        c, s = i[2][...], i[3][...]
        for h in range(RET_HEADS):
            _rot(i[0][:, h * dk:(h + 1) * dk], c, s, o[0], h * dk, inverse=True)
            _rot(i[1][:, h * dk:(h + 1) * dk], c, s, o[1], h * dk, scale=dk ** -0.5, inverse=True)

    return _rowwise(body, rows=dq_rot.shape[0], tl=tl,
                    ins=[_row(dq_rot, tl), _row(dk_rot, tl), _row(cos, tl), _row(sin, tl)],
                    outs=[("row", qkw, BF16), ("row", qkw, BF16)], name=name)


def _ret_weights(lgf, lgb):
    t = lax.broadcasted_iota(jnp.int32, (CHUNK, 1), 0).astype(F32)
    diff = (lax.broadcasted_iota(jnp.int32, (CHUNK, CHUNK), 0)
            - lax.broadcasted_iota(jnp.int32, (CHUNK, CHUNK), 1)).astype(F32)
    dmat = jnp.exp(jnp.where(diff >= 0, lgf * diff, -lgb * diff))
    return dict(dmat=dmat, wqf=jnp.exp(lgf * (t + 1.0)), wkf=jnp.exp(lgf * (CHUNK - 1.0 - t)),
                wqb=jnp.exp(lgb * (CHUNK - t)), wkb=jnp.exp(lgb * t))


def _ret_f_part(q, k, v, lgf, lgb, s_f):
    w = _ret_weights(lgf, lgb)
    y = _dnn(_dnt(q, k) * w["dmat"], v) + _dnn(q * w["wqf"], s_f)
    return y, _dtn(k * w["wkf"], v)


def _ret_b_part(q, k, v, lgb, s_b):
    w = _ret_weights(lgb, lgb)
    return _dnn(q * w["wqb"], s_b), _dtn(k * w["wkb"], v)


def _chunk(c):
    return pl.ds(pl.multiple_of(c * CHUNK, CHUNK), CHUNK)


def _ret_specs(l, qkw, d):
    dk, dv = qkw // RET_HEADS, d // RET_HEADS
    return dk, dv, [pl.BlockSpec(memory_space=pltpu.SMEM),
                    pl.BlockSpec((l, dk), lambda h: (0, h)), pl.BlockSpec((l, dk), lambda h: (0, h)),
                    pl.BlockSpec((l, dv), lambda h: (0, h))]


def _ret_fwd(lg, q, k, v, *, name):
    l, qkw = q.shape
    d = v.shape[1]
    nc = l // CHUNK
    dk, dv, in_specs = _ret_specs(l, qkw, d)

    def body(lg_ref, q_ref, k_ref, v_ref, y_ref, s_ref):
        h = pl.program_id(0)
        lgf = jnp.full((1, 1), lg_ref[0, h], F32)
        lgb = jnp.full((1, 1), lg_ref[1, h], F32)
        dec_f, dec_b = jnp.exp(lgf * CHUNK), jnp.exp(lgb * CHUNK)

        def load(c):
            r = _chunk(c)
            return r, q_ref[r, :].astype(F32), k_ref[r, :].astype(F32), v_ref[r, :].astype(F32)

        s_ref[...] = jnp.zeros_like(s_ref)

        def f_step(c, _):
            r, qc, kc, vc = load(c)
            y, kv = _ret_f_part(qc, kc, vc, lgf, lgb, s_ref[...])
            y_ref[r, :] = y
            s_ref[...] = dec_f * s_ref[...] + kv
            return 0

        lax.fori_loop(0, nc, f_step, 0)
        s_ref[...] = jnp.zeros_like(s_ref)

        def b_step(n, _):
            r, qc, kc, vc = load(nc - 1 - n)
            y, kv = _ret_b_part(qc, kc, vc, lgb, s_ref[...])
            y_ref[r, :] += y
            s_ref[...] = dec_b * s_ref[...] + kv
            return 0

        lax.fori_loop(0, nc, b_step, 0)

    return pl.pallas_call(
        body, name=name, grid=(RET_HEADS,), in_specs=in_specs,
        out_specs=pl.BlockSpec((l, dv), lambda h: (0, h)), out_shape=jax.ShapeDtypeStruct((l, d), F32),
        scratch_shapes=[pltpu.VMEM((dk, dv), F32)], compiler_params=_params("arbitrary"),
    )(lg, q, k, v)


def _ret_bwd(lg, q, k, v, dy, *, name):
    l, qkw = q.shape
    d = v.shape[1]
    nc = l // CHUNK
    dk, dv, in_specs = _ret_specs(l, qkw, d)

    def body(lg_ref, q_ref, k_ref, v_ref, dy_ref, dq_ref, dk_ref, dv_ref, dlg_ref, states, s_ref, sh_ref):
        h = pl.program_id(0)
        lgf = jnp.full((1, 1), lg_ref[0, h], F32)
        lgb = jnp.full((1, 1), lg_ref[1, h], F32)
        dec_f, dec_b = jnp.exp(lgf * CHUNK), jnp.exp(lgb * CHUNK)

        def load(c):
            r = _chunk(c)
            return (r, q_ref[r, :].astype(F32), k_ref[r, :].astype(F32), v_ref[r, :].astype(F32),
                    dy_ref[r, :].astype(F32))

        s_ref[...] = jnp.zeros_like(s_ref)

        def f_states(c, _):
            _, qc, kc, vc, _ = load(c)
            states[c] = s_ref[...]
            w = _ret_weights(lgf, lgb)
            s_ref[...] = dec_f * s_ref[...] + _dtn(kc * w["wkf"], vc)
            return 0

        lax.fori_loop(0, nc, f_states, 0)
        sh_ref[...] = jnp.zeros_like(sh_ref)

        def f_adj(n, carry):
            dlf, dlb, ddec = carry
            c = nc - 1 - n
            r, qc, kc, vc, dyc = load(c)
            sc = states[c]
            _, vjp = jax.vjp(_ret_f_part, qc, kc, vc, lgf, lgb, sc)
            dq, dkk, dvv, g_f, g_b, dsc = vjp((dyc, sh_ref[...]))
            dq_ref[r, :] = dq
            dk_ref[r, :] = dkk
            dv_ref[r, :] = dvv
            ddec = ddec + jnp.sum(sh_ref[...] * sc)
            sh_ref[...] = dsc + dec_f * sh_ref[...]
            return dlf + g_f, dlb + g_b, ddec

        z = jnp.zeros((1, 1), F32)
        dlf, dlb, ddec_f = lax.fori_loop(0, nc, f_adj, (z, z, z))

        s_ref[...] = jnp.zeros_like(s_ref)

        def b_states(n, _):
            c = nc - 1 - n
            _, qc, kc, vc, _ = load(c)
            states[c] = s_ref[...]
            w = _ret_weights(lgb, lgb)
            s_ref[...] = dec_b * s_ref[...] + _dtn(kc * w["wkb"], vc)
            return 0

        lax.fori_loop(0, nc, b_states, 0)
        sh_ref[...] = jnp.zeros_like(sh_ref)

        def b_adj(c, carry):
            dlb, ddec = carry
            r, qc, kc, vc, dyc = load(c)
            sc = states[c]
            _, vjp = jax.vjp(_ret_b_part, qc, kc, vc, lgb, sc)
            dq, dkk, dvv, g_b, dsc = vjp((dyc, sh_ref[...]))
            dq_ref[r, :] += dq
            dk_ref[r, :] += dkk
            dv_ref[r, :] += dvv
            ddec = ddec + jnp.sum(sh_ref[...] * sc)
            sh_ref[...] = dsc + dec_b * sh_ref[...]
            return dlb + g_b, ddec

        dlb, ddec_b = lax.fori_loop(0, nc, b_adj, (dlb, z))
        dlf = dlf + ddec_f * dec_f * CHUNK
        dlb = dlb + ddec_b * dec_b * CHUNK
        row = lax.broadcasted_iota(jnp.int32, (SUBLANE, LANE), 0)
        dlg_ref[...] = jnp.where(row == 0, dlf, jnp.where(row == 1, dlb, 0.0))

    head = lambda w: pl.BlockSpec((l, w), lambda h: (0, h))
    return pl.pallas_call(
        body, name=name, grid=(RET_HEADS,), in_specs=in_specs + [head(dv)],
        out_specs=[head(dk), head(dk), head(dv), pl.BlockSpec((None, SUBLANE, LANE), lambda h: (h, 0, 0))],
        out_shape=[jax.ShapeDtypeStruct((l, qkw), F32), jax.ShapeDtypeStruct((l, qkw), F32),
                   jax.ShapeDtypeStruct((l, d), F32), jax.ShapeDtypeStruct((RET_HEADS, SUBLANE, LANE), F32)],
        scratch_shapes=[pltpu.VMEM((nc, dk, dv), F32), pltpu.VMEM((dk, dv), F32), pltpu.VMEM((dk, dv), F32)],
        compiler_params=_params("arbitrary"),
    )(lg, q, k, v, dy)


def _s5_param_fn(a_re, a_im, log_dt, b_re, b_im, rep):
    dt = jnp.exp(log_dt)
    mag = jnp.exp(a_re * dt)
    lam_re, lam_im = mag * jnp.cos(a_im * dt), mag * jnp.sin(a_im * dt)
    n_re, n_im = lam_re - 1.0, lam_im
    den = a_re * a_re + a_im * a_im
    c_re = (n_re * a_re + n_im * a_im) / den
    c_im = (n_im * a_re - n_re * a_im) / den
    hi = lax.Precision.HIGHEST
    c_re = jnp.dot(c_re, rep, precision=hi, preferred_element_type=F32)
    c_im = jnp.dot(c_im, rep, precision=hi, preferred_element_type=F32)
    return lam_re, lam_im, c_re * b_re - c_im * b_im, c_re * b_im + c_im * b_re


def _s5_param_shapes(a_re, b_re):
    r, p = a_re.shape
    return [jax.ShapeDtypeStruct((r, p), F32)] * 2 + [jax.ShapeDtypeStruct(b_re.shape, F32)] * 2


def _s5_prep(a_re, a_im, log_dt, b_re, b_im, rep, *, name):
    def body(*refs):
        outs = _s5_param_fn(*[r[...] for r in refs[:6]])
        for o_ref, o in zip(refs[6:], outs):
            o_ref[...] = o

    return pl.pallas_call(body, name=name, out_shape=_s5_param_shapes(a_re, b_re),
                          compiler_params=_params())(a_re, a_im, log_dt, b_re, b_im, rep)


def _s5_prep_bwd(a_re, a_im, log_dt, b_re, b_im, rep, cts, *, name):
    def body(*refs):
        ins = [r[...] for r in refs[:6]]
        _, vjp = jax.vjp(lambda *p: _s5_param_fn(*p, ins[5]), *ins[:5])
        grads = vjp(tuple(r[...] for r in refs[6:10]))
        for o_ref, o in zip(refs[10:], grads):
            o_ref[...] = o

    shapes = [jax.ShapeDtypeStruct(t.shape, F32) for t in (a_re, a_im, log_dt, b_re, b_im)]
    return pl.pallas_call(body, name=name, out_shape=shapes,
                          compiler_params=_params())(a_re, a_im, log_dt, b_re, b_im, rep, *cts)


def _eye_tiles():
    return jnp.eye(TILE_GROUPS, dtype=F32)


def _b_tiles(bbar, tiles):
    t = bbar.reshape(2, tiles, TILE_GROUPS, SSM_STATE, SSM_GROUP).transpose(0, 1, 2, 4, 3)
    t = t[:, :, :, :, None, :] * _eye_tiles()[None, None, :, None, :, None]
    return t.reshape(2, tiles, TILE_U, TILE_N)


def _b_untile(dbt, tiles):
    t = dbt.reshape(2, tiles, TILE_GROUPS, SSM_GROUP, TILE_GROUPS, SSM_STATE)
    t = (t * _eye_tiles()[None, None, :, None, :, None]).sum(axis=4)
    return t.transpose(0, 1, 2, 4, 3).reshape(2 * tiles * TILE_GROUPS, SSM_STATE * SSM_GROUP)


def _c_tiles(c, tiles):
    t = c.reshape(2, tiles, TILE_GROUPS, SSM_GROUP, SSM_STATE).transpose(0, 1, 2, 4, 3)
    t = t[:, :, :, :, None, :] * _eye_tiles()[None, None, :, None, :, None]
    return t.reshape(2, tiles, TILE_N, TILE_U)


def _c_untile(dct, tiles):
    t = dct.reshape(2, tiles, TILE_GROUPS, SSM_STATE, TILE_GROUPS, SSM_GROUP)
    t = (t * _eye_tiles()[None, None, :, None, :, None]).sum(axis=4)
    return t.transpose(0, 1, 2, 4, 3).reshape(2, tiles * TILE_GROUPS, SSM_GROUP, SSM_STATE)


def _to_segments(a):
    l, w = a.shape
    return a.reshape(N_SEG, l // N_SEG, w).transpose(1, 0, 2).reshape(l, w)


def _from_segments(a):
    l, w = a.shape
    return a.reshape(l // N_SEG, N_SEG, w).transpose(1, 0, 2).reshape(l, w)


def _s5_scan(xr, xi, a_re, a_im, *, length, reverse, shifted=None):
    ls = length // N_SEG
    assert ls * N_SEG == length and ls & (ls - 1) == 0
    ar = [jnp.broadcast_to(a_re[:, c * LANE:(c + 1) * LANE], (N_SEG, LANE)) for c in range(N_LT)]
    ai = [jnp.broadcast_to(a_im[:, c * LANE:(c + 1) * LANE], (N_SEG, LANE)) for c in range(N_LT)]
    zero = jnp.zeros((N_SEG, LANE), F32)
    row = lax.broadcasted_iota(jnp.int32, (N_SEG, LANE), 0)

    def step_of(n):
        return (ls - 1 - n) if reverse else n

    def block(j):
        return pl.ds(j * N_SEG, N_SEG) if isinstance(j, int) else pl.ds(pl.multiple_of(j * N_SEG, N_SEG), N_SEG)

    def local(n, carry):
        rows = block(step_of(n))
        new = []
        for c in range(N_LT):
            cr, ci = carry[2 * c], carry[2 * c + 1]
            nr = ar[c] * cr - ai[c] * ci + xr[c, rows, :]
            ni = ar[c] * ci + ai[c] * cr + xi[c, rows, :]
            xr[c, rows, :] = nr
            xi[c, rows, :] = ni
            new += [nr, ni]
        return tuple(new)

    ends = lax.fori_loop(0, ls, local, (zero,) * (2 * N_LT), unroll=SCAN_UNROLL)

    init = []
    for c in range(N_LT):
        pr, pi = ar[c][0:1, :], ai[c][0:1, :]
        for _ in range(ls.bit_length() - 1):
            pr, pi = pr * pr - pi * pi, 2.0 * pr * pi
        cr = ci = jnp.zeros((1, LANE), F32)
        ir, ii = zero, zero
        for s in (range(N_SEG - 1, -1, -1) if reverse else range(N_SEG)):
            ir = jnp.where(row == s, cr, ir)
            ii = jnp.where(row == s, ci, ii)
            er, ei = ends[2 * c][s:s + 1, :], ends[2 * c + 1][s:s + 1, :]
            cr, ci = pr * cr - pi * ci + er, pr * ci + pi * cr + ei
        init += [ir, ii]

    def fix(n, carry, last=False):
        j = step_of(n)
        rows = block(j)
        new, sums = [], []
        for c in range(N_LT):
            cr, ci = carry[2 * c], carry[2 * c + 1]
            nr = ar[c] * cr - ai[c] * ci
            ni = ar[c] * ci + ai[c] * cr
            fr = xr[c, rows, :] + nr
            fi = xi[c, rows, :] + ni
            xr[c, rows, :] = fr
            xi[c, rows, :] = fi
            new += [nr, ni]
            if shifted is not None:
                yr, yi, shift = shifted
                if not last:
                    srows = block(j + shift)
                    sr, si = yr[c, srows, :], yi[c, srows, :]
                else:
                    edge = block(ls - 1 if shift < 0 else 0)
                    move, gone = (1, 0) if shift < 0 else (N_SEG - 1, N_SEG - 1)
                    sr = jnp.where(row == gone, 0.0, pltpu.roll(yr[c, edge, :], move, 0))
                    si = jnp.where(row == gone, 0.0, pltpu.roll(yi[c, edge, :], move, 0))
                sums += [carry[2 * N_LT + 2 * c] + fr * sr + fi * si,
                         carry[2 * N_LT + 2 * c + 1] + fi * sr - fr * si]
        return tuple(new + sums)

    if shifted is None:
        lax.fori_loop(0, ls, fix, tuple(init), unroll=SCAN_UNROLL)
        return ()
    assert shifted[2] == (-1 if reverse else 1)
    out = lax.fori_loop(0, ls - 1, fix, tuple(init) + (zero,) * (2 * N_LT), unroll=SCAN_UNROLL)
    return fix(ls - 1, out, last=True)[2 * N_LT:]


def _s5_tile_specs(l, d):
    tile = lambda r, c: pl.BlockSpec((None, None, r, c), lambda t, d=d: (d, t, 0, 0))
    return [pl.BlockSpec((l, TILE_U), lambda t: (0, t)), tile(TILE_U, TILE_N), tile(TILE_U, TILE_N),
            tile(1, TILE_N), tile(1, TILE_N), tile(TILE_N, TILE_U), tile(TILE_N, TILE_U)]


def _lanes(c):
    return slice(c * LANE, (c + 1) * LANE)


def _s5_fwd(u, bt_re, bt_im, lam_re, lam_im, ct_re, ct_im, *, d, name, add=None):
    l = u.shape[0]
    tiles = bt_re.shape[1]
    col = pl.BlockSpec((l, TILE_U), lambda t: (0, t))
    has_add = add is not None

    def body(*refs):
        u_ref, bre, bim, lre, lim, cre, cim = refs[:7]
        y_ref, xr, xi = refs[-3:]
        uu = u_ref[...]
        bu_re, bu_im = _dg(uu, bre[...], 1, 0), _dg(uu, bim[...], 1, 0)
        for c in range(N_LT):
            xr[c] = bu_re[:, _lanes(c)]
            xi[c] = bu_im[:, _lanes(c)]
        _s5_scan(xr, xi, lre[...], lim[...], length=l, reverse=(d == 1))
        y = refs[7][...] if has_add else jnp.zeros((l, TILE_U), F32)
        for c in range(N_LT):
            y = y + _dg(xr[c], cre[_lanes(c), :], 1, 0) - _dg(xi[c], cim[_lanes(c), :], 1, 0)
        y_ref[...] = y

    return pl.pallas_call(
        body, name=name, grid=(tiles,), in_specs=_s5_tile_specs(l, d) + [col] * has_add, out_specs=col,
        out_shape=jax.ShapeDtypeStruct((l, tiles * TILE_U), F32),
        scratch_shapes=[pltpu.VMEM((N_LT, l, LANE), F32)] * 2, compiler_params=_params("arbitrary"),
    )(u, bt_re, bt_im, lam_re, lam_im, ct_re, ct_im, *([add] if has_add else []))


def _s5_bwd(u, dy, bt_re, bt_im, lam_re, lam_im, ct_re, ct_im, *, d, name, add=None):
    l = u.shape[0]
    tiles = bt_re.shape[1]
    col = pl.BlockSpec((l, TILE_U), lambda t: (0, t))
    reverse = d == 1
    has_add = add is not None

    def body(*refs):
        u_ref, bre, bim, lre, lim, cre, cim, dy_ref = refs[:8]
        du_ref, dbre, dbim, dcre, dcim, dlre, dlim, xr, xi, gr, gi = refs[-11:]
        uu, dyy = u_ref[...], dy_ref[...]
        bu_re, bu_im = _dg(uu, bre[...], 1, 0), _dg(uu, bim[...], 1, 0)
        for c in range(N_LT):
            xr[c] = bu_re[:, _lanes(c)]
            xi[c] = bu_im[:, _lanes(c)]
        _s5_scan(xr, xi, lre[...], lim[...], length=l, reverse=reverse)
        gy_re, gy_im = _dg(dyy, cre[...], 1, 1), -_dg(dyy, cim[...], 1, 1)
        for c in range(N_LT):
            gr[c] = gy_re[:, _lanes(c)]
            gi[c] = gy_im[:, _lanes(c)]
        sums = _s5_scan(gr, gi, lre[...], -lim[...], length=l, reverse=not reverse,
                        shifted=(xr, xi, 1 if reverse else -1))
        du = refs[8][...] if has_add else jnp.zeros((l, TILE_U), F32)
        for c in range(N_LT):
            dlre[:, _lanes(c)] = jnp.sum(sums[2 * c], axis=0, keepdims=True)
            dlim[:, _lanes(c)] = jnp.sum(sums[2 * c + 1], axis=0, keepdims=True)
            g_re, g_im = gr[c], gi[c]
            du = du + _dg(g_re, bre[:, _lanes(c)], 1, 1) + _dg(g_im, bim[:, _lanes(c)], 1, 1)
            dbre[:, _lanes(c)] = _dg(uu, g_re, 0, 0)
            dbim[:, _lanes(c)] = _dg(uu, g_im, 0, 0)
            dcre[_lanes(c), :] = _dg(xr[c], dyy, 0, 0)
            dcim[_lanes(c), :] = -_dg(xi[c], dyy, 0, 0)
        du_ref[...] = du

    out3 = lambda r, c: pl.BlockSpec((None, r, c), lambda t: (t, 0, 0))
    f = lambda *s: jax.ShapeDtypeStruct(s, F32)
    return pl.pallas_call(
        body, name=name, grid=(tiles,), in_specs=_s5_tile_specs(l, d) + [col] + [col] * has_add,
        out_specs=[col, out3(TILE_U, TILE_N), out3(TILE_U, TILE_N),
                   out3(TILE_N, TILE_U), out3(TILE_N, TILE_U), out3(1, TILE_N), out3(1, TILE_N)],
        out_shape=[f(l, tiles * TILE_U), f(tiles, TILE_U, TILE_N), f(tiles, TILE_U, TILE_N),
                   f(tiles, TILE_N, TILE_U), f(tiles, TILE_N, TILE_U), f(tiles, 1, TILE_N), f(tiles, 1, TILE_N)],
        scratch_shapes=[pltpu.VMEM((N_LT, l, LANE), F32)] * 4, compiler_params=_params("arbitrary"),
    )(u, bt_re, bt_im, lam_re, lam_im, ct_re, ct_im, dy, *([add] if has_add else []))


def _s5_post(y, proj, u_cb, dskip, *, name, tl=256):
    d = y.shape[1]

    def body(i, o):
        ys = jax.nn.gelu(i[0][...] + i[2][...] * i[1][...])
        o[0][...] = ys
        o[1][...] = ys.astype(BF16)

    return _rowwise(body, rows=y.shape[0], tl=tl, ins=[_row(y, tl), _row(proj, tl, d, u_cb), _par(dskip)],
                    outs=[("row", d, F32), ("row", d, BF16)], name=name)


def _s5_post_bwd(y, proj, u_cb, dskip, dys, *, name, tl=256):
    d = y.shape[1]

    def body(i, o):
        u_ = i[1][...]
        _, vjp = jax.vjp(jax.nn.gelu, i[0][...] + i[2][...] * u_)
        (dpre,) = vjp(i[3][...])
        o[0][...] = dpre
        o[1][...] += jnp.sum(dpre * u_, axis=0, keepdims=True)

    return _rowwise(body, rows=y.shape[0], tl=tl,
                    ins=[_row(y, tl), _row(proj, tl, d, u_cb), _par(dskip), _row(dys, tl)],
                    outs=[("row", d, F32), ("acc", d)], name=name)


def _du_combine(dpre, dskip, du_s5, *, name, tl=256):
    d = dpre.shape[1]

    def body(i, o):
        o[0][...] = (i[0][...] * i[1][...] + i[2][...]).astype(BF16)

    return _rowwise(body, rows=dpre.shape[0], tl=tl, ins=[_row(dpre, tl), _par(dskip), _row(du_s5, tl)],
                    outs=[("row", d, BF16)], name=name)[0]


def _merge_fn(y, g, gate_r, gate_s, ys, glu, b):
    ret = jax.nn.silu(g) * (y * lax.rsqrt(jnp.mean(y * y, axis=-1, keepdims=True) + EPS))
    ssm = ys * jax.nn.sigmoid(glu + b)
    return jax.nn.sigmoid(gate_r) * ret + jax.nn.sigmoid(gate_s) * ssm


def _merge_ins(y_raw, proj, ys, glu, b_glu, cb0, tl):
    d = y_raw.shape[1]
    return [_row(y_raw, tl), _row(proj, tl, d, cb0 + 1), _row(proj, tl, d, cb0 + 3), _row(proj, tl, d, cb0 + 4),
            _row(ys, tl), _row(glu, tl), _par(b_glu)]


def _merge(y_raw, proj, ys, glu, b_glu, *, cb0, name, tl=128):
    d = y_raw.shape[1]
    dv = d // RET_HEADS

    def body(i, o):
        for h in range(RET_HEADS):
            cs = slice(h * dv, (h + 1) * dv)
            o[0][:, cs] = _merge_fn(*[r[:, cs] for r in i]).astype(BF16)

    return _rowwise(body, rows=y_raw.shape[0], tl=tl, ins=_merge_ins(y_raw, proj, ys, glu, b_glu, cb0, tl),
                    outs=[("row", d, BF16)], name=name)[0]


def _merge_bwd(y_raw, proj, ys, glu, b_glu, dmerged, *, cb0, name, tl=128):
    d = y_raw.shape[1]
    dv = d // RET_HEADS

    def body(i, o):
        for h in range(RET_HEADS):
            cs = slice(h * dv, (h + 1) * dv)
            _, vjp = jax.vjp(_merge_fn, *[r[:, cs] for r in i[:7]])
            dy, dg, dgr, dgs, dys, dglu, db = vjp(i[7][:, cs])
            o[0][:, cs] = dy.astype(BF16)
            o[1][:, cs] = dg.astype(BF16)
            o[2][:, cs] = dgr.astype(BF16)
            o[3][:, cs] = dgs.astype(BF16)
            o[4][:, cs] = dglu.astype(BF16)
            o[5][:, cs] = dys
            o[6][:, cs] += db

    return _rowwise(body, rows=y_raw.shape[0], tl=tl,
                    ins=_merge_ins(y_raw, proj, ys, glu, b_glu, cb0, tl) + [_row(dmerged, tl)],
                    outs=[("row", d, BF16)] * 5 + [("row", d, F32), ("acc", d)], name=name)


def _ffn_act_fn(gate, up):
    return jax.nn.silu(gate) * up


def _ffn_act(gate, up, *, name, tl=128):
    def body(i, o):
        o[0][...] = _ffn_act_fn(i[0][...], i[1][...]).astype(BF16)

    return _rowwise(body, rows=gate.shape[0], tl=tl, ins=[_row(gate, tl), _row(up, tl)],
                    outs=[("row", gate.shape[1], BF16)], name=name)[0]


def _ffn_act_bwd(gate, up, dact, *, name, tl=128):
    def body(i, o):
        _, vjp = jax.vjp(_ffn_act_fn, i[0][...], i[1][...])
        dgate, dup = vjp(i[2][...])
        o[0][...] = dgate.astype(BF16)
        o[1][...] = dup.astype(BF16)

    w = gate.shape[1]
    return _rowwise(body, rows=gate.shape[0], tl=tl, ins=[_row(gate, tl), _row(up, tl), _row(dact, tl)],
                    outs=[("row", w, BF16), ("row", w, BF16)], name=name)


def _adamw(w, g, m, v):
    m = ADAM_B1 * m + (1.0 - ADAM_B1) * g
    v = ADAM_B2 * v + (1.0 - ADAM_B2) * (g * g)
    m_hat = m / (1.0 - ADAM_B1 ** ADAM_STEP)
    v_hat = v / (1.0 - ADAM_B2 ** ADAM_STEP)
    return -ADAM_LR * (m_hat / (jnp.sqrt(v_hat) + ADAM_EPS) + ADAM_WD * w), m, v


def _adam_flat(g, w, m, v, *, name, tr=FLAT_ROWS):
    def body(i, o):
        for o_ref, val in zip(o, _adamw(i[1][...], i[0][...], i[2][...], i[3][...])):
            o_ref[...] = val

    return _rowwise(body, rows=g.shape[0], tl=tr, ins=[_row(a, tr) for a in (g, w, m, v)],
                    outs=[("row", LANE, F32)] * 3, name=name)


def _adam_shard(recv, own, me, w, m, v, *, kind, layer, name, tr, tie=None, others=None):
    depth, r, c = w.shape
    assert r % tr == 0 and recv.shape[2] == c
    blk = pl.BlockSpec((None, tr, c), lambda i, me_ref: (layer, i, 0))
    if kind == "col":
        own_spec = pl.BlockSpec((tr, c), lambda i, me_ref: (i, me_ref[0]))
    else:
        per = own.shape[0] // N_DEV // tr
        assert per * tr * N_DEV == own.shape[0]
        own_spec = pl.BlockSpec((tr, c), lambda i, me_ref: (me_ref[0] * per + i, 0))
    extra = ([] if tie is None else [tie]) + list(others or [])

    def body(me_ref, recv_ref, own_ref, w_ref, m_ref, v_ref, *refs):
        g_ref, d_ref, nm_ref, nv_ref = refs[len(extra):]
        g = own_ref[...].astype(F32)
        for k in range(N_DEV - 1):
            g = g + recv_ref[k].astype(F32)
        g_ref[...] = g
        d_ref[...], nm_ref[...], nv_ref[...] = _adamw(w_ref[...], g, m_ref[...], v_ref[...])

    first = 6 + (tie is not None)
    return pl.pallas_call(
        body, name=name, out_shape=[jax.ShapeDtypeStruct(w.shape, F32)] * 4,
        grid_spec=pltpu.PrefetchScalarGridSpec(
            num_scalar_prefetch=1, grid=(r // tr,),
            in_specs=[pl.BlockSpec((N_DEV - 1, tr, c), lambda i, me_ref: (0, i, 0)), own_spec, blk, blk, blk]
            + [pl.BlockSpec(memory_space=pl.ANY)] * len(extra),
            out_specs=[blk] * 4),
        input_output_aliases={first + j: j for j in range(4)} if others else {},
        compiler_params=_params("parallel"),
    )(me, recv, own, w, m, v, *extra)


def _position():
    x, y, c = lax.axis_index("x"), lax.axis_index("y"), lax.axis_index("c")
    return x, y, c, 4 * x + 2 * y + c


def _coords(p):
    return p // 4, (p // 2) % 2, p % 2


def _block_of(ref, kind, p, nb):
    if kind == "col":
        return ref.at[:, pl.ds(pl.multiple_of(p * nb, LANE), nb)]
    return ref.at[pl.ds(pl.multiple_of(p * nb, SUBLANE), nb), :]


def _all_gather(shards, kinds, *, name):
    n = len(shards)
    out_shape = []
    for s, kind in zip(shards, kinds):
        r, c = s.shape
        out_shape.append(jax.ShapeDtypeStruct((r, c * N_DEV) if kind == "col" else (r * N_DEV, c), s.dtype))

    def body(*refs):
        shard_refs, full_refs = refs[:n], refs[n:2 * n]
        send_sems, recv_sems, local_sems = refs[2 * n:]
        x, y, c, me = _position()
        sibling = (x, y, 1 - c)
        chips = [(1 - x, y), (x, 1 - y), (1 - x, 1 - y)]

        def block(t, dev):
            nb = shards[t].shape[1 if kinds[t] == "col" else 0]
            return _block_of(full_refs[t], kinds[t], 4 * dev[0] + 2 * dev[1] + dev[2], nb)

        def copy(t, k, dev, to, src=None):
            return pltpu.make_async_remote_copy(
                src_ref=block(t, dev) if src is None else src, dst_ref=block(t, dev),
                send_sem=send_sems.at[t, k], recv_sem=recv_sems.at[t, k], device_id=to, device_id_type=MESH)

        mine, first, passed = [], [], []
        for t in range(n):
            mine.append(pltpu.make_async_copy(shard_refs[t], block(t, (x, y, c)), local_sems.at[t]))
            mine[-1].start()
            first.append(copy(t, 0, (x, y, c), sibling, src=shard_refs[t]))
            first += [copy(t, 1 + j, (x, y, c), (*chip, c), src=shard_refs[t]) for j, chip in enumerate(chips)]
        for cp in first:
            cp.start()
        for j, chip in enumerate(chips):
            for t in range(n):
                copy(t, 1 + j, (*chip, c), (x, y, c)).wait_recv()
                passed.append(copy(t, 4 + j, (*chip, c), sibling))
                passed[-1].start()
        for t in range(n):
            copy(t, 0, sibling, (x, y, c)).wait_recv()
            for j, chip in enumerate(chips):
                copy(t, 4 + j, (*chip, 1 - c), (x, y, c)).wait_recv()
        for cp in first + passed:
            cp.wait_send()
        for cp in mine:
            cp.wait()

    any_spec = pl.BlockSpec(memory_space=pl.ANY)
    return pl.pallas_call(
        body, name=name, in_specs=[any_spec] * n, out_specs=[any_spec] * n, out_shape=out_shape,
        scratch_shapes=[pltpu.SemaphoreType.DMA((n, N_DEV - 1)), pltpu.SemaphoreType.DMA((n, N_DEV - 1)),
                        pltpu.SemaphoreType.DMA((n,))],
        compiler_params=pltpu.CompilerParams(has_side_effects=True),
    )(*shards)


class _GatherRoute:
    def __init__(self, shards, kinds):
        self.kinds = kinds
        self.nb = [s.shape[1 if k == "col" else 0] for s, k in zip(shards, kinds)]

    def lands(self, shards, me):
        out = []
        for t, (s, k) in enumerate(zip(shards, self.kinds)):
            full = lax.empty((s.shape[0], s.shape[1] * N_DEV) if k == "col" else (s.shape[0] * N_DEV, s.shape[1]), s.dtype)
            out.append(_place_own(s, full, me, k, name=f"own_block_{t}"))
        return out

    def sent(self, t, src_refs, me, k):
        return src_refs[t]

    def lands_at(self, t, land_refs, me, k):
        return _block_of(land_refs[t], self.kinds[t], (me + N_DEV - k) % N_DEV, self.nb[t])


class _ScatterRoute:
    def __init__(self, grads, kinds):
        self.kinds = kinds
        self.nb = [g.shape[1 if k == "col" else 0] // N_DEV for g, k in zip(grads, kinds)]

    def lands(self, grads):
        return [lax.empty((N_DEV - 1, g.shape[0], nb) if k == "col" else (N_DEV - 1, nb, g.shape[1]), g.dtype)
                for g, k, nb in zip(grads, self.kinds, self.nb)]

    def sent(self, t, src_refs, me, k):
        return _block_of(src_refs[t], self.kinds[t], (me + k) % N_DEV, self.nb[t])

    def lands_at(self, t, land_refs, me, k):
        return land_refs[t].at[k - 1]


def _place_own(shard, full, me, kind, *, name):
    r, c = shard.shape
    tr = _row_tile(r)
    if kind == "col":
        dst = pl.BlockSpec((tr, c), lambda i, me_ref: (i, me_ref[0]))
    else:
        dst = pl.BlockSpec((tr, c), lambda i, me_ref: (me_ref[0] * (r // tr) + i, 0))

    def body(me_ref, shard_ref, full_ref, out_ref):
        out_ref[...] = shard_ref[...]

    return pl.pallas_call(
        body, name=name, out_shape=jax.ShapeDtypeStruct(full.shape, full.dtype),
        grid_spec=pltpu.PrefetchScalarGridSpec(
            num_scalar_prefetch=1, grid=(r // tr,),
            in_specs=[pl.BlockSpec((tr, c), lambda i, me_ref: (i, 0)), pl.BlockSpec(memory_space=pl.ANY)],
            out_specs=dst),
        input_output_aliases={2: 0}, compiler_params=_params("parallel"),
    )(me, shard, full)


_HBM = pl.BlockSpec(memory_space=pltpu.HBM)
_SEM = pl.BlockSpec(memory_space=pltpu.SEMAPHORE)
_FLOWING = pltpu.SideEffectType.DATAFLOW_SIDE_EFFECTING


def _exchange_start(srcs, lands, route, after, *, name):
    n = len(srcs)

    def body(*refs):
        src_refs, land_refs = refs[:n], refs[n:2 * n]
        send_sems, recv_sems = refs[2 * n + 1:2 * n + 3]
        token = refs[-1]
        _, _, _, me = _position()
        for t in range(n):
            for k in range(1, N_DEV):
                p = (me + k) % N_DEV
                pltpu.make_async_remote_copy(
                    src_ref=route.sent(t, src_refs, me, k), dst_ref=route.lands_at(t, land_refs, p, k),
                    send_sem=send_sems.at[t * N_DEV + k], recv_sem=recv_sems.at[t * N_DEV + k], device_id=_coords(p),
                    device_id_type=MESH).start()
        token[...] = jnp.zeros_like(token)

    hbm = lambda a: pltpu.HBM(a.shape, a.dtype)
    sems = pltpu.SemaphoreType.DMA((n * N_DEV,))
    out = pl.pallas_call(
        body, name=name,
        out_shape=(sems, sems, *[hbm(a) for a in srcs], *[hbm(a) for a in lands],
                   jax.ShapeDtypeStruct((SUBLANE, LANE), F32)),
        in_specs=[_HBM] * (2 * n) + [pl.BlockSpec(memory_space=pl.ANY)],
        out_specs=(_SEM, _SEM, *[_HBM] * (2 * n), pl.BlockSpec(memory_space=pltpu.VMEM)),
        input_output_aliases={i: 2 + i for i in range(2 * n)},
        compiler_params=pltpu.CompilerParams(has_side_effects=_FLOWING),
    )(*[pltpu.with_memory_space_constraint(a, pltpu.HBM) for a in (*srcs, *lands)], after)
    return out[0], out[1], out[2:2 + n], out[2 + n:2 + 2 * n], out[-1]


def _exchange_wait(started, route, after, *, name):
    send_sems, recv_sems, srcs, lands, _ = started
    n = len(srcs)

    def body(*refs):
        src_refs, land_refs = refs[:n], refs[n:2 * n]
        send_ref, recv_ref = refs[2 * n:2 * n + 2]
        _, _, _, me = _position()
        for t in range(n):
            for k in range(1, N_DEV):
                cp = pltpu.make_async_remote_copy(
                    src_ref=route.sent(t, src_refs, me, k), dst_ref=route.lands_at(t, land_refs, me, k),
                    send_sem=send_ref.at[t * N_DEV + k], recv_sem=recv_ref.at[t * N_DEV + k],
                    device_id=_coords((me + N_DEV - k) % N_DEV), device_id_type=MESH)
                cp.wait_send()
                cp.wait_recv()

    hbm = lambda a: pltpu.HBM(a.shape, a.dtype)
    out = pl.pallas_call(
        body, name=name, out_shape=(*[hbm(a) for a in srcs], *[hbm(a) for a in lands]),
        in_specs=[_HBM] * (2 * n) + [_SEM, _SEM, pl.BlockSpec(memory_space=pl.ANY)],
        out_specs=tuple([_HBM] * (2 * n)), input_output_aliases={i: i for i in range(2 * n)},
        compiler_params=pltpu.CompilerParams(has_side_effects=_FLOWING),
    )(*srcs, *lands, send_sems, recv_sems, after)
    return list(out[:n]), list(out[n:])


def _all_reduce(part, *, name):
    _, r, _ = part.shape

    def body(part_ref, tot_ref, recv_ref, send1, recv1, send2, recv2):
        _, _, _, me = _position()

        def scatter(k, to_me=False):
            p = (me + N_DEV - k) % N_DEV if to_me else (me + k) % N_DEV
            return pltpu.make_async_remote_copy(
                src_ref=part_ref.at[me if to_me else p], dst_ref=recv_ref.at[p if to_me else me],
                send_sem=send1.at[k], recv_sem=recv1.at[k], device_id=_coords(p), device_id_type=MESH)

        def gather(k, to_me=False):
            p = (me + N_DEV - k) % N_DEV if to_me else (me + k) % N_DEV
            return pltpu.make_async_remote_copy(
                src_ref=tot_ref.at[me], dst_ref=tot_ref.at[p if to_me else me],
                send_sem=send2.at[k], recv_sem=recv2.at[k], device_id=_coords(p), device_id_type=MESH)

        for k in range(1, N_DEV):
            scatter(k).start()
        recv_ref[me] = part_ref[me]
        for k in range(1, N_DEV):
            scatter(k, to_me=True).wait_recv()
        total = recv_ref[0]
        for q in range(1, N_DEV):
            total = total + recv_ref[q]
        tot_ref[me] = total
        for k in range(1, N_DEV):
            gather(k).start()
        for k in range(1, N_DEV):
            gather(k, to_me=True).wait_recv()
        for k in range(1, N_DEV):
            scatter(k).wait_send()
            gather(k).wait_send()

    vmem = pl.BlockSpec(memory_space=pltpu.VMEM)
    return pl.pallas_call(
        body, name=name, in_specs=[vmem], out_specs=vmem, out_shape=jax.ShapeDtypeStruct(part.shape, F32),
        scratch_shapes=[pltpu.VMEM(part.shape, F32)] + [pltpu.SemaphoreType.DMA((N_DEV,))] * 4,
        compiler_params=pltpu.CompilerParams(has_side_effects=True, vmem_limit_bytes=VMEM_LIMIT),
    )(part)


def _round_up(n, m):
    return (n + m - 1) // m * m


def _row_tile(rows):
    return next(t for t in (256, 128, 64, 32, 16) if rows % t == 0)


def _local_step(x, target, small, depth, weights_of, grads_done, *, qkw):
    l, d = x.shape
    groups = d // SSM_GROUP
    tiles = groups // TILE_GROUPS
    half = qkw // RET_HEADS // 2
    cb0 = 2 * qkw // d
    inv = 1.0 / (ROPE_BASE ** (jnp.arange(half, dtype=F32) / half))
    ang = jnp.arange(l, dtype=F32)[:, None] * inv[None, :]
    cos, sin = jnp.cos(ang), jnp.sin(ang)
    rep = jnp.repeat(jnp.eye(SSM_STATE, dtype=F32), SSM_GROUP, axis=1)
    row2 = lambda a: a.reshape(1, -1)

    saved = []
    for i in range(depth):
        full_i, x = weights_of(i, x)
        w_in, w_glu, w_out, w_gate, w_up, w_down = full_i
        n = f"l{i}_"
        g_mix, g_ffn = row2(small["ln_mix_g"][i]), row2(small["ln_ffn_g"][i])
        dskip, b_glu = row2(small["ssm_d"][i]), row2(small["b_glu"][i])
        lg = small["ret_log_gamma"][i]
        h = _norm_fwd(x, g_mix, name=n + "norm_mix")
        proj = _matmul(h, w_in, mode="nn", out_dtype=F32, name=n + "proj")
        q_rot, k_rot, v_bf = _ret_prep(proj, cos, sin, qkw=qkw, d=d, name=n + "ret_prep")
        y_raw = _ret_fwd(lg, q_rot, k_rot, v_bf, name=n + "ret_fwd")
        par = [small["ssm_a_re"][i].reshape(2 * groups, SSM_STATE), small["ssm_a_im"][i].reshape(2 * groups, SSM_STATE),
               small["ssm_log_dt"][i].reshape(2 * groups, 1),
               small["ssm_b_re"][i].reshape(2 * groups, SSM_STATE * SSM_GROUP),
               small["ssm_b_im"][i].reshape(2 * groups, SSM_STATE * SSM_GROUP), rep]
        lam_re, lam_im, bbar_re, bbar_im = _s5_prep(*par, name=n + "s5_prep")
        s5 = [_b_tiles(bbar_re, tiles).astype(BF16), _b_tiles(bbar_im, tiles).astype(BF16),
              lam_re.reshape(2, tiles, 1, TILE_N), lam_im.reshape(2, tiles, 1, TILE_N),
              _c_tiles(small["ssm_c_re"][i], tiles).astype(BF16), _c_tiles(small["ssm_c_im"][i], tiles).astype(BF16)]
        u_seg = _to_segments(proj[:, (cb0 + 2) * d:(cb0 + 3) * d])
        y_seg = _s5_fwd(u_seg, *s5, d=0, name=n + "s5_fwd_f")
        y_seg = _s5_fwd(u_seg, *s5, d=1, add=y_seg, name=n + "s5_fwd_b")
        y_s5 = _from_segments(y_seg)
        ys, ys_bf = _s5_post(y_s5, proj, cb0 + 2, dskip, name=n + "s5_post")
        glu = _matmul(ys_bf, w_glu, mode="nn", out_dtype=F32, name=n + "glu")
        merged = _merge(y_raw, proj, ys, glu, b_glu, cb0=cb0, name=n + "merge")
        x1 = _matmul(merged, w_out, mode="nn", out_dtype=F32, res=x, name=n + "out")
        h2 = _norm_fwd(x1, g_ffn, name=n + "norm_ffn")
        gate = _matmul(h2, w_gate, mode="nn", out_dtype=F32, name=n + "gate")
        up = _matmul(h2, w_up, mode="nn", out_dtype=F32, name=n + "up")
        act = _ffn_act(gate, up, name=n + "act")
        x2 = _matmul(act, w_down, mode="nn", out_dtype=F32, res=x1, name=n + "down")
        saved.append(dict(full=full_i, x=x, h=h, proj=proj, q_rot=q_rot, k_rot=k_rot, v_bf=v_bf, y_raw=y_raw, par=par, s5=s5,
                          u_seg=u_seg, y_s5=y_s5, ys=ys, ys_bf=ys_bf, glu=glu, merged=merged, x1=x1, h2=h2, gate=gate,
                          up=up, act=act))
        x = x2

    dx, dg_final, loss = _final(x, row2(small["ln_final_g"]), target, name="final")

    sg = {k: [None] * depth for k in ("ln_mix_g", "ret_log_gamma", "ssm_a_re", "ssm_a_im", "ssm_log_dt", "ssm_b_re",
                                      "ssm_b_im", "ssm_c_re", "ssm_c_im", "ssm_d", "b_glu", "ln_ffn_g")}
    for i in reversed(range(depth)):
        s = saved[i]
        big = [None] * len(BIG)
        w_in, w_glu, w_out, w_gate, w_up, w_down = s["full"]
        n = f"l{i}_b_"
        g_mix, g_ffn = row2(small["ln_mix_g"][i]), row2(small["ln_ffn_g"][i])
        dskip, b_glu = row2(small["ssm_d"][i]), row2(small["b_glu"][i])
        lg = small["ret_log_gamma"][i]
        dact = _matmul(dx, w_down, mode="nt", out_dtype=F32, name=n + "dact")
        big[5] = _matmul(s["act"], dx, mode="tn", out_dtype=BF16, name=n + "dw_down")
        dgate, dup = _ffn_act_bwd(s["gate"], s["up"], dact, name=n + "act")
        dh2 = _matmul(dgate, w_gate, mode="nt", out_dtype=F32, name=n + "dh2_gate")
        dh2 = _matmul(dup, w_up, mode="nt", out_dtype=F32, res=dh2, name=n + "dh2_up")
        big[3] = _matmul(s["h2"], dgate, mode="tn", out_dtype=BF16, name=n + "dw_gate")
        big[4] = _matmul(s["h2"], dup, mode="tn", out_dtype=BF16, name=n + "dw_up")
        dh2 = grads_done(i, FFN_GROUP, [big[t] for t in FFN_GROUP], dh2)
        dx1, dgf = _norm_bwd(s["x1"], g_ffn, dh2, dx, name=n + "norm_ffn")
        sg["ln_ffn_g"][i] = dgf[0]
        dmerged = _matmul(dx1, w_out, mode="nt", out_dtype=F32, name=n + "dmerged")
        big[2] = _matmul(s["merged"], dx1, mode="tn", out_dtype=BF16, name=n + "dw_out")
        dy_raw, dg, dgate_r, dgate_s, dglu, dys_a, db_glu = _merge_bwd(
            s["y_raw"], s["proj"], s["ys"], s["glu"], b_glu, dmerged, cb0=cb0, name=n + "merge")
        sg["b_glu"][i] = db_glu[0]
        dys = _matmul(dglu, w_glu, mode="nt", out_dtype=F32, res=dys_a, name=n + "dys")
        big[1] = _matmul(s["ys_bf"], dglu, mode="tn", out_dtype=BF16, name=n + "dw_glu")
        dpre, dd = _s5_post_bwd(s["y_s5"], s["proj"], cb0 + 2, dskip, dys, name=n + "s5_post")
        sg["ssm_d"][i] = dd[0]
        dpre_seg = _to_segments(dpre)
        r_f = _s5_bwd(s["u_seg"], dpre_seg, *s["s5"], d=0, name=n + "s5_bwd_f")
        r_b = _s5_bwd(s["u_seg"], dpre_seg, *s["s5"], d=1, add=r_f[0], name=n + "s5_bwd_b")
        du = _du_combine(dpre, dskip, _from_segments(r_b[0]), name=n + "du")
        both = lambda k: jnp.stack([r_f[k], r_b[k]])
        cts = [both(5).reshape(2 * groups, SSM_STATE), both(6).reshape(2 * groups, SSM_STATE),
               _b_untile(both(1), tiles), _b_untile(both(2), tiles)]
        da_re, da_im, dldt, db_re, db_im = _s5_prep_bwd(*s["par"], cts, name=n + "s5_prep")
        sg["ssm_a_re"][i] = da_re.reshape(2, groups, SSM_STATE)
        sg["ssm_a_im"][i] = da_im.reshape(2, groups, SSM_STATE)
        sg["ssm_log_dt"][i] = dldt.reshape(2, groups)
        sg["ssm_b_re"][i] = db_re.reshape(2, groups, SSM_STATE, SSM_GROUP)
        sg["ssm_b_im"][i] = db_im.reshape(2, groups, SSM_STATE, SSM_GROUP)
        sg["ssm_c_re"][i] = _c_untile(both(3), tiles)
        sg["ssm_c_im"][i] = _c_untile(both(4), tiles)
        dq_rot, dk_rot, dv, dlg = _ret_bwd(lg, s["q_rot"], s["k_rot"], s["v_bf"], dy_raw, name=n + "ret_bwd")
        sg["ret_log_gamma"][i] = dlg[:, :2, 0].T
        dq, dk = _ret_prep_bwd(dq_rot, dk_rot, cos, sin, name=n + "ret_prep")
        dproj = jnp.concatenate([dq, dk, dv.astype(BF16), dg, du, dgate_r, dgate_s], axis=1)
        dh = _matmul(dproj, w_in, mode="nt", out_dtype=F32, name=n + "dh")
        big[0] = _matmul(s["h"], dproj, mode="tn", out_dtype=BF16, name=n + "dw_in")
        dx, dgm = _norm_bwd(s["x"], g_mix, dh, dx1, name=n + "norm_mix")
        sg["ln_mix_g"][i] = dgm[0]
        dx = grads_done(i, MIX_GROUP, [big[t] for t in MIX_GROUP], dx)

    small_grads = {k: jnp.stack(v) for k, v in sg.items()}
    small_grads["ln_final_g"] = dg_final[0]
    return loss, dx, small_grads


BIG = ("w_in", "w_glu", "w_out", "w_ffn_gate", "w_ffn_up", "w_ffn_down")
BIG_KINDS = ("col", "row", "row", "col", "col", "row")
MIX_GROUP = (0, 1, 2)
FFN_GROUP = (3, 4, 5)
SMALL = ("ln_mix_g", "ret_log_gamma", "ssm_a_re", "ssm_a_im", "ssm_log_dt", "ssm_b_re", "ssm_b_im", "ssm_c_re",
         "ssm_c_im", "ssm_d", "b_glu", "ln_ffn_g", "ln_final_g")
WEIGHTS = ("ln_mix_g", "w_in", "ret_log_gamma", "ssm_a_re", "ssm_a_im", "ssm_log_dt", "ssm_b_re", "ssm_b_im",
           "ssm_c_re", "ssm_c_im", "ssm_d", "w_glu", "b_glu", "w_out", "ln_ffn_g", "w_ffn_gate", "w_ffn_up",
           "w_ffn_down", "ln_final_g")


def _pad_to(a, axis, size):
    pad = [(0, 0)] * a.ndim
    pad[axis] = (0, size - a.shape[axis])
    return jnp.pad(a, pad)


def _flatten_small(tree, extra):
    def as_rows(a):
        a = a.reshape(-1).astype(F32)
        return _pad_to(a, 0, _round_up(a.shape[0], SUBLANE * LANE)).reshape(-1, LANE)

    flat = jnp.concatenate([as_rows(tree[k]) for k in SMALL] + [as_rows(extra)])
    return _pad_to(flat, 0, _round_up(flat.shape[0], FLAT_ROWS))


def _unflatten_small(flat, like):
    out, at = {}, 0
    for k in SMALL:
        n = like[k].size
        rows = _round_up(n, SUBLANE * LANE) // LANE
        out[k] = flat[at:at + rows].reshape(-1)[:n].reshape(like[k].shape)
        at += rows
    return out, flat[at, 0]


def kernel(x, ln_mix_g, w_in, ret_log_gamma, ssm_a_re, ssm_a_im, ssm_log_dt, ssm_b_re, ssm_b_im, ssm_c_re, ssm_c_im, ssm_d, w_glu, b_glu, w_out, ln_ffn_g, w_ffn_gate, w_ffn_up, w_ffn_down, ln_final_g, loss_target, m_ln_mix_g, m_w_in, m_ret_log_gamma, m_ssm_a_re, m_ssm_a_im, m_ssm_log_dt, m_ssm_b_re, m_ssm_b_im, m_ssm_c_re, m_ssm_c_im, m_ssm_d, m_w_glu, m_b_glu, m_w_out, m_ln_ffn_g, m_w_ffn_gate, m_w_ffn_up, m_w_ffn_down, m_ln_final_g, v_ln_mix_g, v_w_in, v_ret_log_gamma, v_ssm_a_re, v_ssm_a_im, v_ssm_log_dt, v_ssm_b_re, v_ssm_b_im, v_ssm_c_re, v_ssm_c_im, v_ssm_d, v_w_glu, v_b_glu, v_w_out, v_ln_ffn_g, v_w_ffn_gate, v_w_ffn_up, v_w_ffn_down, v_ln_final_g):
    w = dict(ln_mix_g=ln_mix_g, w_in=w_in, ret_log_gamma=ret_log_gamma, ssm_a_re=ssm_a_re, ssm_a_im=ssm_a_im, ssm_log_dt=ssm_log_dt, ssm_b_re=ssm_b_re, ssm_b_im=ssm_b_im, ssm_c_re=ssm_c_re, ssm_c_im=ssm_c_im, ssm_d=ssm_d, w_glu=w_glu, b_glu=b_glu, w_out=w_out, ln_ffn_g=ln_ffn_g, w_ffn_gate=w_ffn_gate, w_ffn_up=w_ffn_up, w_ffn_down=w_ffn_down, ln_final_g=ln_final_g)
    m = dict(ln_mix_g=m_ln_mix_g, w_in=m_w_in, ret_log_gamma=m_ret_log_gamma, ssm_a_re=m_ssm_a_re, ssm_a_im=m_ssm_a_im, ssm_log_dt=m_ssm_log_dt, ssm_b_re=m_ssm_b_re, ssm_b_im=m_ssm_b_im, ssm_c_re=m_ssm_c_re, ssm_c_im=m_ssm_c_im, ssm_d=m_ssm_d, w_glu=m_w_glu, b_glu=m_b_glu, w_out=m_w_out, ln_ffn_g=m_ln_ffn_g, w_ffn_gate=m_w_ffn_gate, w_ffn_up=m_w_ffn_up, w_ffn_down=m_w_ffn_down, ln_final_g=m_ln_final_g)
    v = dict(ln_mix_g=v_ln_mix_g, w_in=v_w_in, ret_log_gamma=v_ret_log_gamma, ssm_a_re=v_ssm_a_re, ssm_a_im=v_ssm_a_im, ssm_log_dt=v_ssm_log_dt, ssm_b_re=v_ssm_b_re, ssm_b_im=v_ssm_b_im, ssm_c_re=v_ssm_c_re, ssm_c_im=v_ssm_c_im, ssm_d=v_ssm_d, w_glu=v_w_glu, b_glu=v_b_glu, w_out=v_w_out, ln_ffn_g=v_ln_ffn_g, w_ffn_gate=v_w_ffn_gate, w_ffn_up=v_w_ffn_up, w_ffn_down=v_w_ffn_down, ln_final_g=v_ln_final_g)
    depth, d, nb_in = w_in.shape
    qkw = (nb_in * N_DEV - 5 * d) // 2
    nb_ffn = w_ffn_gate.shape[2]
    nb_pad = _round_up(nb_ffn, LANE)
    pad_axis = {"w_ffn_gate": 2, "w_ffn_up": 2, "w_ffn_down": 1}

    assert depth == 2
    padded = {k: w[k] if k not in pad_axis else _pad_to(w[k], pad_axis[k], nb_pad) for k in BIG}
    shards = [[padded[k][i].astype(BF16) for k in BIG] for i in range(depth)]
    full0 = _all_gather(shards[0], BIG_KINDS, name="gather_l0")
    me = (4 * lax.axis_index("x") + 2 * lax.axis_index("y") + lax.axis_index("c")).astype(jnp.int32).reshape(1)
    gather1 = _GatherRoute(shards[1], BIG_KINDS)
    gathering = _exchange_start(shards[1], gather1.lands(shards[1], me), gather1, full0[0], name="gather_l1_start")

    def weights_of(i, act):
        if i == 0:
            return full0, act + gathering[4][0, 0]
        return _exchange_wait(gathering, gather1, act, name="gather_l1_wait")[1], act

    exchanges, held = [], []

    def grads_done(i, group, grads, act):
        route = _ScatterRoute(grads, [BIG_KINDS[t] for t in group])
        if i == 0 and group == MIX_GROUP:
            held.append((i, group, grads, route))
            return act
        started = _exchange_start(grads, route.lands(grads), route, act, name=f"scatter_l{i}_{group[0]}_start")
        exchanges.append((i, group, started, route))
        return act + started[4][0, 0]

    small = {k: w[k] for k in SMALL}
    loss, dx, small_grads = _local_step(x[0], loss_target[0], small, depth, weights_of, grads_done, qkw=qkw)
    arrived, tie = {}, dx
    for i, group, started, route in exchanges:
        own, recv = _exchange_wait(started, route, tie, name=f"scatter_l{i}_{group[0]}_wait")
        arrived.update({(i, t): (own[j], recv[j]) for j, t in enumerate(group)})
        tie = recv[0]

    part = _flatten_small(small_grads, loss[0, :1])
    rows = part.shape[0]
    total = _all_reduce(part.reshape(N_DEV, rows // N_DEV, LANE), name="reduce_small").reshape(rows, LANE)
    (i, group, held_grads, route), = held
    started = _exchange_start(held_grads, route.lands(held_grads), route, total, name=f"scatter_l{i}_{group[0]}_start")
    zero = jnp.zeros((1,), F32)
    flat = [_flatten_small({k: a[k] for k in SMALL}, zero) for a in (w, m, v)]
    upd = _adam_flat(total + started[4][0, 0], *flat, name="adam_small")
    grads, delta, new_m, new_v = {}, {}, {}, {}
    g_small, loss_total = _unflatten_small(total, small)
    grads.update(g_small)
    for dst, u in zip((delta, new_m, new_v), upd):
        dst.update(_unflatten_small(u, small)[0])

    ops = {k: [_pad_to(a[k], 2, nb_pad) for a in (w, m, v)] if k in ("w_ffn_gate", "w_ffn_up") else [a[k] for a in (w, m, v)]
           for k in BIG}
    half, tie = {}, upd[0]

    def adam(layer, t, tie):
        k = BIG[t]
        own, recv = arrived[layer, t]
        return _adam_shard(recv, own, me, *ops[k], kind=BIG_KINDS[t], layer=layer, name=f"adam_l{layer}_{k}",
                           tr=_row_tile(ops[k][0].shape[1]), tie=tie, others=half.get(k))

    for t in range(len(BIG)):
        half[BIG[t]] = adam(1, t, tie)
        tie = half[BIG[t]][0]
    for t in FFN_GROUP:
        half[BIG[t]] = adam(0, t, tie)
        tie = half[BIG[t]][0]
    own, recv = _exchange_wait(started, route, tie, name=f"scatter_l{i}_{group[0]}_wait")
    arrived.update({(i, t): (own[j], recv[j]) for j, t in enumerate(group)})
    for t in MIX_GROUP:
        half[BIG[t]] = adam(0, t, None)
    for k in BIG:
        res = half[k]
        if k in ("w_ffn_gate", "w_ffn_up"):
            res = [r[:, :, :nb_ffn] for r in res]
        grads[k], delta[k], new_m[k], new_v[k] = res

    return (loss_total, dx[None], *[grads[k] for k in WEIGHTS], *[delta[k] for k in WEIGHTS],
            *[new_m[k] for k in WEIGHTS], *[new_v[k] for k in WEIGHTS])
```

```python
import math

import jax
import jax.numpy as jnp
from jax import lax
from jax.experimental import pallas as pl
from jax.experimental.pallas import tpu as pltpu

F32 = jnp.float32
BF16 = jnp.bfloat16
MESH = pl.DeviceIdType.MESH

N_DEV = 8
RET_HEADS = 4
CHUNK = 128
ROPE_BASE = 10000.0
SSM_GROUP = 16
SSM_STATE = 64
TILE_GROUPS = 8
TILE_U = TILE_GROUPS * SSM_GROUP
TILE_N = TILE_GROUPS * SSM_STATE
LANE = 128
SUBLANE = 8
N_SEG = SUBLANE
N_LT = TILE_N // LANE
SCAN_UNROLL = 4
FLAT_ROWS = 1024
EPS = 1e-6
ADAM_LR = 0.001
ADAM_B1 = 0.9
ADAM_B2 = 0.999
ADAM_EPS = 1e-08
ADAM_WD = 0.01
ADAM_STEP = 10
VMEM_LIMIT = 56 * 1024 * 1024


def _params(*sem):
    return pltpu.CompilerParams(dimension_semantics=sem or None, vmem_limit_bytes=VMEM_LIMIT)


def _dg(a, b, ca, cb):
    return lax.dot_general(a.astype(BF16), b.astype(BF16), (((ca,), (cb,)), ((), ())),
                           preferred_element_type=F32)


@jax.custom_vjp
def _dnn(a, b):
    return _dg(a, b, 1, 0)


@jax.custom_vjp
def _dnt(a, b):
    return _dg(a, b, 1, 1)


@jax.custom_vjp
def _dtn(a, b):
    return _dg(a, b, 0, 0)


_dnn.defvjp(lambda a, b: (_dnn(a, b), (a, b)), lambda r, g: (_dnt(g, r[1]), _dtn(r[0], g)))
_dnt.defvjp(lambda a, b: (_dnt(a, b), (a, b)), lambda r, g: (_dnn(g, r[1]), _dtn(g, r[0])))
_dtn.defvjp(lambda a, b: (_dtn(a, b), (a, b)), lambda r, g: (_dnt(r[1], g), _dnn(r[0], g)))


def _matmul(a, b, *, mode, out_dtype, name, res=None, tm=1024, tn=1024, tk=2048):
    if mode == "nn":
        (m, k), n = a.shape, b.shape[1]
    elif mode == "nt":
        (m, k), n = a.shape, b.shape[0]
    else:
        (k, m), n = a.shape, b.shape[1]
    tm, tn = min(tm, m), min(tn, n)
    tk = next(t for t in (tk, 1024, 512, 256, LANE) if k % t == 0) if k > tk else k
    assert m % tm == 0 and n % tn == 0 and k % tk == 0, (name, m, n, k)
    nk = k // tk
    if mode == "tn":
        a_spec = pl.BlockSpec((tk, tm), lambda i, j, kk: (kk, i))
    else:
        a_spec = pl.BlockSpec((tm, tk), lambda i, j, kk: (i, kk))
    if mode == "nt":
        b_spec = pl.BlockSpec((tn, tk), lambda i, j, kk: (j, kk))
    else:
        b_spec = pl.BlockSpec((tk, tn), lambda i, j, kk: (kk, j))
    ca, cb = {"nn": (1, 0), "nt": (1, 1), "tn": (0, 0)}[mode]
    o_spec = pl.BlockSpec((tm, tn), lambda i, j, kk: (i, j))
    has_res = res is not None

    def body(*refs):
        a_ref, b_ref = refs[:2]
        r_ref = refs[2] if has_res else None
        o_ref = refs[2 + has_res]
        part = _dg(a_ref[...], b_ref[...], ca, cb)
        if nk == 1:
            o_ref[...] = (part + r_ref[...] if has_res else part).astype(out_dtype)
            return
        acc = refs[-1]
        kk = pl.program_id(2)

        @pl.when(kk == 0)
        def _():
            acc[...] = part + r_ref[...] if has_res else part

        @pl.when(kk > 0)
        def _():
            acc[...] += part

        @pl.when(kk == nk - 1)
        def _():
            o_ref[...] = acc[...].astype(out_dtype)

    return pl.pallas_call(
        body, name=name, grid=(m // tm, n // tn, nk),
        in_specs=[a_spec, b_spec] + ([o_spec] if has_res else []),
        out_specs=o_spec, out_shape=jax.ShapeDtypeStruct((m, n), out_dtype),
        scratch_shapes=[pltpu.VMEM((tm, tn), F32)] if nk > 1 else [],
        compiler_params=_params("parallel", "parallel", "arbitrary"),
    )(*((a, b, res) if has_res else (a, b)))


def _row(arr, tl, width=None, cb=0):
    width = arr.shape[1] if width is None else width
    return arr, pl.BlockSpec((tl, width), lambda i, cb=cb: (i, cb))


def _par(arr):
    return arr, pl.BlockSpec(arr.shape, lambda i: (0,) * arr.ndim)


def _rowwise(body, *, rows, tl, ins, outs, name):
    arrays = [a for a, _ in ins]
    in_specs = [s for _, s in ins]
    out_shape, out_specs, acc_ids = [], [], []
    for n, o in enumerate(outs):
        if o[0] == "row":
            out_shape.append(jax.ShapeDtypeStruct((rows, o[1]), o[2]))
            out_specs.append(pl.BlockSpec((tl, o[1]), lambda i: (i, 0)))
        else:
            out_shape.append(jax.ShapeDtypeStruct((1, o[1]), F32))
            out_specs.append(pl.BlockSpec((1, o[1]), lambda i: (0, 0)))
            acc_ids.append(n)
    n_in = len(arrays)
    assert rows % tl == 0, (name, rows, tl)

    def wrapped(*refs):
        in_refs, out_refs = refs[:n_in], refs[n_in:]

        @pl.when(pl.program_id(0) == 0)
        def _():
            for n in acc_ids:
                out_refs[n][...] = jnp.zeros_like(out_refs[n])

        body(in_refs, out_refs)

    return pl.pallas_call(
        wrapped, name=name, grid=(rows // tl,), in_specs=in_specs, out_specs=out_specs,
        out_shape=out_shape, compiler_params=_params("arbitrary"),
    )(*arrays)


def _rms(x, g):
    return x * lax.rsqrt(jnp.mean(x * x, axis=-1, keepdims=True) + EPS) * g


def _norm_fwd(x, g, *, name, tl=256):
    def body(i, o):
        o[0][...] = _rms(i[0][...], i[1][...]).astype(BF16)

    return _rowwise(body, rows=x.shape[0], tl=tl, ins=[_row(x, tl), _par(g)],
                    outs=[("row", x.shape[1], BF16)], name=name)[0]


def _norm_bwd(x, g, dh, dres, *, name, tl=256):
    def body(i, o):
        _, vjp = jax.vjp(_rms, i[0][...], i[1][...])
        dx, dg = vjp(i[2][...])
        o[0][...] = i[3][...] + dx
        o[1][...] += dg

    d = x.shape[1]
    return _rowwise(body, rows=x.shape[0], tl=tl, ins=[_row(x, tl), _par(g), _row(dh, tl), _row(dres, tl)],
                    outs=[("row", d, F32), ("acc", d)], name=name)


def _final(x, g, target, *, name, tl=256):
    d = x.shape[1]

    def body(i, o):
        y, vjp = jax.vjp(_rms, i[0][...], i[1][...])
        err = y - i[2][...]
        dx, dg = vjp(err * (1.0 / d))
        o[0][...] = dx
        o[1][...] += dg
        o[2][...] += jnp.full((1, LANE), 0.5 / d, F32) * jnp.sum(err * err)

    return _rowwise(body, rows=x.shape[0], tl=tl, ins=[_row(x, tl), _par(g), _row(target, tl)],
                    outs=[("row", d, F32), ("acc", d), ("acc", LANE)], name=name)


def _rot(x, cos, sin, out_ref, col, scale=1.0, inverse=False):
    x1, x2 = x[:, :LANE], x[:, LANE:]
    if inverse:
        sin = -sin
    out_ref[:, col:col + LANE] = ((x1 * cos - x2 * sin) * scale).astype(out_ref.dtype)
    out_ref[:, col + LANE:col + 2 * LANE] = ((x1 * sin + x2 * cos) * scale).astype(out_ref.dtype)


def _ret_prep(proj, cos, sin, *, qkw, d, name, tl=256):
    dk = qkw // RET_HEADS
    assert dk == 2 * LANE and (2 * qkw) % d == 0

    def body(i, o):
        c, s = i[3][...], i[4][...]
        for h in range(RET_HEADS):
            _rot(i[0][:, h * dk:(h + 1) * dk], c, s, o[0], h * dk)
            _rot(i[1][:, h * dk:(h + 1) * dk], c, s, o[1], h * dk, scale=dk ** -0.5)
        o[2][...] = i[2][...].astype(BF16)

    return _rowwise(body, rows=proj.shape[0], tl=tl,
                    ins=[_row(proj, tl, qkw, 0), _row(proj, tl, qkw, 1), _row(proj, tl, d, 2 * qkw // d),
                         _row(cos, tl), _row(sin, tl)],
                    outs=[("row", qkw, BF16), ("row", qkw, BF16), ("row", d, BF16)], name=name)


def _ret_prep_bwd(dq_rot, dk_rot, cos, sin, *, name, tl=256):
    qkw = dq_rot.shape[1]
    dk = qkw // RET_HEADS

    def body(i, o):
        c, s = i[2][...], i[3][...]
        for h in range(RET_HEADS):
            _rot(i[0][:, h * dk:(h + 1) * dk], c, s, o[0], h * dk, inverse=True)
            _rot(i[1][:, h * dk:(h + 1) * dk], c, s, o[1], h * dk, scale=dk ** -0.5, inverse=True)

    return _rowwise(body, rows=dq_rot.shape[0], tl=tl,
                    ins=[_row(dq_rot, tl), _row(dk_rot, tl), _row(cos, tl), _row(sin, tl)],
                    outs=[("row", qkw, BF16), ("row", qkw, BF16)], name=name)


def _ret_weights(lgf, lgb):
    t = lax.broadcasted_iota(jnp.int32, (CHUNK, 1), 0).astype(F32)
    diff = (lax.broadcasted_iota(jnp.int32, (CHUNK, CHUNK), 0)
            - lax.broadcasted_iota(jnp.int32, (CHUNK, CHUNK), 1)).astype(F32)
    dmat = jnp.exp(jnp.where(diff >= 0, lgf * diff, -lgb * diff))
    return dict(dmat=dmat, wqf=jnp.exp(lgf * (t + 1.0)), wkf=jnp.exp(lgf * (CHUNK - 1.0 - t)),
                wqb=jnp.exp(lgb * (CHUNK - t)), wkb=jnp.exp(lgb * t))


def _ret_f_part(q, k, v, lgf, lgb, s_f):
    w = _ret_weights(lgf, lgb)
    y = _dnn(_dnt(q, k) * w["dmat"], v) + _dnn(q * w["wqf"], s_f)
    return y, _dtn(k * w["wkf"], v)


def _ret_b_part(q, k, v, lgb, s_b):
    w = _ret_weights(lgb, lgb)
    return _dnn(q * w["wqb"], s_b), _dtn(k * w["wkb"], v)


def _chunk(c):
    return pl.ds(pl.multiple_of(c * CHUNK, CHUNK), CHUNK)


def _ret_specs(l, qkw, d):
    dk, dv = qkw // RET_HEADS, d // RET_HEADS
    return dk, dv, [pl.BlockSpec(memory_space=pltpu.SMEM),
                    pl.BlockSpec((l, dk), lambda h: (0, h)), pl.BlockSpec((l, dk), lambda h: (0, h)),
                    pl.BlockSpec((l, dv), lambda h: (0, h))]


def _ret_fwd(lg, q, k, v, *, name):
    l, qkw = q.shape
    d = v.shape[1]
    nc = l // CHUNK
    dk, dv, in_specs = _ret_specs(l, qkw, d)

    def body(lg_ref, q_ref, k_ref, v_ref, y_ref, s_ref):
        h = pl.program_id(0)
        lgf = jnp.full((1, 1), lg_ref[0, h], F32)
        lgb = jnp.full((1, 1), lg_ref[1, h], F32)
        dec_f, dec_b = jnp.exp(lgf * CHUNK), jnp.exp(lgb * CHUNK)

        def load(c):
            r = _chunk(c)
            return r, q_ref[r, :].astype(F32), k_ref[r, :].astype(F32), v_ref[r, :].astype(F32)

        s_ref[...] = jnp.zeros_like(s_ref)

        def f_step(c, _):
            r, qc, kc, vc = load(c)
            y, kv = _ret_f_part(qc, kc, vc, lgf, lgb, s_ref[...])
            y_ref[r, :] = y
            s_ref[...] = dec_f * s_ref[...] + kv
            return 0

        lax.fori_loop(0, nc, f_step, 0)
        s_ref[...] = jnp.zeros_like(s_ref)

        def b_step(n, _):
            r, qc, kc, vc = load(nc - 1 - n)
            y, kv = _ret_b_part(qc, kc, vc, lgb, s_ref[...])
            y_ref[r, :] += y
            s_ref[...] = dec_b * s_ref[...] + kv
            return 0

        lax.fori_loop(0, nc, b_step, 0)

    return pl.pallas_call(
        body, name=name, grid=(RET_HEADS,), in_specs=in_specs,
        out_specs=pl.BlockSpec((l, dv), lambda h: (0, h)), out_shape=jax.ShapeDtypeStruct((l, d), F32),
        scratch_shapes=[pltpu.VMEM((dk, dv), F32)], compiler_params=_params("arbitrary"),
    )(lg, q, k, v)


def _ret_bwd(lg, q, k, v, dy, *, name):
    l, qkw = q.shape
    d = v.shape[1]
    nc = l // CHUNK
    dk, dv, in_specs = _ret_specs(l, qkw, d)

    def body(lg_ref, q_ref, k_ref, v_ref, dy_ref, dq_ref, dk_ref, dv_ref, dlg_ref, states, s_ref, sh_ref):
        h = pl.program_id(0)
        lgf = jnp.full((1, 1), lg_ref[0, h], F32)
        lgb = jnp.full((1, 1), lg_ref[1, h], F32)
        dec_f, dec_b = jnp.exp(lgf * CHUNK), jnp.exp(lgb * CHUNK)

        def load(c):
            r = _chunk(c)
            return (r, q_ref[r, :].astype(F32), k_ref[r, :].astype(F32), v_ref[r, :].astype(F32),
                    dy_ref[r, :].astype(F32))

        s_ref[...] = jnp.zeros_like(s_ref)

        def f_states(c, _):
            _, qc, kc, vc, _ = load(c)
            states[c] = s_ref[...]
            w = _ret_weights(lgf, lgb)
            s_ref[...] = dec_f * s_ref[...] + _dtn(kc * w["wkf"], vc)
            return 0

        lax.fori_loop(0, nc, f_states, 0)
        sh_ref[...] = jnp.zeros_like(sh_ref)

        def f_adj(n, carry):
            dlf, dlb, ddec = carry
            c = nc - 1 - n
            r, qc, kc, vc, dyc = load(c)
            sc = states[c]
            _, vjp = jax.vjp(_ret_f_part, qc, kc, vc, lgf, lgb, sc)
            dq, dkk, dvv, g_f, g_b, dsc = vjp((dyc, sh_ref[...]))
            dq_ref[r, :] = dq
            dk_ref[r, :] = dkk
            dv_ref[r, :] = dvv
            ddec = ddec + jnp.sum(sh_ref[...] * sc)
            sh_ref[...] = dsc + dec_f * sh_ref[...]
            return dlf + g_f, dlb + g_b, ddec

        z = jnp.zeros((1, 1), F32)
        dlf, dlb, ddec_f = lax.fori_loop(0, nc, f_adj, (z, z, z))

        s_ref[...] = jnp.zeros_like(s_ref)

        def b_states(n, _):
            c = nc - 1 - n
            _, qc, kc, vc, _ = load(c)
            states[c] = s_ref[...]
            w = _ret_weights(lgb, lgb)
            s_ref[...] = dec_b * s_ref[...] + _dtn(kc * w["wkb"], vc)
            return 0

        lax.fori_loop(0, nc, b_states, 0)
        sh_ref[...] = jnp.zeros_like(sh_ref)

        def b_adj(c, carry):
            dlb, ddec = carry
            r, qc, kc, vc, dyc = load(c)
            sc = states[c]
            _, vjp = jax.vjp(_ret_b_part, qc, kc, vc, lgb, sc)
            dq, dkk, dvv, g_b, dsc = vjp((dyc, sh_ref[...]))
            dq_ref[r, :] += dq
            dk_ref[r, :] += dkk
            dv_ref[r, :] += dvv
            ddec = ddec + jnp.sum(sh_ref[...] * sc)
            sh_ref[...] = dsc + dec_b * sh_ref[...]
            return dlb + g_b, ddec

        dlb, ddec_b = lax.fori_loop(0, nc, b_adj, (dlb, z))
        dlf = dlf + ddec_f * dec_f * CHUNK
        dlb = dlb + ddec_b * dec_b * CHUNK
        row = lax.broadcasted_iota(jnp.int32, (SUBLANE, LANE), 0)
        dlg_ref[...] = jnp.where(row == 0, dlf, jnp.where(row == 1, dlb, 0.0))

    head = lambda w: pl.BlockSpec((l, w), lambda h: (0, h))
    return pl.pallas_call(
        body, name=name, grid=(RET_HEADS,), in_specs=in_specs + [head(dv)],
        out_specs=[head(dk), head(dk), head(dv), pl.BlockSpec((None, SUBLANE, LANE), lambda h: (h, 0, 0))],
        out_shape=[jax.ShapeDtypeStruct((l, qkw), F32), jax.ShapeDtypeStruct((l, qkw), F32),
                   jax.ShapeDtypeStruct((l, d), F32), jax.ShapeDtypeStruct((RET_HEADS, SUBLANE, LANE), F32)],
        scratch_shapes=[pltpu.VMEM((nc, dk, dv), F32), pltpu.VMEM((dk, dv), F32), pltpu.VMEM((dk, dv), F32)],
        compiler_params=_params("arbitrary"),
    )(lg, q, k, v, dy)


def _s5_param_fn(a_re, a_im, log_dt, b_re, b_im, rep):
    dt = jnp.exp(log_dt)
    mag = jnp.exp(a_re * dt)
    lam_re, lam_im = mag * jnp.cos(a_im * dt), mag * jnp.sin(a_im * dt)
    n_re, n_im = lam_re - 1.0, lam_im
    den = a_re * a_re + a_im * a_im
    c_re = (n_re * a_re + n_im * a_im) / den
    c_im = (n_im * a_re - n_re * a_im) / den
    hi = lax.Precision.HIGHEST
    c_re = jnp.dot(c_re, rep, precision=hi, preferred_element_type=F32)
    c_im = jnp.dot(c_im, rep, precision=hi, preferred_element_type=F32)
    return lam_re, lam_im, c_re * b_re - c_im * b_im, c_re * b_im + c_im * b_re


def _s5_param_shapes(a_re, b_re):
    r, p = a_re.shape
    return [jax.ShapeDtypeStruct((r, p), F32)] * 2 + [jax.ShapeDtypeStruct(b_re.shape, F32)] * 2


def _s5_prep(a_re, a_im, log_dt, b_re, b_im, rep, *, name):
    def body(*refs):
        outs = _s5_param_fn(*[r[...] for r in refs[:6]])
        for o_ref, o in zip(refs[6:], outs):
            o_ref[...] = o

    return pl.pallas_call(body, name=name, out_shape=_s5_param_shapes(a_re, b_re),
                          compiler_params=_params())(a_re, a_im, log_dt, b_re, b_im, rep)


def _s5_prep_bwd(a_re, a_im, log_dt, b_re, b_im, rep, cts, *, name):
    def body(*refs):
        ins = [r[...] for r in refs[:6]]
        _, vjp = jax.vjp(lambda *p: _s5_param_fn(*p, ins[5]), *ins[:5])
        grads = vjp(tuple(r[...] for r in refs[6:10]))
        for o_ref, o in zip(refs[10:], grads):
            o_ref[...] = o

    shapes = [jax.ShapeDtypeStruct(t.shape, F32) for t in (a_re, a_im, log_dt, b_re, b_im)]
    return pl.pallas_call(body, name=name, out_shape=shapes,
                          compiler_params=_params())(a_re, a_im, log_dt, b_re, b_im, rep, *cts)


def _eye_tiles():
    return jnp.eye(TILE_GROUPS, dtype=F32)


def _b_tiles(bbar, tiles):
    t = bbar.reshape(2, tiles, TILE_GROUPS, SSM_STATE, SSM_GROUP).transpose(0, 1, 2, 4, 3)
    t = t[:, :, :, :, None, :] * _eye_tiles()[None, None, :, None, :, None]
    return t.reshape(2, tiles, TILE_U, TILE_N)


def _b_untile(dbt, tiles):
    t = dbt.reshape(2, tiles, TILE_GROUPS, SSM_GROUP, TILE_GROUPS, SSM_STATE)
    t = (t * _eye_tiles()[None, None, :, None, :, None]).sum(axis=4)
    return t.transpose(0, 1, 2, 4, 3).reshape(2 * tiles * TILE_GROUPS, SSM_STATE * SSM_GROUP)


def _c_tiles(c, tiles):
    t = c.reshape(2, tiles, TILE_GROUPS, SSM_GROUP, SSM_STATE).transpose(0, 1, 2, 4, 3)
    t = t[:, :, :, :, None, :] * _eye_tiles()[None, None, :, None, :, None]
    return t.reshape(2, tiles, TILE_N, TILE_U)


def _c_untile(dct, tiles):
    t = dct.reshape(2, tiles, TILE_GROUPS, SSM_STATE, TILE_GROUPS, SSM_GROUP)
    t = (t * _eye_tiles()[None, None, :, None, :, None]).sum(axis=4)
    return t.transpose(0, 1, 2, 4, 3).reshape(2, tiles * TILE_GROUPS, SSM_GROUP, SSM_STATE)


def _to_segments(a):
    l, w = a.shape
    return a.reshape(N_SEG, l // N_SEG, w).transpose(1, 0, 2).reshape(l, w)


def _from_segments(a):
    l, w = a.shape
    return a.reshape(l // N_SEG, N_SEG, w).transpose(1, 0, 2).reshape(l, w)


def _s5_scan(xr, xi, a_re, a_im, *, length, reverse, shifted=None):
    ls = length // N_SEG
    assert ls * N_SEG == length and ls & (ls - 1) == 0
    ar = [jnp.broadcast_to(a_re[:, c * LANE:(c + 1) * LANE], (N_SEG, LANE)) for c in range(N_LT)]
    ai = [jnp.broadcast_to(a_im[:, c * LANE:(c + 1) * LANE], (N_SEG, LANE)) for c in range(N_LT)]
    zero = jnp.zeros((N_SEG, LANE), F32)
    row = lax.broadcasted_iota(jnp.int32, (N_SEG, LANE), 0)

    def step_of(n):
        return (ls - 1 - n) if reverse else n

    def block(j):
        return pl.ds(j * N_SEG, N_SEG) if isinstance(j, int) else pl.ds(pl.multiple_of(j * N_SEG, N_SEG), N_SEG)

    def local(n, carry):
        rows = block(step_of(n))
        new = []
        for c in range(N_LT):
            cr, ci = carry[2 * c], carry[2 * c + 1]
            nr = ar[c] * cr - ai[c] * ci + xr[rows, _lanes(c)]
            ni = ar[c] * ci + ai[c] * cr + xi[rows, _lanes(c)]
            xr[rows, _lanes(c)] = nr
            xi[rows, _lanes(c)] = ni
            new += [nr, ni]
        return tuple(new)

    ends = lax.fori_loop(0, ls, local, (zero,) * (2 * N_LT), unroll=SCAN_UNROLL)

    init = []
    for c in range(N_LT):
        pr, pi = ar[c][0:1, :], ai[c][0:1, :]
        for _ in range(ls.bit_length() - 1):
            pr, pi = pr * pr - pi * pi, 2.0 * pr * pi
        cr = ci = jnp.zeros((1, LANE), F32)
        ir, ii = zero, zero
        for s in (range(N_SEG - 1, -1, -1) if reverse else range(N_SEG)):
            ir = jnp.where(row == s, cr, ir)
            ii = jnp.where(row == s, ci, ii)
            er, ei = ends[2 * c][s:s + 1, :], ends[2 * c + 1][s:s + 1, :]
            cr, ci = pr * cr - pi * ci + er, pr * ci + pi * cr + ei
        init += [ir, ii]

    def fix(n, carry, last=False):
        j = step_of(n)
        rows = block(j)
        new, sums = [], []
        for c in range(N_LT):
            cr, ci = carry[2 * c], carry[2 * c + 1]
            nr = ar[c] * cr - ai[c] * ci
            ni = ar[c] * ci + ai[c] * cr
            fr = xr[rows, _lanes(c)] + nr
            fi = xi[rows, _lanes(c)] + ni
            xr[rows, _lanes(c)] = fr
            xi[rows, _lanes(c)] = fi
            new += [nr, ni]
            if shifted is not None:
                yr, yi, shift = shifted
                if not last:
                    srows = block(j + shift)
                    sr, si = yr[srows, _lanes(c)], yi[srows, _lanes(c)]
                else:
                    edge = block(ls - 1 if shift < 0 else 0)
                    move, gone = (1, 0) if shift < 0 else (N_SEG - 1, N_SEG - 1)
                    sr = jnp.where(row == gone, 0.0, pltpu.roll(yr[edge, _lanes(c)], move, 0))
                    si = jnp.where(row == gone, 0.0, pltpu.roll(yi[edge, _lanes(c)], move, 0))
                sums += [carry[2 * N_LT + 2 * c] + fr * sr + fi * si,
                         carry[2 * N_LT + 2 * c + 1] + fi * sr - fr * si]
        return tuple(new + sums)

    if shifted is None:
        lax.fori_loop(0, ls, fix, tuple(init), unroll=SCAN_UNROLL)
        return ()
    assert shifted[2] == (-1 if reverse else 1)
    out = lax.fori_loop(0, ls - 1, fix, tuple(init) + (zero,) * (2 * N_LT), unroll=SCAN_UNROLL)
    return fix(ls - 1, out, last=True)[2 * N_LT:]


def _s5_tile_specs(l, d):
    tile = lambda r, c: pl.BlockSpec((None, None, r, c), lambda t, d=d: (d, t, 0, 0))
    return [pl.BlockSpec((l, TILE_U), lambda t: (0, t)), tile(TILE_U, TILE_N), tile(TILE_U, TILE_N),
            tile(1, TILE_N), tile(1, TILE_N), tile(TILE_N, TILE_U), tile(TILE_N, TILE_U)]


def _lanes(c):
    return slice(c * LANE, (c + 1) * LANE)


def _s5_fwd(u, bt_re, bt_im, lam_re, lam_im, ct_re, ct_im, *, d, name, add=None):
    l = u.shape[0]
    tiles = bt_re.shape[1]
    col = pl.BlockSpec((l, TILE_U), lambda t: (0, t))
    has_add = add is not None

    def body(*refs):
        u_ref, bre, bim, lre, lim, cre, cim = refs[:7]
        y_ref, xr, xi = refs[-3:]
        uu = u_ref[...]
        bu_re, bu_im = _dg(uu, bre[...], 1, 0), _dg(uu, bim[...], 1, 0)
        xr[...] = bu_re
        xi[...] = bu_im
        _s5_scan(xr, xi, lre[...], lim[...], length=l, reverse=(d == 1))
        y = _dg(xr[...], cre[...], 1, 0) - _dg(xi[...], cim[...], 1, 0)
        y_ref[...] = y + refs[7][...] if has_add else y

    return pl.pallas_call(
        body, name=name, grid=(tiles,), in_specs=_s5_tile_specs(l, d) + [col] * has_add, out_specs=col,
        out_shape=jax.ShapeDtypeStruct((l, tiles * TILE_U), F32),
        scratch_shapes=[pltpu.VMEM((l, TILE_N), F32)] * 2, compiler_params=_params("arbitrary"),
    )(u, bt_re, bt_im, lam_re, lam_im, ct_re, ct_im, *([add] if has_add else []))


def _s5_bwd(u, dy, bt_re, bt_im, lam_re, lam_im, ct_re, ct_im, *, d, name, add=None):
    l = u.shape[0]
    tiles = bt_re.shape[1]
    col = pl.BlockSpec((l, TILE_U), lambda t: (0, t))
    reverse = d == 1
    has_add = add is not None

    def body(*refs):
        u_ref, bre, bim, lre, lim, cre, cim, dy_ref = refs[:8]
        du_ref, dbre, dbim, dcre, dcim, dlre, dlim, xr, xi, gr, gi = refs[-11:]
        uu, dyy = u_ref[...], dy_ref[...]
        bu_re, bu_im = _dg(uu, bre[...], 1, 0), _dg(uu, bim[...], 1, 0)
        xr[...] = bu_re
        xi[...] = bu_im
        _s5_scan(xr, xi, lre[...], lim[...], length=l, reverse=reverse)
        gr[...] = _dg(dyy, cre[...], 1, 1)
        gi[...] = -_dg(dyy, cim[...], 1, 1)
        sums = _s5_scan(gr, gi, lre[...], -lim[...], length=l, reverse=not reverse,
                        shifted=(xr, xi, 1 if reverse else -1))
        for c in range(N_LT):
            dlre[:, _lanes(c)] = jnp.sum(sums[2 * c], axis=0, keepdims=True)
            dlim[:, _lanes(c)] = jnp.sum(sums[2 * c + 1], axis=0, keepdims=True)
        g_re, g_im = gr[...], gi[...]
        du = _dg(g_re, bre[...], 1, 1) + _dg(g_im, bim[...], 1, 1)
        du_ref[...] = du + refs[8][...] if has_add else du
        dbre[...] = _dg(uu, g_re, 0, 0)
        dbim[...] = _dg(uu, g_im, 0, 0)
        dcre[...] = _dg(xr[...], dyy, 0, 0)
        dcim[...] = -_dg(xi[...], dyy, 0, 0)

    out3 = lambda r, c: pl.BlockSpec((None, r, c), lambda t: (t, 0, 0))
    f = lambda *s: jax.ShapeDtypeStruct(s, F32)
    return pl.pallas_call(
        body, name=name, grid=(tiles,), in_specs=_s5_tile_specs(l, d) + [col] + [col] * has_add,
        out_specs=[col, out3(TILE_U, TILE_N), out3(TILE_U, TILE_N),
                   out3(TILE_N, TILE_U), out3(TILE_N, TILE_U), out3(1, TILE_N), out3(1, TILE_N)],
        out_shape=[f(l, tiles * TILE_U), f(tiles, TILE_U, TILE_N), f(tiles, TILE_U, TILE_N),
                   f(tiles, TILE_N, TILE_U), f(tiles, TILE_N, TILE_U), f(tiles, 1, TILE_N), f(tiles, 1, TILE_N)],
        scratch_shapes=[pltpu.VMEM((l, TILE_N), F32)] * 4, compiler_params=_params("arbitrary"),
    )(u, bt_re, bt_im, lam_re, lam_im, ct_re, ct_im, dy, *([add] if has_add else []))


def _s5_post(y, proj, u_cb, dskip, *, name, tl=256):
    d = y.shape[1]

    def body(i, o):
        ys = jax.nn.gelu(i[0][...] + i[2][...] * i[1][...])
        o[0][...] = ys
        o[1][...] = ys.astype(BF16)

    return _rowwise(body, rows=y.shape[0], tl=tl, ins=[_row(y, tl), _row(proj, tl, d, u_cb), _par(dskip)],
                    outs=[("row", d, F32), ("row", d, BF16)], name=name)


def _s5_post_bwd(y, proj, u_cb, dskip, dys, *, name, tl=256):
    d = y.shape[1]

    def body(i, o):
        u_ = i[1][...]
        _, vjp = jax.vjp(jax.nn.gelu, i[0][...] + i[2][...] * u_)
        (dpre,) = vjp(i[3][...])
        o[0][...] = dpre
        o[1][...] += jnp.sum(dpre * u_, axis=0, keepdims=True)

    return _rowwise(body, rows=y.shape[0], tl=tl,
                    ins=[_row(y, tl), _row(proj, tl, d, u_cb), _par(dskip), _row(dys, tl)],
                    outs=[("row", d, F32), ("acc", d)], name=name)


def _du_combine(dpre, dskip, du_s5, *, name, tl=256):
    d = dpre.shape[1]

    def body(i, o):
        o[0][...] = (i[0][...] * i[1][...] + i[2][...]).astype(BF16)

    return _rowwise(body, rows=dpre.shape[0], tl=tl, ins=[_row(dpre, tl), _par(dskip), _row(du_s5, tl)],
                    outs=[("row", d, BF16)], name=name)[0]


def _merge_fn(y, g, gate_r, gate_s, ys, glu, b):
    ret = jax.nn.silu(g) * (y * lax.rsqrt(jnp.mean(y * y, axis=-1, keepdims=True) + EPS))
    ssm = ys * jax.nn.sigmoid(glu + b)
    return jax.nn.sigmoid(gate_r) * ret + jax.nn.sigmoid(gate_s) * ssm


def _merge_ins(y_raw, proj, ys, glu, b_glu, cb0, tl):
    d = y_raw.shape[1]
    return [_row(y_raw, tl), _row(proj, tl, d, cb0 + 1), _row(proj, tl, d, cb0 + 3), _row(proj, tl, d, cb0 + 4),
            _row(ys, tl), _row(glu, tl), _par(b_glu)]


def _merge(y_raw, proj, ys, glu, b_glu, *, cb0, name, tl=128):
    d = y_raw.shape[1]
    dv = d // RET_HEADS

    def body(i, o):
        for h in range(RET_HEADS):
            cs = slice(h * dv, (h + 1) * dv)
            o[0][:, cs] = _merge_fn(*[r[:, cs] for r in i]).astype(BF16)

    return _rowwise(body, rows=y_raw.shape[0], tl=tl, ins=_merge_ins(y_raw, proj, ys, glu, b_glu, cb0, tl),
                    outs=[("row", d, BF16)], name=name)[0]


def _merge_bwd(y_raw, proj, ys, glu, b_glu, dmerged, *, cb0, name, tl=128):
    d = y_raw.shape[1]
    dv = d // RET_HEADS

    def body(i, o):
        for h in range(RET_HEADS):
            cs = slice(h * dv, (h + 1) * dv)
            _, vjp = jax.vjp(_merge_fn, *[r[:, cs] for r in i[:7]])
            dy, dg, dgr, dgs, dys, dglu, db = vjp(i[7][:, cs])
            o[0][:, cs] = dy.astype(BF16)
            o[1][:, cs] = dg.astype(BF16)
            o[2][:, cs] = dgr.astype(BF16)
            o[3][:, cs] = dgs.astype(BF16)
            o[4][:, cs] = dglu.astype(BF16)
            o[5][:, cs] = dys
            o[6][:, cs] += db

    return _rowwise(body, rows=y_raw.shape[0], tl=tl,
                    ins=_merge_ins(y_raw, proj, ys, glu, b_glu, cb0, tl) + [_row(dmerged, tl)],
                    outs=[("row", d, BF16)] * 5 + [("row", d, F32), ("acc", d)], name=name)


def _ffn_act_fn(gate, up):
    return jax.nn.silu(gate) * up


def _ffn_act(gate, up, *, name, tl=128):
    def body(i, o):
        o[0][...] = _ffn_act_fn(i[0][...], i[1][...]).astype(BF16)

    return _rowwise(body, rows=gate.shape[0], tl=tl, ins=[_row(gate, tl), _row(up, tl)],
                    outs=[("row", gate.shape[1], BF16)], name=name)[0]


def _ffn_act_bwd(gate, up, dact, *, name, tl=128):
    def body(i, o):
        _, vjp = jax.vjp(_ffn_act_fn, i[0][...], i[1][...])
        dgate, dup = vjp(i[2][...])
        o[0][...] = dgate.astype(BF16)
        o[1][...] = dup.astype(BF16)

    w = gate.shape[1]
    return _rowwise(body, rows=gate.shape[0], tl=tl, ins=[_row(gate, tl), _row(up, tl), _row(dact, tl)],
                    outs=[("row", w, BF16), ("row", w, BF16)], name=name)


def _adamw(w, g, m, v):
    m = ADAM_B1 * m + (1.0 - ADAM_B1) * g
    v = ADAM_B2 * v + (1.0 - ADAM_B2) * (g * g)
    m_hat = m / (1.0 - ADAM_B1 ** ADAM_STEP)
    v_hat = v / (1.0 - ADAM_B2 ** ADAM_STEP)
    return -ADAM_LR * (m_hat / (jnp.sqrt(v_hat) + ADAM_EPS) + ADAM_WD * w), m, v


def _adam_flat(g, w, m, v, *, name, tr=FLAT_ROWS):
    def body(i, o):
        for o_ref, val in zip(o, _adamw(i[1][...], i[0][...], i[2][...], i[3][...])):
            o_ref[...] = val

    return _rowwise(body, rows=g.shape[0], tl=tr, ins=[_row(a, tr) for a in (g, w, m, v)],
                    outs=[("row", LANE, F32)] * 3, name=name)


def _adam_shard(recv, own, me, w, m, v, *, kind, layer, name, tr, tie=None, others=None):
    depth, r, c = w.shape
    assert r % tr == 0 and recv.shape[2] == c
    blk = pl.BlockSpec((None, tr, c), lambda i, me_ref: (layer, i, 0))
    if kind == "col":
        own_spec = pl.BlockSpec((tr, c), lambda i, me_ref: (i, me_ref[0]))
    else:
        per = own.shape[0] // N_DEV // tr
        assert per * tr * N_DEV == own.shape[0]
        own_spec = pl.BlockSpec((tr, c), lambda i, me_ref: (me_ref[0] * per + i, 0))
    extra = ([] if tie is None else [tie]) + list(others or [])

    def body(me_ref, recv_ref, own_ref, w_ref, m_ref, v_ref, *refs):
        g_ref, d_ref, nm_ref, nv_ref = refs[len(extra):]
        g = own_ref[...].astype(F32)
        for k in range(N_DEV - 1):
            g = g + recv_ref[k].astype(F32)
        g_ref[...] = g
        d_ref[...], nm_ref[...], nv_ref[...] = _adamw(w_ref[...], g, m_ref[...], v_ref[...])

    first = 6 + (tie is not None)
    return pl.pallas_call(
        body, name=name, out_shape=[jax.ShapeDtypeStruct(w.shape, F32)] * 4,
        grid_spec=pltpu.PrefetchScalarGridSpec(
            num_scalar_prefetch=1, grid=(r // tr,),
            in_specs=[pl.BlockSpec((N_DEV - 1, tr, c), lambda i, me_ref: (0, i, 0)), own_spec, blk, blk, blk]
            + [pl.BlockSpec(memory_space=pl.ANY)] * len(extra),
            out_specs=[blk] * 4),
        input_output_aliases={first + j: j for j in range(4)} if others else {},
        compiler_params=_params("parallel"),
    )(me, recv, own, w, m, v, *extra)


def _position():
    x, y, c = lax.axis_index("x"), lax.axis_index("y"), lax.axis_index("c")
    return x, y, c, 4 * x + 2 * y + c


def _coords(p):
    return p // 4, (p // 2) % 2, p % 2


def _block_of(ref, kind, p, nb):
    if kind == "col":
        return ref.at[:, pl.ds(pl.multiple_of(p * nb, LANE), nb)]
    return ref.at[pl.ds(pl.multiple_of(p * nb, SUBLANE), nb), :]


def _all_gather(shards, kinds, *, name):
    n = len(shards)
    out_shape = []
    for s, kind in zip(shards, kinds):
        r, c = s.shape
        out_shape.append(jax.ShapeDtypeStruct((r, c * N_DEV) if kind == "col" else (r * N_DEV, c), s.dtype))

    def body(*refs):
        shard_refs, full_refs = refs[:n], refs[n:2 * n]
        send_sems, recv_sems, local_sems = refs[2 * n:]
        x, y, c, me = _position()
        sibling = (x, y, 1 - c)
        chips = [(1 - x, y), (x, 1 - y), (1 - x, 1 - y)]

        def block(t, dev):
            nb = shards[t].shape[1 if kinds[t] == "col" else 0]
            return _block_of(full_refs[t], kinds[t], 4 * dev[0] + 2 * dev[1] + dev[2], nb)

        def copy(t, k, dev, to, src=None):
            return pltpu.make_async_remote_copy(
                src_ref=block(t, dev) if src is None else src, dst_ref=block(t, dev),
                send_sem=send_sems.at[t, k], recv_sem=recv_sems.at[t, k], device_id=to, device_id_type=MESH)

        mine, first, passed = [], [], []
        for t in range(n):
            mine.append(pltpu.make_async_copy(shard_refs[t], block(t, (x, y, c)), local_sems.at[t]))
            mine[-1].start()
            first.append(copy(t, 0, (x, y, c), sibling, src=shard_refs[t]))
            first += [copy(t, 1 + j, (x, y, c), (*chip, c), src=shard_refs[t]) for j, chip in enumerate(chips)]
        for cp in first:
            cp.start()
        for j, chip in enumerate(chips):
            for t in range(n):
                copy(t, 1 + j, (*chip, c), (x, y, c)).wait_recv()
                passed.append(copy(t, 4 + j, (*chip, c), sibling))
                passed[-1].start()
        for t in range(n):
            copy(t, 0, sibling, (x, y, c)).wait_recv()
            for j, chip in enumerate(chips):
                copy(t, 4 + j, (*chip, 1 - c), (x, y, c)).wait_recv()
        for cp in first + passed:
            cp.wait_send()
        for cp in mine:
            cp.wait()

    any_spec = pl.BlockSpec(memory_space=pl.ANY)
    return pl.pallas_call(
        body, name=name, in_specs=[any_spec] * n, out_specs=[any_spec] * n, out_shape=out_shape,
        scratch_shapes=[pltpu.SemaphoreType.DMA((n, N_DEV - 1)), pltpu.SemaphoreType.DMA((n, N_DEV - 1)),
                        pltpu.SemaphoreType.DMA((n,))],
        compiler_params=pltpu.CompilerParams(has_side_effects=True),
    )(*shards)


class _GatherRoute:
    def __init__(self, shards, kinds):
        self.kinds = kinds
        self.nb = [s.shape[1 if k == "col" else 0] for s, k in zip(shards, kinds)]

    def lands(self, shards, me):
        out = []
        for t, (s, k) in enumerate(zip(shards, self.kinds)):
            full = lax.empty((s.shape[0], s.shape[1] * N_DEV) if k == "col" else (s.shape[0] * N_DEV, s.shape[1]), s.dtype)
            out.append(_place_own(s, full, me, k, name=f"own_block_{t}"))
        return out

    def sent(self, t, src_refs, me, k):
        return src_refs[t]

    def lands_at(self, t, land_refs, me, k):
        return _block_of(land_refs[t], self.kinds[t], (me + N_DEV - k) % N_DEV, self.nb[t])


class _ScatterRoute:
    def __init__(self, grads, kinds):
        self.kinds = kinds
        self.nb = [g.shape[1 if k == "col" else 0] // N_DEV for g, k in zip(grads, kinds)]

    def lands(self, grads):
        return [lax.empty((N_DEV - 1, g.shape[0], nb) if k == "col" else (N_DEV - 1, nb, g.shape[1]), g.dtype)
                for g, k, nb in zip(grads, self.kinds, self.nb)]

    def sent(self, t, src_refs, me, k):
        return _block_of(src_refs[t], self.kinds[t], (me + k) % N_DEV, self.nb[t])

    def lands_at(self, t, land_refs, me, k):
        return land_refs[t].at[k - 1]


def _place_own(shard, full, me, kind, *, name):
    r, c = shard.shape
    tr = _row_tile(r)
    if kind == "col":
        dst = pl.BlockSpec((tr, c), lambda i, me_ref: (i, me_ref[0]))
    else:
        dst = pl.BlockSpec((tr, c), lambda i, me_ref: (me_ref[0] * (r // tr) + i, 0))

    def body(me_ref, shard_ref, full_ref, out_ref):
        out_ref[...] = shard_ref[...]

    return pl.pallas_call(
        body, name=name, out_shape=jax.ShapeDtypeStruct(full.shape, full.dtype),
        grid_spec=pltpu.PrefetchScalarGridSpec(
            num_scalar_prefetch=1, grid=(r // tr,),
            in_specs=[pl.BlockSpec((tr, c), lambda i, me_ref: (i, 0)), pl.BlockSpec(memory_space=pl.ANY)],
            out_specs=dst),
        input_output_aliases={2: 0}, compiler_params=_params("parallel"),
    )(me, shard, full)


_HBM = pl.BlockSpec(memory_space=pltpu.HBM)
_SEM = pl.BlockSpec(memory_space=pltpu.SEMAPHORE)
_FLOWING = pltpu.SideEffectType.DATAFLOW_SIDE_EFFECTING


def _exchange_start(srcs, lands, route, after, *, name):
    n = len(srcs)

    def body(*refs):
        src_refs, land_refs = refs[:n], refs[n:2 * n]
        send_sems, recv_sems = refs[2 * n + 1:2 * n + 3]
        token = refs[-1]
        _, _, _, me = _position()
        for t in range(n):
            for k in range(1, N_DEV):
                p = (me + k) % N_DEV
                pltpu.make_async_remote_copy(
                    src_ref=route.sent(t, src_refs, me, k), dst_ref=route.lands_at(t, land_refs, p, k),
                    send_sem=send_sems.at[t * N_DEV + k], recv_sem=recv_sems.at[t * N_DEV + k], device_id=_coords(p),
                    device_id_type=MESH).start()
        token[...] = jnp.zeros_like(token)

    hbm = lambda a: pltpu.HBM(a.shape, a.dtype)
    sems = pltpu.SemaphoreType.DMA((n * N_DEV,))
    out = pl.pallas_call(
        body, name=name,
        out_shape=(sems, sems, *[hbm(a) for a in srcs], *[hbm(a) for a in lands],
                   jax.ShapeDtypeStruct((SUBLANE, LANE), F32)),
        in_specs=[_HBM] * (2 * n) + [pl.BlockSpec(memory_space=pl.ANY)],
        out_specs=(_SEM, _SEM, *[_HBM] * (2 * n), pl.BlockSpec(memory_space=pltpu.VMEM)),
        input_output_aliases={i: 2 + i for i in range(2 * n)},
        compiler_params=pltpu.CompilerParams(has_side_effects=_FLOWING),
    )(*[pltpu.with_memory_space_constraint(a, pltpu.HBM) for a in (*srcs, *lands)], after)
    return out[0], out[1], out[2:2 + n], out[2 + n:2 + 2 * n], out[-1]


def _exchange_wait(started, route, after, *, name):
    send_sems, recv_sems, srcs, lands, _ = started
    n = len(srcs)

    def body(*refs):
        src_refs, land_refs = refs[:n], refs[n:2 * n]
        send_ref, recv_ref = refs[2 * n:2 * n + 2]
        _, _, _, me = _position()
        for t in range(n):
            for k in range(1, N_DEV):
                cp = pltpu.make_async_remote_copy(
                    src_ref=route.sent(t, src_refs, me, k), dst_ref=route.lands_at(t, land_refs, me, k),
                    send_sem=send_ref.at[t * N_DEV + k], recv_sem=recv_ref.at[t * N_DEV + k],
                    device_id=_coords((me + N_DEV - k) % N_DEV), device_id_type=MESH)
                cp.wait_send()
                cp.wait_recv()

    hbm = lambda a: pltpu.HBM(a.shape, a.dtype)
    out = pl.pallas_call(
        body, name=name, out_shape=(*[hbm(a) for a in srcs], *[hbm(a) for a in lands]),
        in_specs=[_HBM] * (2 * n) + [_SEM, _SEM, pl.BlockSpec(memory_space=pl.ANY)],
        out_specs=tuple([_HBM] * (2 * n)), input_output_aliases={i: i for i in range(2 * n)},
        compiler_params=pltpu.CompilerParams(has_side_effects=_FLOWING),
    )(*srcs, *lands, send_sems, recv_sems, after)
    return list(out[:n]), list(out[n:])


def _all_reduce(part, *, name):
    _, r, _ = part.shape

    def body(part_ref, tot_ref, recv_ref, send1, recv1, send2, recv2):
        _, _, _, me = _position()

        def scatter(k, to_me=False):
            p = (me + N_DEV - k) % N_DEV if to_me else (me + k) % N_DEV
            return pltpu.make_async_remote_copy(
                src_ref=part_ref.at[me if to_me else p], dst_ref=recv_ref.at[p if to_me else me],
                send_sem=send1.at[k], recv_sem=recv1.at[k], device_id=_coords(p), device_id_type=MESH)

        def gather(k, to_me=False):
            p = (me + N_DEV - k) % N_DEV if to_me else (me + k) % N_DEV
            return pltpu.make_async_remote_copy(
                src_ref=tot_ref.at[me], dst_ref=tot_ref.at[p if to_me else me],
                send_sem=send2.at[k], recv_sem=recv2.at[k], device_id=_coords(p), device_id_type=MESH)

        for k in range(1, N_DEV):
            scatter(k).start()
        recv_ref[me] = part_ref[me]
        for k in range(1, N_DEV):
            scatter(k, to_me=True).wait_recv()
        total = recv_ref[0]
        for q in range(1, N_DEV):
            total = total + recv_ref[q]
        tot_ref[me] = total
        for k in range(1, N_DEV):
            gather(k).start()
        for k in range(1, N_DEV):
            gather(k, to_me=True).wait_recv()
        for k in range(1, N_DEV):
            scatter(k).wait_send()
            gather(k).wait_send()

    vmem = pl.BlockSpec(memory_space=pltpu.VMEM)
    return pl.pallas_call(
        body, name=name, in_specs=[vmem], out_specs=vmem, out_shape=jax.ShapeDtypeStruct(part.shape, F32),
        scratch_shapes=[pltpu.VMEM(part.shape, F32)] + [pltpu.SemaphoreType.DMA((N_DEV,))] * 4,
        compiler_params=pltpu.CompilerParams(has_side_effects=True, vmem_limit_bytes=VMEM_LIMIT),
    )(part)


def _round_up(n, m):
    return (n + m - 1) // m * m


def _row_tile(rows):
    return next(t for t in (256, 128, 64, 32, 16) if rows % t == 0)


def _local_step(x, target, small, depth, weights_of, grads_done, *, qkw):
    l, d = x.shape
    groups = d // SSM_GROUP
    tiles = groups // TILE_GROUPS
    half = qkw // RET_HEADS // 2
    cb0 = 2 * qkw // d
    inv = 1.0 / (ROPE_BASE ** (jnp.arange(half, dtype=F32) / half))
    ang = jnp.arange(l, dtype=F32)[:, None] * inv[None, :]
    cos, sin = jnp.cos(ang), jnp.sin(ang)
    rep = jnp.repeat(jnp.eye(SSM_STATE, dtype=F32), SSM_GROUP, axis=1)
    row2 = lambda a: a.reshape(1, -1)

    saved = []
    for i in range(depth):
        full_i, x = weights_of(i, x)
        w_in, w_glu, w_out, w_gate, w_up, w_down = full_i
        n = f"l{i}_"
        g_mix, g_ffn = row2(small["ln_mix_g"][i]), row2(small["ln_ffn_g"][i])
        dskip, b_glu = row2(small["ssm_d"][i]), row2(small["b_glu"][i])
        lg = small["ret_log_gamma"][i]
        h = _norm_fwd(x, g_mix, name=n + "norm_mix")
        proj = _matmul(h, w_in, mode="nn", out_dtype=F32, name=n + "proj")
        q_rot, k_rot, v_bf = _ret_prep(proj, cos, sin, qkw=qkw, d=d, name=n + "ret_prep")
        y_raw = _ret_fwd(lg, q_rot, k_rot, v_bf, name=n + "ret_fwd")
        par = [small["ssm_a_re"][i].reshape(2 * groups, SSM_STATE), small["ssm_a_im"][i].reshape(2 * groups, SSM_STATE),
               small["ssm_log_dt"][i].reshape(2 * groups, 1),
               small["ssm_b_re"][i].reshape(2 * groups, SSM_STATE * SSM_GROUP),
               small["ssm_b_im"][i].reshape(2 * groups, SSM_STATE * SSM_GROUP), rep]
        lam_re, lam_im, bbar_re, bbar_im = _s5_prep(*par, name=n + "s5_prep")
        s5 = [_b_tiles(bbar_re, tiles).astype(BF16), _b_tiles(bbar_im, tiles).astype(BF16),
              lam_re.reshape(2, tiles, 1, TILE_N), lam_im.reshape(2, tiles, 1, TILE_N),
              _c_tiles(small["ssm_c_re"][i], tiles).astype(BF16), _c_tiles(small["ssm_c_im"][i], tiles).astype(BF16)]
        u_seg = _to_segments(proj[:, (cb0 + 2) * d:(cb0 + 3) * d])
        y_seg = _s5_fwd(u_seg, *s5, d=0, name=n + "s5_fwd_f")
        y_seg = _s5_fwd(u_seg, *s5, d=1, add=y_seg, name=n + "s5_fwd_b")
        y_s5 = _from_segments(y_seg)
        ys, ys_bf = _s5_post(y_s5, proj, cb0 + 2, dskip, name=n + "s5_post")
        glu = _matmul(ys_bf, w_glu, mode="nn", out_dtype=F32, name=n + "glu")
        merged = _merge(y_raw, proj, ys, glu, b_glu, cb0=cb0, name=n + "merge")
        x1 = _matmul(merged, w_out, mode="nn", out_dtype=F32, res=x, name=n + "out")
        h2 = _norm_fwd(x1, g_ffn, name=n + "norm_ffn")
        gate = _matmul(h2, w_gate, mode="nn", out_dtype=F32, name=n + "gate")
        up = _matmul(h2, w_up, mode="nn", out_dtype=F32, name=n + "up")
        act = _ffn_act(gate, up, name=n + "act")
        x2 = _matmul(act, w_down, mode="nn", out_dtype=F32, res=x1, name=n + "down")
        saved.append(dict(full=full_i, x=x, h=h, proj=proj, q_rot=q_rot, k_rot=k_rot, v_bf=v_bf, y_raw=y_raw, par=par, s5=s5,
                          u_seg=u_seg, y_s5=y_s5, ys=ys, ys_bf=ys_bf, glu=glu, merged=merged, x1=x1, h2=h2, gate=gate,
                          up=up, act=act))
        x = x2

    dx, dg_final, loss = _final(x, row2(small["ln_final_g"]), target, name="final")

    sg = {k: [None] * depth for k in ("ln_mix_g", "ret_log_gamma", "ssm_a_re", "ssm_a_im", "ssm_log_dt", "ssm_b_re",
                                      "ssm_b_im", "ssm_c_re", "ssm_c_im", "ssm_d", "b_glu", "ln_ffn_g")}
    for i in reversed(range(depth)):
        s = saved[i]
        big = [None] * len(BIG)
        w_in, w_glu, w_out, w_gate, w_up, w_down = s["full"]
        n = f"l{i}_b_"
        g_mix, g_ffn = row2(small["ln_mix_g"][i]), row2(small["ln_ffn_g"][i])
        dskip, b_glu = row2(small["ssm_d"][i]), row2(small["b_glu"][i])
        lg = small["ret_log_gamma"][i]
        dact = _matmul(dx, w_down, mode="nt", out_dtype=F32, name=n + "dact")
        big[5] = _matmul(s["act"], dx, mode="tn", out_dtype=BF16, name=n + "dw_down")
        dgate, dup = _ffn_act_bwd(s["gate"], s["up"], dact, name=n + "act")
        dh2 = _matmul(dgate, w_gate, mode="nt", out_dtype=F32, name=n + "dh2_gate")
        dh2 = _matmul(dup, w_up, mode="nt", out_dtype=F32, res=dh2, name=n + "dh2_up")
        big[3] = _matmul(s["h2"], dgate, mode="tn", out_dtype=BF16, name=n + "dw_gate")
        big[4] = _matmul(s["h2"], dup, mode="tn", out_dtype=BF16, name=n + "dw_up")
        dh2 = grads_done(i, FFN_GROUP, [big[t] for t in FFN_GROUP], dh2)
        dx1, dgf = _norm_bwd(s["x1"], g_ffn, dh2, dx, name=n + "norm_ffn")
        sg["ln_ffn_g"][i] = dgf[0]
        dmerged = _matmul(dx1, w_out, mode="nt", out_dtype=F32, name=n + "dmerged")
        big[2] = _matmul(s["merged"], dx1, mode="tn", out_dtype=BF16, name=n + "dw_out")
        dy_raw, dg, dgate_r, dgate_s, dglu, dys_a, db_glu = _merge_bwd(
            s["y_raw"], s["proj"], s["ys"], s["glu"], b_glu, dmerged, cb0=cb0, name=n + "merge")
        sg["b_glu"][i] = db_glu[0]
        dys = _matmul(dglu, w_glu, mode="nt", out_dtype=F32, res=dys_a, name=n + "dys")
        big[1] = _matmul(s["ys_bf"], dglu, mode="tn", out_dtype=BF16, name=n + "dw_glu")
        dpre, dd = _s5_post_bwd(s["y_s5"], s["proj"], cb0 + 2, dskip, dys, name=n + "s5_post")
        sg["ssm_d"][i] = dd[0]
        dpre_seg = _to_segments(dpre)
        r_f = _s5_bwd(s["u_seg"], dpre_seg, *s["s5"], d=0, name=n + "s5_bwd_f")
        r_b = _s5_bwd(s["u_seg"], dpre_seg, *s["s5"], d=1, add=r_f[0], name=n + "s5_bwd_b")
        du = _du_combine(dpre, dskip, _from_segments(r_b[0]), name=n + "du")
        both = lambda k: jnp.stack([r_f[k], r_b[k]])
        cts = [both(5).reshape(2 * groups, SSM_STATE), both(6).reshape(2 * groups, SSM_STATE),
               _b_untile(both(1), tiles), _b_untile(both(2), tiles)]
        da_re, da_im, dldt, db_re, db_im = _s5_prep_bwd(*s["par"], cts, name=n + "s5_prep")
        sg["ssm_a_re"][i] = da_re.reshape(2, groups, SSM_STATE)
        sg["ssm_a_im"][i] = da_im.reshape(2, groups, SSM_STATE)
        sg["ssm_log_dt"][i] = dldt.reshape(2, groups)
        sg["ssm_b_re"][i] = db_re.reshape(2, groups, SSM_STATE, SSM_GROUP)
        sg["ssm_b_im"][i] = db_im.reshape(2, groups, SSM_STATE, SSM_GROUP)
        sg["ssm_c_re"][i] = _c_untile(both(3), tiles)
        sg["ssm_c_im"][i] = _c_untile(both(4), tiles)
        dq_rot, dk_rot, dv, dlg = _ret_bwd(lg, s["q_rot"], s["k_rot"], s["v_bf"], dy_raw, name=n + "ret_bwd")
        sg["ret_log_gamma"][i] = dlg[:, :2, 0].T
        dq, dk = _ret_prep_bwd(dq_rot, dk_rot, cos, sin, name=n + "ret_prep")
        dproj = jnp.concatenate([dq, dk, dv.astype(BF16), dg, du, dgate_r, dgate_s], axis=1)
        dh = _matmul(dproj, w_in, mode="nt", out_dtype=F32, name=n + "dh")
        big[0] = _matmul(s["h"], dproj, mode="tn", out_dtype=BF16, name=n + "dw_in")
        dx, dgm = _norm_bwd(s["x"], g_mix, dh, dx1, name=n + "norm_mix")
        sg["ln_mix_g"][i] = dgm[0]
        dx = grads_done(i, MIX_GROUP, [big[t] for t in MIX_GROUP], dx)

    small_grads = {k: jnp.stack(v) for k, v in sg.items()}
    small_grads["ln_final_g"] = dg_final[0]
    return loss, dx, small_grads


BIG = ("w_in", "w_glu", "w_out", "w_ffn_gate", "w_ffn_up", "w_ffn_down")
BIG_KINDS = ("col", "row", "row", "col", "col", "row")
MIX_GROUP = (0, 1, 2)
FFN_GROUP = (3, 4, 5)
SMALL = ("ln_mix_g", "ret_log_gamma", "ssm_a_re", "ssm_a_im", "ssm_log_dt", "ssm_b_re", "ssm_b_im", "ssm_c_re",
         "ssm_c_im", "ssm_d", "b_glu", "ln_ffn_g", "ln_final_g")
WEIGHTS = ("ln_mix_g", "w_in", "ret_log_gamma", "ssm_a_re", "ssm_a_im", "ssm_log_dt", "ssm_b_re", "ssm_b_im",
           "ssm_c_re", "ssm_c_im", "ssm_d", "w_glu", "b_glu", "w_out", "ln_ffn_g", "w_ffn_gate", "w_ffn_up",
           "w_ffn_down", "ln_final_g")


def _pad_to(a, axis, size):
    pad = [(0, 0)] * a.ndim
    pad[axis] = (0, size - a.shape[axis])
    return jnp.pad(a, pad)


def _flatten_small(tree, extra):
    def as_rows(a):
        a = a.reshape(-1).astype(F32)
        return _pad_to(a, 0, _round_up(a.shape[0], SUBLANE * LANE)).reshape(-1, LANE)

    flat = jnp.concatenate([as_rows(tree[k]) for k in SMALL] + [as_rows(extra)])
    return _pad_to(flat, 0, _round_up(flat.shape[0], FLAT_ROWS))


def _unflatten_small(flat, like):
    out, at = {}, 0
    for k in SMALL:
        n = like[k].size
        rows = _round_up(n, SUBLANE * LANE) // LANE
        out[k] = flat[at:at + rows].reshape(-1)[:n].reshape(like[k].shape)
        at += rows
    return out, flat[at, 0]


def kernel(x, ln_mix_g, w_in, ret_log_gamma, ssm_a_re, ssm_a_im, ssm_log_dt, ssm_b_re, ssm_b_im, ssm_c_re, ssm_c_im, ssm_d, w_glu, b_glu, w_out, ln_ffn_g, w_ffn_gate, w_ffn_up, w_ffn_down, ln_final_g, loss_target, m_ln_mix_g, m_w_in, m_ret_log_gamma, m_ssm_a_re, m_ssm_a_im, m_ssm_log_dt, m_ssm_b_re, m_ssm_b_im, m_ssm_c_re, m_ssm_c_im, m_ssm_d, m_w_glu, m_b_glu, m_w_out, m_ln_ffn_g, m_w_ffn_gate, m_w_ffn_up, m_w_ffn_down, m_ln_final_g, v_ln_mix_g, v_w_in, v_ret_log_gamma, v_ssm_a_re, v_ssm_a_im, v_ssm_log_dt, v_ssm_b_re, v_ssm_b_im, v_ssm_c_re, v_ssm_c_im, v_ssm_d, v_w_glu, v_b_glu, v_w_out, v_ln_ffn_g, v_w_ffn_gate, v_w_ffn_up, v_w_ffn_down, v_ln_final_g):
    w = dict(ln_mix_g=ln_mix_g, w_in=w_in, ret_log_gamma=ret_log_gamma, ssm_a_re=ssm_a_re, ssm_a_im=ssm_a_im, ssm_log_dt=ssm_log_dt, ssm_b_re=ssm_b_re, ssm_b_im=ssm_b_im, ssm_c_re=ssm_c_re, ssm_c_im=ssm_c_im, ssm_d=ssm_d, w_glu=w_glu, b_glu=b_glu, w_out=w_out, ln_ffn_g=ln_ffn_g, w_ffn_gate=w_ffn_gate, w_ffn_up=w_ffn_up, w_ffn_down=w_ffn_down, ln_final_g=ln_final_g)
    m = dict(ln_mix_g=m_ln_mix_g, w_in=m_w_in, ret_log_gamma=m_ret_log_gamma, ssm_a_re=m_ssm_a_re, ssm_a_im=m_ssm_a_im, ssm_log_dt=m_ssm_log_dt, ssm_b_re=m_ssm_b_re, ssm_b_im=m_ssm_b_im, ssm_c_re=m_ssm_c_re, ssm_c_im=m_ssm_c_im, ssm_d=m_ssm_d, w_glu=m_w_glu, b_glu=m_b_glu, w_out=m_w_out, ln_ffn_g=m_ln_ffn_g, w_ffn_gate=m_w_ffn_gate, w_ffn_up=m_w_ffn_up, w_ffn_down=m_w_ffn_down, ln_final_g=m_ln_final_g)
    v = dict(ln_mix_g=v_ln_mix_g, w_in=v_w_in, ret_log_gamma=v_ret_log_gamma, ssm_a_re=v_ssm_a_re, ssm_a_im=v_ssm_a_im, ssm_log_dt=v_ssm_log_dt, ssm_b_re=v_ssm_b_re, ssm_b_im=v_ssm_b_im, ssm_c_re=v_ssm_c_re, ssm_c_im=v_ssm_c_im, ssm_d=v_ssm_d, w_glu=v_w_glu, b_glu=v_b_glu, w_out=v_w_out, ln_ffn_g=v_ln_ffn_g, w_ffn_gate=v_w_ffn_gate, w_ffn_up=v_w_ffn_up, w_ffn_down=v_w_ffn_down, ln_final_g=v_ln_final_g)
    depth, d, nb_in = w_in.shape
    qkw = (nb_in * N_DEV - 5 * d) // 2
    nb_ffn = w_ffn_gate.shape[2]
    nb_pad = _round_up(nb_ffn, LANE)
    pad_axis = {"w_ffn_gate": 2, "w_ffn_up": 2, "w_ffn_down": 1}

    assert depth == 2
    padded = {k: w[k] if k not in pad_axis else _pad_to(w[k], pad_axis[k], nb_pad) for k in BIG}
    shards = [[padded[k][i].astype(BF16) for k in BIG] for i in range(depth)]
    full0 = _all_gather(shards[0], BIG_KINDS, name="gather_l0")
    me = (4 * lax.axis_index("x") + 2 * lax.axis_index("y") + lax.axis_index("c")).astype(jnp.int32).reshape(1)
    gather1 = _GatherRoute(shards[1], BIG_KINDS)
    gathering = _exchange_start(shards[1], gather1.lands(shards[1], me), gather1, full0[0], name="gather_l1_start")

    def weights_of(i, act):
        if i == 0:
            return full0, act + gathering[4][0, 0]
        return _exchange_wait(gathering, gather1, act, name="gather_l1_wait")[1], act

    exchanges, held = [], []

    def grads_done(i, group, grads, act):
        route = _ScatterRoute(grads, [BIG_KINDS[t] for t in group])
        if i == 0 and group == MIX_GROUP:
            held.append((i, group, grads, route))
            return act
        started = _exchange_start(grads, route.lands(grads), route, act, name=f"scatter_l{i}_{group[0]}_start")
        exchanges.append((i, group, started, route))
        return act + started[4][0, 0]

    small = {k: w[k] for k in SMALL}
    loss, dx, small_grads = _local_step(x[0], loss_target[0], small, depth, weights_of, grads_done, qkw=qkw)
    arrived, tie = {}, dx
    for i, group, started, route in exchanges:
        own, recv = _exchange_wait(started, route, tie, name=f"scatter_l{i}_{group[0]}_wait")
        arrived.update({(i, t): (own[j], recv[j]) for j, t in enumerate(group)})
        tie = recv[0]

    part = _flatten_small(small_grads, loss[0, :1])
    rows = part.shape[0]
    total = _all_reduce(part.reshape(N_DEV, rows // N_DEV, LANE), name="reduce_small").reshape(rows, LANE)
    (i, group, held_grads, route), = held
    started = _exchange_start(held_grads, route.lands(held_grads), route, total, name=f"scatter_l{i}_{group[0]}_start")
    zero = jnp.zeros((1,), F32)
    flat = [_flatten_small({k: a[k] for k in SMALL}, zero) for a in (w, m, v)]
    upd = _adam_flat(total + started[4][0, 0], *flat, name="adam_small")
    grads, delta, new_m, new_v = {}, {}, {}, {}
    g_small, loss_total = _unflatten_small(total, small)
    grads.update(g_small)
    for dst, u in zip((delta, new_m, new_v), upd):
        dst.update(_unflatten_small(u, small)[0])

    ops = {k: [_pad_to(a[k], 2, nb_pad) for a in (w, m, v)] if k in ("w_ffn_gate", "w_ffn_up") else [a[k] for a in (w, m, v)]
           for k in BIG}
    half, tie = {}, upd[0]

    def adam(layer, t, tie):
        k = BIG[t]
        own, recv = arrived[layer, t]
        return _adam_shard(recv, own, me, *ops[k], kind=BIG_KINDS[t], layer=layer, name=f"adam_l{layer}_{k}",
                           tr=_row_tile(ops[k][0].shape[1]), tie=tie, others=half.get(k))

    for t in range(len(BIG)):
        half[BIG[t]] = adam(1, t, tie)
        tie = half[BIG[t]][0]
    for t in FFN_GROUP:
        half[BIG[t]] = adam(0, t, tie)
        tie = half[BIG[t]][0]
    own, recv = _exchange_wait(started, route, tie, name=f"scatter_l{i}_{group[0]}_wait")
    arrived.update({(i, t): (own[j], recv[j]) for j, t in enumerate(group)})
    for t in MIX_GROUP:
        half[BIG[t]] = adam(0, t, None)
    for k in BIG:
        res = half[k]
        if k in ("w_ffn_gate", "w_ffn_up"):
            res = [r[:, :, :nb_ffn] for r in res]
        grads[k], delta[k], new_m[k], new_v[k] = res

    return (loss_total, dx[None], *[grads[k] for k in WEIGHTS], *[delta[k] for k in WEIGHTS],
            *[new_m[k] for k in WEIGHTS], *[new_v[k] for k in WEIGHTS])
```

```python
import math

import jax
import jax.numpy as jnp
from jax import lax
from jax.experimental import pallas as pl
from jax.experimental.pallas import tpu as pltpu

F32 = jnp.float32
BF16 = jnp.bfloat16
MESH = pl.DeviceIdType.MESH

N_DEV = 8
RET_HEADS = 4
CHUNK = 128
ROPE_BASE = 10000.0
SSM_GROUP = 16
SSM_STATE = 64
TILE_GROUPS = 8
TILE_U = TILE_GROUPS * SSM_GROUP
TILE_N = TILE_GROUPS * SSM_STATE
LANE = 128
SUBLANE = 8
N_SEG = SUBLANE
N_LT = TILE_N // LANE
SCAN_UNROLL = 4
FLAT_ROWS = 1024
EPS = 1e-6
ADAM_LR = 0.001
ADAM_B1 = 0.9
ADAM_B2 = 0.999
ADAM_EPS = 1e-08
ADAM_WD = 0.01
ADAM_STEP = 10
VMEM_LIMIT = 56 * 1024 * 1024


def _params(*sem):
    return pltpu.CompilerParams(dimension_semantics=sem or None, vmem_limit_bytes=VMEM_LIMIT)


def _dg(a, b, ca, cb):
    return lax.dot_general(a.astype(BF16), b.astype(BF16), (((ca,), (cb,)), ((), ())),
                           preferred_element_type=F32)


@jax.custom_vjp
def _dnn(a, b):
    return _dg(a, b, 1, 0)


@jax.custom_vjp
def _dnt(a, b):
    return _dg(a, b, 1, 1)


@jax.custom_vjp
def _dtn(a, b):
    return _dg(a, b, 0, 0)


_dnn.defvjp(lambda a, b: (_dnn(a, b), (a, b)), lambda r, g: (_dnt(g, r[1]), _dtn(r[0], g)))
_dnt.defvjp(lambda a, b: (_dnt(a, b), (a, b)), lambda r, g: (_dnn(g, r[1]), _dtn(g, r[0])))
_dtn.defvjp(lambda a, b: (_dtn(a, b), (a, b)), lambda r, g: (_dnt(r[1], g), _dnn(r[0], g)))


def _matmul(a, b, *, mode, out_dtype, name, res=None, tm=1024, tn=1024, tk=2048):
    if mode == "nn":
        (m, k), n = a.shape, b.shape[1]
    elif mode == "nt":
        (m, k), n = a.shape, b.shape[0]
    else:
        (k, m), n = a.shape, b.shape[1]
    tm, tn = min(tm, m), min(tn, n)
    tk = next(t for t in (tk, 1024, 512, 256, LANE) if k % t == 0) if k > tk else k
    assert m % tm == 0 and n % tn == 0 and k % tk == 0, (name, m, n, k)
    nk = k // tk
    if mode == "tn":
        a_spec = pl.BlockSpec((tk, tm), lambda i, j, kk: (kk, i))
    else:
        a_spec = pl.BlockSpec((tm, tk), lambda i, j, kk: (i, kk))
    if mode == "nt":
        b_spec = pl.BlockSpec((tn, tk), lambda i, j, kk: (j, kk))
    else:
        b_spec = pl.BlockSpec((tk, tn), lambda i, j, kk: (kk, j))
    ca, cb = {"nn": (1, 0), "nt": (1, 1), "tn": (0, 0)}[mode]
    o_spec = pl.BlockSpec((tm, tn), lambda i, j, kk: (i, j))
    has_res = res is not None

    def body(*refs):
        a_ref, b_ref = refs[:2]
        r_ref = refs[2] if has_res else None
        o_ref = refs[2 + has_res]
        part = _dg(a_ref[...], b_ref[...], ca, cb)
        if nk == 1:
            o_ref[...] = (part + r_ref[...] if has_res else part).astype(out_dtype)
            return
        acc = refs[-1]
        kk = pl.program_id(2)

        @pl.when(kk == 0)
        def _():
            acc[...] = part + r_ref[...] if has_res else part

        @pl.when(kk > 0)
        def _():
            acc[...] += part

        @pl.when(kk == nk - 1)
        def _():
            o_ref[...] = acc[...].astype(out_dtype)

    return pl.pallas_call(
        body, name=name, grid=(m // tm, n // tn, nk),
        in_specs=[a_spec, b_spec] + ([o_spec] if has_res else []),
        out_specs=o_spec, out_shape=jax.ShapeDtypeStruct((m, n), out_dtype),
        scratch_shapes=[pltpu.VMEM((tm, tn), F32)] if nk > 1 else [],
        compiler_params=_params("parallel", "parallel", "arbitrary"),
    )(*((a, b, res) if has_res else (a, b)))


def _row(arr, tl, width=None, cb=0):
    width = arr.shape[1] if width is None else width
    return arr, pl.BlockSpec((tl, width), lambda i, cb=cb: (i, cb))


def _par(arr):
    return arr, pl.BlockSpec(arr.shape, lambda i: (0,) * arr.ndim)


def _rowwise(body, *, rows, tl, ins, outs, name):
    arrays = [a for a, _ in ins]
    in_specs = [s for _, s in ins]
    out_shape, out_specs, acc_ids = [], [], []
    for n, o in enumerate(outs):
        if o[0] == "row":
            out_shape.append(jax.ShapeDtypeStruct((rows, o[1]), o[2]))
            out_specs.append(pl.BlockSpec((tl, o[1]), lambda i: (i, 0)))
        else:
            out_shape.append(jax.ShapeDtypeStruct((1, o[1]), F32))
            out_specs.append(pl.BlockSpec((1, o[1]), lambda i: (0, 0)))
            acc_ids.append(n)
    n_in = len(arrays)
    assert rows % tl == 0, (name, rows, tl)

    def wrapped(*refs):
        in_refs, out_refs = refs[:n_in], refs[n_in:]

        @pl.when(pl.program_id(0) == 0)
        def _():
            for n in acc_ids:
                out_refs[n][...] = jnp.zeros_like(out_refs[n])

        body(in_refs, out_refs)

    return pl.pallas_call(
        wrapped, name=name, grid=(rows // tl,), in_specs=in_specs, out_specs=out_specs,
        out_shape=out_shape, compiler_params=_params("arbitrary"),
    )(*arrays)


def _rms(x, g):
    return x * lax.rsqrt(jnp.mean(x * x, axis=-1, keepdims=True) + EPS) * g


def _norm_fwd(x, g, *, name, tl=256):
    def body(i, o):
        o[0][...] = _rms(i[0][...], i[1][...]).astype(BF16)

    return _rowwise(body, rows=x.shape[0], tl=tl, ins=[_row(x, tl), _par(g)],
                    outs=[("row", x.shape[1], BF16)], name=name)[0]


def _norm_bwd(x, g, dh, dres, *, name, tl=256):
    def body(i, o):
        _, vjp = jax.vjp(_rms, i[0][...], i[1][...])
        dx, dg = vjp(i[2][...])
        o[0][...] = i[3][...] + dx
        o[1][...] += dg

    d = x.shape[1]
    return _rowwise(body, rows=x.shape[0], tl=tl, ins=[_row(x, tl), _par(g), _row(dh, tl), _row(dres, tl)],
                    outs=[("row", d, F32), ("acc", d)], name=name)


def _final(x, g, target, *, name, tl=256):
    d = x.shape[1]

    def body(i, o):
        y, vjp = jax.vjp(_rms, i[0][...], i[1][...])
        err = y - i[2][...]
        dx, dg = vjp(err * (1.0 / d))
        o[0][...] = dx
        o[1][...] += dg
        o[2][...] += jnp.full((1, LANE), 0.5 / d, F32) * jnp.sum(err * err)

    return _rowwise(body, rows=x.shape[0], tl=tl, ins=[_row(x, tl), _par(g), _row(target, tl)],
                    outs=[("row", d, F32), ("acc", d), ("acc", LANE)], name=name)


def _rot(x, cos, sin, out_ref, col, scale=1.0, inverse=False):
    x1, x2 = x[:, :LANE], x[:, LANE:]
    if inverse:
        sin = -sin
    out_ref[:, col:col + LANE] = ((x1 * cos - x2 * sin) * scale).astype(out_ref.dtype)
    out_ref[:, col + LANE:col + 2 * LANE] = ((x1 * sin + x2 * cos) * scale).astype(out_ref.dtype)


def _ret_prep(proj, cos, sin, *, qkw, d, name, tl=256):
    dk = qkw // RET_HEADS
    assert dk == 2 * LANE and (2 * qkw) % d == 0

    def body(i, o):
        c, s = i[3][...], i[4][...]
        for h in range(RET_HEADS):
            _rot(i[0][:, h * dk:(h + 1) * dk], c, s, o[0], h * dk)
            _rot(i[1][:, h * dk:(h + 1) * dk], c, s, o[1], h * dk, scale=dk ** -0.5)
        o[2][...] = i[2][...].astype(BF16)

    return _rowwise(body, rows=proj.shape[0], tl=tl,
                    ins=[_row(proj, tl, qkw, 0), _row(proj, tl, qkw, 1), _row(proj, tl, d, 2 * qkw // d),
                         _row(cos, tl), _row(sin, tl)],
                    outs=[("row", qkw, BF16), ("row", qkw, BF16), ("row", d, BF16)], name=name)


def _ret_prep_bwd(dq_rot, dk_rot, cos, sin, *, name, tl=256):
    qkw = dq_rot.shape[1]
    dk = qkw // RET_HEADS

    def body(i, o):
        c, s = i[2][...], i[3][...]
        for h in range(RET_HEADS):
            _rot(i[0][:, h * dk:(h + 1) * dk], c, s, o[0], h * dk, inverse=True)
            _rot(i[1][:, h * dk:(h + 1) * dk], c, s, o[1], h * dk, scale=dk ** -0.5, inverse=True)

    return _rowwise(body, rows=dq_rot.shape[0], tl=tl,
                    ins=[_row(dq_rot, tl), _row(dk_rot, tl), _row(cos, tl), _row(sin, tl)],
                    outs=[("row", qkw, BF16), ("row", qkw, BF16)], name=name)


def _ret_weights(lgf, lgb):
    t = lax.broadcasted_iota(jnp.int32, (CHUNK, 1), 0).astype(F32)
    diff = (lax.broadcasted_iota(jnp.int32, (CHUNK, CHUNK), 0)
            - lax.broadcasted_iota(jnp.int32, (CHUNK, CHUNK), 1)).astype(F32)
    dmat = jnp.exp(jnp.where(diff >= 0, lgf * diff, -lgb * diff))
    return dict(dmat=dmat, wqf=jnp.exp(lgf * (t + 1.0)), wkf=jnp.exp(lgf * (CHUNK - 1.0 - t)),
                wqb=jnp.exp(lgb * (CHUNK - t)), wkb=jnp.exp(lgb * t))


def _ret_f_part(q, k, v, lgf, lgb, s_f):
    w = _ret_weights(lgf, lgb)
    y = _dnn(_dnt(q, k) * w["dmat"], v) + _dnn(q * w["wqf"], s_f)
    return y, _dtn(k * w["wkf"], v)


def _ret_b_part(q, k, v, lgb, s_b):
    w = _ret_weights(lgb, lgb)
    return _dnn(q * w["wqb"], s_b), _dtn(k * w["wkb"], v)


def _chunk(c):
    return pl.ds(pl.multiple_of(c * CHUNK, CHUNK), CHUNK)


def _ret_specs(l, qkw, d):
    dk, dv = qkw // RET_HEADS, d // RET_HEADS
    return dk, dv, [pl.BlockSpec(memory_space=pltpu.SMEM),
                    pl.BlockSpec((l, dk), lambda h: (0, h)), pl.BlockSpec((l, dk), lambda h: (0, h)),
                    pl.BlockSpec((l, dv), lambda h: (0, h))]


def _ret_fwd(lg, q, k, v, *, name):
    l, qkw = q.shape
    d = v.shape[1]
    nc = l // CHUNK
    dk, dv, in_specs = _ret_specs(l, qkw, d)

    def body(lg_ref, q_ref, k_ref, v_ref, y_ref, s_ref):
        h = pl.program_id(0)
        lgf = jnp.full((1, 1), lg_ref[0, h], F32)
        lgb = jnp.full((1, 1), lg_ref[1, h], F32)
        dec_f, dec_b = jnp.exp(lgf * CHUNK), jnp.exp(lgb * CHUNK)

        def load(c):
            r = _chunk(c)
            return r, q_ref[r, :].astype(F32), k_ref[r, :].astype(F32), v_ref[r, :].astype(F32)

        s_ref[...] = jnp.zeros_like(s_ref)

        def f_step(c, _):
            r, qc, kc, vc = load(c)
            y, kv = _ret_f_part(qc, kc, vc, lgf, lgb, s_ref[...])
            y_ref[r, :] = y
            s_ref[...] = dec_f * s_ref[...] + kv
            return 0

        lax.fori_loop(0, nc, f_step, 0)
        s_ref[...] = jnp.zeros_like(s_ref)

        def b_step(n, _):
            r, qc, kc, vc = load(nc - 1 - n)
            y, kv = _ret_b_part(qc, kc, vc, lgb, s_ref[...])
            y_ref[r, :] += y
            s_ref[...] = dec_b * s_ref[...] + kv
            return 0

        lax.fori_loop(0, nc, b_step, 0)

    return pl.pallas_call(
        body, name=name, grid=(RET_HEADS,), in_specs=in_specs,
        out_specs=pl.BlockSpec((l, dv), lambda h: (0, h)), out_shape=jax.ShapeDtypeStruct((l, d), F32),
        scratch_shapes=[pltpu.VMEM((dk, dv), F32)], compiler_params=_params("arbitrary"),
    )(lg, q, k, v)


def _ret_bwd(lg, q, k, v, dy, *, name):
    l, qkw = q.shape
    d = v.shape[1]
    nc = l // CHUNK
    dk, dv, in_specs = _ret_specs(l, qkw, d)

    def body(lg_ref, q_ref, k_ref, v_ref, dy_ref, dq_ref, dk_ref, dv_ref, dlg_ref, states, s_ref, sh_ref):
        h = pl.program_id(0)
        lgf = jnp.full((1, 1), lg_ref[0, h], F32)
        lgb = jnp.full((1, 1), lg_ref[1, h], F32)
        dec_f, dec_b = jnp.exp(lgf * CHUNK), jnp.exp(lgb * CHUNK)

        def load(c):
            r = _chunk(c)
            return (r, q_ref[r, :].astype(F32), k_ref[r, :].astype(F32), v_ref[r, :].astype(F32),
                    dy_ref[r, :].astype(F32))

        s_ref[...] = jnp.zeros_like(s_ref)

        def f_states(c, _):
            _, qc, kc, vc, _ = load(c)
            states[c] = s_ref[...]
            w = _ret_weights(lgf, lgb)
            s_ref[...] = dec_f * s_ref[...] + _dtn(kc * w["wkf"], vc)
            return 0

        lax.fori_loop(0, nc, f_states, 0)
        sh_ref[...] = jnp.zeros_like(sh_ref)

        def f_adj(n, carry):
            dlf, dlb, ddec = carry
            c = nc - 1 - n
            r, qc, kc, vc, dyc = load(c)
            sc = states[c]
            _, vjp = jax.vjp(_ret_f_part, qc, kc, vc, lgf, lgb, sc)
            dq, dkk, dvv, g_f, g_b, dsc = vjp((dyc, sh_ref[...]))
            dq_ref[r, :] = dq
            dk_ref[r, :] = dkk
            dv_ref[r, :] = dvv
            ddec = ddec + jnp.sum(sh_ref[...] * sc)
            sh_ref[...] = dsc + dec_f * sh_ref[...]
            return dlf + g_f, dlb + g_b, ddec

        z = jnp.zeros((1, 1), F32)
        dlf, dlb, ddec_f = lax.fori_loop(0, nc, f_adj, (z, z, z))

        s_ref[...] = jnp.zeros_like(s_ref)

        def b_states(n, _):
            c = nc - 1 - n
            _, qc, kc, vc, _ = load(c)
            states[c] = s_ref[...]
            w = _ret_weights(lgb, lgb)
            s_ref[...] = dec_b * s_ref[...] + _dtn(kc * w["wkb"], vc)
            return 0

        lax.fori_loop(0, nc, b_states, 0)
        sh_ref[...] = jnp.zeros_like(sh_ref)

        def b_adj(c, carry):
            dlb, ddec = carry
            r, qc, kc, vc, dyc = load(c)
            sc = states[c]
            _, vjp = jax.vjp(_ret_b_part, qc, kc, vc, lgb, sc)
            dq, dkk, dvv, g_b, dsc = vjp((dyc, sh_ref[...]))
            dq_ref[r, :] += dq
            dk_ref[r, :] += dkk
            dv_ref[r, :] += dvv
            ddec = ddec + jnp.sum(sh_ref[...] * sc)
            sh_ref[...] = dsc + dec_b * sh_ref[...]
            return dlb + g_b, ddec

        dlb, ddec_b = lax.fori_loop(0, nc, b_adj, (dlb, z))
        dlf = dlf + ddec_f * dec_f * CHUNK
        dlb = dlb + ddec_b * dec_b * CHUNK
        row = lax.broadcasted_iota(jnp.int32, (SUBLANE, LANE), 0)
        dlg_ref[...] = jnp.where(row == 0, dlf, jnp.where(row == 1, dlb, 0.0))

    head = lambda w: pl.BlockSpec((l, w), lambda h: (0, h))
    return pl.pallas_call(
        body, name=name, grid=(RET_HEADS,), in_specs=in_specs + [head(dv)],
        out_specs=[head(dk), head(dk), head(dv), pl.BlockSpec((None, SUBLANE, LANE), lambda h: (h, 0, 0))],
        out_shape=[jax.ShapeDtypeStruct((l, qkw), F32), jax.ShapeDtypeStruct((l, qkw), F32),
                   jax.ShapeDtypeStruct((l, d), F32), jax.ShapeDtypeStruct((RET_HEADS, SUBLANE, LANE), F32)],
        scratch_shapes=[pltpu.VMEM((nc, dk, dv), F32), pltpu.VMEM((dk, dv), F32), pltpu.VMEM((dk, dv), F32)],
        compiler_params=_params("arbitrary"),
    )(lg, q, k, v, dy)


def _s5_param_fn(a_re, a_im, log_dt, b_re, b_im, rep):
    dt = jnp.exp(log_dt)
    mag = jnp.exp(a_re * dt)
    lam_re, lam_im = mag * jnp.cos(a_im * dt), mag * jnp.sin(a_im * dt)
    n_re, n_im = lam_re - 1.0, lam_im
    den = a_re * a_re + a_im * a_im
    c_re = (n_re * a_re + n_im * a_im) / den
    c_im = (n_im * a_re - n_re * a_im) / den
    hi = lax.Precision.HIGHEST
    c_re = jnp.dot(c_re, rep, precision=hi, preferred_element_type=F32)
    c_im = jnp.dot(c_im, rep, precision=hi, preferred_element_type=F32)
    return lam_re, lam_im, c_re * b_re - c_im * b_im, c_re * b_im + c_im * b_re


def _s5_param_shapes(a_re, b_re):
    r, p = a_re.shape
    return [jax.ShapeDtypeStruct((r, p), F32)] * 2 + [jax.ShapeDtypeStruct(b_re.shape, F32)] * 2


def _s5_prep(a_re, a_im, log_dt, b_re, b_im, rep, *, name):
    def body(*refs):
        outs = _s5_param_fn(*[r[...] for r in refs[:6]])
        for o_ref, o in zip(refs[6:], outs):
            o_ref[...] = o

    return pl.pallas_call(body, name=name, out_shape=_s5_param_shapes(a_re, b_re),
                          compiler_params=_params())(a_re, a_im, log_dt, b_re, b_im, rep)


def _s5_prep_bwd(a_re, a_im, log_dt, b_re, b_im, rep, cts, *, name):
    def body(*refs):
        ins = [r[...] for r in refs[:6]]
        _, vjp = jax.vjp(lambda *p: _s5_param_fn(*p, ins[5]), *ins[:5])
        grads = vjp(tuple(r[...] for r in refs[6:10]))
        for o_ref, o in zip(refs[10:], grads):
            o_ref[...] = o

    shapes = [jax.ShapeDtypeStruct(t.shape, F32) for t in (a_re, a_im, log_dt, b_re, b_im)]
    return pl.pallas_call(body, name=name, out_shape=shapes,
                          compiler_params=_params())(a_re, a_im, log_dt, b_re, b_im, rep, *cts)


def _eye_tiles():
    return jnp.eye(TILE_GROUPS, dtype=F32)


def _b_tiles(bbar, tiles):
    t = bbar.reshape(2, tiles, TILE_GROUPS, SSM_STATE, SSM_GROUP).transpose(0, 1, 2, 4, 3)
    t = t[:, :, :, :, None, :] * _eye_tiles()[None, None, :, None, :, None]
    return t.reshape(2, tiles, TILE_U, TILE_N)


def _b_untile(dbt, tiles):
    t = dbt.reshape(2, tiles, TILE_GROUPS, SSM_GROUP, TILE_GROUPS, SSM_STATE)
    t = (t * _eye_tiles()[None, None, :, None, :, None]).sum(axis=4)
    return t.transpose(0, 1, 2, 4, 3).reshape(2 * tiles * TILE_GROUPS, SSM_STATE * SSM_GROUP)


def _c_tiles(c, tiles):
    t = c.reshape(2, tiles, TILE_GROUPS, SSM_GROUP, SSM_STATE).transpose(0, 1, 2, 4, 3)
    t = t[:, :, :, :, None, :] * _eye_tiles()[None, None, :, None, :, None]
    return t.reshape(2, tiles, TILE_N, TILE_U)


def _c_untile(dct, tiles):
    t = dct.reshape(2, tiles, TILE_GROUPS, SSM_STATE, TILE_GROUPS, SSM_GROUP)
    t = (t * _eye_tiles()[None, None, :, None, :, None]).sum(axis=4)
    return t.transpose(0, 1, 2, 4, 3).reshape(2, tiles * TILE_GROUPS, SSM_GROUP, SSM_STATE)


def _to_segments(a):
    l, w = a.shape
    return a.reshape(N_SEG, l // N_SEG, w).transpose(1, 0, 2).reshape(l, w)


def _from_segments(a):
    l, w = a.shape
    return a.reshape(l // N_SEG, N_SEG, w).transpose(1, 0, 2).reshape(l, w)


def _s5_scan(xr, xi, a_re, a_im, *, length, reverse, shifted=None):
    ls = length // N_SEG
    assert ls * N_SEG == length and ls & (ls - 1) == 0
    ar = [jnp.broadcast_to(a_re[:, c * LANE:(c + 1) * LANE], (N_SEG, LANE)) for c in range(N_LT)]
    ai = [jnp.broadcast_to(a_im[:, c * LANE:(c + 1) * LANE], (N_SEG, LANE)) for c in range(N_LT)]
    zero = jnp.zeros((N_SEG, LANE), F32)
    row = lax.broadcasted_iota(jnp.int32, (N_SEG, LANE), 0)

    def step_of(n):
        return (ls - 1 - n) if reverse else n

    def block(j):
        return pl.ds(j * N_SEG, N_SEG) if isinstance(j, int) else pl.ds(pl.multiple_of(j * N_SEG, N_SEG), N_SEG)

    def local(n, carry):
        rows = block(step_of(n))
        new = []
        for c in range(N_LT):
            cr, ci = carry[2 * c], carry[2 * c + 1]
            nr = ar[c] * cr - ai[c] * ci + xr[rows, _lanes(c)]
            ni = ar[c] * ci + ai[c] * cr + xi[rows, _lanes(c)]
            xr[rows, _lanes(c)] = nr
            xi[rows, _lanes(c)] = ni
            new += [nr, ni]
        return tuple(new)

    ends = lax.fori_loop(0, ls, local, (zero,) * (2 * N_LT), unroll=SCAN_UNROLL)

    init = []
    for c in range(N_LT):
        pr, pi = ar[c][0:1, :], ai[c][0:1, :]
        for _ in range(ls.bit_length() - 1):
            pr, pi = pr * pr - pi * pi, 2.0 * pr * pi
        cr = ci = jnp.zeros((1, LANE), F32)
        ir, ii = zero, zero
        for s in (range(N_SEG - 1, -1, -1) if reverse else range(N_SEG)):
            ir = jnp.where(row == s, cr, ir)
            ii = jnp.where(row == s, ci, ii)
            er, ei = ends[2 * c][s:s + 1, :], ends[2 * c + 1][s:s + 1, :]
            cr, ci = pr * cr - pi * ci + er, pr * ci + pi * cr + ei
        init += [ir, ii]

    def fix(n, carry, last=False):
        j = step_of(n)
        rows = block(j)
        new, sums = [], []
        for c in range(N_LT):
            cr, ci = carry[2 * c], carry[2 * c + 1]
            nr = ar[c] * cr - ai[c] * ci
            ni = ar[c] * ci + ai[c] * cr
            fr = xr[rows, _lanes(c)] + nr
            fi = xi[rows, _lanes(c)] + ni
            xr[rows, _lanes(c)] = fr
            xi[rows, _lanes(c)] = fi
            new += [nr, ni]
            if shifted is not None:
                yr, yi, shift = shifted
                if not last:
                    srows = block(j + shift)
                    sr, si = yr[srows, _lanes(c)], yi[srows, _lanes(c)]
                else:
                    edge = block(ls - 1 if shift < 0 else 0)
                    move, gone = (1, 0) if shift < 0 else (N_SEG - 1, N_SEG - 1)
                    sr = jnp.where(row == gone, 0.0, pltpu.roll(yr[edge, _lanes(c)], move, 0))
                    si = jnp.where(row == gone, 0.0, pltpu.roll(yi[edge, _lanes(c)], move, 0))
                sums += [carry[2 * N_LT + 2 * c] + fr * sr + fi * si,
                         carry[2 * N_LT + 2 * c + 1] + fi * sr - fr * si]
        return tuple(new + sums)

    if shifted is None:
        lax.fori_loop(0, ls, fix, tuple(init), unroll=SCAN_UNROLL)
        return ()
    assert shifted[2] == (-1 if reverse else 1)
    out = lax.fori_loop(0, ls - 1, fix, tuple(init) + (zero,) * (2 * N_LT), unroll=SCAN_UNROLL)
    return fix(ls - 1, out, last=True)[2 * N_LT:]


def _s5_tile_specs(l, d):
    tile = lambda r, c: pl.BlockSpec((None, None, r, c), lambda t, d=d: (d, t, 0, 0))
    return [pl.BlockSpec((l, TILE_U), lambda t: (0, t)), tile(TILE_U, TILE_N), tile(TILE_U, TILE_N),
            tile(1, TILE_N), tile(1, TILE_N), tile(TILE_N, TILE_U), tile(TILE_N, TILE_U)]


def _lanes(c):
    return slice(c * LANE, (c + 1) * LANE)


def _s5_fwd(u, bt_re, bt_im, lam_re, lam_im, ct_re, ct_im, *, d, name, add=None):
    l = u.shape[0]
    tiles = bt_re.shape[1]
    col = pl.BlockSpec((l, TILE_U), lambda t: (0, t))
    has_add = add is not None

    def body(*refs):
        u_ref, bre, bim, lre, lim, cre, cim = refs[:7]
        y_ref, xr, xi = refs[-3:]
        uu = u_ref[...]
        bu_re, bu_im = _dg(uu, bre[...], 1, 0), _dg(uu, bim[...], 1, 0)
        xr[...] = bu_re
        xi[...] = bu_im
        _s5_scan(xr, xi, lre[...], lim[...], length=l, reverse=(d == 1))
        y = _dg(xr[...], cre[...], 1, 0) - _dg(xi[...], cim[...], 1, 0)
        y_ref[...] = y + refs[7][...] if has_add else y

    return pl.pallas_call(
        body, name=name, grid=(tiles,), in_specs=_s5_tile_specs(l, d) + [col] * has_add, out_specs=col,
        out_shape=jax.ShapeDtypeStruct((l, tiles * TILE_U), F32),
        scratch_shapes=[pltpu.VMEM((l, TILE_N), F32)] * 2, compiler_params=_params("arbitrary"),
    )(u, bt_re, bt_im, lam_re, lam_im, ct_re, ct_im, *([add] if has_add else []))


def _s5_bwd(u, dy, bt_re, bt_im, lam_re, lam_im, ct_re, ct_im, *, d, name, add=None):
    l = u.shape[0]
    tiles = bt_re.shape[1]
    col = pl.BlockSpec((l, TILE_U), lambda t: (0, t))
    reverse = d == 1
    has_add = add is not None

    def body(*refs):
        u_ref, bre, bim, lre, lim, cre, cim, dy_ref = refs[:8]
        du_ref, dbre, dbim, dcre, dcim, dlre, dlim, xr, xi, gr, gi = refs[-11:]
        uu, dyy = u_ref[...], dy_ref[...]
        bu_re, bu_im = _dg(uu, bre[...], 1, 0), _dg(uu, bim[...], 1, 0)
        xr[...] = bu_re
        xi[...] = bu_im
        _s5_scan(xr, xi, lre[...], lim[...], length=l, reverse=reverse)
        gr[...] = _dg(dyy, cre[...], 1, 1)
        gi[...] = -_dg(dyy, cim[...], 1, 1)
        sums = _s5_scan(gr, gi, lre[...], -lim[...], length=l, reverse=not reverse,
                        shifted=(xr, xi, 1 if reverse else -1))
        for c in range(N_LT):
            dlre[:, _lanes(c)] = jnp.sum(sums[2 * c], axis=0, keepdims=True)
            dlim[:, _lanes(c)] = jnp.sum(sums[2 * c + 1], axis=0, keepdims=True)
        g_re, g_im = gr[...], gi[...]
        du = _dg(g_re, bre[...], 1, 1) + _dg(g_im, bim[...], 1, 1)
        du_ref[...] = du + refs[8][...] if has_add else du
        dbre[...] = _dg(uu, g_re, 0, 0)
        dbim[...] = _dg(uu, g_im, 0, 0)
        dcre[...] = _dg(xr[...], dyy, 0, 0)
        dcim[...] = -_dg(xi[...], dyy, 0, 0)

    out3 = lambda r, c: pl.BlockSpec((None, r, c), lambda t: (t, 0, 0))
    f = lambda *s: jax.ShapeDtypeStruct(s, F32)
    return pl.pallas_call(
        body, name=name, grid=(tiles,), in_specs=_s5_tile_specs(l, d) + [col] + [col] * has_add,
        out_specs=[col, out3(TILE_U, TILE_N), out3(TILE_U, TILE_N),
                   out3(TILE_N, TILE_U), out3(TILE_N, TILE_U), out3(1, TILE_N), out3(1, TILE_N)],
        out_shape=[f(l, tiles * TILE_U), f(tiles, TILE_U, TILE_N), f(tiles, TILE_U, TILE_N),
                   f(tiles, TILE_N, TILE_U), f(tiles, TILE_N, TILE_U), f(tiles, 1, TILE_N), f(tiles, 1, TILE_N)],
        scratch_shapes=[pltpu.VMEM((l, TILE_N), F32)] * 4, compiler_params=_params("arbitrary"),
    )(u, bt_re, bt_im, lam_re, lam_im, ct_re, ct_im, dy, *([add] if has_add else []))


def _s5_post(y, proj, u_cb, dskip, *, name, tl=256):
    d = y.shape[1]

    def body(i, o):
        ys = jax.nn.gelu(i[0][...] + i[2][...] * i[1][...])
        o[0][...] = ys
        o[1][...] = ys.astype(BF16)

    return _rowwise(body, rows=y.shape[0], tl=tl, ins=[_row(y, tl), _row(proj, tl, d, u_cb), _par(dskip)],
                    outs=[("row", d, F32), ("row", d, BF16)], name=name)


def _s5_post_bwd(y, proj, u_cb, dskip, dys, *, name, tl=256):
    d = y.shape[1]

    def body(i, o):
        u_ = i[1][...]
        _, vjp = jax.vjp(jax.nn.gelu, i[0][...] + i[2][...] * u_)
        (dpre,) = vjp(i[3][...])
        o[0][...] = dpre
        o[1][...] += jnp.sum(dpre * u_, axis=0, keepdims=True)

    return _rowwise(body, rows=y.shape[0], tl=tl,
                    ins=[_row(y, tl), _row(proj, tl, d, u_cb), _par(dskip), _row(dys, tl)],
                    outs=[("row", d, F32), ("acc", d)], name=name)


def _du_combine(dpre, dskip, du_s5, *, name, tl=256):
    d = dpre.shape[1]

    def body(i, o):
        o[0][...] = (i[0][...] * i[1][...] + i[2][...]).astype(BF16)

    return _rowwise(body, rows=dpre.shape[0], tl=tl, ins=[_row(dpre, tl), _par(dskip), _row(du_s5, tl)],
                    outs=[("row", d, BF16)], name=name)[0]


def _merge_fn(y, g, gate_r, gate_s, ys, glu, b):
    ret = jax.nn.silu(g) * (y * lax.rsqrt(jnp.mean(y * y, axis=-1, keepdims=True) + EPS))
    ssm = ys * jax.nn.sigmoid(glu + b)
    return jax.nn.sigmoid(gate_r) * ret + jax.nn.sigmoid(gate_s) * ssm


def _merge_ins(y_raw, proj, ys, glu, b_glu, cb0, tl):
    d = y_raw.shape[1]
    return [_row(y_raw, tl), _row(proj, tl, d, cb0 + 1), _row(proj, tl, d, cb0 + 3), _row(proj, tl, d, cb0 + 4),
            _row(ys, tl), _row(glu, tl), _par(b_glu)]


def _merge(y_raw, proj, ys, glu, b_glu, *, cb0, name, tl=128):
    d = y_raw.shape[1]
    dv = d // RET_HEADS

    def body(i, o):
        for h in range(RET_HEADS):
            cs = slice(h * dv, (h + 1) * dv)
            o[0][:, cs] = _merge_fn(*[r[:, cs] for r in i]).astype(BF16)

    return _rowwise(body, rows=y_raw.shape[0], tl=tl, ins=_merge_ins(y_raw, proj, ys, glu, b_glu, cb0, tl),
                    outs=[("row", d, BF16)], name=name)[0]


def _merge_bwd(y_raw, proj, ys, glu, b_glu, dmerged, *, cb0, name, tl=128):
    d = y_raw.shape[1]
    dv = d // RET_HEADS

    def body(i, o):
        for h in range(RET_HEADS):
            cs = slice(h * dv, (h + 1) * dv)
            _, vjp = jax.vjp(_merge_fn, *[r[:, cs] for r in i[:7]])
            dy, dg, dgr, dgs, dys, dglu, db = vjp(i[7][:, cs])
            o[0][:, cs] = dy.astype(BF16)
            o[1][:, cs] = dg.astype(BF16)
            o[2][:, cs] = dgr.astype(BF16)
            o[3][:, cs] = dgs.astype(BF16)
            o[4][:, cs] = dglu.astype(BF16)
            o[5][:, cs] = dys
            o[6][:, cs] += db

    return _rowwise(body, rows=y_raw.shape[0], tl=tl,
                    ins=_merge_ins(y_raw, proj, ys, glu, b_glu, cb0, tl) + [_row(dmerged, tl)],
                    outs=[("row", d, BF16)] * 5 + [("row", d, F32), ("acc", d)], name=name)


def _ffn_act_fn(gate, up):
    return jax.nn.silu(gate) * up


def _ffn_act(gate, up, *, name, tl=128):
    def body(i, o):
        o[0][...] = _ffn_act_fn(i[0][...], i[1][...]).astype(BF16)

    return _rowwise(body, rows=gate.shape[0], tl=tl, ins=[_row(gate, tl), _row(up, tl)],
                    outs=[("row", gate.shape[1], BF16)], name=name)[0]


def _ffn_act_bwd(gate, up, dact, *, name, tl=128):
    def body(i, o):
        _, vjp = jax.vjp(_ffn_act_fn, i[0][...], i[1][...])
        dgate, dup = vjp(i[2][...])
        o[0][...] = dgate.astype(BF16)
        o[1][...] = dup.astype(BF16)

    w = gate.shape[1]
    return _rowwise(body, rows=gate.shape[0], tl=tl, ins=[_row(gate, tl), _row(up, tl), _row(dact, tl)],
                    outs=[("row", w, BF16), ("row", w, BF16)], name=name)


def _adamw(w, g, m, v):
    m = ADAM_B1 * m + (1.0 - ADAM_B1) * g
    v = ADAM_B2 * v + (1.0 - ADAM_B2) * (g * g)
    m_hat = m / (1.0 - ADAM_B1 ** ADAM_STEP)
    v_hat = v / (1.0 - ADAM_B2 ** ADAM_STEP)
    return -ADAM_LR * (m_hat / (jnp.sqrt(v_hat) + ADAM_EPS) + ADAM_WD * w), m, v


def _adam_flat(g, w, m, v, *, name, tr=FLAT_ROWS):
    def body(i, o):
        for o_ref, val in zip(o, _adamw(i[1][...], i[0][...], i[2][...], i[3][...])):
            o_ref[...] = val

    return _rowwise(body, rows=g.shape[0], tl=tr, ins=[_row(a, tr) for a in (g, w, m, v)],
                    outs=[("row", LANE, F32)] * 3, name=name)


def _adam_shard(recv, own, me, w, m, v, *, kind, layer, name, tr, tie=None, others=None):
    depth, r, c = w.shape
    assert r % tr == 0 and recv.shape[2] == c
    blk = pl.BlockSpec((None, tr, c), lambda i, me_ref: (layer, i, 0))
    if kind == "col":
        own_spec = pl.BlockSpec((tr, c), lambda i, me_ref: (i, me_ref[0]))
    else:
        per = own.shape[0] // N_DEV // tr
        assert per * tr * N_DEV == own.shape[0]
        own_spec = pl.BlockSpec((tr, c), lambda i, me_ref: (me_ref[0] * per + i, 0))
    extra = ([] if tie is None else [tie]) + list(others or [])

    def body(me_ref, recv_ref, own_ref, w_ref, m_ref, v_ref, *refs):
        g_ref, d_ref, nm_ref, nv_ref = refs[len(extra):]
        g = own_ref[...].astype(F32)
        for k in range(N_DEV - 1):
            g = g + recv_ref[k].astype(F32)
        g_ref[...] = g
        d_ref[...], nm_ref[...], nv_ref[...] = _adamw(w_ref[...], g, m_ref[...], v_ref[...])

    first = 6 + (tie is not None)
    return pl.pallas_call(
        body, name=name, out_shape=[jax.ShapeDtypeStruct(w.shape, F32)] * 4,
        grid_spec=pltpu.PrefetchScalarGridSpec(
            num_scalar_prefetch=1, grid=(r // tr,),
            in_specs=[pl.BlockSpec((N_DEV - 1, tr, c), lambda i, me_ref: (0, i, 0)), own_spec, blk, blk, blk]
            + [pl.BlockSpec(memory_space=pl.ANY)] * len(extra),
            out_specs=[blk] * 4),
        input_output_aliases={first + j: j for j in range(4)} if others else {},
        compiler_params=_params("parallel"),
    )(me, recv, own, w, m, v, *extra)


def _position():
    x, y, c = lax.axis_index("x"), lax.axis_index("y"), lax.axis_index("c")
    return x, y, c, 4 * x + 2 * y + c


def _coords(p):
    return p // 4, (p // 2) % 2, p % 2


def _block_of(ref, kind, p, nb):
    if kind == "col":
        return ref.at[:, pl.ds(pl.multiple_of(p * nb, LANE), nb)]
    return ref.at[pl.ds(pl.multiple_of(p * nb, SUBLANE), nb), :]


def _all_gather(shards, kinds, *, name):
    n = len(shards)
    out_shape = []
    for s, kind in zip(shards, kinds):
        r, c = s.shape
        out_shape.append(jax.ShapeDtypeStruct((r, c * N_DEV) if kind == "col" else (r * N_DEV, c), s.dtype))

    def body(*refs):
        shard_refs, full_refs = refs[:n], refs[n:2 * n]
        send_sems, recv_sems, local_sems = refs[2 * n:]
        x, y, c, me = _position()
        sibling = (x, y, 1 - c)
        chips = [(1 - x, y), (x, 1 - y), (1 - x, 1 - y)]

        def block(t, dev):
            nb = shards[t].shape[1 if kinds[t] == "col" else 0]
            return _block_of(full_refs[t], kinds[t], 4 * dev[0] + 2 * dev[1] + dev[2], nb)

        def copy(t, k, dev, to, src=None):
            return pltpu.make_async_remote_copy(
                src_ref=block(t, dev) if src is None else src, dst_ref=block(t, dev),
                send_sem=send_sems.at[t, k], recv_sem=recv_sems.at[t, k], device_id=to, device_id_type=MESH)

        mine, first, passed = [], [], []
        for t in range(n):
            mine.append(pltpu.make_async_copy(shard_refs[t], block(t, (x, y, c)), local_sems.at[t]))
            mine[-1].start()
            first.append(copy(t, 0, (x, y, c), sibling, src=shard_refs[t]))
            first += [copy(t, 1 + j, (x, y, c), (*chip, c), src=shard_refs[t]) for j, chip in enumerate(chips)]
        for cp in first:
            cp.start()
        for j, chip in enumerate(chips):
            for t in range(n):
                copy(t, 1 + j, (*chip, c), (x, y, c)).wait_recv()
                passed.append(copy(t, 4 + j, (*chip, c), sibling))
                passed[-1].start()
        for t in range(n):
            copy(t, 0, sibling, (x, y, c)).wait_recv()
            for j, chip in enumerate(chips):
                copy(t, 4 + j, (*chip, 1 - c), (x, y, c)).wait_recv()
        for cp in first + passed:
            cp.wait_send()
        for cp in mine:
            cp.wait()

    any_spec = pl.BlockSpec(memory_space=pl.ANY)
    return pl.pallas_call(
        body, name=name, in_specs=[any_spec] * n, out_specs=[any_spec] * n, out_shape=out_shape,
        scratch_shapes=[pltpu.SemaphoreType.DMA((n, N_DEV - 1)), pltpu.SemaphoreType.DMA((n, N_DEV - 1)),
                        pltpu.SemaphoreType.DMA((n,))],
        compiler_params=pltpu.CompilerParams(has_side_effects=True),
    )(*shards)


class _GatherRoute:
    def __init__(self, shards, kinds):
        self.kinds = kinds
        self.nb = [s.shape[1 if k == "col" else 0] for s, k in zip(shards, kinds)]

    def lands(self, shards, me, tag):
        out = []
        for t, (s, k) in enumerate(zip(shards, self.kinds)):
            full = lax.empty((s.shape[0], s.shape[1] * N_DEV) if k == "col" else (s.shape[0] * N_DEV, s.shape[1]), s.dtype)
            out.append(_place_own(s, full, me, k, name=f"own_block_{tag}_{t}"))
        return out

    def sent(self, t, src_refs, me, k):
        return src_refs[t]

    def lands_at(self, t, land_refs, me, k):
        return _block_of(land_refs[t], self.kinds[t], (me + N_DEV - k) % N_DEV, self.nb[t])


class _ScatterRoute:
    def __init__(self, grads, kinds):
        self.kinds = kinds
        self.nb = [g.shape[1 if k == "col" else 0] // N_DEV for g, k in zip(grads, kinds)]

    def lands(self, grads):
        return [lax.empty((N_DEV - 1, g.shape[0], nb) if k == "col" else (N_DEV - 1, nb, g.shape[1]), g.dtype)
                for g, k, nb in zip(grads, self.kinds, self.nb)]

    def sent(self, t, src_refs, me, k):
        return _block_of(src_refs[t], self.kinds[t], (me + k) % N_DEV, self.nb[t])

    def lands_at(self, t, land_refs, me, k):
        return land_refs[t].at[k - 1]


def _place_own(shard, full, me, kind, *, name):
    r, c = shard.shape
    tr = _row_tile(r)
    if kind == "col":
        dst = pl.BlockSpec((tr, c), lambda i, me_ref: (i, me_ref[0]))
    else:
        dst = pl.BlockSpec((tr, c), lambda i, me_ref: (me_ref[0] * (r // tr) + i, 0))

    def body(me_ref, shard_ref, full_ref, out_ref):
        out_ref[...] = shard_ref[...]

    return pl.pallas_call(
        body, name=name, out_shape=jax.ShapeDtypeStruct(full.shape, full.dtype),
        grid_spec=pltpu.PrefetchScalarGridSpec(
            num_scalar_prefetch=1, grid=(r // tr,),
            in_specs=[pl.BlockSpec((tr, c), lambda i, me_ref: (i, 0)), pl.BlockSpec(memory_space=pl.ANY)],
            out_specs=dst),
        input_output_aliases={2: 0}, compiler_params=_params("parallel"),
    )(me, shard, full)


_HBM = pl.BlockSpec(memory_space=pltpu.HBM)
_SEM = pl.BlockSpec(memory_space=pltpu.SEMAPHORE)
_FLOWING = pltpu.SideEffectType.DATAFLOW_SIDE_EFFECTING


def _exchange_start(srcs, lands, route, after, *, name):
    n = len(srcs)

    def body(*refs):
        src_refs, land_refs = refs[:n], refs[n:2 * n]
        send_sems, recv_sems = refs[2 * n + 1:2 * n + 3]
        token = refs[-1]
        _, _, _, me = _position()
        for t in range(n):
            for k in range(1, N_DEV):
                p = (me + k) % N_DEV
                pltpu.make_async_remote_copy(
                    src_ref=route.sent(t, src_refs, me, k), dst_ref=route.lands_at(t, land_refs, p, k),
                    send_sem=send_sems.at[t * N_DEV + k], recv_sem=recv_sems.at[t * N_DEV + k], device_id=_coords(p),
                    device_id_type=MESH).start()
        token[...] = jnp.zeros_like(token)

    hbm = lambda a: pltpu.HBM(a.shape, a.dtype)
    sems = pltpu.SemaphoreType.DMA((n * N_DEV,))
    out = pl.pallas_call(
        body, name=name,
        out_shape=(sems, sems, *[hbm(a) for a in srcs], *[hbm(a) for a in lands],
                   jax.ShapeDtypeStruct((SUBLANE, LANE), F32)),
        in_specs=[_HBM] * (2 * n) + [pl.BlockSpec(memory_space=pl.ANY)],
        out_specs=(_SEM, _SEM, *[_HBM] * (2 * n), pl.BlockSpec(memory_space=pltpu.VMEM)),
        input_output_aliases={i: 2 + i for i in range(2 * n)},
        compiler_params=pltpu.CompilerParams(has_side_effects=_FLOWING),
    )(*[pltpu.with_memory_space_constraint(a, pltpu.HBM) for a in (*srcs, *lands)], after)
    return out[0], out[1], out[2:2 + n], out[2 + n:2 + 2 * n], out[-1]


def _exchange_wait(started, route, after, *, name):
    send_sems, recv_sems, srcs, lands, _ = started
    n = len(srcs)

    def body(*refs):
        src_refs, land_refs = refs[:n], refs[n:2 * n]
        send_ref, recv_ref = refs[2 * n:2 * n + 2]
        _, _, _, me = _position()
        for t in range(n):
            for k in range(1, N_DEV):
                cp = pltpu.make_async_remote_copy(
                    src_ref=route.sent(t, src_refs, me, k), dst_ref=route.lands_at(t, land_refs, me, k),
                    send_sem=send_ref.at[t * N_DEV + k], recv_sem=recv_ref.at[t * N_DEV + k],
                    device_id=_coords((me + N_DEV - k) % N_DEV), device_id_type=MESH)
                cp.wait_send()
                cp.wait_recv()

    hbm = lambda a: pltpu.HBM(a.shape, a.dtype)
    out = pl.pallas_call(
        body, name=name, out_shape=(*[hbm(a) for a in srcs], *[hbm(a) for a in lands]),
        in_specs=[_HBM] * (2 * n) + [_SEM, _SEM, pl.BlockSpec(memory_space=pl.ANY)],
        out_specs=tuple([_HBM] * (2 * n)), input_output_aliases={i: i for i in range(2 * n)},
        compiler_params=pltpu.CompilerParams(has_side_effects=_FLOWING),
    )(*srcs, *lands, send_sems, recv_sems, after)
    return list(out[:n]), list(out[n:])


def _all_reduce(part, *, name):
    _, r, _ = part.shape

    def body(part_ref, tot_ref, recv_ref, send1, recv1, send2, recv2):
        _, _, _, me = _position()

        def scatter(k, to_me=False):
            p = (me + N_DEV - k) % N_DEV if to_me else (me + k) % N_DEV
            return pltpu.make_async_remote_copy(
                src_ref=part_ref.at[me if to_me else p], dst_ref=recv_ref.at[p if to_me else me],
                send_sem=send1.at[k], recv_sem=recv1.at[k], device_id=_coords(p), device_id_type=MESH)

        def gather(k, to_me=False):
            p = (me + N_DEV - k) % N_DEV if to_me else (me + k) % N_DEV
            return pltpu.make_async_remote_copy(
                src_ref=tot_ref.at[me], dst_ref=tot_ref.at[p if to_me else me],
                send_sem=send2.at[k], recv_sem=recv2.at[k], device_id=_coords(p), device_id_type=MESH)

        for k in range(1, N_DEV):
            scatter(k).start()
        recv_ref[me] = part_ref[me]
        for k in range(1, N_DEV):
            scatter(k, to_me=True).wait_recv()
        total = recv_ref[0]
        for q in range(1, N_DEV):
            total = total + recv_ref[q]
        tot_ref[me] = total
        for k in range(1, N_DEV):
            gather(k).start()
        for k in range(1, N_DEV):
            gather(k, to_me=True).wait_recv()
        for k in range(1, N_DEV):
            scatter(k).wait_send()
            gather(k).wait_send()

    vmem = pl.BlockSpec(memory_space=pltpu.VMEM)
    return pl.pallas_call(
        body, name=name, in_specs=[vmem], out_specs=vmem, out_shape=jax.ShapeDtypeStruct(part.shape, F32),
        scratch_shapes=[pltpu.VMEM(part.shape, F32)] + [pltpu.SemaphoreType.DMA((N_DEV,))] * 4,
        compiler_params=pltpu.CompilerParams(has_side_effects=True, vmem_limit_bytes=VMEM_LIMIT),
    )(part)


def _round_up(n, m):
    return (n + m - 1) // m * m


def _row_tile(rows):
    return next(t for t in (256, 128, 64, 32, 16) if rows % t == 0)


def _local_step(x, target, small, depth, weights_of, grads_done, *, qkw):
    l, d = x.shape
    groups = d // SSM_GROUP
    tiles = groups // TILE_GROUPS
    half = qkw // RET_HEADS // 2
    cb0 = 2 * qkw // d
    inv = 1.0 / (ROPE_BASE ** (jnp.arange(half, dtype=F32) / half))
    ang = jnp.arange(l, dtype=F32)[:, None] * inv[None, :]
    cos, sin = jnp.cos(ang), jnp.sin(ang)
    rep = jnp.repeat(jnp.eye(SSM_STATE, dtype=F32), SSM_GROUP, axis=1)
    row2 = lambda a: a.reshape(1, -1)

    saved = []
    for i in range(depth):
        w_in, rest_of, x = weights_of(i, x)
        n = f"l{i}_"
        g_mix, g_ffn = row2(small["ln_mix_g"][i]), row2(small["ln_ffn_g"][i])
        dskip, b_glu = row2(small["ssm_d"][i]), row2(small["b_glu"][i])
        lg = small["ret_log_gamma"][i]
        h = _norm_fwd(x, g_mix, name=n + "norm_mix")
        proj = _matmul(h, w_in, mode="nn", out_dtype=F32, name=n + "proj")
        q_rot, k_rot, v_bf = _ret_prep(proj, cos, sin, qkw=qkw, d=d, name=n + "ret_prep")
        y_raw = _ret_fwd(lg, q_rot, k_rot, v_bf, name=n + "ret_fwd")
        par = [small["ssm_a_re"][i].reshape(2 * groups, SSM_STATE), small["ssm_a_im"][i].reshape(2 * groups, SSM_STATE),
               small["ssm_log_dt"][i].reshape(2 * groups, 1),
               small["ssm_b_re"][i].reshape(2 * groups, SSM_STATE * SSM_GROUP),
               small["ssm_b_im"][i].reshape(2 * groups, SSM_STATE * SSM_GROUP), rep]
        lam_re, lam_im, bbar_re, bbar_im = _s5_prep(*par, name=n + "s5_prep")
        s5 = [_b_tiles(bbar_re, tiles).astype(BF16), _b_tiles(bbar_im, tiles).astype(BF16),
              lam_re.reshape(2, tiles, 1, TILE_N), lam_im.reshape(2, tiles, 1, TILE_N),
              _c_tiles(small["ssm_c_re"][i], tiles).astype(BF16), _c_tiles(small["ssm_c_im"][i], tiles).astype(BF16)]
        u_seg = _to_segments(proj[:, (cb0 + 2) * d:(cb0 + 3) * d])
        y_seg = _s5_fwd(u_seg, *s5, d=0, name=n + "s5_fwd_f")
        y_seg = _s5_fwd(u_seg, *s5, d=1, add=y_seg, name=n + "s5_fwd_b")
        y_s5 = _from_segments(y_seg)
        ys, ys_bf = _s5_post(y_s5, proj, cb0 + 2, dskip, name=n + "s5_post")
        w_glu, w_out, w_gate, w_up, w_down = rest_of(ys_bf)
        full_i = (w_in, w_glu, w_out, w_gate, w_up, w_down)
        glu = _matmul(ys_bf, w_glu, mode="nn", out_dtype=F32, name=n + "glu")
        merged = _merge(y_raw, proj, ys, glu, b_glu, cb0=cb0, name=n + "merge")
        x1 = _matmul(merged, w_out, mode="nn", out_dtype=F32, res=x, name=n + "out")
        h2 = _norm_fwd(x1, g_ffn, name=n + "norm_ffn")
        gate = _matmul(h2, w_gate, mode="nn", out_dtype=F32, name=n + "gate")
        up = _matmul(h2, w_up, mode="nn", out_dtype=F32, name=n + "up")
        act = _ffn_act(gate, up, name=n + "act")
        x2 = _matmul(act, w_down, mode="nn", out_dtype=F32, res=x1, name=n + "down")
        saved.append(dict(full=full_i, x=x, h=h, proj=proj, q_rot=q_rot, k_rot=k_rot, v_bf=v_bf, y_raw=y_raw, par=par, s5=s5,
                          u_seg=u_seg, y_s5=y_s5, ys=ys, ys_bf=ys_bf, glu=glu, merged=merged, x1=x1, h2=h2, gate=gate,
                          up=up, act=act))
        x = x2

    dx, dg_final, loss = _final(x, row2(small["ln_final_g"]), target, name="final")

    sg = {k: [None] * depth for k in ("ln_mix_g", "ret_log_gamma", "ssm_a_re", "ssm_a_im", "ssm_log_dt", "ssm_b_re",
                                      "ssm_b_im", "ssm_c_re", "ssm_c_im", "ssm_d", "b_glu", "ln_ffn_g")}
    for i in reversed(range(depth)):
        s = saved[i]
        big = [None] * len(BIG)
        w_in, w_glu, w_out, w_gate, w_up, w_down = s["full"]
        n = f"l{i}_b_"
        g_mix, g_ffn = row2(small["ln_mix_g"][i]), row2(small["ln_ffn_g"][i])
        dskip, b_glu = row2(small["ssm_d"][i]), row2(small["b_glu"][i])
        lg = small["ret_log_gamma"][i]
        dact = _matmul(dx, w_down, mode="nt", out_dtype=F32, name=n + "dact")
        big[5] = _matmul(s["act"], dx, mode="tn", out_dtype=BF16, name=n + "dw_down")
        dgate, dup = _ffn_act_bwd(s["gate"], s["up"], dact, name=n + "act")
        dh2 = _matmul(dgate, w_gate, mode="nt", out_dtype=F32, name=n + "dh2_gate")
        dh2 = _matmul(dup, w_up, mode="nt", out_dtype=F32, res=dh2, name=n + "dh2_up")
        big[3] = _matmul(s["h2"], dgate, mode="tn", out_dtype=BF16, name=n + "dw_gate")
        big[4] = _matmul(s["h2"], dup, mode="tn", out_dtype=BF16, name=n + "dw_up")
        dh2 = grads_done(i, FFN_GROUP, [big[t] for t in FFN_GROUP], dh2)
        dx1, dgf = _norm_bwd(s["x1"], g_ffn, dh2, dx, name=n + "norm_ffn")
        sg["ln_ffn_g"][i] = dgf[0]
        dmerged = _matmul(dx1, w_out, mode="nt", out_dtype=F32, name=n + "dmerged")
        big[2] = _matmul(s["merged"], dx1, mode="tn", out_dtype=BF16, name=n + "dw_out")
        dy_raw, dg, dgate_r, dgate_s, dglu, dys_a, db_glu = _merge_bwd(
            s["y_raw"], s["proj"], s["ys"], s["glu"], b_glu, dmerged, cb0=cb0, name=n + "merge")
        sg["b_glu"][i] = db_glu[0]
        dys = _matmul(dglu, w_glu, mode="nt", out_dtype=F32, res=dys_a, name=n + "dys")
        big[1] = _matmul(s["ys_bf"], dglu, mode="tn", out_dtype=BF16, name=n + "dw_glu")
        dpre, dd = _s5_post_bwd(s["y_s5"], s["proj"], cb0 + 2, dskip, dys, name=n + "s5_post")
        sg["ssm_d"][i] = dd[0]
        dpre_seg = _to_segments(dpre)
        r_f = _s5_bwd(s["u_seg"], dpre_seg, *s["s5"], d=0, name=n + "s5_bwd_f")
        r_b = _s5_bwd(s["u_seg"], dpre_seg, *s["s5"], d=1, add=r_f[0], name=n + "s5_bwd_b")
        du = _du_combine(dpre, dskip, _from_segments(r_b[0]), name=n + "du")
        both = lambda k: jnp.stack([r_f[k], r_b[k]])
        cts = [both(5).reshape(2 * groups, SSM_STATE), both(6).reshape(2 * groups, SSM_STATE),
               _b_untile(both(1), tiles), _b_untile(both(2), tiles)]
        da_re, da_im, dldt, db_re, db_im = _s5_prep_bwd(*s["par"], cts, name=n + "s5_prep")
        sg["ssm_a_re"][i] = da_re.reshape(2, groups, SSM_STATE)
        sg["ssm_a_im"][i] = da_im.reshape(2, groups, SSM_STATE)
        sg["ssm_log_dt"][i] = dldt.reshape(2, groups)
        sg["ssm_b_re"][i] = db_re.reshape(2, groups, SSM_STATE, SSM_GROUP)
        sg["ssm_b_im"][i] = db_im.reshape(2, groups, SSM_STATE, SSM_GROUP)
        sg["ssm_c_re"][i] = _c_untile(both(3), tiles)
        sg["ssm_c_im"][i] = _c_untile(both(4), tiles)
        dq_rot, dk_rot, dv, dlg = _ret_bwd(lg, s["q_rot"], s["k_rot"], s["v_bf"], dy_raw, name=n + "ret_bwd")
        sg["ret_log_gamma"][i] = dlg[:, :2, 0].T
        dq, dk = _ret_prep_bwd(dq_rot, dk_rot, cos, sin, name=n + "ret_prep")
        dproj = jnp.concatenate([dq, dk, dv.astype(BF16), dg, du, dgate_r, dgate_s], axis=1)
        dh = _matmul(dproj, w_in, mode="nt", out_dtype=F32, name=n + "dh")
        big[0] = _matmul(s["h"], dproj, mode="tn", out_dtype=BF16, name=n + "dw_in")
        dx, dgm = _norm_bwd(s["x"], g_mix, dh, dx1, name=n + "norm_mix")
        sg["ln_mix_g"][i] = dgm[0]
        dx = grads_done(i, MIX_GROUP, [big[t] for t in MIX_GROUP], dx)

    small_grads = {k: jnp.stack(v) for k, v in sg.items()}
    small_grads["ln_final_g"] = dg_final[0]
    return loss, dx, small_grads


BIG = ("w_in", "w_glu", "w_out", "w_ffn_gate", "w_ffn_up", "w_ffn_down")
BIG_KINDS = ("col", "row", "row", "col", "col", "row")
MIX_GROUP = (0, 1, 2)
FFN_GROUP = (3, 4, 5)
SMALL = ("ln_mix_g", "ret_log_gamma", "ssm_a_re", "ssm_a_im", "ssm_log_dt", "ssm_b_re", "ssm_b_im", "ssm_c_re",
         "ssm_c_im", "ssm_d", "b_glu", "ln_ffn_g", "ln_final_g")
WEIGHTS = ("ln_mix_g", "w_in", "ret_log_gamma", "ssm_a_re", "ssm_a_im", "ssm_log_dt", "ssm_b_re", "ssm_b_im",
           "ssm_c_re", "ssm_c_im", "ssm_d", "w_glu", "b_glu", "w_out", "ln_ffn_g", "w_ffn_gate", "w_ffn_up",
           "w_ffn_down", "ln_final_g")


def _pad_to(a, axis, size):
    pad = [(0, 0)] * a.ndim
    pad[axis] = (0, size - a.shape[axis])
    return jnp.pad(a, pad)


def _flatten_small(tree, extra):
    def as_rows(a):
        a = a.reshape(-1).astype(F32)
        return _pad_to(a, 0, _round_up(a.shape[0], SUBLANE * LANE)).reshape(-1, LANE)

    flat = jnp.concatenate([as_rows(tree[k]) for k in SMALL] + [as_rows(extra)])
    return _pad_to(flat, 0, _round_up(flat.shape[0], FLAT_ROWS))


def _unflatten_small(flat, like):
    out, at = {}, 0
    for k in SMALL:
        n = like[k].size
        rows = _round_up(n, SUBLANE * LANE) // LANE
        out[k] = flat[at:at + rows].reshape(-1)[:n].reshape(like[k].shape)
        at += rows
    return out, flat[at, 0]


def kernel(x, ln_mix_g, w_in, ret_log_gamma, ssm_a_re, ssm_a_im, ssm_log_dt, ssm_b_re, ssm_b_im, ssm_c_re, ssm_c_im, ssm_d, w_glu, b_glu, w_out, ln_ffn_g, w_ffn_gate, w_ffn_up, w_ffn_down, ln_final_g, loss_target, m_ln_mix_g, m_w_in, m_ret_log_gamma, m_ssm_a_re, m_ssm_a_im, m_ssm_log_dt, m_ssm_b_re, m_ssm_b_im, m_ssm_c_re, m_ssm_c_im, m_ssm_d, m_w_glu, m_b_glu, m_w_out, m_ln_ffn_g, m_w_ffn_gate, m_w_ffn_up, m_w_ffn_down, m_ln_final_g, v_ln_mix_g, v_w_in, v_ret_log_gamma, v_ssm_a_re, v_ssm_a_im, v_ssm_log_dt, v_ssm_b_re, v_ssm_b_im, v_ssm_c_re, v_ssm_c_im, v_ssm_d, v_w_glu, v_b_glu, v_w_out, v_ln_ffn_g, v_w_ffn_gate, v_w_ffn_up, v_w_ffn_down, v_ln_final_g):
    w = dict(ln_mix_g=ln_mix_g, w_in=w_in, ret_log_gamma=ret_log_gamma, ssm_a_re=ssm_a_re, ssm_a_im=ssm_a_im, ssm_log_dt=ssm_log_dt, ssm_b_re=ssm_b_re, ssm_b_im=ssm_b_im, ssm_c_re=ssm_c_re, ssm_c_im=ssm_c_im, ssm_d=ssm_d, w_glu=w_glu, b_glu=b_glu, w_out=w_out, ln_ffn_g=ln_ffn_g, w_ffn_gate=w_ffn_gate, w_ffn_up=w_ffn_up, w_ffn_down=w_ffn_down, ln_final_g=ln_final_g)
    m = dict(ln_mix_g=m_ln_mix_g, w_in=m_w_in, ret_log_gamma=m_ret_log_gamma, ssm_a_re=m_ssm_a_re, ssm_a_im=m_ssm_a_im, ssm_log_dt=m_ssm_log_dt, ssm_b_re=m_ssm_b_re, ssm_b_im=m_ssm_b_im, ssm_c_re=m_ssm_c_re, ssm_c_im=m_ssm_c_im, ssm_d=m_ssm_d, w_glu=m_w_glu, b_glu=m_b_glu, w_out=m_w_out, ln_ffn_g=m_ln_ffn_g, w_ffn_gate=m_w_ffn_gate, w_ffn_up=m_w_ffn_up, w_ffn_down=m_w_ffn_down, ln_final_g=m_ln_final_g)
    v = dict(ln_mix_g=v_ln_mix_g, w_in=v_w_in, ret_log_gamma=v_ret_log_gamma, ssm_a_re=v_ssm_a_re, ssm_a_im=v_ssm_a_im, ssm_log_dt=v_ssm_log_dt, ssm_b_re=v_ssm_b_re, ssm_b_im=v_ssm_b_im, ssm_c_re=v_ssm_c_re, ssm_c_im=v_ssm_c_im, ssm_d=v_ssm_d, w_glu=v_w_glu, b_glu=v_b_glu, w_out=v_w_out, ln_ffn_g=v_ln_ffn_g, w_ffn_gate=v_w_ffn_gate, w_ffn_up=v_w_ffn_up, w_ffn_down=v_w_ffn_down, ln_final_g=v_ln_final_g)
    depth, d, nb_in = w_in.shape
    qkw = (nb_in * N_DEV - 5 * d) // 2
    nb_ffn = w_ffn_gate.shape[2]
    nb_pad = _round_up(nb_ffn, LANE)
    pad_axis = {"w_ffn_gate": 2, "w_ffn_up": 2, "w_ffn_down": 1}

    assert depth == 2
    padded = {k: w[k] if k not in pad_axis else _pad_to(w[k], pad_axis[k], nb_pad) for k in BIG}
    shards = [[padded[k][i].astype(BF16) for k in BIG] for i in range(depth)]
    me = (4 * lax.axis_index("x") + 2 * lax.axis_index("y") + lax.axis_index("c")).astype(jnp.int32).reshape(1)
    w_in0 = _all_gather(shards[0][:1], BIG_KINDS[:1], name="gather_l0_in")[0]
    rest = tuple(range(1, len(BIG)))

    def start_gather(i, ids, after, tag):
        mine = [shards[i][t] for t in ids]
        route = _GatherRoute(mine, [BIG_KINDS[t] for t in ids])
        return _exchange_start(mine, route.lands(mine, me, tag), route, after, name=f"gather_{tag}_start"), route

    rest0 = start_gather(0, rest, w_in0, "l0_rest")
    in1 = start_gather(1, (0,), rest0[0][4], "l1_in")
    rest1 = start_gather(1, rest, in1[0][4], "l1_rest")

    def weights_of(i, act):
        if i == 0:
            return (w_in0, lambda later: _exchange_wait(*rest0, later, name="gather_l0_rest_wait")[1],
                    act + rest1[0][4][0, 0])
        w_in1 = _exchange_wait(*in1, act, name="gather_l1_in_wait")[1][0]
        return w_in1, lambda later: _exchange_wait(*rest1, later, name="gather_l1_rest_wait")[1], act

    exchanges, held = [], []

    def grads_done(i, group, grads, act):
        route = _ScatterRoute(grads, [BIG_KINDS[t] for t in group])
        if i == 0 and group == MIX_GROUP:
            held.append((i, group, grads, route))
            return act
        started = _exchange_start(grads, route.lands(grads), route, act, name=f"scatter_l{i}_{group[0]}_start")
        exchanges.append((i, group, started, route))
        return act + started[4][0, 0]

    small = {k: w[k] for k in SMALL}
    loss, dx, small_grads = _local_step(x[0], loss_target[0], small, depth, weights_of, grads_done, qkw=qkw)
    arrived, tie = {}, dx
    for i, group, started, route in exchanges:
        own, recv = _exchange_wait(started, route, tie, name=f"scatter_l{i}_{group[0]}_wait")
        arrived.update({(i, t): (own[j], recv[j]) for j, t in enumerate(group)})
        tie = recv[0]

    part = _flatten_small(small_grads, loss[0, :1])
    rows = part.shape[0]
    total = _all_reduce(part.reshape(N_DEV, rows // N_DEV, LANE), name="reduce_small").reshape(rows, LANE)
    (i, group, held_grads, route), = held
    started = _exchange_start(held_grads, route.lands(held_grads), route, total, name=f"scatter_l{i}_{group[0]}_start")
    zero = jnp.zeros((1,), F32)
    flat = [_flatten_small({k: a[k] for k in SMALL}, zero) for a in (w, m, v)]
    upd = _adam_flat(total + started[4][0, 0], *flat, name="adam_small")
    grads, delta, new_m, new_v = {}, {}, {}, {}
    g_small, loss_total = _unflatten_small(total, small)
    grads.update(g_small)
    for dst, u in zip((delta, new_m, new_v), upd):
        dst.update(_unflatten_small(u, small)[0])

    ops = {k: [_pad_to(a[k], 2, nb_pad) for a in (w, m, v)] if k in ("w_ffn_gate", "w_ffn_up") else [a[k] for a in (w, m, v)]
           for k in BIG}
    half, tie = {}, upd[0]

    def adam(layer, t, tie):
        k = BIG[t]
        own, recv = arrived[layer, t]
        return _adam_shard(recv, own, me, *ops[k], kind=BIG_KINDS[t], layer=layer, name=f"adam_l{layer}_{k}",
                           tr=_row_tile(ops[k][0].shape[1]), tie=tie, others=half.get(k))

    for t in range(len(BIG)):
        half[BIG[t]] = adam(1, t, tie)
        tie = half[BIG[t]][0]
    for t in FFN_GROUP:
        half[BIG[t]] = adam(0, t, tie)
        tie = half[BIG[t]][0]
    own, recv = _exchange_wait(started, route, tie, name=f"scatter_l{i}_{group[0]}_wait")
    arrived.update({(i, t): (own[j], recv[j]) for j, t in enumerate(group)})
    for t in MIX_GROUP:
        half[BIG[t]] = adam(0, t, None)
    for k in BIG:
        res = half[k]
        if k in ("w_ffn_gate", "w_ffn_up"):
            res = [r[:, :, :nb_ffn] for r in res]
        grads[k], delta[k], new_m[k], new_v[k] = res

    return (loss_total, dx[None], *[grads[k] for k in WEIGHTS], *[delta[k] for k in WEIGHTS],
            *[new_m[k] for k in WEIGHTS], *[new_v[k] for k in WEIGHTS])
```

```python
import math

import jax
import jax.numpy as jnp
from jax import lax
from jax.experimental import pallas as pl
from jax.experimental.pallas import tpu as pltpu

F32 = jnp.float32
BF16 = jnp.bfloat16
MESH = pl.DeviceIdType.MESH

N_DEV = 8
RET_HEADS = 4
CHUNK = 128
ROPE_BASE = 10000.0
SSM_GROUP = 16
SSM_STATE = 64
TILE_GROUPS = 8
TILE_U = TILE_GROUPS * SSM_GROUP
TILE_N = TILE_GROUPS * SSM_STATE
LANE = 128
SUBLANE = 8
N_SEG = SUBLANE
N_LT = TILE_N // LANE
SCAN_UNROLL = 4
FLAT_ROWS = 1024
EPS = 1e-6
ADAM_LR = 0.001
ADAM_B1 = 0.9
ADAM_B2 = 0.999
ADAM_EPS = 1e-08
ADAM_WD = 0.01
ADAM_STEP = 10
VMEM_LIMIT = 56 * 1024 * 1024


def _params(*sem):
    return pltpu.CompilerParams(dimension_semantics=sem or None, vmem_limit_bytes=VMEM_LIMIT)


def _dg(a, b, ca, cb):
    return lax.dot_general(a.astype(BF16), b.astype(BF16), (((ca,), (cb,)), ((), ())),
                           preferred_element_type=F32)


@jax.custom_vjp
def _dnn(a, b):
    return _dg(a, b, 1, 0)


@jax.custom_vjp
def _dnt(a, b):
    return _dg(a, b, 1, 1)


@jax.custom_vjp
def _dtn(a, b):
    return _dg(a, b, 0, 0)


_dnn.defvjp(lambda a, b: (_dnn(a, b), (a, b)), lambda r, g: (_dnt(g, r[1]), _dtn(r[0], g)))
_dnt.defvjp(lambda a, b: (_dnt(a, b), (a, b)), lambda r, g: (_dnn(g, r[1]), _dtn(g, r[0])))
_dtn.defvjp(lambda a, b: (_dtn(a, b), (a, b)), lambda r, g: (_dnt(r[1], g), _dnn(r[0], g)))


def _matmul(a, b, *, mode, out_dtype, name, res=None, tm=1024, tn=1024, tk=2048):
    if mode == "nn":
        (m, k), n = a.shape, b.shape[1]
    elif mode == "nt":
        (m, k), n = a.shape, b.shape[0]
    else:
        (k, m), n = a.shape, b.shape[1]
    tm, tn = min(tm, m), min(tn, n)
    tk = next(t for t in (tk, 1024, 512, 256, LANE) if k % t == 0) if k > tk else k
    assert m % tm == 0 and n % tn == 0 and k % tk == 0, (name, m, n, k)
    nk = k // tk
    if mode == "tn":
        a_spec = pl.BlockSpec((tk, tm), lambda i, j, kk: (kk, i))
    else:
        a_spec = pl.BlockSpec((tm, tk), lambda i, j, kk: (i, kk))
    if mode == "nt":
        b_spec = pl.BlockSpec((tn, tk), lambda i, j, kk: (j, kk))
    else:
        b_spec = pl.BlockSpec((tk, tn), lambda i, j, kk: (kk, j))
    ca, cb = {"nn": (1, 0), "nt": (1, 1), "tn": (0, 0)}[mode]
    o_spec = pl.BlockSpec((tm, tn), lambda i, j, kk: (i, j))
    has_res = res is not None

    def body(*refs):
        a_ref, b_ref = refs[:2]
        r_ref = refs[2] if has_res else None
        o_ref = refs[2 + has_res]
        part = _dg(a_ref[...], b_ref[...], ca, cb)
        if nk == 1:
            o_ref[...] = (part + r_ref[...] if has_res else part).astype(out_dtype)
            return
        acc = refs[-1]
        kk = pl.program_id(2)

        @pl.when(kk == 0)
        def _():
            acc[...] = part + r_ref[...] if has_res else part

        @pl.when(kk > 0)
        def _():
            acc[...] += part

        @pl.when(kk == nk - 1)
        def _():
            o_ref[...] = acc[...].astype(out_dtype)

    return pl.pallas_call(
        body, name=name, grid=(m // tm, n // tn, nk),
        in_specs=[a_spec, b_spec] + ([o_spec] if has_res else []),
        out_specs=o_spec, out_shape=jax.ShapeDtypeStruct((m, n), out_dtype),
        scratch_shapes=[pltpu.VMEM((tm, tn), F32)] if nk > 1 else [],
        compiler_params=_params("parallel", "parallel", "arbitrary"),
    )(*((a, b, res) if has_res else (a, b)))


def _row(arr, tl, width=None, cb=0):
    width = arr.shape[1] if width is None else width
    return arr, pl.BlockSpec((tl, width), lambda i, cb=cb: (i, cb))


def _par(arr):
    return arr, pl.BlockSpec(arr.shape, lambda i: (0,) * arr.ndim)


def _rowwise(body, *, rows, tl, ins, outs, name):
    arrays = [a for a, _ in ins]
    in_specs = [s for _, s in ins]
    out_shape, out_specs, acc_ids = [], [], []
    for n, o in enumerate(outs):
        if o[0] == "row":
            out_shape.append(jax.ShapeDtypeStruct((rows, o[1]), o[2]))
            out_specs.append(pl.BlockSpec((tl, o[1]), lambda i: (i, 0)))
        else:
            out_shape.append(jax.ShapeDtypeStruct((1, o[1]), F32))
            out_specs.append(pl.BlockSpec((1, o[1]), lambda i: (0, 0)))
            acc_ids.append(n)
    n_in = len(arrays)
    assert rows % tl == 0, (name, rows, tl)

    def wrapped(*refs):
        in_refs, out_refs = refs[:n_in], refs[n_in:]

        @pl.when(pl.program_id(0) == 0)
        def _():
            for n in acc_ids:
                out_refs[n][...] = jnp.zeros_like(out_refs[n])

        body(in_refs, out_refs)

    return pl.pallas_call(
        wrapped, name=name, grid=(rows // tl,), in_specs=in_specs, out_specs=out_specs,
        out_shape=out_shape, compiler_params=_params("arbitrary"),
    )(*arrays)


def _rms(x, g):
    return x * lax.rsqrt(jnp.mean(x * x, axis=-1, keepdims=True) + EPS) * g


def _norm_fwd(x, g, *, name, tl=256):
    def body(i, o):
        o[0][...] = _rms(i[0][...], i[1][...]).astype(BF16)

    return _rowwise(body, rows=x.shape[0], tl=tl, ins=[_row(x, tl), _par(g)],
                    outs=[("row", x.shape[1], BF16)], name=name)[0]


def _norm_bwd(x, g, dh, dres, *, name, tl=256):
    def body(i, o):
        _, vjp = jax.vjp(_rms, i[0][...], i[1][...])
        dx, dg = vjp(i[2][...])
        o[0][...] = i[3][...] + dx
        o[1][...] += dg

    d = x.shape[1]
    return _rowwise(body, rows=x.shape[0], tl=tl, ins=[_row(x, tl), _par(g), _row(dh, tl), _row(dres, tl)],
                    outs=[("row", d, F32), ("acc", d)], name=name)


def _final(x, g, target, *, name, tl=256):
    d = x.shape[1]

    def body(i, o):
        y, vjp = jax.vjp(_rms, i[0][...], i[1][...])
        err = y - i[2][...]
        dx, dg = vjp(err * (1.0 / d))
        o[0][...] = dx
        o[1][...] += dg
        o[2][...] += jnp.full((1, LANE), 0.5 / d, F32) * jnp.sum(err * err)

    return _rowwise(body, rows=x.shape[0], tl=tl, ins=[_row(x, tl), _par(g), _row(target, tl)],
                    outs=[("row", d, F32), ("acc", d), ("acc", LANE)], name=name)


def _rot(x, cos, sin, out_ref, col, scale=1.0, inverse=False):
    x1, x2 = x[:, :LANE], x[:, LANE:]
    if inverse:
        sin = -sin
    out_ref[:, col:col + LANE] = ((x1 * cos - x2 * sin) * scale).astype(out_ref.dtype)
    out_ref[:, col + LANE:col + 2 * LANE] = ((x1 * sin + x2 * cos) * scale).astype(out_ref.dtype)


def _ret_prep(proj, cos, sin, *, qkw, d, name, tl=256):
    dk = qkw // RET_HEADS
    assert dk == 2 * LANE and (2 * qkw) % d == 0

    def body(i, o):
        c, s = i[3][...], i[4][...]
        for h in range(RET_HEADS):
            _rot(i[0][:, h * dk:(h + 1) * dk], c, s, o[0], h * dk)
            _rot(i[1][:, h * dk:(h + 1) * dk], c, s, o[1], h * dk, scale=dk ** -0.5)
        o[2][...] = i[2][...].astype(BF16)

    return _rowwise(body, rows=proj.shape[0], tl=tl,
                    ins=[_row(proj, tl, qkw, 0), _row(proj, tl, qkw, 1), _row(proj, tl, d, 2 * qkw // d),
                         _row(cos, tl), _row(sin, tl)],
                    outs=[("row", qkw, BF16), ("row", qkw, BF16), ("row", d, BF16)], name=name)


def _ret_prep_bwd(dq_rot, dk_rot, cos, sin, *, name, tl=256):
    qkw = dq_rot.shape[1]
    dk = qkw // RET_HEADS

    def body(i, o):
        c, s = i[2][...], i[3][...]
        for h in range(RET_HEADS):
            _rot(i[0][:, h * dk:(h + 1) * dk], c, s, o[0], h * dk, inverse=True)
            _rot(i[1][:, h * dk:(h + 1) * dk], c, s, o[1], h * dk, scale=dk ** -0.5, inverse=True)

    return _rowwise(body, rows=dq_rot.shape[0], tl=tl,
                    ins=[_row(dq_rot, tl), _row(dk_rot, tl), _row(cos, tl), _row(sin, tl)],
                    outs=[("row", qkw, BF16), ("row", qkw, BF16)], name=name)


def _ret_weights(lgf, lgb):
    t = lax.broadcasted_iota(jnp.int32, (CHUNK, 1), 0).astype(F32)
    diff = (lax.broadcasted_iota(jnp.int32, (CHUNK, CHUNK), 0)
            - lax.broadcasted_iota(jnp.int32, (CHUNK, CHUNK), 1)).astype(F32)
    dmat = jnp.exp(jnp.where(diff >= 0, lgf * diff, -lgb * diff))
    return dict(dmat=dmat, wqf=jnp.exp(lgf * (t + 1.0)), wkf=jnp.exp(lgf * (CHUNK - 1.0 - t)),
                wqb=jnp.exp(lgb * (CHUNK - t)), wkb=jnp.exp(lgb * t))


def _ret_f_part(q, k, v, lgf, lgb, s_f):
    w = _ret_weights(lgf, lgb)
    y = _dnn(_dnt(q, k) * w["dmat"], v) + _dnn(q * w["wqf"], s_f)
    return y, _dtn(k * w["wkf"], v)


def _ret_b_part(q, k, v, lgb, s_b):
    w = _ret_weights(lgb, lgb)
    return _dnn(q * w["wqb"], s_b), _dtn(k * w["wkb"], v)


def _chunk(c):
    return pl.ds(pl.multiple_of(c * CHUNK, CHUNK), CHUNK)


def _ret_specs(l, qkw, d):
    dk, dv = qkw // RET_HEADS, d // RET_HEADS
    return dk, dv, [pl.BlockSpec(memory_space=pltpu.SMEM),
                    pl.BlockSpec((l, dk), lambda h: (0, h)), pl.BlockSpec((l, dk), lambda h: (0, h)),
                    pl.BlockSpec((l, dv), lambda h: (0, h))]


def _ret_fwd(lg, q, k, v, *, name):
    l, qkw = q.shape
    d = v.shape[1]
    nc = l // CHUNK
    dk, dv, in_specs = _ret_specs(l, qkw, d)

    def body(lg_ref, q_ref, k_ref, v_ref, y_ref, s_ref):
        h = pl.program_id(0)
        lgf = jnp.full((1, 1), lg_ref[0, h], F32)
        lgb = jnp.full((1, 1), lg_ref[1, h], F32)
        dec_f, dec_b = jnp.exp(lgf * CHUNK), jnp.exp(lgb * CHUNK)

        def load(c):
            r = _chunk(c)
            return r, q_ref[r, :].astype(F32), k_ref[r, :].astype(F32), v_ref[r, :].astype(F32)

        s_ref[...] = jnp.zeros_like(s_ref)

        def f_step(c, _):
            r, qc, kc, vc = load(c)
            y, kv = _ret_f_part(qc, kc, vc, lgf, lgb, s_ref[...])
            y_ref[r, :] = y
            s_ref[...] = dec_f * s_ref[...] + kv
            return 0

        lax.fori_loop(0, nc, f_step, 0)
        s_ref[...] = jnp.zeros_like(s_ref)

        def b_step(n, _):
            r, qc, kc, vc = load(nc - 1 - n)
            y, kv = _ret_b_part(qc, kc, vc, lgb, s_ref[...])
            y_ref[r, :] += y
            s_ref[...] = dec_b * s_ref[...] + kv
            return 0

        lax.fori_loop(0, nc, b_step, 0)

    return pl.pallas_call(
        body, name=name, grid=(RET_HEADS,), in_specs=in_specs,
        out_specs=pl.BlockSpec((l, dv), lambda h: (0, h)), out_shape=jax.ShapeDtypeStruct((l, d), F32),
        scratch_shapes=[pltpu.VMEM((dk, dv), F32)], compiler_params=_params("arbitrary"),
    )(lg, q, k, v)


def _ret_bwd(lg, q, k, v, dy, *, name):
    l, qkw = q.shape
    d = v.shape[1]
    nc = l // CHUNK
    dk, dv, in_specs = _ret_specs(l, qkw, d)

    def body(lg_ref, q_ref, k_ref, v_ref, dy_ref, dq_ref, dk_ref, dv_ref, dlg_ref, states, s_ref, sh_ref):
        h = pl.program_id(0)
        lgf = jnp.full((1, 1), lg_ref[0, h], F32)
        lgb = jnp.full((1, 1), lg_ref[1, h], F32)
        dec_f, dec_b = jnp.exp(lgf * CHUNK), jnp.exp(lgb * CHUNK)

        def load(c):
            r = _chunk(c)
            return (r, q_ref[r, :].astype(F32), k_ref[r, :].astype(F32), v_ref[r, :].astype(F32),
                    dy_ref[r, :].astype(F32))

        s_ref[...] = jnp.zeros_like(s_ref)

        def f_states(c, _):
            _, qc, kc, vc, _ = load(c)
            states[c] = s_ref[...]
            w = _ret_weights(lgf, lgb)
            s_ref[...] = dec_f * s_ref[...] + _dtn(kc * w["wkf"], vc)
            return 0

        lax.fori_loop(0, nc, f_states, 0)
        sh_ref[...] = jnp.zeros_like(sh_ref)

        def f_adj(n, carry):
            dlf, dlb, ddec = carry
            c = nc - 1 - n
            r, qc, kc, vc, dyc = load(c)
            sc = states[c]
            _, vjp = jax.vjp(_ret_f_part, qc, kc, vc, lgf, lgb, sc)
            dq, dkk, dvv, g_f, g_b, dsc = vjp((dyc, sh_ref[...]))
            dq_ref[r, :] = dq
            dk_ref[r, :] = dkk
            dv_ref[r, :] = dvv
            ddec = ddec + jnp.sum(sh_ref[...] * sc)
            sh_ref[...] = dsc + dec_f * sh_ref[...]
            return dlf + g_f, dlb + g_b, ddec

        z = jnp.zeros((1, 1), F32)
        dlf, dlb, ddec_f = lax.fori_loop(0, nc, f_adj, (z, z, z))

        s_ref[...] = jnp.zeros_like(s_ref)

        def b_states(n, _):
            c = nc - 1 - n
            _, qc, kc, vc, _ = load(c)
            states[c] = s_ref[...]
            w = _ret_weights(lgb, lgb)
            s_ref[...] = dec_b * s_ref[...] + _dtn(kc * w["wkb"], vc)
            return 0

        lax.fori_loop(0, nc, b_states, 0)
        sh_ref[...] = jnp.zeros_like(sh_ref)

        def b_adj(c, carry):
            dlb, ddec = carry
            r, qc, kc, vc, dyc = load(c)
            sc = states[c]
            _, vjp = jax.vjp(_ret_b_part, qc, kc, vc, lgb, sc)
            dq, dkk, dvv, g_b, dsc = vjp((dyc, sh_ref[...]))
            dq_ref[r, :] += dq
            dk_ref[r, :] += dkk
            dv_ref[r, :] += dvv
            ddec = ddec + jnp.sum(sh_ref[...] * sc)
            sh_ref[...] = dsc + dec_b * sh_ref[...]
            return dlb + g_b, ddec

        dlb, ddec_b = lax.fori_loop(0, nc, b_adj, (dlb, z))
        dlf = dlf + ddec_f * dec_f * CHUNK
        dlb = dlb + ddec_b * dec_b * CHUNK
        row = lax.broadcasted_iota(jnp.int32, (SUBLANE, LANE), 0)
        dlg_ref[...] = jnp.where(row == 0, dlf, jnp.where(row == 1, dlb, 0.0))

    head = lambda w: pl.BlockSpec((l, w), lambda h: (0, h))
    return pl.pallas_call(
        body, name=name, grid=(RET_HEADS,), in_specs=in_specs + [head(dv)],
        out_specs=[head(dk), head(dk), head(dv), pl.BlockSpec((None, SUBLANE, LANE), lambda h: (h, 0, 0))],
        out_shape=[jax.ShapeDtypeStruct((l, qkw), F32), jax.ShapeDtypeStruct((l, qkw), F32),
                   jax.ShapeDtypeStruct((l, d), F32), jax.ShapeDtypeStruct((RET_HEADS, SUBLANE, LANE), F32)],
        scratch_shapes=[pltpu.VMEM((nc, dk, dv), F32), pltpu.VMEM((dk, dv), F32), pltpu.VMEM((dk, dv), F32)],
        compiler_params=_params("arbitrary"),
    )(lg, q, k, v, dy)


def _s5_param_fn(a_re, a_im, log_dt, b_re, b_im, rep):
    dt = jnp.exp(log_dt)
    mag = jnp.exp(a_re * dt)
    lam_re, lam_im = mag * jnp.cos(a_im * dt), mag * jnp.sin(a_im * dt)
    n_re, n_im = lam_re - 1.0, lam_im
    den = a_re * a_re + a_im * a_im
    c_re = (n_re * a_re + n_im * a_im) / den
    c_im = (n_im * a_re - n_re * a_im) / den
    hi = lax.Precision.HIGHEST
    c_re = jnp.dot(c_re, rep, precision=hi, preferred_element_type=F32)
    c_im = jnp.dot(c_im, rep, precision=hi, preferred_element_type=F32)
    return lam_re, lam_im, c_re * b_re - c_im * b_im, c_re * b_im + c_im * b_re


def _s5_param_shapes(a_re, b_re):
    r, p = a_re.shape
    return [jax.ShapeDtypeStruct((r, p), F32)] * 2 + [jax.ShapeDtypeStruct(b_re.shape, F32)] * 2


def _s5_prep(a_re, a_im, log_dt, b_re, b_im, rep, *, name):
    def body(*refs):
        outs = _s5_param_fn(*[r[...] for r in refs[:6]])
        for o_ref, o in zip(refs[6:], outs):
            o_ref[...] = o

    return pl.pallas_call(body, name=name, out_shape=_s5_param_shapes(a_re, b_re),
                          compiler_params=_params())(a_re, a_im, log_dt, b_re, b_im, rep)


def _s5_prep_bwd(a_re, a_im, log_dt, b_re, b_im, rep, cts, *, name):
    def body(*refs):
        ins = [r[...] for r in refs[:6]]
        _, vjp = jax.vjp(lambda *p: _s5_param_fn(*p, ins[5]), *ins[:5])
        grads = vjp(tuple(r[...] for r in refs[6:10]))
        for o_ref, o in zip(refs[10:], grads):
            o_ref[...] = o

    shapes = [jax.ShapeDtypeStruct(t.shape, F32) for t in (a_re, a_im, log_dt, b_re, b_im)]
    return pl.pallas_call(body, name=name, out_shape=shapes,
                          compiler_params=_params())(a_re, a_im, log_dt, b_re, b_im, rep, *cts)


def _eye_tiles():
    return jnp.eye(TILE_GROUPS, dtype=F32)


def _b_tiles(bbar, tiles):
    t = bbar.reshape(2, tiles, TILE_GROUPS, SSM_STATE, SSM_GROUP).transpose(0, 1, 2, 4, 3)
    t = t[:, :, :, :, None, :] * _eye_tiles()[None, None, :, None, :, None]
    return t.reshape(2, tiles, TILE_U, TILE_N)


def _b_untile(dbt, tiles):
    t = dbt.reshape(2, tiles, TILE_GROUPS, SSM_GROUP, TILE_GROUPS, SSM_STATE)
    t = (t * _eye_tiles()[None, None, :, None, :, None]).sum(axis=4)
    return t.transpose(0, 1, 2, 4, 3).reshape(2 * tiles * TILE_GROUPS, SSM_STATE * SSM_GROUP)


def _c_tiles(c, tiles):
    t = c.reshape(2, tiles, TILE_GROUPS, SSM_GROUP, SSM_STATE).transpose(0, 1, 2, 4, 3)
    t = t[:, :, :, :, None, :] * _eye_tiles()[None, None, :, None, :, None]
    return t.reshape(2, tiles, TILE_N, TILE_U)


def _c_untile(dct, tiles):
    t = dct.reshape(2, tiles, TILE_GROUPS, SSM_STATE, TILE_GROUPS, SSM_GROUP)
    t = (t * _eye_tiles()[None, None, :, None, :, None]).sum(axis=4)
    return t.transpose(0, 1, 2, 4, 3).reshape(2, tiles * TILE_GROUPS, SSM_GROUP, SSM_STATE)


def _to_segments(a):
    l, w = a.shape
    return a.reshape(N_SEG, l // N_SEG, w).transpose(1, 0, 2).reshape(l, w)


def _from_segments(a):
    l, w = a.shape
    return a.reshape(l // N_SEG, N_SEG, w).transpose(1, 0, 2).reshape(l, w)


def _s5_scan(xr, xi, a_re, a_im, *, length, reverse, shifted=None):
    ls = length // N_SEG
    assert ls * N_SEG == length and ls & (ls - 1) == 0
    ar = [jnp.broadcast_to(a_re[:, c * LANE:(c + 1) * LANE], (N_SEG, LANE)) for c in range(N_LT)]
    ai = [jnp.broadcast_to(a_im[:, c * LANE:(c + 1) * LANE], (N_SEG, LANE)) for c in range(N_LT)]
    zero = jnp.zeros((N_SEG, LANE), F32)
    row = lax.broadcasted_iota(jnp.int32, (N_SEG, LANE), 0)

    def step_of(n):
        return (ls - 1 - n) if reverse else n

    def block(j):
        return pl.ds(j * N_SEG, N_SEG) if isinstance(j, int) else pl.ds(pl.multiple_of(j * N_SEG, N_SEG), N_SEG)

    def local(n, carry):
        rows = block(step_of(n))
        new = []
        for c in range(N_LT):
            cr, ci = carry[2 * c], carry[2 * c + 1]
            nr = ar[c] * cr - ai[c] * ci + xr[rows, _lanes(c)]
            ni = ar[c] * ci + ai[c] * cr + xi[rows, _lanes(c)]
            xr[rows, _lanes(c)] = nr
            xi[rows, _lanes(c)] = ni
            new += [nr, ni]
        return tuple(new)

    def unrolled(body, steps, carry):
        whole = steps // SCAN_UNROLL

        def group(g, carry):
            for u in range(SCAN_UNROLL):
                carry = body(g * SCAN_UNROLL + u, carry)
            return carry

        carry = lax.fori_loop(0, whole, group, carry)
        for n in range(whole * SCAN_UNROLL, steps):
            carry = body(n, carry)
        return carry

    ends = unrolled(local, ls, (zero,) * (2 * N_LT))

    init = []
    for c in range(N_LT):
        pr, pi = ar[c][0:1, :], ai[c][0:1, :]
        for _ in range(ls.bit_length() - 1):
            pr, pi = pr * pr - pi * pi, 2.0 * pr * pi
        cr = ci = jnp.zeros((1, LANE), F32)
        ir, ii = zero, zero
        for s in (range(N_SEG - 1, -1, -1) if reverse else range(N_SEG)):
            ir = jnp.where(row == s, cr, ir)
            ii = jnp.where(row == s, ci, ii)
            er, ei = ends[2 * c][s:s + 1, :], ends[2 * c + 1][s:s + 1, :]
            cr, ci = pr * cr - pi * ci + er, pr * ci + pi * cr + ei
        init += [ir, ii]

    def fix(n, carry, last=False):
        j = step_of(n)
        rows = block(j)
        new, sums = [], []
        for c in range(N_LT):
            cr, ci = carry[2 * c], carry[2 * c + 1]
            nr = ar[c] * cr - ai[c] * ci
            ni = ar[c] * ci + ai[c] * cr
            fr = xr[rows, _lanes(c)] + nr
            fi = xi[rows, _lanes(c)] + ni
            xr[rows, _lanes(c)] = fr
            xi[rows, _lanes(c)] = fi
            new += [nr, ni]
            if shifted is not None:
                yr, yi, shift = shifted
                if not last:
                    srows = block(j + shift)
                    sr, si = yr[srows, _lanes(c)], yi[srows, _lanes(c)]
                else:
                    edge = block(ls - 1 if shift < 0 else 0)
                    move, gone = (1, 0) if shift < 0 else (N_SEG - 1, N_SEG - 1)
                    sr = jnp.where(row == gone, 0.0, pltpu.roll(yr[edge, _lanes(c)], move, 0))
                    si = jnp.where(row == gone, 0.0, pltpu.roll(yi[edge, _lanes(c)], move, 0))
                sums += [carry[2 * N_LT + 2 * c] + fr * sr + fi * si,
                         carry[2 * N_LT + 2 * c + 1] + fi * sr - fr * si]
        return tuple(new + sums)

    if shifted is None:
        unrolled(fix, ls, tuple(init))
        return ()
    assert shifted[2] == (-1 if reverse else 1)
    out = unrolled(fix, ls - 1, tuple(init) + (zero,) * (2 * N_LT))
    return fix(ls - 1, out, last=True)[2 * N_LT:]


def _s5_tile_specs(l, d):
    tile = lambda r, c: pl.BlockSpec((None, None, r, c), lambda t, d=d: (d, t, 0, 0))
    return [pl.BlockSpec((l, TILE_U), lambda t: (0, t)), tile(TILE_U, TILE_N), tile(TILE_U, TILE_N),
            tile(1, TILE_N), tile(1, TILE_N), tile(TILE_N, TILE_U), tile(TILE_N, TILE_U)]


def _lanes(c):
    return slice(c * LANE, (c + 1) * LANE)


def _s5_fwd(u, bt_re, bt_im, lam_re, lam_im, ct_re, ct_im, *, d, name, add=None):
    l = u.shape[0]
    tiles = bt_re.shape[1]
    col = pl.BlockSpec((l, TILE_U), lambda t: (0, t))
    has_add = add is not None

    def body(*refs):
        u_ref, bre, bim, lre, lim, cre, cim = refs[:7]
        y_ref, xr, xi = refs[-3:]
        uu = u_ref[...]
        bu_re, bu_im = _dg(uu, bre[...], 1, 0), _dg(uu, bim[...], 1, 0)
        xr[...] = bu_re
        xi[...] = bu_im
        _s5_scan(xr, xi, lre[...], lim[...], length=l, reverse=(d == 1))
        y = _dg(xr[...], cre[...], 1, 0) - _dg(xi[...], cim[...], 1, 0)
        y_ref[...] = y + refs[7][...] if has_add else y

    return pl.pallas_call(
        body, name=name, grid=(tiles,), in_specs=_s5_tile_specs(l, d) + [col] * has_add, out_specs=col,
        out_shape=jax.ShapeDtypeStruct((l, tiles * TILE_U), F32),
        scratch_shapes=[pltpu.VMEM((l, TILE_N), F32)] * 2, compiler_params=_params("arbitrary"),
    )(u, bt_re, bt_im, lam_re, lam_im, ct_re, ct_im, *([add] if has_add else []))


def _s5_bwd(u, dy, bt_re, bt_im, lam_re, lam_im, ct_re, ct_im, *, d, name, add=None):
    l = u.shape[0]
    tiles = bt_re.shape[1]
    col = pl.BlockSpec((l, TILE_U), lambda t: (0, t))
    reverse = d == 1
    has_add = add is not None

    def body(*refs):
        u_ref, bre, bim, lre, lim, cre, cim, dy_ref = refs[:8]
        du_ref, dbre, dbim, dcre, dcim, dlre, dlim, xr, xi, gr, gi = refs[-11:]
        uu, dyy = u_ref[...], dy_ref[...]
        bu_re, bu_im = _dg(uu, bre[...], 1, 0), _dg(uu, bim[...], 1, 0)
        xr[...] = bu_re
        xi[...] = bu_im
        _s5_scan(xr, xi, lre[...], lim[...], length=l, reverse=reverse)
        gr[...] = _dg(dyy, cre[...], 1, 1)
        gi[...] = -_dg(dyy, cim[...], 1, 1)
        sums = _s5_scan(gr, gi, lre[...], -lim[...], length=l, reverse=not reverse,
                        shifted=(xr, xi, 1 if reverse else -1))
        for c in range(N_LT):
            dlre[:, _lanes(c)] = jnp.sum(sums[2 * c], axis=0, keepdims=True)
            dlim[:, _lanes(c)] = jnp.sum(sums[2 * c + 1], axis=0, keepdims=True)
        g_re, g_im = gr[...], gi[...]
        du = _dg(g_re, bre[...], 1, 1) + _dg(g_im, bim[...], 1, 1)
        du_ref[...] = du + refs[8][...] if has_add else du
        dbre[...] = _dg(uu, g_re, 0, 0)
        dbim[...] = _dg(uu, g_im, 0, 0)
        dcre[...] = _dg(xr[...], dyy, 0, 0)
        dcim[...] = -_dg(xi[...], dyy, 0, 0)

    out3 = lambda r, c: pl.BlockSpec((None, r, c), lambda t: (t, 0, 0))
    f = lambda *s: jax.ShapeDtypeStruct(s, F32)
    return pl.pallas_call(
        body, name=name, grid=(tiles,), in_specs=_s5_tile_specs(l, d) + [col] + [col] * has_add,
        out_specs=[col, out3(TILE_U, TILE_N), out3(TILE_U, TILE_N),
                   out3(TILE_N, TILE_U), out3(TILE_N, TILE_U), out3(1, TILE_N), out3(1, TILE_N)],
        out_shape=[f(l, tiles * TILE_U), f(tiles, TILE_U, TILE_N), f(tiles, TILE_U, TILE_N),
                   f(tiles, TILE_N, TILE_U), f(tiles, TILE_N, TILE_U), f(tiles, 1, TILE_N), f(tiles, 1, TILE_N)],
        scratch_shapes=[pltpu.VMEM((l, TILE_N), F32)] * 4, compiler_params=_params("arbitrary"),
    )(u, bt_re, bt_im, lam_re, lam_im, ct_re, ct_im, dy, *([add] if has_add else []))


def _s5_post(y, proj, u_cb, dskip, *, name, tl=256):
    d = y.shape[1]

    def body(i, o):
        ys = jax.nn.gelu(i[0][...] + i[2][...] * i[1][...])
        o[0][...] = ys
        o[1][...] = ys.astype(BF16)

    return _rowwise(body, rows=y.shape[0], tl=tl, ins=[_row(y, tl), _row(proj, tl, d, u_cb), _par(dskip)],
                    outs=[("row", d, F32), ("row", d, BF16)], name=name)


def _s5_post_bwd(y, proj, u_cb, dskip, dys, *, name, tl=256):
    d = y.shape[1]

    def body(i, o):
        u_ = i[1][...]
        _, vjp = jax.vjp(jax.nn.gelu, i[0][...] + i[2][...] * u_)
        (dpre,) = vjp(i[3][...])
        o[0][...] = dpre
        o[1][...] += jnp.sum(dpre * u_, axis=0, keepdims=True)

    return _rowwise(body, rows=y.shape[0], tl=tl,
                    ins=[_row(y, tl), _row(proj, tl, d, u_cb), _par(dskip), _row(dys, tl)],
                    outs=[("row", d, F32), ("acc", d)], name=name)


def _du_combine(dpre, dskip, du_s5, *, name, tl=256):
    d = dpre.shape[1]

    def body(i, o):
        o[0][...] = (i[0][...] * i[1][...] + i[2][...]).astype(BF16)

    return _rowwise(body, rows=dpre.shape[0], tl=tl, ins=[_row(dpre, tl), _par(dskip), _row(du_s5, tl)],
                    outs=[("row", d, BF16)], name=name)[0]


def _merge_fn(y, g, gate_r, gate_s, ys, glu, b):
    ret = jax.nn.silu(g) * (y * lax.rsqrt(jnp.mean(y * y, axis=-1, keepdims=True) + EPS))
    ssm = ys * jax.nn.sigmoid(glu + b)
    return jax.nn.sigmoid(gate_r) * ret + jax.nn.sigmoid(gate_s) * ssm


def _merge_ins(y_raw, proj, ys, glu, b_glu, cb0, tl):
    d = y_raw.shape[1]
    return [_row(y_raw, tl), _row(proj, tl, d, cb0 + 1), _row(proj, tl, d, cb0 + 3), _row(proj, tl, d, cb0 + 4),
            _row(ys, tl), _row(glu, tl), _par(b_glu)]


def _merge(y_raw, proj, ys, glu, b_glu, *, cb0, name, tl=128):
    d = y_raw.shape[1]
    dv = d // RET_HEADS

    def body(i, o):
        for h in range(RET_HEADS):
            cs = slice(h * dv, (h + 1) * dv)
            o[0][:, cs] = _merge_fn(*[r[:, cs] for r in i]).astype(BF16)

    return _rowwise(body, rows=y_raw.shape[0], tl=tl, ins=_merge_ins(y_raw, proj, ys, glu, b_glu, cb0, tl),
                    outs=[("row", d, BF16)], name=name)[0]


def _merge_bwd(y_raw, proj, ys, glu, b_glu, dmerged, *, cb0, name, tl=128):
    d = y_raw.shape[1]
    dv = d // RET_HEADS

    def body(i, o):
        for h in range(RET_HEADS):
            cs = slice(h * dv, (h + 1) * dv)
            _, vjp = jax.vjp(_merge_fn, *[r[:, cs] for r in i[:7]])
            dy, dg, dgr, dgs, dys, dglu, db = vjp(i[7][:, cs])
            o[0][:, cs] = dy.astype(BF16)
            o[1][:, cs] = dg.astype(BF16)
            o[2][:, cs] = dgr.astype(BF16)
            o[3][:, cs] = dgs.astype(BF16)
            o[4][:, cs] = dglu.astype(BF16)
            o[5][:, cs] = dys
            o[6][:, cs] += db

    return _rowwise(body, rows=y_raw.shape[0], tl=tl,
                    ins=_merge_ins(y_raw, proj, ys, glu, b_glu, cb0, tl) + [_row(dmerged, tl)],
                    outs=[("row", d, BF16)] * 5 + [("row", d, F32), ("acc", d)], name=name)


def _ffn_act_fn(gate, up):
    return jax.nn.silu(gate) * up


def _ffn_act(gate, up, *, name, tl=128):
    def body(i, o):
        o[0][...] = _ffn_act_fn(i[0][...], i[1][...]).astype(BF16)

    return _rowwise(body, rows=gate.shape[0], tl=tl, ins=[_row(gate, tl), _row(up, tl)],
                    outs=[("row", gate.shape[1], BF16)], name=name)[0]


def _ffn_act_bwd(gate, up, dact, *, name, tl=128):
    def body(i, o):
        _, vjp = jax.vjp(_ffn_act_fn, i[0][...], i[1][...])
        dgate, dup = vjp(i[2][...])
        o[0][...] = dgate.astype(BF16)
        o[1][...] = dup.astype(BF16)

    w = gate.shape[1]
    return _rowwise(body, rows=gate.shape[0], tl=tl, ins=[_row(gate, tl), _row(up, tl), _row(dact, tl)],
                    outs=[("row", w, BF16), ("row", w, BF16)], name=name)


def _adamw(w, g, m, v):
    m = ADAM_B1 * m + (1.0 - ADAM_B1) * g
    v = ADAM_B2 * v + (1.0 - ADAM_B2) * (g * g)
    m_hat = m / (1.0 - ADAM_B1 ** ADAM_STEP)
    v_hat = v / (1.0 - ADAM_B2 ** ADAM_STEP)
    return -ADAM_LR * (m_hat / (jnp.sqrt(v_hat) + ADAM_EPS) + ADAM_WD * w), m, v


def _adam_flat(g, w, m, v, *, name, tr=FLAT_ROWS):
    def body(i, o):
        for o_ref, val in zip(o, _adamw(i[1][...], i[0][...], i[2][...], i[3][...])):
            o_ref[...] = val

    return _rowwise(body, rows=g.shape[0], tl=tr, ins=[_row(a, tr) for a in (g, w, m, v)],
                    outs=[("row", LANE, F32)] * 3, name=name)


def _adam_shard(recv, own, me, w, m, v, *, kind, layer, name, tr, tie=None, others=None):
    depth, r, c = w.shape
    assert r % tr == 0 and recv.shape[2] == c
    blk = pl.BlockSpec((None, tr, c), lambda i, me_ref: (layer, i, 0))
    if kind == "col":
        own_spec = pl.BlockSpec((tr, c), lambda i, me_ref: (i, me_ref[0]))
    else:
        per = own.shape[0] // N_DEV // tr
        assert per * tr * N_DEV == own.shape[0]
        own_spec = pl.BlockSpec((tr, c), lambda i, me_ref: (me_ref[0] * per + i, 0))
    extra = ([] if tie is None else [tie]) + list(others or [])

    def body(me_ref, recv_ref, own_ref, w_ref, m_ref, v_ref, *refs):
        g_ref, d_ref, nm_ref, nv_ref = refs[len(extra):]
        g = own_ref[...].astype(F32)
        for k in range(N_DEV - 1):
            g = g + recv_ref[k].astype(F32)
        g_ref[...] = g
        d_ref[...], nm_ref[...], nv_ref[...] = _adamw(w_ref[...], g, m_ref[...], v_ref[...])

    first = 6 + (tie is not None)
    return pl.pallas_call(
        body, name=name, out_shape=[jax.ShapeDtypeStruct(w.shape, F32)] * 4,
        grid_spec=pltpu.PrefetchScalarGridSpec(
            num_scalar_prefetch=1, grid=(r // tr,),
            in_specs=[pl.BlockSpec((N_DEV - 1, tr, c), lambda i, me_ref: (0, i, 0)), own_spec, blk, blk, blk]
            + [pl.BlockSpec(memory_space=pl.ANY)] * len(extra),
            out_specs=[blk] * 4),
        input_output_aliases={first + j: j for j in range(4)} if others else {},
        compiler_params=_params("parallel"),
    )(me, recv, own, w, m, v, *extra)


def _position():
    x, y, c = lax.axis_index("x"), lax.axis_index("y"), lax.axis_index("c")
    return x, y, c, 4 * x + 2 * y + c


def _coords(p):
    return p // 4, (p // 2) % 2, p % 2


def _block_of(ref, kind, p, nb):
    if kind == "col":
        return ref.at[:, pl.ds(pl.multiple_of(p * nb, LANE), nb)]
    return ref.at[pl.ds(pl.multiple_of(p * nb, SUBLANE), nb), :]


def _all_gather(shards, kinds, *, name):
    n = len(shards)
    out_shape = []
    for s, kind in zip(shards, kinds):
        r, c = s.shape
        out_shape.append(jax.ShapeDtypeStruct((r, c * N_DEV) if kind == "col" else (r * N_DEV, c), s.dtype))

    def body(*refs):
        shard_refs, full_refs = refs[:n], refs[n:2 * n]
        send_sems, recv_sems, local_sems = refs[2 * n:]
        x, y, c, me = _position()
        sibling = (x, y, 1 - c)
        chips = [(1 - x, y), (x, 1 - y), (1 - x, 1 - y)]

        def block(t, dev):
            nb = shards[t].shape[1 if kinds[t] == "col" else 0]
            return _block_of(full_refs[t], kinds[t], 4 * dev[0] + 2 * dev[1] + dev[2], nb)

        def copy(t, k, dev, to, src=None):
            return pltpu.make_async_remote_copy(
                src_ref=block(t, dev) if src is None else src, dst_ref=block(t, dev),
                send_sem=send_sems.at[t, k], recv_sem=recv_sems.at[t, k], device_id=to, device_id_type=MESH)

        mine, first, passed = [], [], []
        for t in range(n):
            mine.append(pltpu.make_async_copy(shard_refs[t], block(t, (x, y, c)), local_sems.at[t]))
            mine[-1].start()
            first.append(copy(t, 0, (x, y, c), sibling, src=shard_refs[t]))
            first += [copy(t, 1 + j, (x, y, c), (*chip, c), src=shard_refs[t]) for j, chip in enumerate(chips)]
        for cp in first:
            cp.start()
        for j, chip in enumerate(chips):
            for t in range(n):
                copy(t, 1 + j, (*chip, c), (x, y, c)).wait_recv()
                passed.append(copy(t, 4 + j, (*chip, c), sibling))
                passed[-1].start()
        for t in range(n):
            copy(t, 0, sibling, (x, y, c)).wait_recv()
            for j, chip in enumerate(chips):
                copy(t, 4 + j, (*chip, 1 - c), (x, y, c)).wait_recv()
        for cp in first + passed:
            cp.wait_send()
        for cp in mine:
            cp.wait()

    any_spec = pl.BlockSpec(memory_space=pl.ANY)
    return pl.pallas_call(
        body, name=name, in_specs=[any_spec] * n, out_specs=[any_spec] * n, out_shape=out_shape,
        scratch_shapes=[pltpu.SemaphoreType.DMA((n, N_DEV - 1)), pltpu.SemaphoreType.DMA((n, N_DEV - 1)),
                        pltpu.SemaphoreType.DMA((n,))],
        compiler_params=pltpu.CompilerParams(has_side_effects=True),
    )(*shards)


class _GatherRoute:
    def __init__(self, shards, kinds):
        self.kinds = kinds
        self.nb = [s.shape[1 if k == "col" else 0] for s, k in zip(shards, kinds)]

    def lands(self, shards, me):
        out = []
        for t, (s, k) in enumerate(zip(shards, self.kinds)):
            full = lax.empty((s.shape[0], s.shape[1] * N_DEV) if k == "col" else (s.shape[0] * N_DEV, s.shape[1]), s.dtype)
            out.append(_place_own(s, full, me, k, name=f"own_block_{t}"))
        return out

    def sent(self, t, src_refs, me, k):
        return src_refs[t]

    def lands_at(self, t, land_refs, me, k):
        return _block_of(land_refs[t], self.kinds[t], (me + N_DEV - k) % N_DEV, self.nb[t])


class _ScatterRoute:
    def __init__(self, grads, kinds):
        self.kinds = kinds
        self.nb = [g.shape[1 if k == "col" else 0] // N_DEV for g, k in zip(grads, kinds)]

    def lands(self, grads):
        return [lax.empty((N_DEV - 1, g.shape[0], nb) if k == "col" else (N_DEV - 1, nb, g.shape[1]), g.dtype)
                for g, k, nb in zip(grads, self.kinds, self.nb)]

    def sent(self, t, src_refs, me, k):
        return _block_of(src_refs[t], self.kinds[t], (me + k) % N_DEV, self.nb[t])

    def lands_at(self, t, land_refs, me, k):
        return land_refs[t].at[k - 1]


def _place_own(shard, full, me, kind, *, name):
    r, c = shard.shape
    tr = _row_tile(r)
    if kind == "col":
        dst = pl.BlockSpec((tr, c), lambda i, me_ref: (i, me_ref[0]))
    else:
        dst = pl.BlockSpec((tr, c), lambda i, me_ref: (me_ref[0] * (r // tr) + i, 0))

    def body(me_ref, shard_ref, full_ref, out_ref):
        out_ref[...] = shard_ref[...]

    return pl.pallas_call(
        body, name=name, out_shape=jax.ShapeDtypeStruct(full.shape, full.dtype),
        grid_spec=pltpu.PrefetchScalarGridSpec(
            num_scalar_prefetch=1, grid=(r // tr,),
            in_specs=[pl.BlockSpec((tr, c), lambda i, me_ref: (i, 0)), pl.BlockSpec(memory_space=pl.ANY)],
            out_specs=dst),
        input_output_aliases={2: 0}, compiler_params=_params("parallel"),
    )(me, shard, full)


_HBM = pl.BlockSpec(memory_space=pltpu.HBM)
_SEM = pl.BlockSpec(memory_space=pltpu.SEMAPHORE)
_FLOWING = pltpu.SideEffectType.DATAFLOW_SIDE_EFFECTING


def _exchange_start(srcs, lands, route, after, *, name):
    n = len(srcs)

    def body(*refs):
        src_refs, land_refs = refs[:n], refs[n:2 * n]
        send_sems, recv_sems = refs[2 * n + 1:2 * n + 3]
        token = refs[-1]
        _, _, _, me = _position()
        for t in range(n):
            for k in range(1, N_DEV):
                p = (me + k) % N_DEV
                pltpu.make_async_remote_copy(
                    src_ref=route.sent(t, src_refs, me, k), dst_ref=route.lands_at(t, land_refs, p, k),
                    send_sem=send_sems.at[t * N_DEV + k], recv_sem=recv_sems.at[t * N_DEV + k], device_id=_coords(p),
                    device_id_type=MESH).start()
        token[...] = jnp.zeros_like(token)

    hbm = lambda a: pltpu.HBM(a.shape, a.dtype)
    sems = pltpu.SemaphoreType.DMA((n * N_DEV,))
    out = pl.pallas_call(
        body, name=name,
        out_shape=(sems, sems, *[hbm(a) for a in srcs], *[hbm(a) for a in lands],
                   jax.ShapeDtypeStruct((SUBLANE, LANE), F32)),
        in_specs=[_HBM] * (2 * n) + [pl.BlockSpec(memory_space=pl.ANY)],
        out_specs=(_SEM, _SEM, *[_HBM] * (2 * n), pl.BlockSpec(memory_space=pltpu.VMEM)),
        input_output_aliases={i: 2 + i for i in range(2 * n)},
        compiler_params=pltpu.CompilerParams(has_side_effects=_FLOWING),
    )(*[pltpu.with_memory_space_constraint(a, pltpu.HBM) for a in (*srcs, *lands)], after)
    return out[0], out[1], out[2:2 + n], out[2 + n:2 + 2 * n], out[-1]


def _exchange_wait(started, route, after, *, name):
    send_sems, recv_sems, srcs, lands, _ = started
    n = len(srcs)

    def body(*refs):
        src_refs, land_refs = refs[:n], refs[n:2 * n]
        send_ref, recv_ref = refs[2 * n:2 * n + 2]
        _, _, _, me = _position()
        for t in range(n):
            for k in range(1, N_DEV):
                cp = pltpu.make_async_remote_copy(
                    src_ref=route.sent(t, src_refs, me, k), dst_ref=route.lands_at(t, land_refs, me, k),
                    send_sem=send_ref.at[t * N_DEV + k], recv_sem=recv_ref.at[t * N_DEV + k],
                    device_id=_coords((me + N_DEV - k) % N_DEV), device_id_type=MESH)
                cp.wait_send()
                cp.wait_recv()

    hbm = lambda a: pltpu.HBM(a.shape, a.dtype)
    out = pl.pallas_call(
        body, name=name, out_shape=(*[hbm(a) for a in srcs], *[hbm(a) for a in lands]),
        in_specs=[_HBM] * (2 * n) + [_SEM, _SEM, pl.BlockSpec(memory_space=pl.ANY)],
        out_specs=tuple([_HBM] * (2 * n)), input_output_aliases={i: i for i in range(2 * n)},
        compiler_params=pltpu.CompilerParams(has_side_effects=_FLOWING),
    )(*srcs, *lands, send_sems, recv_sems, after)
    return list(out[:n]), list(out[n:])


def _all_reduce(part, *, name):
    _, r, _ = part.shape

    def body(part_ref, tot_ref, recv_ref, send1, recv1, send2, recv2):
        _, _, _, me = _position()

        def scatter(k, to_me=False):
            p = (me + N_DEV - k) % N_DEV if to_me else (me + k) % N_DEV
            return pltpu.make_async_remote_copy(
                src_ref=part_ref.at[me if to_me else p], dst_ref=recv_ref.at[p if to_me else me],
                send_sem=send1.at[k], recv_sem=recv1.at[k], device_id=_coords(p), device_id_type=MESH)

        def gather(k, to_me=False):
            p = (me + N_DEV - k) % N_DEV if to_me else (me + k) % N_DEV
            return pltpu.make_async_remote_copy(
                src_ref=tot_ref.at[me], dst_ref=tot_ref.at[p if to_me else me],
                send_sem=send2.at[k], recv_sem=recv2.at[k], device_id=_coords(p), device_id_type=MESH)

        for k in range(1, N_DEV):
            scatter(k).start()
        recv_ref[me] = part_ref[me]
        for k in range(1, N_DEV):
            scatter(k, to_me=True).wait_recv()
        total = recv_ref[0]
        for q in range(1, N_DEV):
            total = total + recv_ref[q]
        tot_ref[me] = total
        for k in range(1, N_DEV):
            gather(k).start()
        for k in range(1, N_DEV):
            gather(k, to_me=True).wait_recv()
        for k in range(1, N_DEV):
            scatter(k).wait_send()
            gather(k).wait_send()

    vmem = pl.BlockSpec(memory_space=pltpu.VMEM)
    return pl.pallas_call(
        body, name=name, in_specs=[vmem], out_specs=vmem, out_shape=jax.ShapeDtypeStruct(part.shape, F32),
        scratch_shapes=[pltpu.VMEM(part.shape, F32)] + [pltpu.SemaphoreType.DMA((N_DEV,))] * 4,
        compiler_params=pltpu.CompilerParams(has_side_effects=True, vmem_limit_bytes=VMEM_LIMIT),
    )(part)


def _round_up(n, m):
    return (n + m - 1) // m * m


def _row_tile(rows):
    return next(t for t in (256, 128, 64, 32, 16) if rows % t == 0)


def _local_step(x, target, small, depth, weights_of, grads_done, *, qkw):
    l, d = x.shape
    groups = d // SSM_GROUP
    tiles = groups // TILE_GROUPS
    half = qkw // RET_HEADS // 2
    cb0 = 2 * qkw // d
    inv = 1.0 / (ROPE_BASE ** (jnp.arange(half, dtype=F32) / half))
    ang = jnp.arange(l, dtype=F32)[:, None] * inv[None, :]
    cos, sin = jnp.cos(ang), jnp.sin(ang)
    rep = jnp.repeat(jnp.eye(SSM_STATE, dtype=F32), SSM_GROUP, axis=1)
    row2 = lambda a: a.reshape(1, -1)

    saved = []
    for i in range(depth):
        full_i, x = weights_of(i, x)
        w_in, w_glu, w_out, w_gate, w_up, w_down = full_i
        n = f"l{i}_"
        g_mix, g_ffn = row2(small["ln_mix_g"][i]), row2(small["ln_ffn_g"][i])
        dskip, b_glu = row2(small["ssm_d"][i]), row2(small["b_glu"][i])
        lg = small["ret_log_gamma"][i]
        h = _norm_fwd(x, g_mix, name=n + "norm_mix")
        proj = _matmul(h, w_in, mode="nn", out_dtype=F32, name=n + "proj")
        q_rot, k_rot, v_bf = _ret_prep(proj, cos, sin, qkw=qkw, d=d, name=n + "ret_prep")
        y_raw = _ret_fwd(lg, q_rot, k_rot, v_bf, name=n + "ret_fwd")
        par = [small["ssm_a_re"][i].reshape(2 * groups, SSM_STATE), small["ssm_a_im"][i].reshape(2 * groups, SSM_STATE),
               small["ssm_log_dt"][i].reshape(2 * groups, 1),
               small["ssm_b_re"][i].reshape(2 * groups, SSM_STATE * SSM_GROUP),
               small["ssm_b_im"][i].reshape(2 * groups, SSM_STATE * SSM_GROUP), rep]
        lam_re, lam_im, bbar_re, bbar_im = _s5_prep(*par, name=n + "s5_prep")
        s5 = [_b_tiles(bbar_re, tiles).astype(BF16), _b_tiles(bbar_im, tiles).astype(BF16),
              lam_re.reshape(2, tiles, 1, TILE_N), lam_im.reshape(2, tiles, 1, TILE_N),
              _c_tiles(small["ssm_c_re"][i], tiles).astype(BF16), _c_tiles(small["ssm_c_im"][i], tiles).astype(BF16)]
        u_seg = _to_segments(proj[:, (cb0 + 2) * d:(cb0 + 3) * d])
        y_seg = _s5_fwd(u_seg, *s5, d=0, name=n + "s5_fwd_f")
        y_seg = _s5_fwd(u_seg, *s5, d=1, add=y_seg, name=n + "s5_fwd_b")
        y_s5 = _from_segments(y_seg)
        ys, ys_bf = _s5_post(y_s5, proj, cb0 + 2, dskip, name=n + "s5_post")
        glu = _matmul(ys_bf, w_glu, mode="nn", out_dtype=F32, name=n + "glu")
        merged = _merge(y_raw, proj, ys, glu, b_glu, cb0=cb0, name=n + "merge")
        x1 = _matmul(merged, w_out, mode="nn", out_dtype=F32, res=x, name=n + "out")
        h2 = _norm_fwd(x1, g_ffn, name=n + "norm_ffn")
        gate = _matmul(h2, w_gate, mode="nn", out_dtype=F32, name=n + "gate")
        up = _matmul(h2, w_up, mode="nn", out_dtype=F32, name=n + "up")
        act = _ffn_act(gate, up, name=n + "act")
        x2 = _matmul(act, w_down, mode="nn", out_dtype=F32, res=x1, name=n + "down")
        saved.append(dict(full=full_i, x=x, h=h, proj=proj, q_rot=q_rot, k_rot=k_rot, v_bf=v_bf, y_raw=y_raw, par=par, s5=s5,
                          u_seg=u_seg, y_s5=y_s5, ys=ys, ys_bf=ys_bf, glu=glu, merged=merged, x1=x1, h2=h2, gate=gate,
                          up=up, act=act))
        x = x2

    dx, dg_final, loss = _final(x, row2(small["ln_final_g"]), target, name="final")

    sg = {k: [None] * depth for k in ("ln_mix_g", "ret_log_gamma", "ssm_a_re", "ssm_a_im", "ssm_log_dt", "ssm_b_re",
                                      "ssm_b_im", "ssm_c_re", "ssm_c_im", "ssm_d", "b_glu", "ln_ffn_g")}
    for i in reversed(range(depth)):
        s = saved[i]
        big = [None] * len(BIG)
        w_in, w_glu, w_out, w_gate, w_up, w_down = s["full"]
        n = f"l{i}_b_"
        g_mix, g_ffn = row2(small["ln_mix_g"][i]), row2(small["ln_ffn_g"][i])
        dskip, b_glu = row2(small["ssm_d"][i]), row2(small["b_glu"][i])
        lg = small["ret_log_gamma"][i]
        dact = _matmul(dx, w_down, mode="nt", out_dtype=F32, name=n + "dact")
        big[5] = _matmul(s["act"], dx, mode="tn", out_dtype=BF16, name=n + "dw_down")
        dgate, dup = _ffn_act_bwd(s["gate"], s["up"], dact, name=n + "act")
        dh2 = _matmul(dgate, w_gate, mode="nt", out_dtype=F32, name=n + "dh2_gate")
        dh2 = _matmul(dup, w_up, mode="nt", out_dtype=F32, res=dh2, name=n + "dh2_up")
        big[3] = _matmul(s["h2"], dgate, mode="tn", out_dtype=BF16, name=n + "dw_gate")
        big[4] = _matmul(s["h2"], dup, mode="tn", out_dtype=BF16, name=n + "dw_up")
        dh2 = grads_done(i, FFN_GROUP, [big[t] for t in FFN_GROUP], dh2)
        dx1, dgf = _norm_bwd(s["x1"], g_ffn, dh2, dx, name=n + "norm_ffn")
        sg["ln_ffn_g"][i] = dgf[0]
        dmerged = _matmul(dx1, w_out, mode="nt", out_dtype=F32, name=n + "dmerged")
        big[2] = _matmul(s["merged"], dx1, mode="tn", out_dtype=BF16, name=n + "dw_out")
        dy_raw, dg, dgate_r, dgate_s, dglu, dys_a, db_glu = _merge_bwd(
            s["y_raw"], s["proj"], s["ys"], s["glu"], b_glu, dmerged, cb0=cb0, name=n + "merge")
        sg["b_glu"][i] = db_glu[0]
        dys = _matmul(dglu, w_glu, mode="nt", out_dtype=F32, res=dys_a, name=n + "dys")
        big[1] = _matmul(s["ys_bf"], dglu, mode="tn", out_dtype=BF16, name=n + "dw_glu")
        dpre, dd = _s5_post_bwd(s["y_s5"], s["proj"], cb0 + 2, dskip, dys, name=n + "s5_post")
        sg["ssm_d"][i] = dd[0]
        dpre_seg = _to_segments(dpre)
        r_f = _s5_bwd(s["u_seg"], dpre_seg, *s["s5"], d=0, name=n + "s5_bwd_f")
        r_b = _s5_bwd(s["u_seg"], dpre_seg, *s["s5"], d=1, add=r_f[0], name=n + "s5_bwd_b")
        du = _du_combine(dpre, dskip, _from_segments(r_b[0]), name=n + "du")
        both = lambda k: jnp.stack([r_f[k], r_b[k]])
        cts = [both(5).reshape(2 * groups, SSM_STATE), both(6).reshape(2 * groups, SSM_STATE),
               _b_untile(both(1), tiles), _b_untile(both(2), tiles)]
        da_re, da_im, dldt, db_re, db_im = _s5_prep_bwd(*s["par"], cts, name=n + "s5_prep")
        sg["ssm_a_re"][i] = da_re.reshape(2, groups, SSM_STATE)
        sg["ssm_a_im"][i] = da_im.reshape(2, groups, SSM_STATE)
        sg["ssm_log_dt"][i] = dldt.reshape(2, groups)
        sg["ssm_b_re"][i] = db_re.reshape(2, groups, SSM_STATE, SSM_GROUP)
        sg["ssm_b_im"][i] = db_im.reshape(2, groups, SSM_STATE, SSM_GROUP)
        sg["ssm_c_re"][i] = _c_untile(both(3), tiles)
        sg["ssm_c_im"][i] = _c_untile(both(4), tiles)
        dq_rot, dk_rot, dv, dlg = _ret_bwd(lg, s["q_rot"], s["k_rot"], s["v_bf"], dy_raw, name=n + "ret_bwd")
        sg["ret_log_gamma"][i] = dlg[:, :2, 0].T
        dq, dk = _ret_prep_bwd(dq_rot, dk_rot, cos, sin, name=n + "ret_prep")
        dproj = jnp.concatenate([dq, dk, dv.astype(BF16), dg, du, dgate_r, dgate_s], axis=1)
        dh = _matmul(dproj, w_in, mode="nt", out_dtype=F32, name=n + "dh")
        big[0] = _matmul(s["h"], dproj, mode="tn", out_dtype=BF16, name=n + "dw_in")
        dx, dgm = _norm_bwd(s["x"], g_mix, dh, dx1, name=n + "norm_mix")
        sg["ln_mix_g"][i] = dgm[0]
        dx = grads_done(i, MIX_GROUP, [big[t] for t in MIX_GROUP], dx)

    small_grads = {k: jnp.stack(v) for k, v in sg.items()}
    small_grads["ln_final_g"] = dg_final[0]
    return loss, dx, small_grads


BIG = ("w_in", "w_glu", "w_out", "w_ffn_gate", "w_ffn_up", "w_ffn_down")
BIG_KINDS = ("col", "row", "row", "col", "col", "row")
MIX_GROUP = (0, 1, 2)
FFN_GROUP = (3, 4, 5)
SMALL = ("ln_mix_g", "ret_log_gamma", "ssm_a_re", "ssm_a_im", "ssm_log_dt", "ssm_b_re", "ssm_b_im", "ssm_c_re",
         "ssm_c_im", "ssm_d", "b_glu", "ln_ffn_g", "ln_final_g")
WEIGHTS = ("ln_mix_g", "w_in", "ret_log_gamma", "ssm_a_re", "ssm_a_im", "ssm_log_dt", "ssm_b_re", "ssm_b_im",
           "ssm_c_re", "ssm_c_im", "ssm_d", "w_glu", "b_glu", "w_out", "ln_ffn_g", "w_ffn_gate", "w_ffn_up",
           "w_ffn_down", "ln_final_g")


def _pad_to(a, axis, size):
    pad = [(0, 0)] * a.ndim
    pad[axis] = (0, size - a.shape[axis])
    return jnp.pad(a, pad)


def _flatten_small(tree, extra):
    def as_rows(a):
        a = a.reshape(-1).astype(F32)
        return _pad_to(a, 0, _round_up(a.shape[0], SUBLANE * LANE)).reshape(-1, LANE)

    flat = jnp.concatenate([as_rows(tree[k]) for k in SMALL] + [as_rows(extra)])
    return _pad_to(flat, 0, _round_up(flat.shape[0], FLAT_ROWS))


def _unflatten_small(flat, like):
    out, at = {}, 0
    for k in SMALL:
        n = like[k].size
        rows = _round_up(n, SUBLANE * LANE) // LANE
        out[k] = flat[at:at + rows].reshape(-1)[:n].reshape(like[k].shape)
        at += rows
    return out, flat[at, 0]


def kernel(x, ln_mix_g, w_in, ret_log_gamma, ssm_a_re, ssm_a_im, ssm_log_dt, ssm_b_re, ssm_b_im, ssm_c_re, ssm_c_im, ssm_d, w_glu, b_glu, w_out, ln_ffn_g, w_ffn_gate, w_ffn_up, w_ffn_down, ln_final_g, loss_target, m_ln_mix_g, m_w_in, m_ret_log_gamma, m_ssm_a_re, m_ssm_a_im, m_ssm_log_dt, m_ssm_b_re, m_ssm_b_im, m_ssm_c_re, m_ssm_c_im, m_ssm_d, m_w_glu, m_b_glu, m_w_out, m_ln_ffn_g, m_w_ffn_gate, m_w_ffn_up, m_w_ffn_down, m_ln_final_g, v_ln_mix_g, v_w_in, v_ret_log_gamma, v_ssm_a_re, v_ssm_a_im, v_ssm_log_dt, v_ssm_b_re, v_ssm_b_im, v_ssm_c_re, v_ssm_c_im, v_ssm_d, v_w_glu, v_b_glu, v_w_out, v_ln_ffn_g, v_w_ffn_gate, v_w_ffn_up, v_w_ffn_down, v_ln_final_g):
    w = dict(ln_mix_g=ln_mix_g, w_in=w_in, ret_log_gamma=ret_log_gamma, ssm_a_re=ssm_a_re, ssm_a_im=ssm_a_im, ssm_log_dt=ssm_log_dt, ssm_b_re=ssm_b_re, ssm_b_im=ssm_b_im, ssm_c_re=ssm_c_re, ssm_c_im=ssm_c_im, ssm_d=ssm_d, w_glu=w_glu, b_glu=b_glu, w_out=w_out, ln_ffn_g=ln_ffn_g, w_ffn_gate=w_ffn_gate, w_ffn_up=w_ffn_up, w_ffn_down=w_ffn_down, ln_final_g=ln_final_g)
    m = dict(ln_mix_g=m_ln_mix_g, w_in=m_w_in, ret_log_gamma=m_ret_log_gamma, ssm_a_re=m_ssm_a_re, ssm_a_im=m_ssm_a_im, ssm_log_dt=m_ssm_log_dt, ssm_b_re=m_ssm_b_re, ssm_b_im=m_ssm_b_im, ssm_c_re=m_ssm_c_re, ssm_c_im=m_ssm_c_im, ssm_d=m_ssm_d, w_glu=m_w_glu, b_glu=m_b_glu, w_out=m_w_out, ln_ffn_g=m_ln_ffn_g, w_ffn_gate=m_w_ffn_gate, w_ffn_up=m_w_ffn_up, w_ffn_down=m_w_ffn_down, ln_final_g=m_ln_final_g)
    v = dict(ln_mix_g=v_ln_mix_g, w_in=v_w_in, ret_log_gamma=v_ret_log_gamma, ssm_a_re=v_ssm_a_re, ssm_a_im=v_ssm_a_im, ssm_log_dt=v_ssm_log_dt, ssm_b_re=v_ssm_b_re, ssm_b_im=v_ssm_b_im, ssm_c_re=v_ssm_c_re, ssm_c_im=v_ssm_c_im, ssm_d=v_ssm_d, w_glu=v_w_glu, b_glu=v_b_glu, w_out=v_w_out, ln_ffn_g=v_ln_ffn_g, w_ffn_gate=v_w_ffn_gate, w_ffn_up=v_w_ffn_up, w_ffn_down=v_w_ffn_down, ln_final_g=v_ln_final_g)
    depth, d, nb_in = w_in.shape
    qkw = (nb_in * N_DEV - 5 * d) // 2
    nb_ffn = w_ffn_gate.shape[2]
    nb_pad = _round_up(nb_ffn, LANE)
    pad_axis = {"w_ffn_gate": 2, "w_ffn_up": 2, "w_ffn_down": 1}

    assert depth == 2
    padded = {k: w[k] if k not in pad_axis else _pad_to(w[k], pad_axis[k], nb_pad) for k in BIG}
    shards = [[padded[k][i].astype(BF16) for k in BIG] for i in range(depth)]
    full0 = _all_gather(shards[0], BIG_KINDS, name="gather_l0")
    me = (4 * lax.axis_index("x") + 2 * lax.axis_index("y") + lax.axis_index("c")).astype(jnp.int32).reshape(1)
    gather1 = _GatherRoute(shards[1], BIG_KINDS)
    gathering = _exchange_start(shards[1], gather1.lands(shards[1], me), gather1, full0[0], name="gather_l1_start")

    def weights_of(i, act):
        if i == 0:
            return full0, act + gathering[4][0, 0]
        return _exchange_wait(gathering, gather1, act, name="gather_l1_wait")[1], act

    exchanges, held = [], []

    def grads_done(i, group, grads, act):
        route = _ScatterRoute(grads, [BIG_KINDS[t] for t in group])
        if i == 0 and group == MIX_GROUP:
            held.append((i, group, grads, route))
            return act
        started = _exchange_start(grads, route.lands(grads), route, act, name=f"scatter_l{i}_{group[0]}_start")
        exchanges.append((i, group, started, route))
        return act + started[4][0, 0]

    small = {k: w[k] for k in SMALL}
    loss, dx, small_grads = _local_step(x[0], loss_target[0], small, depth, weights_of, grads_done, qkw=qkw)
    arrived, tie = {}, dx
    for i, group, started, route in exchanges:
        own, recv = _exchange_wait(started, route, tie, name=f"scatter_l{i}_{group[0]}_wait")
        arrived.update({(i, t): (own[j], recv[j]) for j, t in enumerate(group)})
        tie = recv[0]

    part = _flatten_small(small_grads, loss[0, :1])
    rows = part.shape[0]
    total = _all_reduce(part.reshape(N_DEV, rows // N_DEV, LANE), name="reduce_small").reshape(rows, LANE)
    (i, group, held_grads, route), = held
    started = _exchange_start(held_grads, route.lands(held_grads), route, total, name=f"scatter_l{i}_{group[0]}_start")
    zero = jnp.zeros((1,), F32)
    flat = [_flatten_small({k: a[k] for k in SMALL}, zero) for a in (w, m, v)]
    upd = _adam_flat(total + started[4][0, 0], *flat, name="adam_small")
    grads, delta, new_m, new_v = {}, {}, {}, {}
    g_small, loss_total = _unflatten_small(total, small)
    grads.update(g_small)
    for dst, u in zip((delta, new_m, new_v), upd):
        dst.update(_unflatten_small(u, small)[0])

    ops = {k: [_pad_to(a[k], 2, nb_pad) for a in (w, m, v)] if k in ("w_ffn_gate", "w_ffn_up") else [a[k] for a in (w, m, v)]
           for k in BIG}
    half, tie = {}, upd[0]

    def adam(layer, t, tie):
        k = BIG[t]
        own, recv = arrived[layer, t]
        return _adam_shard(recv, own, me, *ops[k], kind=BIG_KINDS[t], layer=layer, name=f"adam_l{layer}_{k}",
                           tr=_row_tile(ops[k][0].shape[1]), tie=tie, others=half.get(k))

    for t in range(len(BIG)):
        half[BIG[t]] = adam(1, t, tie)
        tie = half[BIG[t]][0]
    for t in FFN_GROUP:
        half[BIG[t]] = adam(0, t, tie)
        tie = half[BIG[t]][0]
    own, recv = _exchange_wait(started, route, tie, name=f"scatter_l{i}_{group[0]}_wait")
    arrived.update({(i, t): (own[j], recv[j]) for j, t in enumerate(group)})
    for t in MIX_GROUP:
        half[BIG[t]] = adam(0, t, None)
    for k in BIG:
        res = half[k]
        if k in ("w_ffn_gate", "w_ffn_up"):
            res = [r[:, :, :nb_ffn] for r in res]
        grads[k], delta[k], new_m[k], new_v[k] = res

    return (loss_total, dx[None], *[grads[k] for k in WEIGHTS], *[delta[k] for k in WEIGHTS],
            *[new_m[k] for k in WEIGHTS], *[new_v[k] for k in WEIGHTS])
```
